```python
import math
import jax
import jax.numpy as jnp
from jax import lax
import numpy as np

D_MODEL = 1024
BATCH = 8
SEQ = 2048
DEPTH = 4

CTX_LEN = 256
GRID_W = 64
N_MIXERS = 2
NA_HEADS = 16
NA_HEAD_DIM = D_MODEL // NA_HEADS
NA_WIN_ROWS = 8
NA_WIN_COLS = 16
DIFF_HEADS = 8
DIFF_HEAD_DIM = D_MODEL // (2 * DIFF_HEADS)
ROPE_THETA = 10000.0
Q_BLOCK = 128
N_EXPERTS = 16
N_GROUPS = 4
EXPERTS_PER_GROUP = N_EXPERTS // N_GROUPS
TOP_K = 2
D_EXPERT = 1024
N_MOD = 6
NORM_EPS = 1e-6
N_NA_LAYERS = (DEPTH + 1) // 2
N_DIFF_LAYERS = DEPTH // 2

kernel_name = "hybrid_natten_diffattn_grouped_moe_dit"


def rms_norm(x, g):
    xf = x.astype(jnp.float32)
    y = xf * lax.rsqrt(jnp.mean(xf * xf, axis=-1, keepdims=True) + NORM_EPS)
    return (y * g.astype(jnp.float32)).astype(x.dtype)


def modulate(h, shift, scale):
    return h * (1 + scale) + shift


def to_heads(t, n_heads, head_dim):
    b, n, _ = t.shape
    return t.reshape(b, n, n_heads, head_dim).transpose(0, 2, 1, 3)


def from_heads(t):
    b, h, n, d = t.shape
    return t.transpose(0, 2, 1, 3).reshape(b, n, h * d)


def axial_rope(n_tokens, head_dim):
    t = jnp.arange(n_tokens)
    row = (t // GRID_W).astype(jnp.float32)
    col = (t % GRID_W).astype(jnp.float32)
    n_freq = head_dim // 4
    inv_freq = ROPE_THETA ** (-jnp.arange(n_freq, dtype=jnp.float32) / n_freq)
    ang = jnp.concatenate([row[:, None] * inv_freq, col[:, None] * inv_freq], axis=-1)
    return jnp.cos(ang), jnp.sin(ang)


def apply_rope(x, cos, sin):
    cos = cos.astype(x.dtype)
    sin = sin.astype(x.dtype)
    x1, x2 = jnp.split(x, 2, axis=-1)
    return jnp.concatenate([x1 * cos - x2 * sin, x1 * sin + x2 * cos], axis=-1)


def dense_attention(q, k, v, scale):
    s = jnp.einsum("bhqd,bhkd->bhqk", q, k).astype(jnp.float32) * scale
    p = jax.nn.softmax(s, axis=-1).astype(v.dtype)
    return jnp.einsum("bhqk,bhkd->bhqd", p, v)


def neighbourhood_attention(hl, hc, w_qkv, w_o, rpb, ctx_out):
    b, s, _ = hl.shape
    rows = s // GRID_W
    wr = min(NA_WIN_ROWS, rows)
    n_band = wr * GRID_W
    scale = NA_HEAD_DIM ** -0.5
    ql, kl, vl = [to_heads(t, NA_HEADS, NA_HEAD_DIM) for t in jnp.split(hl @ w_qkv, 3, axis=-1)]
    qc, kc, vc = [to_heads(t, NA_HEADS, NA_HEAD_DIM) for t in jnp.split(hc @ w_qkv, 3, axis=-1)]
    kl_rows = kl.reshape(b, NA_HEADS, rows, GRID_W, NA_HEAD_DIM)
    vl_rows = vl.reshape(b, NA_HEADS, rows, GRID_W, NA_HEAD_DIM)
    q_blocks = ql.reshape(b, NA_HEADS, rows, GRID_W, NA_HEAD_DIM).transpose(2, 0, 1, 3, 4)
    row_ids = jnp.arange(rows)
    row_start = jnp.clip(row_ids - wr // 2, 0, rows - wr)
    col = jnp.arange(GRID_W)
    col_start = jnp.clip(col - NA_WIN_COLS // 2, 0, GRID_W - NA_WIN_COLS)
    in_window = (col[None, :] >= col_start[:, None]) & (col[None, :] < col_start[:, None] + NA_WIN_COLS)
    band_mask = jnp.broadcast_to(in_window[:, None, :], (GRID_W, wr, GRID_W)).reshape(GRID_W, n_band)
    n_dc = 2 * NA_WIN_COLS - 1
    dc_idx = jnp.clip(col[None, :] - col[:, None] + NA_WIN_COLS - 1, 0, n_dc - 1)
    rpb_flat = rpb.reshape(NA_HEADS, -1)

    def row_block(args):
        q, r, r0 = args
        k = lax.dynamic_slice_in_dim(kl_rows, r0, wr, axis=2).reshape(b, NA_HEADS, n_band, NA_HEAD_DIM)
        v = lax.dynamic_slice_in_dim(vl_rows, r0, wr, axis=2).reshape(b, NA_HEADS, n_band, NA_HEAD_DIM)
        dr_idx = r0 + jnp.arange(wr) - r + NA_WIN_ROWS - 1
        bias_idx = (dr_idx[None, :, None] * n_dc + dc_idx[:, None, :]).reshape(GRID_W, n_band)
        bias = rpb_flat[:, bias_idx].astype(jnp.float32)
        s_nb = jnp.einsum("bhqd,bhkd->bhqk", q, k).astype(jnp.float32) * scale + bias
        s_nb = jnp.where(band_mask, s_nb, -jnp.inf)
        s_ctx = jnp.einsum("bhqd,bhkd->bhqk", q, kc).astype(jnp.float32) * scale
        p = jax.nn.softmax(jnp.concatenate([s_nb, s_ctx], axis=-1), axis=-1).astype(v.dtype)
        return (jnp.einsum("bhqk,bhkd->bhqd", p[..., :n_band], v)
                + jnp.einsum("bhqk,bhkd->bhqd", p[..., n_band:], vc))

    o = lax.map(row_block, (q_blocks, row_ids, row_start))
    o = o.transpose(1, 2, 0, 3, 4).reshape(b, NA_HEADS, s, NA_HEAD_DIM)
    yl = from_heads(o) @ w_o
    yc = from_heads(dense_attention(qc, kc, vc, scale)) @ w_o if ctx_out else None
    return yl, yc


def diff_lambda_init(layer_idx):
    return 0.8 - 0.6 * math.exp(-0.3 * (layer_idx - 1))


def diff_attention(hl, hc, w_qkv, w_o, lam, subln_g, lam_init, cos, sin, ctx_out):
    b, s, _ = hl.shape
    scale = DIFF_HEAD_DIM ** -0.5

    def project(h):
        n = h.shape[1]
        q, k, v = jnp.split(h @ w_qkv, 3, axis=-1)
        q = q.reshape(b, n, DIFF_HEADS, 2, DIFF_HEAD_DIM).transpose(0, 2, 3, 1, 4)
        k = k.reshape(b, n, DIFF_HEADS, 2, DIFF_HEAD_DIM).transpose(0, 2, 3, 1, 4)
        v = to_heads(v, DIFF_HEADS, 2 * DIFF_HEAD_DIM)
        return q, k, v

    ql, kl, vl = project(hl)
    qc, kc, vc = project(hc)
    ql = apply_rope(ql, cos, sin)
    kl = apply_rope(kl, cos, sin)
    lam_f = lam.astype(jnp.float32)
    lam_full = jnp.exp(jnp.sum(lam_f[0] * lam_f[1])) - jnp.exp(jnp.sum(lam_f[2] * lam_f[3])) + lam_init

    def diff_map(q, k, v):
        sc = jnp.einsum("bhiqd,bhikd->bhiqk", q, k).astype(jnp.float32) * scale
        p = jax.nn.softmax(sc, axis=-1)
        a = (p[:, :, 0] - lam_full * p[:, :, 1]).astype(v.dtype)
        return jnp.einsum("bhqk,bhkd->bhqd", a, v)

    k_all = jnp.concatenate([kl, kc], axis=3)
    v_all = jnp.concatenate([vl, vc], axis=2)
    nb = s // Q_BLOCK
    q_blocks = ql.reshape(b, DIFF_HEADS, 2, nb, Q_BLOCK, DIFF_HEAD_DIM).transpose(3, 0, 1, 2, 4, 5)
    o = lax.map(lambda q: diff_map(q, k_all, v_all), q_blocks)
    o = o.transpose(1, 2, 0, 3, 4).reshape(b, DIFF_HEADS, s, 2 * DIFF_HEAD_DIM)

    def finish(t):
        return from_heads(rms_norm(t, subln_g) * (1 - lam_init)) @ w_o

    yl = finish(o)
    yc = finish(diff_map(qc, kc, vc)) if ctx_out else None
    return yl, yc


def grouped_moe(h, router_w, router_b, w_gate, w_up, w_down):
    t = h.shape[0]
    scores = jax.nn.sigmoid((h @ router_w).astype(jnp.float32))
    sel = (scores + router_b.astype(jnp.float32)).reshape(t, N_GROUPS, EXPERTS_PER_GROUP)
    group_score = jnp.sum(lax.top_k(sel, 2)[0], axis=-1)
    group = jnp.argmax(group_score, axis=-1)
    sel_in = jnp.take_along_axis(sel, group[:, None, None], axis=1)[:, 0]
    _, local = lax.top_k(sel_in, TOP_K)
    expert = group[:, None] * EXPERTS_PER_GROUP + local
    gate = jnp.take_along_axis(scores, expert, axis=-1)
    gate = gate / jnp.sum(gate, axis=-1, keepdims=True)
    flat_e = expert.reshape(-1)
    order = jnp.argsort(flat_e)
    tok = order // TOP_K
    sizes = jnp.bincount(flat_e, length=N_EXPERTS).astype(jnp.int32)
    xs = h[tok]
    a = lax.ragged_dot(xs, w_gate, sizes)
    u = lax.ragged_dot(xs, w_up, sizes)
    y = lax.ragged_dot(jax.nn.silu(a) * u, w_down, sizes)
    y = y * gate.reshape(-1)[order][:, None].astype(y.dtype)
    return jnp.zeros_like(h).at[tok].add(y.astype(h.dtype))


def setup_inputs(seed: int = 0) -> dict:
    key = jax.random.key(seed)
    ks = jax.random.split(key, 19)
    d = D_MODEL
    inv = d ** -0.5

    def normal(k, shape, std):
        return jax.random.normal(k, shape, jnp.float32) * std

    return {
        "x": normal(ks[0], (BATCH, SEQ, d), 1.0),
        "c": normal(ks[1], (BATCH, d), 1.0),
        "ctx": normal(ks[2], (BATCH, CTX_LEN, d), 1.0),
        "c_ctx": normal(ks[3], (d,), 1.0),
        "mod_w": normal(ks[4], (DEPTH, d, N_MOD * d), 0.5 * inv),
        "mod_b": normal(ks[5], (DEPTH, N_MOD * d), 0.02),
        "norm_mix": 1.0 + normal(ks[6], (DEPTH, d), 0.02),
        "norm_ffn": 1.0 + normal(ks[7], (DEPTH, d), 0.02),
        "w_qkv": normal(ks[8], (DEPTH, d, 3 * d), inv),
        "w_o": normal(ks[9], (DEPTH, d, d), inv),
        "na_rpb": normal(ks[10], (N_NA_LAYERS, NA_HEADS, 2 * NA_WIN_ROWS - 1, 2 * NA_WIN_COLS - 1), 0.1),
        "diff_lambda": normal(ks[11], (N_DIFF_LAYERS, 4, DIFF_HEAD_DIM), 0.1),
        "diff_subln": 1.0 + normal(ks[12], (N_DIFF_LAYERS, 2 * DIFF_HEAD_DIM), 0.02),
        "router_w": normal(ks[13], (d, N_EXPERTS), inv),
        "router_b": normal(ks[14], (N_EXPERTS,), 0.01),
        "expert_w_gate": normal(ks[15], (DEPTH, N_EXPERTS, d, D_EXPERT), inv),
        "expert_w_up": normal(ks[16], (DEPTH, N_EXPERTS, d, D_EXPERT), inv),
        "expert_w_down": normal(ks[17], (DEPTH, N_EXPERTS, D_EXPERT, d), D_EXPERT ** -0.5),
        "final_norm": 1.0 + normal(ks[18], (d,), 0.02),
    }


def reference(x, c, ctx, c_ctx, mod_w, mod_b, norm_mix, norm_ffn, w_qkv, w_o, na_rpb, diff_lambda,
              diff_subln, router_w, router_b, expert_w_gate, expert_w_up, expert_w_down, final_norm):
    b, s, d_model = x.shape
    cos, sin = axial_rope(s, DIFF_HEAD_DIM)
    c_act = jax.nn.silu(c)
    c_ctx_act = jax.nn.silu(c_ctx)
    xl, xc = x, ctx
    for i in range(DEPTH):
        last = i == DEPTH - 1
        mod_l = jnp.split((c_act @ mod_w[i] + mod_b[i])[:, None, :], N_MOD, axis=-1)
        mod_c = jnp.split((c_ctx_act @ mod_w[i] + mod_b[i])[None, None, :], N_MOD, axis=-1)
        hl = modulate(rms_norm(xl, norm_mix[i]), mod_l[0], mod_l[1])
        hc = modulate(rms_norm(xc, norm_mix[i]), mod_c[0], mod_c[1])
        j = i // N_MIXERS
        if i % N_MIXERS == 0:
            yl, yc = neighbourhood_attention(hl, hc, w_qkv[i], w_o[i], na_rpb[j], not last)
        else:
            yl, yc = diff_attention(hl, hc, w_qkv[i], w_o[i], diff_lambda[j], diff_subln[j],
                                    diff_lambda_init(i + 1), cos, sin, not last)
        xl = xl + mod_l[2] * yl
        hl = modulate(rms_norm(xl, norm_ffn[i]), mod_l[3], mod_l[4])
        if last:
            y = grouped_moe(hl.reshape(-1, d_model), router_w, router_b,
                            expert_w_gate[i], expert_w_up[i], expert_w_down[i])
            xl = xl + mod_l[5] * y.reshape(b, s, d_model)
        else:
            xc = xc + mod_c[2] * yc
            hc = modulate(rms_norm(xc, norm_ffn[i]), mod_c[3], mod_c[4])
            n_lat = b * s
            tokens = jnp.concatenate([hl.reshape(-1, d_model), hc.reshape(-1, d_model)], axis=0)
            y = grouped_moe(tokens, router_w, router_b, expert_w_gate[i], expert_w_up[i], expert_w_down[i])
            xl = xl + mod_l[5] * y[:n_lat].reshape(b, s, d_model)
            xc = xc + mod_c[5] * y[n_lat:].reshape(xc.shape)
    return rms_norm(xl, final_norm)
```

```python
import functools
import math

import jax
import jax.numpy as jnp
from jax import lax
from jax.experimental import pallas as pl
from jax.experimental.pallas import tpu as pltpu

F32 = jnp.float32
BF16 = jnp.bfloat16

D_MODEL = 1024
BATCH = 8
SEQ = 2048
DEPTH = 4
CTX_LEN = 256
GRID_W = 64
ROWS = SEQ // GRID_W
NA_HEADS = 16
NA_WIN_ROWS = 8
NA_WIN_COLS = 16
DIFF_HEADS = 8
DIFF_HEAD_DIM = 64
ROPE_THETA = 10000.0
N_EXPERTS = 16
N_GROUPS = 4
EXPERTS_PER_GROUP = 4
D_EXPERT = 1024
N_MOD = 6
NORM_EPS = 1e-6

T_LAT = BATCH * SEQ
T_CTX = BATCH * CTX_LEN
T_ALL = T_LAT + T_CTX
TM = 512
LAT_TILES = T_LAT // TM
CTX_TILES = T_CTX // TM
TILES_PER_BATCH = SEQ // TM
CTX_ROW = BATCH
MOD_ROWS = 16
HB = 128
N_HB = D_MODEL // HB
NA_RB = 4
NA_BAND_ROWS = 12
NA_Q = NA_RB * GRID_W
NA_BAND = NA_BAND_ROWS * GRID_W
TQ_DIFF = 512
MASK_VALUE = -1e30
VMEM_LIMIT = 56 * 1024 * 1024


def _cparams(*sem):
    return pltpu.CompilerParams(dimension_semantics=sem, vmem_limit_bytes=VMEM_LIMIT)


def _dot(a, b):
    return jnp.dot(a, b, preferred_element_type=F32)


def _dot_nt(a, b):
    return lax.dot_general(a, b, (((1,), (1,)), ((), ())), preferred_element_type=F32)


def _sigmoid(x):
    return 1.0 / (1.0 + jnp.exp(-x))


def _mod_row(i):
    return jnp.minimum(i // TILES_PER_BATCH, CTX_ROW)


def _mod_kernel(act_ref, w_ref, b_ref, o_ref):
    a = act_ref[...]
    a = a * _sigmoid(a)
    o_ref[0] = _dot(a.astype(BF16), w_ref[0].astype(BF16)) + b_ref[0]


def _modulation(c, c_ctx, mod_w, mod_b):
    tn = 1536
    act = jnp.zeros((MOD_ROWS, D_MODEL), F32).at[:BATCH].set(c).at[CTX_ROW].set(c_ctx)
    out = pl.pallas_call(
        _mod_kernel,
        grid=(DEPTH, N_MOD * D_MODEL // tn),
        in_specs=[
            pl.BlockSpec((MOD_ROWS, D_MODEL), lambda l, j: (0, 0)),
            pl.BlockSpec((1, D_MODEL, tn), lambda l, j: (l, 0, j)),
            pl.BlockSpec((1, 1, tn), lambda l, j: (l, 0, j)),
        ],
        out_specs=pl.BlockSpec((1, MOD_ROWS, tn), lambda l, j: (l, 0, j)),
        out_shape=jax.ShapeDtypeStruct((DEPTH, MOD_ROWS, N_MOD * D_MODEL), F32),
        compiler_params=_cparams("parallel", "parallel"),
        name="adaln_mod",
    )(act, mod_w, mod_b.reshape(DEPTH, 1, N_MOD * D_MODEL))
    return out.reshape(DEPTH, MOD_ROWS, N_MOD, D_MODEL)


def _rms_modulate(x, g, shift, scale):
    ms = jnp.mean(x * x, axis=-1, keepdims=True)
    y = x * lax.rsqrt(ms + NORM_EPS) * g
    return y * (1.0 + scale) + shift


def _rope_block(xb, cs, sn):
    lane = lax.broadcasted_iota(jnp.int32, xb.shape, 1)
    partner = jnp.where((lane & 63) < 32, pltpu.roll(xb, 96, 1), pltpu.roll(xb, 32, 1))
    return xb * cs + partner * sn


def _qkv_kernel(x_ref, mod_ref, g_ref, w_ref, cs_ref, sn_ref, q_ref, k_ref, v_ref, *, rope):
    m = mod_ref[0]
    h = _rms_modulate(x_ref[...], g_ref[...], m[0:1], m[1:2]).astype(BF16)
    for idx, out in enumerate((q_ref, k_ref, v_ref)):
        acc = _dot(h, w_ref[:, idx * D_MODEL:(idx + 1) * D_MODEL])
        if rope and idx < 2:
            cs = cs_ref[...]
            sn = sn_ref[...]
            acc = jnp.concatenate(
                [_rope_block(acc[:, j * HB:(j + 1) * HB], cs, sn) for j in range(N_HB)], axis=1)
        if idx == 0:
            acc = acc * 0.125
        out[...] = acc.astype(BF16)


def _qkv(x, mod_l, g, w, cs, sn, *, rope):
    n_tiles = T_ALL // TM
    tok = lambda i: (i, 0)
    rope_idx = lambda i: (jnp.where(i < LAT_TILES, i % TILES_PER_BATCH, TILES_PER_BATCH), 0)
    out = jax.ShapeDtypeStruct((T_ALL, D_MODEL), BF16)
    return pl.pallas_call(
        functools.partial(_qkv_kernel, rope=rope),
        grid=(n_tiles,),
        in_specs=[
            pl.BlockSpec((TM, D_MODEL), tok),
            pl.BlockSpec((1, N_MOD, D_MODEL), lambda i: (_mod_row(i), 0, 0)),
            pl.BlockSpec((1, D_MODEL), lambda i: (0, 0)),
            pl.BlockSpec((D_MODEL, 3 * D_MODEL), lambda i: (0, 0)),
            pl.BlockSpec((TM, HB), rope_idx),
            pl.BlockSpec((TM, HB), rope_idx),
        ],
        out_specs=[pl.BlockSpec((TM, D_MODEL), tok)] * 3,
        out_shape=[out, out, out],
        compiler_params=_cparams("parallel"),
        name="norm_qkv_rope" if rope else "norm_qkv",
    )(x, mod_l, g.reshape(1, D_MODEL), w, cs, sn)


def _rope_tables():
    t = jnp.arange(SEQ)
    row = (t // GRID_W).astype(F32)
    col = (t % GRID_W).astype(F32)
    n_freq = DIFF_HEAD_DIM // 4
    inv_freq = ROPE_THETA ** (-jnp.arange(n_freq, dtype=F32) / n_freq)
    ang = jnp.concatenate([row[:, None] * inv_freq, col[:, None] * inv_freq], axis=-1)
    cos, sin = jnp.cos(ang), jnp.sin(ang)
    cs = jnp.concatenate([cos, cos, cos, cos], axis=-1)
    sn = jnp.concatenate([-sin, sin, -sin, sin], axis=-1)
    cs = jnp.concatenate([cs, jnp.ones((TM, HB), F32)], axis=0)
    sn = jnp.concatenate([sn, jnp.zeros((TM, HB), F32)], axis=0)
    return cs, sn


def _softmax_pv(scores, values):
    m = functools.reduce(jnp.maximum, [jnp.max(s, axis=-1, keepdims=True) for s in scores])
    num = None
    den = None
    for s, v in zip(scores, values):
        e = jnp.exp(s - m)
        d = jnp.sum(e, axis=-1, keepdims=True)
        o = _dot(e.astype(BF16), v)
        num = o if num is None else num + o
        den = d if den is None else den + d
    return num / den


def _half_masks(shape):
    lane = lax.broadcasted_iota(jnp.int32, shape, 1)
    return lane < 64, lane >= 64


def _na_bias_table(rpb):
    a = jnp.arange(NA_RB)
    j = jnp.arange(NA_BAND_ROWS)
    r_rel = jnp.stack([a, NA_RB + a, 2 * NA_RB + a])
    r0_rel = jnp.stack([0 * a, a, NA_RB + 0 * a])
    dr = jnp.clip(j[None, None, :] - r_rel[:, :, None] + NA_WIN_ROWS - 1, 0, 2 * NA_WIN_ROWS - 2)
    rvalid = (j[None, None, :] >= r0_rel[:, :, None]) & (j[None, None, :] < r0_rel[:, :, None] + NA_WIN_ROWS)
    col = jnp.arange(GRID_W)
    cstart = jnp.clip(col - NA_WIN_COLS // 2, 0, GRID_W - NA_WIN_COLS)
    cvalid = (col[None, :] >= cstart[:, None]) & (col[None, :] < cstart[:, None] + NA_WIN_COLS)
    dc = jnp.clip(col[None, :] - col[:, None] + NA_WIN_COLS - 1, 0, 2 * NA_WIN_COLS - 2)
    bias = rpb[:, dr[:, :, None, :, None], dc[None, None, :, None, :]]
    valid = rvalid[:, :, None, :, None] & cvalid[None, None, :, None, :]
    bias = jnp.where(valid[None], bias.astype(F32), MASK_VALUE)
    return bias.reshape(NA_HEADS, 3, NA_Q, NA_BAND)


def _na_kernel(bias_ref, q_ref, k_ref, v_ref, kc_ref, vc_ref, o_ref):
    kc = kc_ref[...]
    vc = vc_ref[...]
    n_blocks = ROWS // NA_RB

    def block(rb, carry):
        u0 = jnp.clip(NA_RB * rb - NA_RB, 0, ROWS - NA_BAND_ROWS)
        pat = jnp.where(rb == 0, 0, jnp.where(rb == n_blocks - 1, 2, 1))
        q = q_ref[pl.ds(pl.multiple_of(rb * NA_Q, NA_Q), NA_Q), :]
        start = pl.multiple_of(u0 * GRID_W, GRID_W)
        kb = k_ref[pl.ds(start, NA_BAND), :]
        vb = v_ref[pl.ds(start, NA_BAND), :]
        lo, hi = _half_masks(q.shape)
        outs = []
        for hh, msk in enumerate((lo, hi)):
            qm = jnp.where(msk, q, jnp.zeros_like(q))
            s_nb = _dot_nt(qm, kb) + bias_ref[hh, pat]
            s_ctx = _dot_nt(qm, kc)
            outs.append(_softmax_pv([s_nb, s_ctx], [vb, vc]))
        o = jnp.where(lo, outs[0], outs[1])
        o_ref[pl.ds(pl.multiple_of(rb * NA_Q, NA_Q), NA_Q), :] = o.astype(BF16)
        return carry

    lax.fori_loop(0, n_blocks, block, 0)


def _na_attention(bias, q, k, v):
    lat = lambda hp, b: (b, hp)
    ctx = lambda hp, b: (T_LAT // CTX_LEN + b, hp)
    return pl.pallas_call(
        _na_kernel,
        grid=(N_HB, BATCH),
        in_specs=[
            pl.BlockSpec((2, 3, NA_Q, NA_BAND), lambda hp, b: (hp, 0, 0, 0)),
            pl.BlockSpec((SEQ, HB), lat),
            pl.BlockSpec((SEQ, HB), lat),
            pl.BlockSpec((SEQ, HB), lat),
            pl.BlockSpec((CTX_LEN, HB), ctx),
            pl.BlockSpec((CTX_LEN, HB), ctx),
        ],
        out_specs=pl.BlockSpec((SEQ, HB), lat),
        out_shape=jax.ShapeDtypeStruct((T_ALL, D_MODEL), BF16),
        compiler_params=_cparams("parallel", "parallel"),
        name="na_attention",
    )(bias, q, k, v, k, v)


def _na_ctx_kernel(o_in_ref, q_ref, kc_ref, vc_ref, o_ref):
    del o_in_ref
    q = q_ref[...]
    kc = kc_ref[...]
    vc = vc_ref[...]
    lo, hi = _half_masks(q.shape)
    outs = []
    for msk in (lo, hi):
        qm = jnp.where(msk, q, jnp.zeros_like(q))
        outs.append(_softmax_pv([_dot_nt(qm, kc)], [vc]))
    o_ref[...] = jnp.where(lo, outs[0], outs[1]).astype(BF16)


def _ctx_attention(kernel, o, q, k, v, extra=(), name="ctx_attention"):
    ctx = lambda b, hb: (T_LAT // CTX_LEN + b, hb)
    n_extra = len(extra)
    return pl.pallas_call(
        kernel,
        grid=(BATCH, N_HB),
        in_specs=[pl.BlockSpec(e.shape, lambda b, hb, nd=e.ndim: (0,) * nd) for e in extra] + [
            pl.BlockSpec(memory_space=pl.ANY),
            pl.BlockSpec((CTX_LEN, HB), ctx),
            pl.BlockSpec((CTX_LEN, HB), ctx),
            pl.BlockSpec((CTX_LEN, HB), ctx),
        ],
        out_specs=pl.BlockSpec((CTX_LEN, HB), ctx),
        out_shape=jax.ShapeDtypeStruct((T_ALL, D_MODEL), BF16),
        input_output_aliases={n_extra: 0},
        compiler_params=_cparams("parallel", "parallel"),
        name=name,
    )(*extra, o, q, k, v)


def _diff_lambda(lam_ref, lam_init):
    lam = lam_ref[...]
    a = jnp.sum(lam[0:1] * lam[1:2], axis=-1, keepdims=True)
    b = jnp.sum(lam[2:3] * lam[3:4], axis=-1, keepdims=True)
    return jnp.exp(a) - jnp.exp(b) + lam_init


def _diff_finish(o1, o2, lam, sg, lam_init):
    o = o1 - lam * o2
    ms = jnp.mean(o * o, axis=-1, keepdims=True)
    return (o * lax.rsqrt(ms + NORM_EPS) * sg * (1.0 - lam_init)).astype(BF16)


def _diff_kernel(lam_ref, sg_ref, q_ref, k_ref, v_ref, kc_ref, vc_ref, o_ref, *, lam_init):
    q = q_ref[...]
    k = k_ref[...]
    v = v_ref[...]
    kc = kc_ref[...]
    vc = vc_ref[...]
    lo, hi = _half_masks(q.shape)
    outs = []
    for msk in (lo, hi):
        qm = jnp.where(msk, q, jnp.zeros_like(q))
        outs.append(_softmax_pv([_dot_nt(qm, k), _dot_nt(qm, kc)], [v, vc]))
    o_ref[...] = _diff_finish(outs[0], outs[1], _diff_lambda(lam_ref, lam_init), sg_ref[...], lam_init)


def _diff_attention(lam, sg, q, k, v, lam_init):
    nq = SEQ // TQ_DIFF
    qmap = lambda b, h, qi: (b * nq + qi, h)
    lat = lambda b, h, qi: (b, h)
    ctx = lambda b, h, qi: (T_LAT // CTX_LEN + b, h)
    return pl.pallas_call(
        functools.partial(_diff_kernel, lam_init=lam_init),
        grid=(BATCH, DIFF_HEADS, nq),
        in_specs=[
            pl.BlockSpec(lam.shape, lambda b, h, qi: (0, 0)),
            pl.BlockSpec(sg.shape, lambda b, h, qi: (0, 0)),
            pl.BlockSpec((TQ_DIFF, HB), qmap),
            pl.BlockSpec((SEQ, HB), lat),
            pl.BlockSpec((SEQ, HB), lat),
            pl.BlockSpec((CTX_LEN, HB), ctx),
            pl.BlockSpec((CTX_LEN, HB), ctx),
        ],
        out_specs=pl.BlockSpec((TQ_DIFF, HB), qmap),
        out_shape=jax.ShapeDtypeStruct((T_ALL, D_MODEL), BF16),
        compiler_params=_cparams("parallel", "parallel", "parallel"),
        name="diff_attention",
    )(lam, sg, q, k, v, k, v)


def _diff_ctx_kernel(lam_ref, sg_ref, o_in_ref, q_ref, kc_ref, vc_ref, o_ref, *, lam_init):
    del o_in_ref
    q = q_ref[...]
    kc = kc_ref[...]
    vc = vc_ref[...]
    lo, hi = _half_masks(q.shape)
    outs = []
    for msk in (lo, hi):
        qm = jnp.where(msk, q, jnp.zeros_like(q))
        outs.append(_softmax_pv([_dot_nt(qm, kc)], [vc]))
    o_ref[...] = _diff_finish(outs[0], outs[1], _diff_lambda(lam_ref, lam_init), sg_ref[...], lam_init)


def _route_rows(logits_t, bias_col):
    s = _sigmoid(logits_t)
    sel = s + bias_col
    sel_r = [sel[e:e + 1] for e in range(N_EXPERTS)]
    s_r = [s[e:e + 1] for e in range(N_EXPERTS)]
    group_scores = []
    for g in range(N_GROUPS):
        v = sel_r[g * EXPERTS_PER_GROUP:(g + 1) * EXPERTS_PER_GROUP]
        pairs = [v[a] + v[b] for a in range(EXPERTS_PER_GROUP) for b in range(a + 1, EXPERTS_PER_GROUP)]
        group_scores.append(functools.reduce(jnp.maximum, pairs))
    best = group_scores[0]
    grp = jnp.zeros(best.shape, jnp.int32)
    for g in range(1, N_GROUPS):
        upd = group_scores[g] > best
        best = jnp.where(upd, group_scores[g], best)
        grp = jnp.where(upd, g, grp)

    def pick(rows, j):
        out = rows[j]
        for g in range(1, N_GROUPS):
            out = jnp.where(grp == g, rows[g * EXPERTS_PER_GROUP + j], out)
        return out

    w = [pick(sel_r, j) for j in range(EXPERTS_PER_GROUP)]
    sc = [pick(s_r, j) for j in range(EXPERTS_PER_GROUP)]

    def argmax_first(vals):
        bv = vals[0]
        bi = jnp.zeros(bv.shape, jnp.int32)
        for j in range(1, len(vals)):
            upd = vals[j] > bv
            bv = jnp.where(upd, vals[j], bv)
            bi = jnp.where(upd, j, bi)
        return bi

    i1 = argmax_first(w)
    i2 = argmax_first([jnp.where(i1 == j, -jnp.inf, w[j]) for j in range(EXPERTS_PER_GROUP)])

    def take(vals, idx):
        out = vals[0]
        for j in range(1, len(vals)):
            out = jnp.where(idx == j, vals[j], out)
        return out

    s1 = take(sc, i1)
    s2 = take(sc, i2)
    den = s1 + s2
    return grp * EXPERTS_PER_GROUP + i1, grp * EXPERTS_PER_GROUP + i2, s1 / den, s2 / den


def _oproj_kernel(o_ref, x_ref, mod_ref, w_ref, g_ref, rw_ref, rb_ref, tri_ref,
                  x1_ref, h_ref, ri_ref, rf_ref, cnt_ref, carry_ref):
    i = pl.program_id(0)

    @pl.when(i == 0)
    def _():
        carry_ref[...] = jnp.zeros_like(carry_ref)

    m = mod_ref[0]
    x1 = x_ref[...] + m[2:3] * _dot(o_ref[...], w_ref[...])
    x1_ref[...] = x1
    h = _rms_modulate(x1, g_ref[...], m[3:4], m[4:5])
    hb = h.astype(BF16)
    h_ref[...] = hb

    h_lo = (h - hb.astype(F32)).astype(BF16)
    rw = rw_ref[...]
    rw_hi = rw.astype(BF16)
    rw_lo = (rw - rw_hi.astype(F32)).astype(BF16)
    logits = _dot(hb, rw_hi) + _dot(h_lo, rw_hi) + _dot(hb, rw_lo)
    logits_t = logits.T[:N_EXPERTS]
    e0, e1, g0, g1 = _route_rows(logits_t, rb_ref[...])

    eiota = lax.broadcasted_iota(jnp.int32, (N_EXPERTS, TM), 0)
    oh0 = eiota == e0
    oh1 = eiota == e1
    oh = jnp.where(oh0 | oh1, 1.0, 0.0)
    before = _dot(oh.astype(BF16), tri_ref[...]) + carry_ref[...]
    rank0 = jnp.sum(jnp.where(oh0, before, 0.0), axis=0, keepdims=True)
    rank1 = jnp.sum(jnp.where(oh1, before, 0.0), axis=0, keepdims=True)
    carry_ref[...] = carry_ref[...] + jnp.sum(oh, axis=1, keepdims=True)

    ri_ref[...] = jnp.zeros_like(ri_ref)
    ri_ref[0:1, :] = e0
    ri_ref[1:2, :] = e1
    ri_ref[2:3, :] = rank0.astype(jnp.int32)
    ri_ref[3:4, :] = rank1.astype(jnp.int32)
    rf_ref[...] = jnp.zeros_like(rf_ref)
    rf_ref[0:1, :] = g0
    rf_ref[1:2, :] = g1
    cnt_ref[...] = jnp.broadcast_to(carry_ref[...], cnt_ref.shape)


def _oproj_route(o, x, mod_l, w_o, g, rw_pad, rb_col, tri, n_tok):
    n_tiles = n_tok // TM
    tok = lambda i: (i, 0)
    const2 = lambda i: (0, 0)
    lane_tok = lambda i: (0, i)
    return pl.pallas_call(
        _oproj_kernel,
        grid=(n_tiles,),
        in_specs=[
            pl.BlockSpec((TM, D_MODEL), tok),
            pl.BlockSpec((TM, D_MODEL), tok),
            pl.BlockSpec((1, N_MOD, D_MODEL), lambda i: (_mod_row(i), 0, 0)),
            pl.BlockSpec((D_MODEL, D_MODEL), const2),
            pl.BlockSpec((1, D_MODEL), const2),
            pl.BlockSpec((D_MODEL, HB), const2),
            pl.BlockSpec((N_EXPERTS, 1), const2),
            pl.BlockSpec((TM, TM), const2),
        ],
        out_specs=[
            pl.BlockSpec((TM, D_MODEL), tok),
            pl.BlockSpec((TM, D_MODEL), tok),
            pl.BlockSpec((8, TM), lane_tok),
            pl.BlockSpec((8, TM), lane_tok),
            pl.BlockSpec((N_EXPERTS, HB), const2),
        ],
        out_shape=[
            jax.ShapeDtypeStruct((n_tok, D_MODEL), F32),
            jax.ShapeDtypeStruct((n_tok, D_MODEL), BF16),
            jax.ShapeDtypeStruct((8, n_tok), jnp.int32),
            jax.ShapeDtypeStruct((8, n_tok), F32),
            jax.ShapeDtypeStruct((N_EXPERTS, HB), F32),
        ],
        scratch_shapes=[pltpu.VMEM((N_EXPERTS, 1), F32)],
        compiler_params=_cparams("arbitrary"),
        name="oproj_norm_route",
    )(o, x, mod_l, w_o, g.reshape(1, D_MODEL), rw_pad, rb_col, tri)


def _moe_kernel(te_ref, tv_ref, xs_ref, wg_ref, wu_ref, wd_ref, ys_ref, wgb, wub, wdb):
    i = pl.program_id(0)
    valid = tv_ref[i]
    e = te_ref[i]
    prev = te_ref[jnp.maximum(i - 1, 0)]

    @pl.when((i == 0) | (e != prev))
    def _():
        wgb[...] = wg_ref[0].astype(BF16)
        wub[...] = wu_ref[0].astype(BF16)
        wdb[...] = wd_ref[0].astype(BF16)

    @pl.when(valid > 0)
    def _():
        row = lax.broadcasted_iota(jnp.int32, (TM, 1), 0)
        x = xs_ref[...]
        x = jnp.where(row < valid, x, jnp.zeros_like(x))
        a = _dot(x, wgb[...])
        u = _dot(x, wub[...])
        hm = (a * _sigmoid(a) * u).astype(BF16)
        ys_ref[...] = _dot(hm, wdb[...])

    @pl.when(valid == 0)
    def _():
        ys_ref[...] = jnp.zeros_like(ys_ref)


def _grouped_ffn(tile_expert, tile_valid, xs, w_gate, w_up, w_down):
    n_tiles = xs.shape[0] // TM
    wmap = lambda i, te, tv: (te[i], 0, 0)
    tok = lambda i, te, tv: (i, 0)
    return pl.pallas_call(
        _moe_kernel,
        grid_spec=pltpu.PrefetchScalarGridSpec(
            num_scalar_prefetch=2,
            grid=(n_tiles,),
            in_specs=[
                pl.BlockSpec((TM, D_MODEL), tok),
                pl.BlockSpec((1, D_MODEL, D_EXPERT), wmap),
                pl.BlockSpec((1, D_MODEL, D_EXPERT), wmap),
                pl.BlockSpec((1, D_EXPERT, D_MODEL), wmap),
            ],
            out_specs=pl.BlockSpec((TM, D_MODEL), tok),
            scratch_shapes=[
                pltpu.VMEM((D_MODEL, D_EXPERT), BF16),
                pltpu.VMEM((D_MODEL, D_EXPERT), BF16),
                pltpu.VMEM((D_EXPERT, D_MODEL), BF16),
            ],
        ),
        out_shape=jax.ShapeDtypeStruct((xs.shape[0], D_MODEL), F32),
        compiler_params=_cparams("arbitrary"),
        name="grouped_ffn",
    )(tile_expert, tile_valid, xs, w_gate, w_up, w_down)


def _moe_layout(route_i, counts, n_tok):
    n_tiles = 2 * n_tok // TM + N_EXPERTS
    counts = counts.astype(jnp.int32)
    tiles_e = (counts + TM - 1) // TM
    tiles_end = jnp.cumsum(tiles_e)
    tile_start = tiles_end - tiles_e
    pos = tile_start[route_i[0:2]] * TM + route_i[2:4]
    tile_ids = jnp.arange(n_tiles, dtype=jnp.int32)
    n_used = tiles_end[-1]
    te = jnp.searchsorted(tiles_end, jnp.minimum(tile_ids, n_used - 1), side="right").astype(jnp.int32)
    tv = jnp.clip(counts[te] - (tile_ids - tile_start[te]) * TM, 0, TM)
    tv = jnp.where(tile_ids < n_used, tv, 0).astype(jnp.int32)
    return pos, te, tv, n_tiles


def _combine_kernel(x1_ref, yg_ref, gt_ref, mod_ref, fg_ref, x2_ref, *, final):
    g = gt_ref[...]
    y = g[:, 0:1] * yg_ref[0] + g[:, 1:2] * yg_ref[1]
    x2 = x1_ref[...] + mod_ref[0][5:6] * y
    if final:
        ms = jnp.mean(x2 * x2, axis=-1, keepdims=True)
        x2 = x2 * lax.rsqrt(ms + NORM_EPS) * fg_ref[...]
    x2_ref[...] = x2


def _combine(x1, yg, gates_t, mod_l, final_g, n_tok, *, final):
    tok = lambda i: (i, 0)
    return pl.pallas_call(
        functools.partial(_combine_kernel, final=final),
        grid=(n_tok // TM,),
        in_specs=[
            pl.BlockSpec((TM, D_MODEL), tok),
            pl.BlockSpec((2, TM, D_MODEL), lambda i: (0, i, 0)),
            pl.BlockSpec((TM, 2), tok),
            pl.BlockSpec((1, N_MOD, D_MODEL), lambda i: (_mod_row(i), 0, 0)),
            pl.BlockSpec((1, D_MODEL), lambda i: (0, 0)),
        ],
        out_specs=pl.BlockSpec((TM, D_MODEL), tok),
        out_shape=jax.ShapeDtypeStruct((n_tok, D_MODEL), F32),
        compiler_params=_cparams("parallel"),
        name="moe_combine_final" if final else "moe_combine",
    )(x1, yg, gates_t, mod_l, final_g.reshape(1, D_MODEL))


def _diff_lambda_init(layer_idx):
    return 0.8 - 0.6 * math.exp(-0.3 * (layer_idx - 1))


def kernel(x, c, ctx, c_ctx, mod_w, mod_b, norm_mix, norm_ffn, w_qkv, w_o, na_rpb, diff_lambda, diff_subln,
           router_w, router_b, expert_w_gate, expert_w_up, expert_w_down, final_norm):
    mod = _modulation(c, c_ctx, mod_w, mod_b)
    xa = jnp.concatenate([x.reshape(T_LAT, D_MODEL), ctx.reshape(T_CTX, D_MODEL)], axis=0)
    w_qkv_b = w_qkv.astype(BF16)
    w_o_b = w_o.astype(BF16)
    cs, sn = _rope_tables()
    rw_pad = jnp.zeros((D_MODEL, HB), F32).at[:, :N_EXPERTS].set(router_w)
    rb_col = router_b.reshape(N_EXPERTS, 1).astype(F32)
    tri = (jnp.arange(TM)[:, None] < jnp.arange(TM)[None, :]).astype(BF16)

    for i in range(DEPTH):
        last = i == DEPTH - 1
        j = i // 2
        is_diff = i % 2 == 1
        q, k, v = _qkv(xa, mod[i], norm_mix[i], w_qkv_b[i], cs, sn, rope=is_diff)
        if not is_diff:
            o = _na_attention(_na_bias_table(na_rpb[j]), q, k, v)
            if not last:
                o = _ctx_attention(_na_ctx_kernel, o, q, k, v, name="na_ctx_attention")
        else:
            lam_init = _diff_lambda_init(i + 1)
            sg = diff_subln[j].reshape(1, HB)
            o = _diff_attention(diff_lambda[j], sg, q, k, v, lam_init)
            if not last:
                o = _ctx_attention(functools.partial(_diff_ctx_kernel, lam_init=lam_init), o, q, k, v,
                                   extra=(diff_lambda[j], sg), name="diff_ctx_attention")
        n_tok = T_LAT if last else T_ALL
        x1, h, route_i, route_f, cnt = _oproj_route(o, xa, mod[i], w_o_b[i], norm_ffn[i], rw_pad, rb_col,
                                                    tri, n_tok)
        pos, te, tv, n_tiles = _moe_layout(route_i, cnt[:, 0], n_tok)
        slot_tok = jnp.zeros((n_tiles * TM,), jnp.int32).at[pos.reshape(-1)].set(
            jnp.tile(jnp.arange(n_tok, dtype=jnp.int32), 2))
        xs = h[slot_tok]
        ys = _grouped_ffn(te, tv, xs, expert_w_gate[i], expert_w_up[i], expert_w_down[i])
        yg = ys[pos]
        xa = _combine(x1, yg, route_f[0:2].T, mod[i], final_norm, n_tok, final=last)
    return xa.reshape(BATCH, SEQ, D_MODEL)
```

```python
import functools
import math

import jax
import jax.numpy as jnp
from jax import lax
from jax.experimental import pallas as pl
from jax.experimental.pallas import tpu as pltpu
from jax.experimental.pallas import tpu_sc as plsc

F32 = jnp.float32
BF16 = jnp.bfloat16

D_MODEL = 1024
BATCH = 8
SEQ = 2048
DEPTH = 4
CTX_LEN = 256
GRID_W = 64
ROWS = SEQ // GRID_W
NA_HEADS = 16
NA_WIN_ROWS = 8
NA_WIN_COLS = 16
DIFF_HEADS = 8
DIFF_HEAD_DIM = 64
ROPE_THETA = 10000.0
N_EXPERTS = 16
N_GROUPS = 4
EXPERTS_PER_GROUP = 4
D_EXPERT = 1024
N_MOD = 6
NORM_EPS = 1e-6

T_LAT = BATCH * SEQ
T_CTX = BATCH * CTX_LEN
T_ALL = T_LAT + T_CTX
TM = 512
LAT_TILES = T_LAT // TM
CTX_TILES = T_CTX // TM
TILES_PER_BATCH = SEQ // TM
CTX_ROW = BATCH
MOD_ROWS = 16
HB = 128
N_HB = D_MODEL // HB
NA_RB = 4
NA_BAND_ROWS = 12
NA_Q = NA_RB * GRID_W
NA_BAND = NA_BAND_ROWS * GRID_W
TQ_DIFF = 512
MASK_VALUE = -1e30
VMEM_LIMIT = 56 * 1024 * 1024


def _cparams(*sem):
    return pltpu.CompilerParams(dimension_semantics=sem, vmem_limit_bytes=VMEM_LIMIT)


def _dot(a, b):
    return jnp.dot(a, b, preferred_element_type=F32)


def _dot_nt(a, b):
    return lax.dot_general(a, b, (((1,), (1,)), ((), ())), preferred_element_type=F32)


def _sigmoid(x):
    return 1.0 / (1.0 + jnp.exp(-x))


def _mod_row(i):
    return jnp.minimum(i // TILES_PER_BATCH, CTX_ROW)


def _mod_kernel(act_ref, w_ref, b_ref, o_ref):
    a = act_ref[...]
    a = a * _sigmoid(a)
    o_ref[0] = _dot(a.astype(BF16), w_ref[0].astype(BF16)) + b_ref[0]


def _modulation(c, c_ctx, mod_w, mod_b):
    tn = 1536
    act = jnp.zeros((MOD_ROWS, D_MODEL), F32).at[:BATCH].set(c).at[CTX_ROW].set(c_ctx)
    out = pl.pallas_call(
        _mod_kernel,
        grid=(DEPTH, N_MOD * D_MODEL // tn),
        in_specs=[
            pl.BlockSpec((MOD_ROWS, D_MODEL), lambda l, j: (0, 0)),
            pl.BlockSpec((1, D_MODEL, tn), lambda l, j: (l, 0, j)),
            pl.BlockSpec((1, 1, tn), lambda l, j: (l, 0, j)),
        ],
        out_specs=pl.BlockSpec((1, MOD_ROWS, tn), lambda l, j: (l, 0, j)),
        out_shape=jax.ShapeDtypeStruct((DEPTH, MOD_ROWS, N_MOD * D_MODEL), F32),
        compiler_params=_cparams("parallel", "parallel"),
        name="adaln_mod",
    )(act, mod_w, mod_b.reshape(DEPTH, 1, N_MOD * D_MODEL))
    return out.reshape(DEPTH, MOD_ROWS, N_MOD, D_MODEL)


def _rms_modulate(x, g, shift, scale):
    ms = jnp.mean(x * x, axis=-1, keepdims=True)
    y = x * lax.rsqrt(ms + NORM_EPS) * g
    return y * (1.0 + scale) + shift


def _rope_block(xb, cs, sn):
    lane = lax.broadcasted_iota(jnp.int32, xb.shape, 1)
    partner = jnp.where((lane & 63) < 32, pltpu.roll(xb, 96, 1), pltpu.roll(xb, 32, 1))
    return xb * cs + partner * sn


def _qkv_kernel(x_ref, mod_ref, g_ref, w_ref, cs_ref, sn_ref, q_ref, k_ref, v_ref, *, rope):
    m = mod_ref[0]
    h = _rms_modulate(x_ref[...], g_ref[...], m[0:1], m[1:2]).astype(BF16)
    for idx, out in enumerate((q_ref, k_ref, v_ref)):
        acc = _dot(h, w_ref[:, idx * D_MODEL:(idx + 1) * D_MODEL])
        if rope and idx < 2:
            cs = cs_ref[...]
            sn = sn_ref[...]
            acc = jnp.concatenate(
                [_rope_block(acc[:, j * HB:(j + 1) * HB], cs, sn) for j in range(N_HB)], axis=1)
        if idx == 0:
            acc = acc * 0.125
        out[...] = acc.astype(BF16)


def _qkv(x, mod_l, g, w, cs, sn, *, rope):
    n_tiles = T_ALL // TM
    tok = lambda i: (i, 0)
    rope_idx = lambda i: (jnp.where(i < LAT_TILES, i % TILES_PER_BATCH, TILES_PER_BATCH), 0)
    out = jax.ShapeDtypeStruct((T_ALL, D_MODEL), BF16)
    return pl.pallas_call(
        functools.partial(_qkv_kernel, rope=rope),
        grid=(n_tiles,),
        in_specs=[
            pl.BlockSpec((TM, D_MODEL), tok),
            pl.BlockSpec((1, N_MOD, D_MODEL), lambda i: (_mod_row(i), 0, 0)),
            pl.BlockSpec((1, D_MODEL), lambda i: (0, 0)),
            pl.BlockSpec((D_MODEL, 3 * D_MODEL), lambda i: (0, 0)),
            pl.BlockSpec((TM, HB), rope_idx),
            pl.BlockSpec((TM, HB), rope_idx),
        ],
        out_specs=[pl.BlockSpec((TM, D_MODEL), tok)] * 3,
        out_shape=[out, out, out],
        compiler_params=_cparams("parallel"),
        name="norm_qkv_rope" if rope else "norm_qkv",
    )(x, mod_l, g.reshape(1, D_MODEL), w, cs, sn)


def _rope_tables():
    t = jnp.arange(SEQ)
    row = (t // GRID_W).astype(F32)
    col = (t % GRID_W).astype(F32)
    n_freq = DIFF_HEAD_DIM // 4
    inv_freq = ROPE_THETA ** (-jnp.arange(n_freq, dtype=F32) / n_freq)
    ang = jnp.concatenate([row[:, None] * inv_freq, col[:, None] * inv_freq], axis=-1)
    cos, sin = jnp.cos(ang), jnp.sin(ang)
    cs = jnp.concatenate([cos, cos, cos, cos], axis=-1)
    sn = jnp.concatenate([-sin, sin, -sin, sin], axis=-1)
    cs = jnp.concatenate([cs, jnp.ones((TM, HB), F32)], axis=0)
    sn = jnp.concatenate([sn, jnp.zeros((TM, HB), F32)], axis=0)
    return cs, sn


def _softmax_pv(scores, values):
    m = functools.reduce(jnp.maximum, [jnp.max(s, axis=-1, keepdims=True) for s in scores])
    num = None
    den = None
    for s, v in zip(scores, values):
        e = jnp.exp(s - m)
        d = jnp.sum(e, axis=-1, keepdims=True)
        o = _dot(e.astype(BF16), v)
        num = o if num is None else num + o
        den = d if den is None else den + d
    return num / den


def _half_masks(shape):
    lane = lax.broadcasted_iota(jnp.int32, shape, 1)
    return lane < 64, lane >= 64


N_DR = 2 * NA_WIN_ROWS - 1
N_DC = 2 * NA_WIN_COLS - 1
PAIR_MASK_FIRST = N_DR - 1
PAIR_MASK_SECOND = N_DR
PAIR_MASKED = N_DR + 1
N_PAIRS = N_DR + 2
DR_LO = NA_WIN_ROWS - 1 - NA_WIN_ROWS // 2
DR_HI = DR_LO + NA_WIN_ROWS - 1


def _na_pair_table(rpb):
    h = rpb.shape[0]
    w = GRID_W
    v = jnp.zeros((h, N_DR, 2 * w), F32)
    v = v.at[..., 0:NA_WIN_COLS].set(rpb[..., NA_WIN_COLS - 1:].astype(F32))
    v = v.at[..., 2 * w - (NA_WIN_COLS - 1):].set(rpb[..., :NA_WIN_COLS - 1].astype(F32))
    y = jnp.broadcast_to(v[:, :, None, :], (h, N_DR, w, 2 * w)).reshape(h, N_DR, w * 2 * w)
    t = y[..., :w * (2 * w - 1)].reshape(h, N_DR, w, 2 * w - 1)[..., :w]
    col = jnp.arange(w)
    cstart = jnp.clip(col - NA_WIN_COLS // 2, 0, w - NA_WIN_COLS)
    cvalid = (col[None, :] >= cstart[:, None]) & (col[None, :] < cstart[:, None] + NA_WIN_COLS)
    t = jnp.where(cvalid, t, MASK_VALUE)
    masked = jnp.full((h, 1, w, w), MASK_VALUE, F32)
    regular = jnp.concatenate([t[:, :-1], t[:, 1:]], axis=-1)
    first = jnp.concatenate([masked, t[:, DR_LO:DR_LO + 1]], axis=-1)
    second = jnp.concatenate([t[:, DR_HI:DR_HI + 1], masked], axis=-1)
    return jnp.concatenate([regular, first, second, jnp.concatenate([masked, masked], axis=-1)], axis=1)


def _na_pair_entries(q_row0, band_row0, band_rows):
    entries = []
    for a in range(NA_RB):
        r = q_row0 + a
        r0 = min(max(r - NA_WIN_ROWS // 2, 0), ROWS - NA_WIN_ROWS)
        row = []
        for m in range(band_rows // 2):
            kr = (band_row0 + 2 * m, band_row0 + 2 * m + 1)
            ok = [r0 <= x < r0 + NA_WIN_ROWS for x in kr]
            d = [x - r + NA_WIN_ROWS - 1 for x in kr]
            if ok[0] and ok[1]:
                row.append(d[0])
            elif ok[1]:
                assert d[1] == DR_LO
                row.append(PAIR_MASK_FIRST)
            elif ok[0]:
                assert d[0] == DR_HI
                row.append(PAIR_MASK_SECOND)
            else:
                row.append(PAIR_MASKED)
        entries.append(row)
    return entries


def _two_map_attention(q, key_pieces, value_pieces):
    lo, hi = _half_masks(q.shape)
    outs = []
    for hh, msk in enumerate((lo, hi)):
        qm = jnp.where(msk, q, jnp.zeros_like(q))
        scores = [kp(qm, hh) if callable(kp) else _dot_nt(qm, kp) for kp in key_pieces]
        outs.append(_softmax_pv(scores, value_pieces))
    return outs, lo


def _na_kernel(tab_ref, q_ref, k_ref, v_ref, qc_ref, kc_ref, vc_ref, o_ref, oc_ref):
    kc = kc_ref[...]
    vc = vc_ref[...]
    n_blocks = ROWS // NA_RB

    def block(q_start, band_start, band_rows, entries):
        n_band = band_rows * GRID_W
        q = q_ref[pl.ds(q_start, NA_Q), :]
        kb = k_ref[pl.ds(band_start, n_band), :]
        vb = v_ref[pl.ds(band_start, n_band), :]

        def band_scores(qm, hh):
            bias = jnp.concatenate(
                [jnp.concatenate([tab_ref[hh, e] for e in row], axis=1) for row in entries], axis=0)
            return _dot_nt(qm, kb) + bias

        outs, lo = _two_map_attention(q, [band_scores, kc], [vb, vc])
        o_ref[pl.ds(q_start, NA_Q), :] = jnp.where(lo, outs[0], outs[1]).astype(BF16)

    block(0, 0, NA_WIN_ROWS, _na_pair_entries(0, 0, NA_WIN_ROWS))
    last_band = ROWS - NA_WIN_ROWS
    block((n_blocks - 1) * NA_Q, last_band * GRID_W, NA_WIN_ROWS,
          _na_pair_entries(ROWS - NA_RB, last_band, NA_WIN_ROWS))
    interior = _na_pair_entries(NA_RB, 0, NA_BAND_ROWS)

    def body(rb, carry):
        u0 = jnp.minimum(NA_RB * rb - NA_RB, ROWS - NA_BAND_ROWS)
        block(pl.multiple_of(rb * NA_Q, NA_Q), pl.multiple_of(u0 * GRID_W, GRID_W), NA_BAND_ROWS, interior)
        return carry

    lax.fori_loop(1, n_blocks - 1, body, 0)

    outs, lo = _two_map_attention(qc_ref[...], [kc], [vc])
    oc_ref[...] = jnp.where(lo, outs[0], outs[1]).astype(BF16)


def _na_attention(table, q, k, v):
    lat = lambda hp, b: (b, hp)
    ctx = lambda hp, b: (T_LAT // CTX_LEN + b, hp)
    return pl.pallas_call(
        _na_kernel,
        grid=(N_HB, BATCH),
        in_specs=[
            pl.BlockSpec((2, N_PAIRS, GRID_W, 2 * GRID_W), lambda hp, b: (hp, 0, 0, 0)),
            pl.BlockSpec((SEQ, HB), lat),
            pl.BlockSpec((SEQ, HB), lat),
            pl.BlockSpec((SEQ, HB), lat),
            pl.BlockSpec((CTX_LEN, HB), ctx),
            pl.BlockSpec((CTX_LEN, HB), ctx),
            pl.BlockSpec((CTX_LEN, HB), ctx),
        ],
        out_specs=[pl.BlockSpec((SEQ, HB), lat), pl.BlockSpec((CTX_LEN, HB), lat)],
        out_shape=[jax.ShapeDtypeStruct((T_LAT, D_MODEL), BF16), jax.ShapeDtypeStruct((T_CTX, D_MODEL), BF16)],
        compiler_params=_cparams("parallel", "parallel"),
        name="na_attention",
    )(table, q, k, v, q, k, v)


def _diff_lambda(lam_ref, lam_init):
    lam = lam_ref[...]
    a = jnp.sum(lam[0:1] * lam[1:2], axis=-1, keepdims=True)
    b = jnp.sum(lam[2:3] * lam[3:4], axis=-1, keepdims=True)
    return jnp.exp(a) - jnp.exp(b) + lam_init


def _diff_finish(outs, lam, sg, lam_init):
    o = outs[0] - lam * outs[1]
    ms = jnp.mean(o * o, axis=-1, keepdims=True)
    return (o * lax.rsqrt(ms + NORM_EPS) * sg * (1.0 - lam_init)).astype(BF16)


def _diff_kernel(lam_ref, sg_ref, q_ref, k_ref, v_ref, qc_ref, kc_ref, vc_ref, o_ref, *maybe_oc_ref, lam_init):
    kc = kc_ref[...]
    vc = vc_ref[...]
    lam = _diff_lambda(lam_ref, lam_init)
    sg = sg_ref[...]
    outs, _ = _two_map_attention(q_ref[...], [k_ref[...], kc], [v_ref[...], vc])
    o_ref[...] = _diff_finish(outs, lam, sg, lam_init)
    if maybe_oc_ref:
        oc_ref, = maybe_oc_ref

        @pl.when(pl.program_id(2) == 0)
        def _():
            outs_c, _ = _two_map_attention(qc_ref[...], [kc], [vc])
            oc_ref[...] = _diff_finish(outs_c, lam, sg, lam_init)


def _diff_attention(lam, sg, q, k, v, lam_init, *, ctx_out):
    nq = SEQ // TQ_DIFF
    qmap = lambda b, h, qi: (b * nq + qi, h)
    lat = lambda b, h, qi: (b, h)
    ctx = lambda b, h, qi: (T_LAT // CTX_LEN + b, h)
    out_specs = [pl.BlockSpec((TQ_DIFF, HB), qmap)]
    out_shape = [jax.ShapeDtypeStruct((T_LAT, D_MODEL), BF16)]
    if ctx_out:
        out_specs.append(pl.BlockSpec((CTX_LEN, HB), lat))
        out_shape.append(jax.ShapeDtypeStruct((T_CTX, D_MODEL), BF16))
    return pl.pallas_call(
        functools.partial(_diff_kernel, lam_init=lam_init),
        grid=(BATCH, DIFF_HEADS, nq),
        in_specs=[
            pl.BlockSpec(lam.shape, lambda b, h, qi: (0, 0)),
            pl.BlockSpec(sg.shape, lambda b, h, qi: (0, 0)),
            pl.BlockSpec((TQ_DIFF, HB), qmap),
            pl.BlockSpec((SEQ, HB), lat),
            pl.BlockSpec((SEQ, HB), lat),
            pl.BlockSpec((CTX_LEN, HB), ctx),
            pl.BlockSpec((CTX_LEN, HB), ctx),
            pl.BlockSpec((CTX_LEN, HB), ctx),
        ],
        out_specs=out_specs,
        out_shape=out_shape,
        compiler_params=_cparams("parallel", "parallel", "arbitrary"),
        name="diff_attention",
    )(lam, sg, q, k, v, q, k, v)


def _route_rows(logits_t, bias_col):
    s = _sigmoid(logits_t)
    sel = s + bias_col
    sel_r = [sel[e:e + 1] for e in range(N_EXPERTS)]
    s_r = [s[e:e + 1] for e in range(N_EXPERTS)]
    group_scores = []
    for g in range(N_GROUPS):
        v = sel_r[g * EXPERTS_PER_GROUP:(g + 1) * EXPERTS_PER_GROUP]
        pairs = [v[a] + v[b] for a in range(EXPERTS_PER_GROUP) for b in range(a + 1, EXPERTS_PER_GROUP)]
        group_scores.append(functools.reduce(jnp.maximum, pairs))
    best = group_scores[0]
    grp = jnp.zeros(best.shape, jnp.int32)
    for g in range(1, N_GROUPS):
        upd = group_scores[g] > best
        best = jnp.where(upd, group_scores[g], best)
        grp = jnp.where(upd, g, grp)

    def pick(rows, j):
        out = rows[j]
        for g in range(1, N_GROUPS):
            out = jnp.where(grp == g, rows[g * EXPERTS_PER_GROUP + j], out)
        return out

    w = [pick(sel_r, j) for j in range(EXPERTS_PER_GROUP)]
    sc = [pick(s_r, j) for j in range(EXPERTS_PER_GROUP)]

    def argmax_first(vals):
        bv = vals[0]
        bi = jnp.zeros(bv.shape, jnp.int32)
        for j in range(1, len(vals)):
            upd = vals[j] > bv
            bv = jnp.where(upd, vals[j], bv)
            bi = jnp.where(upd, j, bi)
        return bi

    i1 = argmax_first(w)
    i2 = argmax_first([jnp.where(i1 == j, -jnp.inf, w[j]) for j in range(EXPERTS_PER_GROUP)])

    def take(vals, idx):
        out = vals[0]
        for j in range(1, len(vals)):
            out = jnp.where(idx == j, vals[j], out)
        return out

    s1 = take(sc, i1)
    s2 = take(sc, i2)
    den = s1 + s2
    return grp * EXPERTS_PER_GROUP + i1, grp * EXPERTS_PER_GROUP + i2, s1 / den, s2 / den


def _oproj_kernel(o_ref, oc_ref, x_ref, mod_ref, w_ref, g_ref, rw_ref, rb_ref, tri_ref,
                  x1_ref, h_ref, ri_ref, rf_ref, cnt_ref, carry_ref):
    i = pl.program_id(0)

    @pl.when(i == 0)
    def _():
        carry_ref[...] = jnp.zeros_like(carry_ref)

    m = mod_ref[0]
    o = jnp.where(i < LAT_TILES, o_ref[...], oc_ref[...])
    x1 = x_ref[...] + m[2:3] * _dot(o, w_ref[...])
    x1_ref[...] = x1
    h = _rms_modulate(x1, g_ref[...], m[3:4], m[4:5])
    hb = h.astype(BF16)
    h_ref[...] = h

    h_lo = (h - hb.astype(F32)).astype(BF16)
    rw = rw_ref[...]
    rw_hi = rw.astype(BF16)
    rw_lo = (rw - rw_hi.astype(F32)).astype(BF16)
    logits = _dot(hb, rw_hi) + _dot(h_lo, rw_hi) + _dot(hb, rw_lo)
    logits_t = logits.T[:N_EXPERTS]
    e0, e1, g0, g1 = _route_rows(logits_t, rb_ref[...])

    eiota = lax.broadcasted_iota(jnp.int32, (N_EXPERTS, TM), 0)
    oh0 = eiota == e0
    oh1 = eiota == e1
    oh = jnp.where(oh0 | oh1, 1.0, 0.0)
    before = _dot(oh.astype(BF16), tri_ref[...]) + carry_ref[...]
    rank0 = jnp.sum(jnp.where(oh0, before, 0.0), axis=0, keepdims=True)
    rank1 = jnp.sum(jnp.where(oh1, before, 0.0), axis=0, keepdims=True)
    carry_ref[...] = carry_ref[...] + jnp.sum(oh, axis=1, keepdims=True)

    ri_ref[...] = jnp.zeros_like(ri_ref)
    ri_ref[0:1, :] = e0
    ri_ref[1:2, :] = e1
    ri_ref[2:3, :] = rank0.astype(jnp.int32)
    ri_ref[3:4, :] = rank1.astype(jnp.int32)
    rf_ref[...] = jnp.zeros_like(rf_ref)
    rf_ref[0:1, :] = g0
    rf_ref[1:2, :] = g1
    cnt_ref[...] = jnp.broadcast_to(carry_ref[...], cnt_ref.shape)


def _oproj_route(o, o_ctx, x, mod_l, w_o, g, rw_pad, rb_col, tri, n_tok):
    n_tiles = n_tok // TM
    tok = lambda i: (i, 0)
    const2 = lambda i: (0, 0)
    lane_tok = lambda i: (0, i)
    return pl.pallas_call(
        _oproj_kernel,
        grid=(n_tiles,),
        in_specs=[
            pl.BlockSpec((TM, D_MODEL), lambda i: (jnp.minimum(i, LAT_TILES - 1), 0)),
            pl.BlockSpec((TM, D_MODEL), lambda i: (jnp.maximum(i - LAT_TILES, 0), 0)),
            pl.BlockSpec((TM, D_MODEL), tok),
            pl.BlockSpec((1, N_MOD, D_MODEL), lambda i: (_mod_row(i), 0, 0)),
            pl.BlockSpec((D_MODEL, D_MODEL), const2),
            pl.BlockSpec((1, D_MODEL), const2),
            pl.BlockSpec((D_MODEL, HB), const2),
            pl.BlockSpec((N_EXPERTS, 1), const2),
            pl.BlockSpec((TM, TM), const2),
        ],
        out_specs=[
            pl.BlockSpec((TM, D_MODEL), tok),
            pl.BlockSpec((TM, D_MODEL), tok),
            pl.BlockSpec((8, TM), lane_tok),
            pl.BlockSpec((8, TM), lane_tok),
            pl.BlockSpec((N_EXPERTS, HB), const2),
        ],
        out_shape=[
            jax.ShapeDtypeStruct((n_tok, D_MODEL), F32),
            jax.ShapeDtypeStruct((n_tok, D_MODEL), F32),
            jax.ShapeDtypeStruct((8, n_tok), jnp.int32),
            jax.ShapeDtypeStruct((8, n_tok), F32),
            jax.ShapeDtypeStruct((N_EXPERTS, HB), F32),
        ],
        scratch_shapes=[pltpu.VMEM((N_EXPERTS, 1), F32)],
        compiler_params=_cparams("arbitrary"),
        name="oproj_norm_route",
    )(o, o if o_ctx is None else o_ctx, x, mod_l, w_o, g.reshape(1, D_MODEL), rw_pad, rb_col, tri)


def _moe_kernel(te_ref, tv_ref, xs_ref, wg_ref, wu_ref, wd_ref, ys_ref, wgb, wub, wdb):
    i = pl.program_id(0)
    valid = tv_ref[i]
    e = te_ref[i]
    prev = te_ref[jnp.maximum(i - 1, 0)]

    @pl.when((i == 0) | (e != prev))
    def _():
        wgb[...] = wg_ref[0].astype(BF16)
        wub[...] = wu_ref[0].astype(BF16)
        wdb[...] = wd_ref[0].astype(BF16)

    @pl.when(valid > 0)
    def _():
        row = lax.broadcasted_iota(jnp.int32, (TM, 1), 0)
        x = xs_ref[...]
        x = jnp.where(row < valid, x, jnp.zeros_like(x)).astype(BF16)
        a = _dot(x, wgb[...])
        u = _dot(x, wub[...])
        hm = (a * _sigmoid(a) * u).astype(BF16)
        ys_ref[...] = _dot(hm, wdb[...])

    @pl.when(valid == 0)
    def _():
        ys_ref[...] = jnp.zeros_like(ys_ref)


def _grouped_ffn(tile_expert, tile_valid, xs, w_gate, w_up, w_down):
    n_tiles = xs.shape[0] // TM
    wmap = lambda i, te, tv: (te[i], 0, 0)
    tok = lambda i, te, tv: (i, 0)
    return pl.pallas_call(
        _moe_kernel,
        grid_spec=pltpu.PrefetchScalarGridSpec(
            num_scalar_prefetch=2,
            grid=(n_tiles,),
            in_specs=[
                pl.BlockSpec((TM, D_MODEL), tok),
                pl.BlockSpec((1, D_MODEL, D_EXPERT), wmap),
                pl.BlockSpec((1, D_MODEL, D_EXPERT), wmap),
                pl.BlockSpec((1, D_EXPERT, D_MODEL), wmap),
            ],
            out_specs=pl.BlockSpec((TM, D_MODEL), tok),
            scratch_shapes=[
                pltpu.VMEM((D_MODEL, D_EXPERT), BF16),
                pltpu.VMEM((D_MODEL, D_EXPERT), BF16),
                pltpu.VMEM((D_EXPERT, D_MODEL), BF16),
            ],
        ),
        out_shape=jax.ShapeDtypeStruct((xs.shape[0], D_MODEL), F32),
        compiler_params=_cparams("arbitrary"),
        name="grouped_ffn",
    )(tile_expert, tile_valid, xs, w_gate, w_up, w_down)


def _moe_layout(route_i, counts, n_tok):
    n_tiles = 2 * n_tok // TM + N_EXPERTS
    counts = counts.astype(jnp.int32)
    tiles_e = (counts + TM - 1) // TM
    tiles_end = jnp.cumsum(tiles_e)
    tile_start = tiles_end - tiles_e
    eid = jnp.arange(N_EXPERTS, dtype=jnp.int32)
    tok_oh = route_i[0:2][:, :, None] == eid
    pos = jnp.sum(jnp.where(tok_oh, tile_start * TM, 0), axis=-1) + route_i[2:4]
    tile_ids = jnp.arange(n_tiles, dtype=jnp.int32)
    n_used = tiles_end[-1]
    te = jnp.sum(tiles_end[None, :] <= jnp.minimum(tile_ids, n_used - 1)[:, None], axis=1).astype(jnp.int32)
    tile_oh = te[:, None] == eid
    cnt_t = jnp.sum(jnp.where(tile_oh, counts, 0), axis=1)
    start_t = jnp.sum(jnp.where(tile_oh, tile_start, 0), axis=1)
    tv = jnp.clip(cnt_t - (tile_ids - start_t) * TM, 0, TM)
    tv = jnp.where(tile_ids < n_used, tv, 0).astype(jnp.int32)
    return pos.astype(jnp.int32), te, tv, n_tiles


SC_CORES = 2
SC_SUBCORES = 16
SC_WORKERS = SC_CORES * SC_SUBCORES
SC_CHUNK = 32


def _sc_mesh():
    return plsc.VectorSubcoreMesh(core_axis_name="c", subcore_axis_name="s")


def _sc_worker_indices(pos, n_tok):
    n_ch = n_tok // SC_WORKERS // SC_CHUNK
    return pos.reshape(2, SC_WORKERS, n_ch, SC_CHUNK).transpose(1, 2, 0, 3), n_ch


def _sc_dispatch(h, pos, n_slots):
    n_tok = h.shape[0]
    per_w = n_tok // SC_WORKERS
    pos_w, n_ch = _sc_worker_indices(pos, n_tok)
    assert n_ch % 2 == 0 and n_ch * SC_CHUNK * SC_WORKERS == n_tok

    @functools.partial(
        pl.kernel, mesh=_sc_mesh(), out_type=jax.ShapeDtypeStruct((n_slots, D_MODEL), F32),
        scratch_types=[pltpu.VMEM((n_ch, 2, SC_CHUNK), jnp.int32), pltpu.VMEM((2, SC_CHUNK, D_MODEL), F32),
                       pltpu.SemaphoreType.DMA((2,)), pltpu.SemaphoreType.DMA((2,))],
        name="sc_dispatch")
    def dispatch(h_hbm, pos_hbm, xs_hbm, idx_v, rows_v, load_sem, scat_sem):
        wid = lax.axis_index("s") * SC_CORES + lax.axis_index("c")
        base = wid * per_w
        pltpu.sync_copy(pos_hbm.at[wid], idx_v)

        def load(c, b):
            return pltpu.make_async_copy(h_hbm.at[pl.ds(base + c * SC_CHUNK, SC_CHUNK)], rows_v.at[b],
                                         load_sem.at[b])

        def scat(c, b, k):
            return pltpu.make_async_copy(rows_v.at[b], xs_hbm.at[idx_v.at[c, k]], scat_sem.at[b])

        load(0, 0).start()

        @pl.loop(0, n_ch, step=2)
        def _(c0):
            for b in range(2):
                c = c0 + b
                load(c, b).wait()
                scat(c, b, 0).start()
                scat(c, b, 1).start()

                @pl.when(c >= 1)
                def _():
                    scat(c - 1, 1 - b, 0).wait()
                    scat(c - 1, 1 - b, 1).wait()

                @pl.when(c + 1 < n_ch)
                def _():
                    load(c + 1, 1 - b).start()

        scat(n_ch - 1, 1, 0).wait()
        scat(n_ch - 1, 1, 1).wait()

    return dispatch(h, pos_w)


def _sc_combine_gather(ys, pos):
    n_tok = pos.shape[1]
    per_w = n_tok // SC_WORKERS
    pos_w, n_ch = _sc_worker_indices(pos, n_tok)
    assert n_ch * SC_CHUNK * SC_WORKERS == n_tok

    @functools.partial(
        pl.kernel, mesh=_sc_mesh(), out_type=jax.ShapeDtypeStruct((2, n_tok, D_MODEL), F32),
        scratch_types=[pltpu.VMEM((n_ch, 2, SC_CHUNK), jnp.int32), pltpu.VMEM((2, SC_CHUNK, D_MODEL), F32),
                       pltpu.SemaphoreType.DMA((2,)), pltpu.SemaphoreType.DMA((2,))],
        name="sc_combine_gather")
    def gather(ys_hbm, pos_hbm, yg_hbm, idx_v, rows_v, gath_sem, write_sem):
        wid = lax.axis_index("s") * SC_CORES + lax.axis_index("c")
        base = wid * per_w
        pltpu.sync_copy(pos_hbm.at[wid], idx_v)

        def gath(c, k):
            return pltpu.make_async_copy(ys_hbm.at[idx_v.at[c, k]], rows_v.at[k], gath_sem.at[k])

        def write(c, k):
            return pltpu.make_async_copy(rows_v.at[k], yg_hbm.at[k, pl.ds(base + c * SC_CHUNK, SC_CHUNK)],
                                         write_sem.at[k])

        gath(0, 0).start()

        @pl.loop(0, n_ch)
        def _(c):
            gath(c, 0).wait()
            write(c, 0).start()

            @pl.when(c >= 1)
            def _():
                write(c - 1, 1).wait()

            gath(c, 1).start()
            gath(c, 1).wait()
            write(c, 1).start()
            write(c, 0).wait()

            @pl.when(c + 1 < n_ch)
            def _():
                gath(c + 1, 0).start()

        write(n_ch - 1, 1).wait()

    return gather(ys, pos_w)


def _combine_kernel(x1_ref, yg_ref, gt_ref, mod_ref, fg_ref, x2_ref, *, final):
    g = gt_ref[...]
    y = g[:, 0:1] * yg_ref[0] + g[:, 1:2] * yg_ref[1]
    x2 = x1_ref[...] + mod_ref[0][5:6] * y
    if final:
        ms = jnp.mean(x2 * x2, axis=-1, keepdims=True)
        x2 = x2 * lax.rsqrt(ms + NORM_EPS) * fg_ref[...]
    x2_ref[...] = x2


def _combine(x1, yg, gates_t, mod_l, final_g, n_tok, *, final):
    tok = lambda i: (i, 0)
    return pl.pallas_call(
        functools.partial(_combine_kernel, final=final),
        grid=(n_tok // TM,),
        in_specs=[
            pl.BlockSpec((TM, D_MODEL), tok),
            pl.BlockSpec((2, TM, D_MODEL), lambda i: (0, i, 0)),
            pl.BlockSpec((TM, 2), tok),
            pl.BlockSpec((1, N_MOD, D_MODEL), lambda i: (_mod_row(i), 0, 0)),
            pl.BlockSpec((1, D_MODEL), lambda i: (0, 0)),
        ],
        out_specs=pl.BlockSpec((TM, D_MODEL), tok),
        out_shape=jax.ShapeDtypeStruct((n_tok, D_MODEL), F32),
        compiler_params=_cparams("parallel"),
        name="moe_combine_final" if final else "moe_combine",
    )(x1, yg, gates_t, mod_l, final_g.reshape(1, D_MODEL))


def _diff_lambda_init(layer_idx):
    return 0.8 - 0.6 * math.exp(-0.3 * (layer_idx - 1))


def kernel(x, c, ctx, c_ctx, mod_w, mod_b, norm_mix, norm_ffn, w_qkv, w_o, na_rpb, diff_lambda, diff_subln,
           router_w, router_b, expert_w_gate, expert_w_up, expert_w_down, final_norm):
    mod = _modulation(c, c_ctx, mod_w, mod_b)
    xa = jnp.concatenate([x.reshape(T_LAT, D_MODEL), ctx.reshape(T_CTX, D_MODEL)], axis=0)
    w_qkv_b = w_qkv.astype(BF16)
    w_o_b = w_o.astype(BF16)
    cs, sn = _rope_tables()
    rw_pad = jnp.zeros((D_MODEL, HB), F32).at[:, :N_EXPERTS].set(router_w)
    rb_col = router_b.reshape(N_EXPERTS, 1).astype(F32)
    tri = (jnp.arange(TM)[:, None] < jnp.arange(TM)[None, :]).astype(BF16)

    for i in range(DEPTH):
        last = i == DEPTH - 1
        j = i // 2
        is_diff = i % 2 == 1
        q, k, v = _qkv(xa, mod[i], norm_mix[i], w_qkv_b[i], cs, sn, rope=is_diff)
        if not is_diff:
            o, o_ctx = _na_attention(_na_pair_table(na_rpb[j]), q, k, v)
        else:
            lam_init = _diff_lambda_init(i + 1)
            sg = diff_subln[j].reshape(1, HB)
            outs = _diff_attention(diff_lambda[j], sg, q, k, v, lam_init, ctx_out=not last)
            o, o_ctx = outs if not last else (outs[0], None)
        n_tok = T_LAT if last else T_ALL
        x1, h, route_i, route_f, cnt = _oproj_route(o, o_ctx, xa, mod[i], w_o_b[i], norm_ffn[i], rw_pad,
                                                    rb_col, tri, n_tok)
        pos, te, tv, n_tiles = _moe_layout(route_i, cnt[:, 0], n_tok)
        xs = _sc_dispatch(h, pos, n_tiles * TM)
        ys = _grouped_ffn(te, tv, xs, expert_w_gate[i], expert_w_up[i], expert_w_down[i])
        yg = _sc_combine_gather(ys, pos)
        xa = _combine(x1, yg, route_f[0:2].T, mod[i], final_norm, n_tok, final=last)
    return xa.reshape(BATCH, SEQ, D_MODEL)
```

```python
import functools
import math

import jax
import jax.numpy as jnp
from jax import lax
from jax.experimental import pallas as pl
from jax.experimental.pallas import tpu as pltpu
from jax.experimental.pallas import tpu_sc as plsc

F32 = jnp.float32
BF16 = jnp.bfloat16

D_MODEL = 1024
BATCH = 8
SEQ = 2048
DEPTH = 4
CTX_LEN = 256
GRID_W = 64
ROWS = SEQ // GRID_W
NA_HEADS = 16
NA_WIN_ROWS = 8
NA_WIN_COLS = 16
DIFF_HEADS = 8
DIFF_HEAD_DIM = 64
ROPE_THETA = 10000.0
N_EXPERTS = 16
N_GROUPS = 4
EXPERTS_PER_GROUP = 4
D_EXPERT = 1024
N_MOD = 6
NORM_EPS = 1e-6

T_LAT = BATCH * SEQ
T_CTX = BATCH * CTX_LEN
T_ALL = T_LAT + T_CTX
TM = 512
LAT_TILES = T_LAT // TM
CTX_TILES = T_CTX // TM
TILES_PER_BATCH = SEQ // TM
CTX_ROW = BATCH
MOD_ROWS = 16
HB = 128
N_HB = D_MODEL // HB
NA_RB = 4
NA_BAND_ROWS = 12
NA_Q = NA_RB * GRID_W
NA_BAND = NA_BAND_ROWS * GRID_W
TQ_DIFF = 512
MASK_VALUE = -1e30
VMEM_LIMIT = 56 * 1024 * 1024


def _cparams(*sem):
    return pltpu.CompilerParams(dimension_semantics=sem, vmem_limit_bytes=VMEM_LIMIT)


def _dot(a, b):
    return jnp.dot(a, b, preferred_element_type=F32)


def _dot_nt(a, b):
    return lax.dot_general(a, b, (((1,), (1,)), ((), ())), preferred_element_type=F32)


def _sigmoid(x):
    return 1.0 / (1.0 + jnp.exp(-x))


def _mod_row(i):
    return jnp.minimum(i // TILES_PER_BATCH, CTX_ROW)


def _mod_kernel(act_ref, w_ref, b_ref, o_ref):
    a = act_ref[...]
    a = a * _sigmoid(a)
    o_ref[0] = _dot(a.astype(BF16), w_ref[0].astype(BF16)) + b_ref[0]


def _modulation(c, c_ctx, mod_w, mod_b):
    tn = 1536
    act = jnp.zeros((MOD_ROWS, D_MODEL), F32).at[:BATCH].set(c).at[CTX_ROW].set(c_ctx)
    out = pl.pallas_call(
        _mod_kernel,
        grid=(DEPTH, N_MOD * D_MODEL // tn),
        in_specs=[
            pl.BlockSpec((MOD_ROWS, D_MODEL), lambda l, j: (0, 0)),
            pl.BlockSpec((1, D_MODEL, tn), lambda l, j: (l, 0, j)),
            pl.BlockSpec((1, 1, tn), lambda l, j: (l, 0, j)),
        ],
        out_specs=pl.BlockSpec((1, MOD_ROWS, tn), lambda l, j: (l, 0, j)),
        out_shape=jax.ShapeDtypeStruct((DEPTH, MOD_ROWS, N_MOD * D_MODEL), F32),
        compiler_params=_cparams("parallel", "parallel"),
        name="adaln_mod",
    )(act, mod_w, mod_b.reshape(DEPTH, 1, N_MOD * D_MODEL))
    return out.reshape(DEPTH, MOD_ROWS, N_MOD, D_MODEL)


def _rms_modulate(x, g, shift, scale):
    ms = jnp.mean(x * x, axis=-1, keepdims=True)
    y = x * lax.rsqrt(ms + NORM_EPS) * g
    return y * (1.0 + scale) + shift


def _rope_block(xb, cs, sn):
    lane = lax.broadcasted_iota(jnp.int32, xb.shape, 1)
    partner = jnp.where((lane & 63) < 32, pltpu.roll(xb, 96, 1), pltpu.roll(xb, 32, 1))
    return xb * cs + partner * sn


def _qkv_kernel(x_ref, mod_ref, g_ref, wqk_ref, wv_ref, cs_ref, sn_ref, q_ref, k_ref, v_ref, *, diff):
    m = mod_ref[0]
    h = _rms_modulate(x_ref[...], g_ref[...], m[0:1], m[1:2]).astype(BF16)
    for idx, out in enumerate((q_ref, k_ref)):
        acc = _dot(h, wqk_ref[:, idx * D_MODEL:(idx + 1) * D_MODEL])
        if diff:
            cs = cs_ref[...]
            sn = sn_ref[...]
            acc = jnp.concatenate(
                [_rope_block(acc[:, j * HB:(j + 1) * HB], cs, sn) for j in range(N_HB)], axis=1)
        if idx == 0:
            acc = acc * 0.125
        out[...] = acc.astype(BF16)
    if diff:
        v_ref[...] = _dot_nt(wv_ref[...], h).astype(BF16)
    else:
        v_ref[...] = _dot(h, wv_ref[...]).astype(BF16)


def _qkv(x, mod_l, g, w, cs, sn, *, diff):
    n_tiles = T_ALL // TM
    tok = lambda i: (i, 0)
    const2 = lambda i: (0, 0)
    rope_idx = lambda i: (jnp.where(i < LAT_TILES, i % TILES_PER_BATCH, TILES_PER_BATCH), 0)
    out = jax.ShapeDtypeStruct((T_ALL, D_MODEL), BF16)
    w_qk = w[:, :2 * D_MODEL]
    w_v = w[:, 2 * D_MODEL:]
    if diff:
        w_v = w_v.T
        v_spec = pl.BlockSpec((D_MODEL, TM), lambda i: (0, i))
        v_shape = jax.ShapeDtypeStruct((D_MODEL, T_ALL), BF16)
    else:
        v_spec = pl.BlockSpec((TM, D_MODEL), tok)
        v_shape = out
    return pl.pallas_call(
        functools.partial(_qkv_kernel, diff=diff),
        grid=(n_tiles,),
        in_specs=[
            pl.BlockSpec((TM, D_MODEL), tok),
            pl.BlockSpec((1, N_MOD, D_MODEL), lambda i: (_mod_row(i), 0, 0)),
            pl.BlockSpec((1, D_MODEL), const2),
            pl.BlockSpec((D_MODEL, 2 * D_MODEL), const2),
            pl.BlockSpec((D_MODEL, D_MODEL), const2),
            pl.BlockSpec((TM, HB), rope_idx),
            pl.BlockSpec((TM, HB), rope_idx),
        ],
        out_specs=[pl.BlockSpec((TM, D_MODEL), tok), pl.BlockSpec((TM, D_MODEL), tok), v_spec],
        out_shape=[out, out, v_shape],
        compiler_params=_cparams("parallel"),
        name="norm_qkv_rope" if diff else "norm_qkv",
    )(x, mod_l, g.reshape(1, D_MODEL), w_qk, w_v, cs, sn)


def _rope_tables():
    t = jnp.arange(SEQ)
    row = (t // GRID_W).astype(F32)
    col = (t % GRID_W).astype(F32)
    n_freq = DIFF_HEAD_DIM // 4
    inv_freq = ROPE_THETA ** (-jnp.arange(n_freq, dtype=F32) / n_freq)
    ang = jnp.concatenate([row[:, None] * inv_freq, col[:, None] * inv_freq], axis=-1)
    cos, sin = jnp.cos(ang), jnp.sin(ang)
    cs = jnp.concatenate([cos, cos, cos, cos], axis=-1)
    sn = jnp.concatenate([-sin, sin, -sin, sin], axis=-1)
    cs = jnp.concatenate([cs, jnp.ones((TM, HB), F32)], axis=0)
    sn = jnp.concatenate([sn, jnp.zeros((TM, HB), F32)], axis=0)
    return cs, sn


def _softmax_pv(scores, values):
    m = functools.reduce(jnp.maximum, [jnp.max(s, axis=-1, keepdims=True) for s in scores])
    num = None
    den = None
    for s, v in zip(scores, values):
        e = jnp.exp(s - m)
        d = jnp.sum(e, axis=-1, keepdims=True)
        o = _dot(e.astype(BF16), v)
        num = o if num is None else num + o
        den = d if den is None else den + d
    return num / den


Q_BLK = 256
KEY_BLK = 256
SUBLANES = 8
DIFF_Q_PER_TRIP = 2


def _colwise(reduce_fn, x):
    return reduce_fn(x.reshape(x.shape[0] // SUBLANES, SUBLANES, x.shape[1]), axis=0)


def _attention_t(qms, pieces, s_scr):
    blocks = [(k_ref, vt_ref, s0) for k_ref, vt_ref in pieces for s0 in range(0, k_ref.shape[0], KEY_BLK)]

    def score(u, c, mx):
        k_ref, _, s0 = blocks[c]
        st = _dot_nt(k_ref[s0:s0 + KEY_BLK, :], qms[u])
        s_scr[u, c * KEY_BLK:(c + 1) * KEY_BLK, :] = st
        return jnp.maximum(mx, _colwise(jnp.max, st))

    def expo(u, c, m, acc, den):
        _, vt_ref, s0 = blocks[c]
        e = jnp.exp(s_scr[u, c * KEY_BLK:(c + 1) * KEY_BLK, :] - m)
        o = _dot(vt_ref[:, s0:s0 + KEY_BLK], e.astype(BF16))
        return (o if acc is None else acc + o), den + _colwise(jnp.sum, e)

    neg = jnp.full((SUBLANES, Q_BLK), -jnp.inf, F32)
    zero = jnp.zeros((SUBLANES, Q_BLK), F32)
    mx = neg
    for c in range(len(blocks)):
        mx = score(0, c, mx)
    outs = []
    for u in range(len(qms)):
        m = jnp.max(mx, axis=0, keepdims=True)
        acc, den, mx = None, zero, neg
        for c in range(len(blocks)):
            if u + 1 < len(qms):
                mx = score(u + 1, c, mx)
            acc, den = expo(u, c, m, acc, den)
        outs.append(acc / jnp.sum(den, axis=0, keepdims=True))
    return outs


def _half_masks(shape):
    lane = lax.broadcasted_iota(jnp.int32, shape, 1)
    return lane < 64, lane >= 64


N_DR = 2 * NA_WIN_ROWS - 1
N_DC = 2 * NA_WIN_COLS - 1
PAIR_MASK_FIRST = N_DR - 1
PAIR_MASK_SECOND = N_DR
PAIR_MASKED = N_DR + 1
N_PAIRS = N_DR + 2
DR_LO = NA_WIN_ROWS - 1 - NA_WIN_ROWS // 2
DR_HI = DR_LO + NA_WIN_ROWS - 1


def _na_pair_table(rpb):
    h = rpb.shape[0]
    w = GRID_W
    v = jnp.zeros((h, N_DR, 2 * w), F32)
    v = v.at[..., 0:NA_WIN_COLS].set(rpb[..., NA_WIN_COLS - 1:].astype(F32))
    v = v.at[..., 2 * w - (NA_WIN_COLS - 1):].set(rpb[..., :NA_WIN_COLS - 1].astype(F32))
    y = jnp.broadcast_to(v[:, :, None, :], (h, N_DR, w, 2 * w)).reshape(h, N_DR, w * 2 * w)
    t = y[..., :w * (2 * w - 1)].reshape(h, N_DR, w, 2 * w - 1)[..., :w]
    col = jnp.arange(w)
    cstart = jnp.clip(col - NA_WIN_COLS // 2, 0, w - NA_WIN_COLS)
    cvalid = (col[None, :] >= cstart[:, None]) & (col[None, :] < cstart[:, None] + NA_WIN_COLS)
    t = jnp.where(cvalid, t, MASK_VALUE)
    masked = jnp.full((h, 1, w, w), MASK_VALUE, F32)
    regular = jnp.concatenate([t[:, :-1], t[:, 1:]], axis=-1)
    first = jnp.concatenate([masked, t[:, DR_LO:DR_LO + 1]], axis=-1)
    second = jnp.concatenate([t[:, DR_HI:DR_HI + 1], masked], axis=-1)
    return jnp.concatenate([regular, first, second, jnp.concatenate([masked, masked], axis=-1)], axis=1)


def _na_pair_entries(q_row0, band_row0, band_rows):
    entries = []
    for a in range(NA_RB):
        r = q_row0 + a
        r0 = min(max(r - NA_WIN_ROWS // 2, 0), ROWS - NA_WIN_ROWS)
        row = []
        for m in range(band_rows // 2):
            kr = (band_row0 + 2 * m, band_row0 + 2 * m + 1)
            ok = [r0 <= x < r0 + NA_WIN_ROWS for x in kr]
            d = [x - r + NA_WIN_ROWS - 1 for x in kr]
            if ok[0] and ok[1]:
                row.append(d[0])
            elif ok[1]:
                assert d[1] == DR_LO
                row.append(PAIR_MASK_FIRST)
            elif ok[0]:
                assert d[0] == DR_HI
                row.append(PAIR_MASK_SECOND)
            else:
                row.append(PAIR_MASKED)
        entries.append(row)
    return entries


def _two_map_attention(q, key_pieces, value_pieces):
    lo, hi = _half_masks(q.shape)
    outs = []
    for hh, msk in enumerate((lo, hi)):
        qm = jnp.where(msk, q, jnp.zeros_like(q))
        scores = [kp(qm, hh) if callable(kp) else _dot_nt(qm, kp) for kp in key_pieces]
        outs.append(_softmax_pv(scores, value_pieces))
    return outs, lo


def _na_kernel(tab_ref, q_ref, k_ref, v_ref, qc_ref, kc_ref, vc_ref, o_ref, oc_ref):
    kc = kc_ref[...]
    vc = vc_ref[...]
    n_blocks = ROWS // NA_RB

    def block(q_start, band_start, band_rows, entries):
        n_band = band_rows * GRID_W
        q = q_ref[pl.ds(q_start, NA_Q), :]
        kb = k_ref[pl.ds(band_start, n_band), :]
        vb = v_ref[pl.ds(band_start, n_band), :]

        def band_scores(qm, hh):
            bias = jnp.concatenate(
                [jnp.concatenate([tab_ref[hh, e] for e in row], axis=1) for row in entries], axis=0)
            return _dot_nt(qm, kb) + bias

        outs, lo = _two_map_attention(q, [band_scores, kc], [vb, vc])
        o_ref[pl.ds(q_start, NA_Q), :] = jnp.where(lo, outs[0], outs[1]).astype(BF16)

    block(0, 0, NA_WIN_ROWS, _na_pair_entries(0, 0, NA_WIN_ROWS))
    last_band = ROWS - NA_WIN_ROWS
    block((n_blocks - 1) * NA_Q, last_band * GRID_W, NA_WIN_ROWS,
          _na_pair_entries(ROWS - NA_RB, last_band, NA_WIN_ROWS))
    interior = _na_pair_entries(NA_RB, 0, NA_BAND_ROWS)

    def body(rb, carry):
        u0 = jnp.minimum(NA_RB * rb - NA_RB, ROWS - NA_BAND_ROWS)
        block(pl.multiple_of(rb * NA_Q, NA_Q), pl.multiple_of(u0 * GRID_W, GRID_W), NA_BAND_ROWS, interior)
        return carry

    lax.fori_loop(1, n_blocks - 1, body, 0)

    outs, lo = _two_map_attention(qc_ref[...], [kc], [vc])
    oc_ref[...] = jnp.where(lo, outs[0], outs[1]).astype(BF16)


def _na_attention(table, q, k, v):
    lat = lambda hp, b: (b, hp)
    ctx = lambda hp, b: (T_LAT // CTX_LEN + b, hp)
    return pl.pallas_call(
        _na_kernel,
        grid=(N_HB, BATCH),
        in_specs=[
            pl.BlockSpec((2, N_PAIRS, GRID_W, 2 * GRID_W), lambda hp, b: (hp, 0, 0, 0)),
            pl.BlockSpec((SEQ, HB), lat),
            pl.BlockSpec((SEQ, HB), lat),
            pl.BlockSpec((SEQ, HB), lat),
            pl.BlockSpec((CTX_LEN, HB), ctx),
            pl.BlockSpec((CTX_LEN, HB), ctx),
            pl.BlockSpec((CTX_LEN, HB), ctx),
        ],
        out_specs=[pl.BlockSpec((SEQ, HB), lat), pl.BlockSpec((CTX_LEN, HB), lat)],
        out_shape=[jax.ShapeDtypeStruct((T_LAT, D_MODEL), BF16), jax.ShapeDtypeStruct((T_CTX, D_MODEL), BF16)],
        compiler_params=_cparams("parallel", "parallel"),
        name="na_attention",
    )(table, q, k, v, q, k, v)


def _diff_lambda(lam_ref, lam_init):
    lam = lam_ref[...]
    a = jnp.sum(lam[0:1] * lam[1:2], axis=-1, keepdims=True)
    b = jnp.sum(lam[2:3] * lam[3:4], axis=-1, keepdims=True)
    return jnp.exp(a) - jnp.exp(b) + lam_init


def _diff_kernel(lam_ref, sg_ref, q_ref, k_ref, vt_ref, qc_ref, kc_ref, vct_ref, o_ref, *rest, lam_init):
    *maybe_oc_ref, s_scr = rest
    lam = _diff_lambda(lam_ref, lam_init)
    sg_col = sg_ref[...]

    def rows(qs, pieces):
        qms = [jnp.where(msk, q, jnp.zeros_like(q)) for q in qs for msk in _half_masks(q.shape)]
        outs = _attention_t(qms, pieces, s_scr)
        res = []
        for o1, o2 in zip(outs[0::2], outs[1::2]):
            ot = o1 - lam * o2
            ms = jnp.mean(ot * ot, axis=0, keepdims=True)
            ot = ot * lax.rsqrt(ms + NORM_EPS) * sg_col * (1.0 - lam_init)
            res.append(ot.T.astype(BF16))
        return res

    def body(t, carry):
        starts = [pl.multiple_of((t * DIFF_Q_PER_TRIP + j) * Q_BLK, Q_BLK) for j in range(DIFF_Q_PER_TRIP)]
        outs = rows([q_ref[pl.ds(r0, Q_BLK), :] for r0 in starts], [(k_ref, vt_ref), (kc_ref, vct_ref)])
        for r0, o in zip(starts, outs):
            o_ref[pl.ds(r0, Q_BLK), :] = o
        return carry

    lax.fori_loop(0, SEQ // (Q_BLK * DIFF_Q_PER_TRIP), body, 0)
    if maybe_oc_ref:
        oc_ref, = maybe_oc_ref
        assert CTX_LEN == Q_BLK
        oc_ref[...], = rows([qc_ref[...]], [(kc_ref, vct_ref)])


def _diff_attention(lam, sg, q, k, vt, lam_init, *, ctx_out):
    lat = lambda b, h: (b, h)
    ctx = lambda b, h: (T_LAT // CTX_LEN + b, h)
    lat_t = lambda b, h: (h, b)
    ctx_t = lambda b, h: (h, T_LAT // CTX_LEN + b)
    out_specs = [pl.BlockSpec((SEQ, HB), lat)]
    out_shape = [jax.ShapeDtypeStruct((T_LAT, D_MODEL), BF16)]
    if ctx_out:
        out_specs.append(pl.BlockSpec((CTX_LEN, HB), lat))
        out_shape.append(jax.ShapeDtypeStruct((T_CTX, D_MODEL), BF16))
    return pl.pallas_call(
        functools.partial(_diff_kernel, lam_init=lam_init),
        grid=(BATCH, DIFF_HEADS),
        in_specs=[
            pl.BlockSpec(lam.shape, lambda b, h: (0, 0)),
            pl.BlockSpec(sg.shape, lambda b, h: (0, 0)),
            pl.BlockSpec((SEQ, HB), lat),
            pl.BlockSpec((SEQ, HB), lat),
            pl.BlockSpec((HB, SEQ), lat_t),
            pl.BlockSpec((CTX_LEN, HB), ctx),
            pl.BlockSpec((CTX_LEN, HB), ctx),
            pl.BlockSpec((HB, CTX_LEN), ctx_t),
        ],
        out_specs=out_specs,
        out_shape=out_shape,
        scratch_shapes=[pltpu.VMEM((2 * DIFF_Q_PER_TRIP, SEQ + CTX_LEN, Q_BLK), F32)],
        compiler_params=_cparams("parallel", "parallel"),
        name="diff_attention",
    )(lam, sg, q, k, vt, q, k, vt)


def _route_rows(logits_t, bias_col):
    s = _sigmoid(logits_t)
    sel = s + bias_col
    sel_r = [sel[e:e + 1] for e in range(N_EXPERTS)]
    s_r = [s[e:e + 1] for e in range(N_EXPERTS)]
    group_scores = []
    for g in range(N_GROUPS):
        v = sel_r[g * EXPERTS_PER_GROUP:(g + 1) * EXPERTS_PER_GROUP]
        pairs = [v[a] + v[b] for a in range(EXPERTS_PER_GROUP) for b in range(a + 1, EXPERTS_PER_GROUP)]
        group_scores.append(functools.reduce(jnp.maximum, pairs))
    best = group_scores[0]
    grp = jnp.zeros(best.shape, jnp.int32)
    for g in range(1, N_GROUPS):
        upd = group_scores[g] > best
        best = jnp.where(upd, group_scores[g], best)
        grp = jnp.where(upd, g, grp)

    def pick(rows, j):
        out = rows[j]
        for g in range(1, N_GROUPS):
            out = jnp.where(grp == g, rows[g * EXPERTS_PER_GROUP + j], out)
        return out

    w = [pick(sel_r, j) for j in range(EXPERTS_PER_GROUP)]
    sc = [pick(s_r, j) for j in range(EXPERTS_PER_GROUP)]

    def argmax_first(vals):
        bv = vals[0]
        bi = jnp.zeros(bv.shape, jnp.int32)
        for j in range(1, len(vals)):
            upd = vals[j] > bv
            bv = jnp.where(upd, vals[j], bv)
            bi = jnp.where(upd, j, bi)
        return bi

    i1 = argmax_first(w)
    i2 = argmax_first([jnp.where(i1 == j, -jnp.inf, w[j]) for j in range(EXPERTS_PER_GROUP)])

    def take(vals, idx):
        out = vals[0]
        for j in range(1, len(vals)):
            out = jnp.where(idx == j, vals[j], out)
        return out

    s1 = take(sc, i1)
    s2 = take(sc, i2)
    den = s1 + s2
    return grp * EXPERTS_PER_GROUP + i1, grp * EXPERTS_PER_GROUP + i2, s1 / den, s2 / den


def _oproj_kernel(o_ref, oc_ref, x_ref, mod_ref, w_ref, g_ref, rw_ref, rb_ref, tri_ref,
                  x1_ref, h_ref, ri_ref, rf_ref, cnt_ref, carry_ref):
    i = pl.program_id(0)

    @pl.when(i == 0)
    def _():
        carry_ref[...] = jnp.zeros_like(carry_ref)

    m = mod_ref[0]
    o = jnp.where(i < LAT_TILES, o_ref[...], oc_ref[...])
    x1 = x_ref[...] + m[2:3] * _dot(o, w_ref[...])
    x1_ref[...] = x1
    h = _rms_modulate(x1, g_ref[...], m[3:4], m[4:5])
    hb = h.astype(BF16)
    h_ref[...] = h

    h_lo = (h - hb.astype(F32)).astype(BF16)
    rw = rw_ref[...]
    rw_hi = rw.astype(BF16)
    rw_lo = (rw - rw_hi.astype(F32)).astype(BF16)
    logits = _dot(hb, rw_hi) + _dot(h_lo, rw_hi) + _dot(hb, rw_lo)
    logits_t = logits.T[:N_EXPERTS]
    e0, e1, g0, g1 = _route_rows(logits_t, rb_ref[...])

    eiota = lax.broadcasted_iota(jnp.int32, (N_EXPERTS, TM), 0)
    oh0 = eiota == e0
    oh1 = eiota == e1
    oh = jnp.where(oh0 | oh1, 1.0, 0.0)
    before = _dot(oh.astype(BF16), tri_ref[...]) + carry_ref[...]
    rank0 = jnp.sum(jnp.where(oh0, before, 0.0), axis=0, keepdims=True)
    rank1 = jnp.sum(jnp.where(oh1, before, 0.0), axis=0, keepdims=True)
    carry_ref[...] = carry_ref[...] + jnp.sum(oh, axis=1, keepdims=True)

    ri_ref[...] = jnp.zeros_like(ri_ref)
    ri_ref[0:1, :] = e0
    ri_ref[1:2, :] = e1
    ri_ref[2:3, :] = rank0.astype(jnp.int32)
    ri_ref[3:4, :] = rank1.astype(jnp.int32)
    rf_ref[...] = jnp.zeros_like(rf_ref)
    rf_ref[0:1, :] = g0
    rf_ref[1:2, :] = g1
    cnt_ref[...] = jnp.broadcast_to(carry_ref[...], cnt_ref.shape)


def _oproj_route(o, o_ctx, x, mod_l, w_o, g, rw_pad, rb_col, tri, n_tok):
    n_tiles = n_tok // TM
    tok = lambda i: (i, 0)
    const2 = lambda i: (0, 0)
    lane_tok = lambda i: (0, i)
    return pl.pallas_call(
        _oproj_kernel,
        grid=(n_tiles,),
        in_specs=[
            pl.BlockSpec((TM, D_MODEL), lambda i: (jnp.minimum(i, LAT_TILES - 1), 0)),
            pl.BlockSpec((TM, D_MODEL), lambda i: (jnp.maximum(i - LAT_TILES, 0), 0)),
            pl.BlockSpec((TM, D_MODEL), tok),
            pl.BlockSpec((1, N_MOD, D_MODEL), lambda i: (_mod_row(i), 0, 0)),
            pl.BlockSpec((D_MODEL, D_MODEL), const2),
            pl.BlockSpec((1, D_MODEL), const2),
            pl.BlockSpec((D_MODEL, HB), const2),
            pl.BlockSpec((N_EXPERTS, 1), const2),
            pl.BlockSpec((TM, TM), const2),
        ],
        out_specs=[
            pl.BlockSpec((TM, D_MODEL), tok),
            pl.BlockSpec((TM, D_MODEL), tok),
            pl.BlockSpec((8, TM), lane_tok),
            pl.BlockSpec((8, TM), lane_tok),
            pl.BlockSpec((N_EXPERTS, HB), const2),
        ],
        out_shape=[
            jax.ShapeDtypeStruct((n_tok, D_MODEL), F32),
            jax.ShapeDtypeStruct((n_tok, D_MODEL), F32),
            jax.ShapeDtypeStruct((8, n_tok), jnp.int32),
            jax.ShapeDtypeStruct((8, n_tok), F32),
            jax.ShapeDtypeStruct((N_EXPERTS, HB), F32),
        ],
        scratch_shapes=[pltpu.VMEM((N_EXPERTS, 1), F32)],
        compiler_params=_cparams("arbitrary"),
        name="oproj_norm_route",
    )(o, o if o_ctx is None else o_ctx, x, mod_l, w_o, g.reshape(1, D_MODEL), rw_pad, rb_col, tri)


def _moe_kernel(te_ref, tv_ref, xs_ref, wg_ref, wu_ref, wd_ref, ys_ref, wgb, wub, wdb):
    i = pl.program_id(0)
    valid = tv_ref[i]
    e = te_ref[i]
    prev = te_ref[jnp.maximum(i - 1, 0)]

    @pl.when((i == 0) | (e != prev))
    def _():
        wgb[...] = wg_ref[0, 0].astype(BF16)
        wub[...] = wu_ref[0, 0].astype(BF16)
        wdb[...] = wd_ref[0, 0].astype(BF16)

    @pl.when(valid > 0)
    def _():
        row = lax.broadcasted_iota(jnp.int32, (TM, 1), 0)
        x = xs_ref[...]
        x = jnp.where(row < valid, x, jnp.zeros_like(x)).astype(BF16)
        a = _dot(x, wgb[...])
        u = _dot(x, wub[...])
        hm = (a * _sigmoid(a) * u).astype(BF16)
        ys_ref[...] = _dot(hm, wdb[...])

    @pl.when(valid == 0)
    def _():
        ys_ref[...] = jnp.zeros_like(ys_ref)


def _grouped_ffn(tile_expert, tile_valid, xs, w_gate, w_up, w_down, layer):
    n_tiles = xs.shape[0] // TM
    wmap = lambda i, te, tv: (layer, te[i], 0, 0)
    tok = lambda i, te, tv: (i, 0)
    return pl.pallas_call(
        _moe_kernel,
        grid_spec=pltpu.PrefetchScalarGridSpec(
            num_scalar_prefetch=2,
            grid=(n_tiles,),
            in_specs=[
                pl.BlockSpec((TM, D_MODEL), tok),
                pl.BlockSpec((1, 1, D_MODEL, D_EXPERT), wmap),
                pl.BlockSpec((1, 1, D_MODEL, D_EXPERT), wmap),
                pl.BlockSpec((1, 1, D_EXPERT, D_MODEL), wmap),
            ],
            out_specs=pl.BlockSpec((TM, D_MODEL), tok),
            scratch_shapes=[
                pltpu.VMEM((D_MODEL, D_EXPERT), BF16),
                pltpu.VMEM((D_MODEL, D_EXPERT), BF16),
                pltpu.VMEM((D_EXPERT, D_MODEL), BF16),
            ],
        ),
        out_shape=jax.ShapeDtypeStruct((xs.shape[0], D_MODEL), F32),
        compiler_params=_cparams("arbitrary"),
        name="grouped_ffn",
    )(tile_expert, tile_valid, xs, w_gate, w_up, w_down)


def _moe_layout(route_i, counts, n_tok):
    n_tiles = 2 * n_tok // TM + N_EXPERTS
    counts = counts.astype(jnp.int32)
    tiles_e = (counts + TM - 1) // TM
    tiles_end = jnp.cumsum(tiles_e)
    tile_start = tiles_end - tiles_e
    eid = jnp.arange(N_EXPERTS, dtype=jnp.int32)
    tok_oh = route_i[0:2][:, :, None] == eid
    pos = jnp.sum(jnp.where(tok_oh, tile_start * TM, 0), axis=-1) + route_i[2:4]
    tile_ids = jnp.arange(n_tiles, dtype=jnp.int32)
    n_used = tiles_end[-1]
    te = jnp.sum(tiles_end[None, :] <= jnp.minimum(tile_ids, n_used - 1)[:, None], axis=1).astype(jnp.int32)
    tile_oh = te[:, None] == eid
    cnt_t = jnp.sum(jnp.where(tile_oh, counts, 0), axis=1)
    start_t = jnp.sum(jnp.where(tile_oh, tile_start, 0), axis=1)
    tv = jnp.clip(cnt_t - (tile_ids - start_t) * TM, 0, TM)
    tv = jnp.where(tile_ids < n_used, tv, 0).astype(jnp.int32)
    return pos.astype(jnp.int32), te, tv, n_tiles


SC_CORES = 2
SC_SUBCORES = 16
SC_WORKERS = SC_CORES * SC_SUBCORES
SC_CHUNK = 32


def _sc_mesh():
    return plsc.VectorSubcoreMesh(core_axis_name="c", subcore_axis_name="s")


def _sc_worker_indices(pos, n_tok):
    n_ch = n_tok // SC_WORKERS // SC_CHUNK
    return pos.reshape(2, SC_WORKERS, n_ch, SC_CHUNK).transpose(1, 2, 0, 3), n_ch


def _sc_dispatch(h, pos, n_slots):
    n_tok = h.shape[0]
    per_w = n_tok // SC_WORKERS
    pos_w, n_ch = _sc_worker_indices(pos, n_tok)
    assert n_ch % 2 == 0 and n_ch * SC_CHUNK * SC_WORKERS == n_tok

    @functools.partial(
        pl.kernel, mesh=_sc_mesh(), out_type=jax.ShapeDtypeStruct((n_slots, D_MODEL), F32),
        scratch_types=[pltpu.VMEM((n_ch, 2, SC_CHUNK), jnp.int32), pltpu.VMEM((2, SC_CHUNK, D_MODEL), F32),
                       pltpu.SemaphoreType.DMA((2,)), pltpu.SemaphoreType.DMA((2,))],
        name="sc_dispatch")
    def dispatch(h_hbm, pos_hbm, xs_hbm, idx_v, rows_v, load_sem, scat_sem):
        wid = lax.axis_index("s") * SC_CORES + lax.axis_index("c")
        base = wid * per_w
        pltpu.sync_copy(pos_hbm.at[wid], idx_v)

        def load(c, b):
            return pltpu.make_async_copy(h_hbm.at[pl.ds(base + c * SC_CHUNK, SC_CHUNK)], rows_v.at[b],
                                         load_sem.at[b])

        def scat(c, b, k):
            return pltpu.make_async_copy(rows_v.at[b], xs_hbm.at[idx_v.at[c, k]], scat_sem.at[b])

        load(0, 0).start()

        @pl.loop(0, n_ch, step=2)
        def _(c0):
            for b in range(2):
                c = c0 + b
                load(c, b).wait()
                scat(c, b, 0).start()
                scat(c, b, 1).start()

                @pl.when(c >= 1)
                def _():
                    scat(c - 1, 1 - b, 0).wait()
                    scat(c - 1, 1 - b, 1).wait()

                @pl.when(c + 1 < n_ch)
                def _():
                    load(c + 1, 1 - b).start()

        scat(n_ch - 1, 1, 0).wait()
        scat(n_ch - 1, 1, 1).wait()

    return dispatch(h, pos_w)


def _sc_combine_gather(ys, pos):
    n_tok = pos.shape[1]
    per_w = n_tok // SC_WORKERS
    pos_w, n_ch = _sc_worker_indices(pos, n_tok)
    assert n_ch * SC_CHUNK * SC_WORKERS == n_tok

    @functools.partial(
        pl.kernel, mesh=_sc_mesh(), out_type=jax.ShapeDtypeStruct((2, n_tok, D_MODEL), F32),
        scratch_types=[pltpu.VMEM((n_ch, 2, SC_CHUNK), jnp.int32), pltpu.VMEM((2, SC_CHUNK, D_MODEL), F32),
                       pltpu.SemaphoreType.DMA((2,)), pltpu.SemaphoreType.DMA((2,))],
        name="sc_combine_gather")
    def gather(ys_hbm, pos_hbm, yg_hbm, idx_v, rows_v, gath_sem, write_sem):
        wid = lax.axis_index("s") * SC_CORES + lax.axis_index("c")
        base = wid * per_w
        pltpu.sync_copy(pos_hbm.at[wid], idx_v)

        def gath(c, k):
            return pltpu.make_async_copy(ys_hbm.at[idx_v.at[c, k]], rows_v.at[k], gath_sem.at[k])

        def write(c, k):
            return pltpu.make_async_copy(rows_v.at[k], yg_hbm.at[k, pl.ds(base + c * SC_CHUNK, SC_CHUNK)],
                                         write_sem.at[k])

        gath(0, 0).start()

        @pl.loop(0, n_ch)
        def _(c):
            gath(c, 0).wait()
            write(c, 0).start()

            @pl.when(c >= 1)
            def _():
                write(c - 1, 1).wait()

            gath(c, 1).start()
            gath(c, 1).wait()
            write(c, 1).start()
            write(c, 0).wait()

            @pl.when(c + 1 < n_ch)
            def _():
                gath(c + 1, 0).start()

        write(n_ch - 1, 1).wait()

    return gather(ys, pos_w)


def _combine_kernel(x1_ref, yg_ref, gt_ref, mod_ref, fg_ref, x2_ref, *, final):
    g = gt_ref[...]
    y = g[:, 0:1] * yg_ref[0] + g[:, 1:2] * yg_ref[1]
    x2 = x1_ref[...] + mod_ref[0][5:6] * y
    if final:
        ms = jnp.mean(x2 * x2, axis=-1, keepdims=True)
        x2 = x2 * lax.rsqrt(ms + NORM_EPS) * fg_ref[...]
    x2_ref[...] = x2


def _combine(x1, yg, gates_t, mod_l, final_g, n_tok, *, final):
    tok = lambda i: (i, 0)
    return pl.pallas_call(
        functools.partial(_combine_kernel, final=final),
        grid=(n_tok // TM,),
        in_specs=[
            pl.BlockSpec((TM, D_MODEL), tok),
            pl.BlockSpec((2, TM, D_MODEL), lambda i: (0, i, 0)),
            pl.BlockSpec((TM, 2), tok),
            pl.BlockSpec((1, N_MOD, D_MODEL), lambda i: (_mod_row(i), 0, 0)),
            pl.BlockSpec((1, D_MODEL), lambda i: (0, 0)),
        ],
        out_specs=pl.BlockSpec((TM, D_MODEL), tok),
        out_shape=jax.ShapeDtypeStruct((n_tok, D_MODEL), F32),
        compiler_params=_cparams("parallel"),
        name="moe_combine_final" if final else "moe_combine",
    )(x1, yg, gates_t, mod_l, final_g.reshape(1, D_MODEL))


def _diff_lambda_init(layer_idx):
    return 0.8 - 0.6 * math.exp(-0.3 * (layer_idx - 1))


def kernel(x, c, ctx, c_ctx, mod_w, mod_b, norm_mix, norm_ffn, w_qkv, w_o, na_rpb, diff_lambda, diff_subln,
           router_w, router_b, expert_w_gate, expert_w_up, expert_w_down, final_norm):
    mod = _modulation(c, c_ctx, mod_w, mod_b)
    xa = jnp.concatenate([x.reshape(T_LAT, D_MODEL), ctx.reshape(T_CTX, D_MODEL)], axis=0)
    w_qkv_b = w_qkv.astype(BF16)
    w_o_b = w_o.astype(BF16)
    cs, sn = _rope_tables()
    rw_pad = jnp.zeros((D_MODEL, HB), F32).at[:, :N_EXPERTS].set(router_w)
    rb_col = router_b.reshape(N_EXPERTS, 1).astype(F32)
    tri = (jnp.arange(TM)[:, None] < jnp.arange(TM)[None, :]).astype(BF16)

    for i in range(DEPTH):
        last = i == DEPTH - 1
        j = i // 2
        is_diff = i % 2 == 1
        q, k, v = _qkv(xa, mod[i], norm_mix[i], w_qkv_b[i], cs, sn, diff=is_diff)
        if not is_diff:
            o, o_ctx = _na_attention(_na_pair_table(na_rpb[j]), q, k, v)
        else:
            lam_init = _diff_lambda_init(i + 1)
            sg = diff_subln[j].reshape(HB, 1)
            outs = _diff_attention(diff_lambda[j], sg, q, k, v, lam_init, ctx_out=not last)
            o, o_ctx = outs if not last else (outs[0], None)
        n_tok = T_LAT if last else T_ALL
        x1, h, route_i, route_f, cnt = _oproj_route(o, o_ctx, xa, mod[i], w_o_b[i], norm_ffn[i], rw_pad,
                                                    rb_col, tri, n_tok)
        pos, te, tv, n_tiles = _moe_layout(route_i, cnt[:, 0], n_tok)
        xs = _sc_dispatch(h, pos, n_tiles * TM)
        ys = _grouped_ffn(te, tv, xs, expert_w_gate, expert_w_up, expert_w_down, i)
        yg = _sc_combine_gather(ys, pos)
        xa = _combine(x1, yg, route_f[0:2].T, mod[i], final_norm, n_tok, final=last)
    return xa.reshape(BATCH, SEQ, D_MODEL)
```

```python
import functools
import math

import jax
import jax.numpy as jnp
from jax import lax
from jax.experimental import pallas as pl
from jax.experimental.pallas import tpu as pltpu
from jax.experimental.pallas import tpu_sc as plsc

F32 = jnp.float32
BF16 = jnp.bfloat16

D_MODEL = 1024
BATCH = 8
SEQ = 2048
DEPTH = 4
CTX_LEN = 256
GRID_W = 64
ROWS = SEQ // GRID_W
NA_HEADS = 16
NA_WIN_ROWS = 8
NA_WIN_COLS = 16
DIFF_HEADS = 8
DIFF_HEAD_DIM = 64
ROPE_THETA = 10000.0
N_EXPERTS = 16
N_GROUPS = 4
EXPERTS_PER_GROUP = 4
D_EXPERT = 1024
N_MOD = 6
NORM_EPS = 1e-6

T_LAT = BATCH * SEQ
T_CTX = BATCH * CTX_LEN
T_ALL = T_LAT + T_CTX
TM = 512
LAT_TILES = T_LAT // TM
CTX_TILES = T_CTX // TM
TILES_PER_BATCH = SEQ // TM
CTX_ROW = BATCH
MOD_ROWS = 16
HB = 128
N_HB = D_MODEL // HB
NA_RB = 4
NA_BAND_ROWS = 12
NA_Q = NA_RB * GRID_W
NA_BAND = NA_BAND_ROWS * GRID_W
TQ_DIFF = 512
MASK_VALUE = -1e30
VMEM_LIMIT = 56 * 1024 * 1024


def _cparams(*sem):
    return pltpu.CompilerParams(dimension_semantics=sem, vmem_limit_bytes=VMEM_LIMIT)


def _dot(a, b):
    return jnp.dot(a, b, preferred_element_type=F32)


def _dot_nt(a, b):
    return lax.dot_general(a, b, (((1,), (1,)), ((), ())), preferred_element_type=F32)


def _sigmoid(x):
    return 1.0 / (1.0 + jnp.exp(-x))


def _mod_row(i):
    return jnp.minimum(i // TILES_PER_BATCH, CTX_ROW)


def _mod_kernel(act_ref, w_ref, b_ref, o_ref):
    a = act_ref[...]
    a = a * _sigmoid(a)
    o_ref[0] = _dot(a.astype(BF16), w_ref[0].astype(BF16)) + b_ref[0]


def _modulation(c, c_ctx, mod_w, mod_b):
    tn = 1536
    act = jnp.zeros((MOD_ROWS, D_MODEL), F32).at[:BATCH].set(c).at[CTX_ROW].set(c_ctx)
    out = pl.pallas_call(
        _mod_kernel,
        grid=(DEPTH, N_MOD * D_MODEL // tn),
        in_specs=[
            pl.BlockSpec((MOD_ROWS, D_MODEL), lambda l, j: (0, 0)),
            pl.BlockSpec((1, D_MODEL, tn), lambda l, j: (l, 0, j)),
            pl.BlockSpec((1, 1, tn), lambda l, j: (l, 0, j)),
        ],
        out_specs=pl.BlockSpec((1, MOD_ROWS, tn), lambda l, j: (l, 0, j)),
        out_shape=jax.ShapeDtypeStruct((DEPTH, MOD_ROWS, N_MOD * D_MODEL), F32),
        compiler_params=_cparams("parallel", "parallel"),
        name="adaln_mod",
    )(act, mod_w, mod_b.reshape(DEPTH, 1, N_MOD * D_MODEL))
    return out.reshape(DEPTH, MOD_ROWS, N_MOD, D_MODEL)


def _rms_modulate(x, g, shift, scale):
    ms = jnp.mean(x * x, axis=-1, keepdims=True)
    y = x * lax.rsqrt(ms + NORM_EPS) * g
    return y * (1.0 + scale) + shift


def _rope_block(xb, cs, sn):
    lane = lax.broadcasted_iota(jnp.int32, xb.shape, 1)
    partner = jnp.where((lane & 63) < 32, pltpu.roll(xb, 96, 1), pltpu.roll(xb, 32, 1))
    return xb * cs + partner * sn


def _qkv_kernel(x_ref, mod_ref, g_ref, wqk_ref, wvt_ref, cs_ref, sn_ref, q_ref, k_ref, vt_ref, *, rope):
    m = mod_ref[0]
    h = _rms_modulate(x_ref[...], g_ref[...], m[0:1], m[1:2]).astype(BF16)
    for idx, out in enumerate((q_ref, k_ref)):
        acc = _dot(h, wqk_ref[:, idx * D_MODEL:(idx + 1) * D_MODEL])
        if rope:
            cs = cs_ref[...]
            sn = sn_ref[...]
            acc = jnp.concatenate(
                [_rope_block(acc[:, j * HB:(j + 1) * HB], cs, sn) for j in range(N_HB)], axis=1)
        if idx == 0:
            acc = acc * 0.125
        out[...] = acc.astype(BF16)
    vt_ref[...] = _dot_nt(wvt_ref[...], h).astype(BF16)


def _qkv(x, mod_l, g, w, cs, sn, *, rope):
    n_tiles = T_ALL // TM
    tok = lambda i: (i, 0)
    const2 = lambda i: (0, 0)
    rope_idx = lambda i: (jnp.where(i < LAT_TILES, i % TILES_PER_BATCH, TILES_PER_BATCH), 0)
    out = jax.ShapeDtypeStruct((T_ALL, D_MODEL), BF16)
    w_qk = w[:, :2 * D_MODEL]
    w_v = w[:, 2 * D_MODEL:].T
    v_spec = pl.BlockSpec((D_MODEL, TM), lambda i: (0, i))
    v_shape = jax.ShapeDtypeStruct((D_MODEL, T_ALL), BF16)
    return pl.pallas_call(
        functools.partial(_qkv_kernel, rope=rope),
        grid=(n_tiles,),
        in_specs=[
            pl.BlockSpec((TM, D_MODEL), tok),
            pl.BlockSpec((1, N_MOD, D_MODEL), lambda i: (_mod_row(i), 0, 0)),
            pl.BlockSpec((1, D_MODEL), const2),
            pl.BlockSpec((D_MODEL, 2 * D_MODEL), const2),
            pl.BlockSpec((D_MODEL, D_MODEL), const2),
            pl.BlockSpec((TM, HB), rope_idx),
            pl.BlockSpec((TM, HB), rope_idx),
        ],
        out_specs=[pl.BlockSpec((TM, D_MODEL), tok), pl.BlockSpec((TM, D_MODEL), tok), v_spec],
        out_shape=[out, out, v_shape],
        compiler_params=_cparams("parallel"),
        name="norm_qkv_rope" if rope else "norm_qkv",
    )(x, mod_l, g.reshape(1, D_MODEL), w_qk, w_v, cs, sn)


def _rope_tables():
    t = jnp.arange(SEQ)
    row = (t // GRID_W).astype(F32)
    col = (t % GRID_W).astype(F32)
    n_freq = DIFF_HEAD_DIM // 4
    inv_freq = ROPE_THETA ** (-jnp.arange(n_freq, dtype=F32) / n_freq)
    ang = jnp.concatenate([row[:, None] * inv_freq, col[:, None] * inv_freq], axis=-1)
    cos, sin = jnp.cos(ang), jnp.sin(ang)
    cs = jnp.concatenate([cos, cos, cos, cos], axis=-1)
    sn = jnp.concatenate([-sin, sin, -sin, sin], axis=-1)
    cs = jnp.concatenate([cs, jnp.ones((TM, HB), F32)], axis=0)
    sn = jnp.concatenate([sn, jnp.zeros((TM, HB), F32)], axis=0)
    return cs, sn


Q_BLK = 256
KEY_BLK = 256
SUBLANES = 8
DIFF_Q_PER_TRIP = 2


def _colwise(reduce_fn, x):
    return reduce_fn(x.reshape(x.shape[0] // SUBLANES, SUBLANES, x.shape[1]), axis=0)


def _key_blocks(k_ref, vt_ref, key_start=0, n_keys=None, bias=None):
    n_keys = k_ref.shape[0] if n_keys is None else n_keys
    blocks = []
    for c, s0 in enumerate(range(0, n_keys, KEY_BLK)):
        blocks.append((
            lambda s0=s0: k_ref[pl.ds(key_start + s0, KEY_BLK), :],
            lambda s0=s0: vt_ref[:, pl.ds(key_start + s0, KEY_BLK)],
            None if bias is None else functools.partial(bias, c),
        ))
    return blocks


def _attention_t(operands, s_scr):
    def score(u, c, mx):
        qm, blocks = operands[u]
        k_tile, _, bias = blocks[c]
        st = _dot_nt(k_tile(), qm)
        if bias is not None:
            st = st + bias()
        s_scr[u % 2, c * KEY_BLK:(c + 1) * KEY_BLK, :] = st
        return jnp.maximum(mx, _colwise(jnp.max, st))

    def expo(u, c, m, acc, den):
        _, vt_tile, _ = operands[u][1][c]
        e = jnp.exp(s_scr[u % 2, c * KEY_BLK:(c + 1) * KEY_BLK, :] - m)
        o = _dot(vt_tile(), e.astype(BF16))
        return (o if acc is None else acc + o), den + _colwise(jnp.sum, e)

    neg = jnp.full((SUBLANES, Q_BLK), -jnp.inf, F32)
    zero = jnp.zeros((SUBLANES, Q_BLK), F32)
    n_blocks = [len(blocks) for _, blocks in operands]
    mx = neg
    for c in range(n_blocks[0]):
        mx = score(0, c, mx)
    outs = []
    for u in range(len(operands)):
        m = jnp.max(mx, axis=0, keepdims=True)
        acc, den, mx = None, zero, neg
        n_next = n_blocks[u + 1] if u + 1 < len(operands) else 0
        for c in range(max(n_blocks[u], n_next)):
            if c < n_next:
                mx = score(u + 1, c, mx)
            if c < n_blocks[u]:
                acc, den = expo(u, c, m, acc, den)
        outs.append(acc / jnp.sum(den, axis=0, keepdims=True))
    return outs


def _half_masks(shape):
    lane = lax.broadcasted_iota(jnp.int32, shape, 1)
    return lane < 64, lane >= 64


N_DR = 2 * NA_WIN_ROWS - 1
N_DC = 2 * NA_WIN_COLS - 1
PAIR_MASK_FIRST = N_DR - 1
PAIR_MASK_SECOND = N_DR
PAIR_MASKED = N_DR + 1
N_PAIRS = N_DR + 2
NA_Q_PER_TRIP = 2
DR_LO = NA_WIN_ROWS - 1 - NA_WIN_ROWS // 2
DR_HI = DR_LO + NA_WIN_ROWS - 1


def _na_pair_table(rpb):
    h = rpb.shape[0]
    w = GRID_W
    v = jnp.zeros((h, N_DR, 2 * w), F32)
    v = v.at[..., 0:NA_WIN_COLS].set(rpb[..., NA_WIN_COLS - 1:].astype(F32))
    v = v.at[..., 2 * w - (NA_WIN_COLS - 1):].set(rpb[..., :NA_WIN_COLS - 1].astype(F32))
    y = jnp.broadcast_to(v[:, :, None, :], (h, N_DR, w, 2 * w)).reshape(h, N_DR, w * 2 * w)
    t = y[..., :w * (2 * w - 1)].reshape(h, N_DR, w, 2 * w - 1)[..., :w]
    col = jnp.arange(w)
    cstart = jnp.clip(col - NA_WIN_COLS // 2, 0, w - NA_WIN_COLS)
    cvalid = (col[None, :] >= cstart[:, None]) & (col[None, :] < cstart[:, None] + NA_WIN_COLS)
    tt = jnp.swapaxes(jnp.where(cvalid, t, MASK_VALUE), -1, -2)
    masked = jnp.full((h, 1, w, w), MASK_VALUE, F32)
    regular = jnp.concatenate([tt[:, 1:], tt[:, :-1]], axis=-1)
    first = jnp.concatenate([masked, tt[:, DR_HI:DR_HI + 1]], axis=-1)
    second = jnp.concatenate([tt[:, DR_LO:DR_LO + 1], masked], axis=-1)
    return jnp.concatenate([regular, first, second, jnp.concatenate([masked, masked], axis=-1)], axis=1)


def _na_pair_entries(q_row0, band_row0, band_rows):
    entries = []
    for j in range(band_rows):
        kr = band_row0 + j
        row = []
        for a in range(0, NA_RB, 2):
            ok, d = [], []
            for r in (q_row0 + a, q_row0 + a + 1):
                r0 = min(max(r - NA_WIN_ROWS // 2, 0), ROWS - NA_WIN_ROWS)
                ok.append(r0 <= kr < r0 + NA_WIN_ROWS)
                d.append(kr - r + NA_WIN_ROWS - 1)
            if ok[0] and ok[1]:
                row.append(d[0] - 1)
            elif ok[1]:
                assert d[1] == DR_HI
                row.append(PAIR_MASK_FIRST)
            elif ok[0]:
                assert d[0] == DR_LO
                row.append(PAIR_MASK_SECOND)
            else:
                row.append(PAIR_MASKED)
        entries.append(row)
    return entries


def _na_kernel(tab_ref, q_ref, k_ref, vt_ref, qc_ref, kc_ref, vct_ref, o_ref, oc_ref, s_scr):
    n_blocks = ROWS // NA_RB
    rows_per_chunk = KEY_BLK // GRID_W
    ctx_blocks = _key_blocks(kc_ref, vct_ref)

    def operands(q_start, band_start, band_rows, entries):
        q = q_ref[pl.ds(q_start, NA_Q), :]
        ops = []
        for hh, msk in enumerate(_half_masks(q.shape)):
            def bias(c, hh=hh):
                rows = entries[c * rows_per_chunk:(c + 1) * rows_per_chunk]
                return jnp.concatenate(
                    [jnp.concatenate([tab_ref[hh, e] for e in row], axis=1) for row in rows], axis=0)

            band = _key_blocks(k_ref, vt_ref, band_start, band_rows * GRID_W, bias)
            ops.append((jnp.where(msk, q, jnp.zeros_like(q)), band + ctx_blocks))
        return ops

    def merge_heads(o_lo, o_hi):
        feat = lax.broadcasted_iota(jnp.int32, o_lo.shape, 0)
        return jnp.where(feat < HB // 2, o_lo, o_hi).T.astype(BF16)

    def run(blocks):
        outs = _attention_t([op for blk in blocks for op in operands(*blk)], s_scr)
        for i, blk in enumerate(blocks):
            o_ref[pl.ds(blk[0], NA_Q), :] = merge_heads(outs[2 * i], outs[2 * i + 1])

    last_band = ROWS - NA_WIN_ROWS
    run([(0, 0, NA_WIN_ROWS, _na_pair_entries(0, 0, NA_WIN_ROWS)),
         ((n_blocks - 1) * NA_Q, last_band * GRID_W, NA_WIN_ROWS,
          _na_pair_entries(ROWS - NA_RB, last_band, NA_WIN_ROWS))])
    interior = _na_pair_entries(NA_RB, 0, NA_BAND_ROWS)

    def body(t, carry):
        blocks = []
        for i in range(NA_Q_PER_TRIP):
            rb = 1 + t * NA_Q_PER_TRIP + i
            band_start = pl.multiple_of((rb - 1) * NA_Q, NA_Q)
            blocks.append((pl.multiple_of(rb * NA_Q, NA_Q), band_start, NA_BAND_ROWS, interior))
        run(blocks)
        return carry

    lax.fori_loop(0, (n_blocks - 2) // NA_Q_PER_TRIP, body, 0)

    qc = qc_ref[...]
    outs = _attention_t([(jnp.where(msk, qc, jnp.zeros_like(qc)), ctx_blocks) for msk in _half_masks(qc.shape)],
                        s_scr)
    oc_ref[...] = merge_heads(outs[0], outs[1])


def _na_attention(table, q, k, vt):
    assert NA_Q == Q_BLK and CTX_LEN == Q_BLK and (ROWS // NA_RB - 2) % NA_Q_PER_TRIP == 0
    lat = lambda hp, b: (b, hp)
    ctx = lambda hp, b: (T_LAT // CTX_LEN + b, hp)
    return pl.pallas_call(
        _na_kernel,
        grid=(N_HB, BATCH),
        in_specs=[
            pl.BlockSpec((2, N_PAIRS, GRID_W, 2 * GRID_W), lambda hp, b: (hp, 0, 0, 0)),
            pl.BlockSpec((SEQ, HB), lat),
            pl.BlockSpec((SEQ, HB), lat),
            pl.BlockSpec((HB, SEQ), lambda hp, b: (hp, b)),
            pl.BlockSpec((CTX_LEN, HB), ctx),
            pl.BlockSpec((CTX_LEN, HB), ctx),
            pl.BlockSpec((HB, CTX_LEN), lambda hp, b: (hp, T_LAT // CTX_LEN + b)),
        ],
        out_specs=[pl.BlockSpec((SEQ, HB), lat), pl.BlockSpec((CTX_LEN, HB), lat)],
        out_shape=[jax.ShapeDtypeStruct((T_LAT, D_MODEL), BF16), jax.ShapeDtypeStruct((T_CTX, D_MODEL), BF16)],
        scratch_shapes=[pltpu.VMEM((2, NA_BAND + CTX_LEN, Q_BLK), F32)],
        compiler_params=_cparams("parallel", "parallel"),
        name="na_attention",
    )(table, q, k, vt, q, k, vt)


def _diff_lambda(lam_ref, lam_init):
    lam = lam_ref[...]
    a = jnp.sum(lam[0:1] * lam[1:2], axis=-1, keepdims=True)
    b = jnp.sum(lam[2:3] * lam[3:4], axis=-1, keepdims=True)
    return jnp.exp(a) - jnp.exp(b) + lam_init


def _diff_kernel(lam_ref, sg_ref, q_ref, k_ref, vt_ref, qc_ref, kc_ref, vct_ref, o_ref, *rest, lam_init):
    *maybe_oc_ref, s_scr = rest
    lam = _diff_lambda(lam_ref, lam_init)
    sg_col = sg_ref[...]

    def rows(qs, pieces):
        blocks = [blk for k_r, vt_r in pieces for blk in _key_blocks(k_r, vt_r)]
        outs = _attention_t([(jnp.where(msk, q, jnp.zeros_like(q)), blocks)
                             for q in qs for msk in _half_masks(q.shape)], s_scr)
        res = []
        for o1, o2 in zip(outs[0::2], outs[1::2]):
            ot = o1 - lam * o2
            ms = jnp.mean(ot * ot, axis=0, keepdims=True)
            ot = ot * lax.rsqrt(ms + NORM_EPS) * sg_col * (1.0 - lam_init)
            res.append(ot.T.astype(BF16))
        return res

    def body(t, carry):
        starts = [pl.multiple_of((t * DIFF_Q_PER_TRIP + j) * Q_BLK, Q_BLK) for j in range(DIFF_Q_PER_TRIP)]
        outs = rows([q_ref[pl.ds(r0, Q_BLK), :] for r0 in starts], [(k_ref, vt_ref), (kc_ref, vct_ref)])
        for r0, o in zip(starts, outs):
            o_ref[pl.ds(r0, Q_BLK), :] = o
        return carry

    lax.fori_loop(0, SEQ // (Q_BLK * DIFF_Q_PER_TRIP), body, 0)
    if maybe_oc_ref:
        oc_ref, = maybe_oc_ref
        assert CTX_LEN == Q_BLK
        oc_ref[...], = rows([qc_ref[...]], [(kc_ref, vct_ref)])


def _diff_attention(lam, sg, q, k, vt, lam_init, *, ctx_out):
    lat = lambda b, h: (b, h)
    ctx = lambda b, h: (T_LAT // CTX_LEN + b, h)
    lat_t = lambda b, h: (h, b)
    ctx_t = lambda b, h: (h, T_LAT // CTX_LEN + b)
    out_specs = [pl.BlockSpec((SEQ, HB), lat)]
    out_shape = [jax.ShapeDtypeStruct((T_LAT, D_MODEL), BF16)]
    if ctx_out:
        out_specs.append(pl.BlockSpec((CTX_LEN, HB), lat))
        out_shape.append(jax.ShapeDtypeStruct((T_CTX, D_MODEL), BF16))
    return pl.pallas_call(
        functools.partial(_diff_kernel, lam_init=lam_init),
        grid=(BATCH, DIFF_HEADS),
        in_specs=[
            pl.BlockSpec(lam.shape, lambda b, h: (0, 0)),
            pl.BlockSpec(sg.shape, lambda b, h: (0, 0)),
            pl.BlockSpec((SEQ, HB), lat),
            pl.BlockSpec((SEQ, HB), lat),
            pl.BlockSpec((HB, SEQ), lat_t),
            pl.BlockSpec((CTX_LEN, HB), ctx),
            pl.BlockSpec((CTX_LEN, HB), ctx),
            pl.BlockSpec((HB, CTX_LEN), ctx_t),
        ],
        out_specs=out_specs,
        out_shape=out_shape,
        scratch_shapes=[pltpu.VMEM((2, SEQ + CTX_LEN, Q_BLK), F32)],
        compiler_params=_cparams("parallel", "parallel"),
        name="diff_attention",
    )(lam, sg, q, k, vt, q, k, vt)


def _route_rows(logits_t, bias_col):
    s = _sigmoid(logits_t)
    sel = s + bias_col
    sel_r = [sel[e:e + 1] for e in range(N_EXPERTS)]
    s_r = [s[e:e + 1] for e in range(N_EXPERTS)]
    group_scores = []
    for g in range(N_GROUPS):
        v = sel_r[g * EXPERTS_PER_GROUP:(g + 1) * EXPERTS_PER_GROUP]
        pairs = [v[a] + v[b] for a in range(EXPERTS_PER_GROUP) for b in range(a + 1, EXPERTS_PER_GROUP)]
        group_scores.append(functools.reduce(jnp.maximum, pairs))
    best = group_scores[0]
    grp = jnp.zeros(best.shape, jnp.int32)
    for g in range(1, N_GROUPS):
        upd = group_scores[g] > best
        best = jnp.where(upd, group_scores[g], best)
        grp = jnp.where(upd, g, grp)

    def pick(rows, j):
        out = rows[j]
        for g in range(1, N_GROUPS):
            out = jnp.where(grp == g, rows[g * EXPERTS_PER_GROUP + j], out)
        return out

    w = [pick(sel_r, j) for j in range(EXPERTS_PER_GROUP)]
    sc = [pick(s_r, j) for j in range(EXPERTS_PER_GROUP)]

    def argmax_first(vals):
        bv = vals[0]
        bi = jnp.zeros(bv.shape, jnp.int32)
        for j in range(1, len(vals)):
            upd = vals[j] > bv
            bv = jnp.where(upd, vals[j], bv)
            bi = jnp.where(upd, j, bi)
        return bi

    i1 = argmax_first(w)
    i2 = argmax_first([jnp.where(i1 == j, -jnp.inf, w[j]) for j in range(EXPERTS_PER_GROUP)])

    def take(vals, idx):
        out = vals[0]
        for j in range(1, len(vals)):
            out = jnp.where(idx == j, vals[j], out)
        return out

    s1 = take(sc, i1)
    s2 = take(sc, i2)
    den = s1 + s2
    return grp * EXPERTS_PER_GROUP + i1, grp * EXPERTS_PER_GROUP + i2, s1 / den, s2 / den


def _oproj_kernel(o_ref, oc_ref, x_ref, mod_ref, w_ref, g_ref, rw_ref, rb_ref, tri_ref,
                  x1_ref, h_ref, ri_ref, rf_ref, cnt_ref, carry_ref):
    i = pl.program_id(0)

    @pl.when(i == 0)
    def _():
        carry_ref[...] = jnp.zeros_like(carry_ref)

    m = mod_ref[0]
    o = jnp.where(i < LAT_TILES, o_ref[...], oc_ref[...])
    x1 = x_ref[...] + m[2:3] * _dot(o, w_ref[...])
    x1_ref[...] = x1
    h = _rms_modulate(x1, g_ref[...], m[3:4], m[4:5])
    hb = h.astype(BF16)
    h_ref[...] = h

    h_lo = (h - hb.astype(F32)).astype(BF16)
    rw = rw_ref[...]
    rw_hi = rw.astype(BF16)
    rw_lo = (rw - rw_hi.astype(F32)).astype(BF16)
    logits = _dot(hb, rw_hi) + _dot(h_lo, rw_hi) + _dot(hb, rw_lo)
    logits_t = logits.T[:N_EXPERTS]
    e0, e1, g0, g1 = _route_rows(logits_t, rb_ref[...])

    eiota = lax.broadcasted_iota(jnp.int32, (N_EXPERTS, TM), 0)
    oh0 = eiota == e0
    oh1 = eiota == e1
    oh = jnp.where(oh0 | oh1, 1.0, 0.0)
    before = _dot(oh.astype(BF16), tri_ref[...]) + carry_ref[...]
    rank0 = jnp.sum(jnp.where(oh0, before, 0.0), axis=0, keepdims=True)
    rank1 = jnp.sum(jnp.where(oh1, before, 0.0), axis=0, keepdims=True)
    carry_ref[...] = carry_ref[...] + jnp.sum(oh, axis=1, keepdims=True)

    ri_ref[...] = jnp.zeros_like(ri_ref)
    ri_ref[0:1, :] = e0
    ri_ref[1:2, :] = e1
    ri_ref[2:3, :] = rank0.astype(jnp.int32)
    ri_ref[3:4, :] = rank1.astype(jnp.int32)
    rf_ref[...] = jnp.zeros_like(rf_ref)
    rf_ref[0:1, :] = g0
    rf_ref[1:2, :] = g1
    cnt_ref[...] = jnp.broadcast_to(carry_ref[...], cnt_ref.shape)


def _oproj_route(o, o_ctx, x, mod_l, w_o, g, rw_pad, rb_col, tri, n_tok):
    n_tiles = n_tok // TM
    tok = lambda i: (i, 0)
    const2 = lambda i: (0, 0)
    lane_tok = lambda i: (0, i)
    return pl.pallas_call(
        _oproj_kernel,
        grid=(n_tiles,),
        in_specs=[
            pl.BlockSpec((TM, D_MODEL), lambda i: (jnp.minimum(i, LAT_TILES - 1), 0)),
            pl.BlockSpec((TM, D_MODEL), lambda i: (jnp.maximum(i - LAT_TILES, 0), 0)),
            pl.BlockSpec((TM, D_MODEL), tok),
            pl.BlockSpec((1, N_MOD, D_MODEL), lambda i: (_mod_row(i), 0, 0)),
            pl.BlockSpec((D_MODEL, D_MODEL), const2),
            pl.BlockSpec((1, D_MODEL), const2),
            pl.BlockSpec((D_MODEL, HB), const2),
            pl.BlockSpec((N_EXPERTS, 1), const2),
            pl.BlockSpec((TM, TM), const2),
        ],
        out_specs=[
            pl.BlockSpec((TM, D_MODEL), tok),
            pl.BlockSpec((TM, D_MODEL), tok),
            pl.BlockSpec((8, TM), lane_tok),
            pl.BlockSpec((8, TM), lane_tok),
            pl.BlockSpec((N_EXPERTS, HB), const2),
        ],
        out_shape=[
            jax.ShapeDtypeStruct((n_tok, D_MODEL), F32),
            jax.ShapeDtypeStruct((n_tok, D_MODEL), F32),
            jax.ShapeDtypeStruct((8, n_tok), jnp.int32),
            jax.ShapeDtypeStruct((8, n_tok), F32),
            jax.ShapeDtypeStruct((N_EXPERTS, HB), F32),
        ],
        scratch_shapes=[pltpu.VMEM((N_EXPERTS, 1), F32)],
        compiler_params=_cparams("arbitrary"),
        name="oproj_norm_route",
    )(o, o if o_ctx is None else o_ctx, x, mod_l, w_o, g.reshape(1, D_MODEL), rw_pad, rb_col, tri)


def _moe_kernel(te_ref, tv_ref, xs_ref, wg_ref, wu_ref, wd_ref, ys_ref, wgb, wub, wdb):
    i = pl.program_id(0)
    valid = tv_ref[i]
    e = te_ref[i]
    prev = te_ref[jnp.maximum(i - 1, 0)]

    @pl.when((i == 0) | (e != prev))
    def _():
        wgb[...] = wg_ref[0, 0].astype(BF16)
        wub[...] = wu_ref[0, 0].astype(BF16)
        wdb[...] = wd_ref[0, 0].astype(BF16)

    @pl.when(valid > 0)
    def _():
        row = lax.broadcasted_iota(jnp.int32, (TM, 1), 0)
        x = xs_ref[...]
        x = jnp.where(row < valid, x, jnp.zeros_like(x)).astype(BF16)
        a = _dot(x, wgb[...])
        u = _dot(x, wub[...])
        hm = (a * _sigmoid(a) * u).astype(BF16)
        ys_ref[...] = _dot(hm, wdb[...])

    @pl.when(valid == 0)
    def _():
        ys_ref[...] = jnp.zeros_like(ys_ref)


def _grouped_ffn(tile_expert, tile_valid, xs, w_gate, w_up, w_down, layer):
    n_tiles = xs.shape[0] // TM
    wmap = lambda i, te, tv: (layer, te[i], 0, 0)
    tok = lambda i, te, tv: (i, 0)
    return pl.pallas_call(
        _moe_kernel,
        grid_spec=pltpu.PrefetchScalarGridSpec(
            num_scalar_prefetch=2,
            grid=(n_tiles,),
            in_specs=[
                pl.BlockSpec((TM, D_MODEL), tok),
                pl.BlockSpec((1, 1, D_MODEL, D_EXPERT), wmap),
                pl.BlockSpec((1, 1, D_MODEL, D_EXPERT), wmap),
                pl.BlockSpec((1, 1, D_EXPERT, D_MODEL), wmap),
            ],
            out_specs=pl.BlockSpec((TM, D_MODEL), tok),
            scratch_shapes=[
                pltpu.VMEM((D_MODEL, D_EXPERT), BF16),
                pltpu.VMEM((D_MODEL, D_EXPERT), BF16),
                pltpu.VMEM((D_EXPERT, D_MODEL), BF16),
            ],
        ),
        out_shape=jax.ShapeDtypeStruct((xs.shape[0], D_MODEL), F32),
        compiler_params=_cparams("arbitrary"),
        name="grouped_ffn",
    )(tile_expert, tile_valid, xs, w_gate, w_up, w_down)


def _moe_layout(route_i, counts, n_tok):
    n_tiles = 2 * n_tok // TM + N_EXPERTS
    counts = counts.astype(jnp.int32)
    tiles_e = (counts + TM - 1) // TM
    tiles_end = jnp.cumsum(tiles_e)
    tile_start = tiles_end - tiles_e
    eid = jnp.arange(N_EXPERTS, dtype=jnp.int32)
    tok_oh = route_i[0:2][:, :, None] == eid
    pos = jnp.sum(jnp.where(tok_oh, tile_start * TM, 0), axis=-1) + route_i[2:4]
    tile_ids = jnp.arange(n_tiles, dtype=jnp.int32)
    n_used = tiles_end[-1]
    te = jnp.sum(tiles_end[None, :] <= jnp.minimum(tile_ids, n_used - 1)[:, None], axis=1).astype(jnp.int32)
    tile_oh = te[:, None] == eid
    cnt_t = jnp.sum(jnp.where(tile_oh, counts, 0), axis=1)
    start_t = jnp.sum(jnp.where(tile_oh, tile_start, 0), axis=1)
    tv = jnp.clip(cnt_t - (tile_ids - start_t) * TM, 0, TM)
    tv = jnp.where(tile_ids < n_used, tv, 0).astype(jnp.int32)
    return pos.astype(jnp.int32), te, tv, n_tiles


SC_CORES = 2
SC_SUBCORES = 16
SC_WORKERS = SC_CORES * SC_SUBCORES
SC_CHUNK = 32


def _sc_mesh():
    return plsc.VectorSubcoreMesh(core_axis_name="c", subcore_axis_name="s")


def _sc_worker_indices(pos, n_tok):
    n_ch = n_tok // SC_WORKERS // SC_CHUNK
    return pos.reshape(2, SC_WORKERS, n_ch, SC_CHUNK).transpose(1, 2, 0, 3), n_ch


def _sc_dispatch(h, pos, n_slots):
    n_tok = h.shape[0]
    per_w = n_tok // SC_WORKERS
    pos_w, n_ch = _sc_worker_indices(pos, n_tok)
    assert n_ch % 2 == 0 and n_ch * SC_CHUNK * SC_WORKERS == n_tok

    @functools.partial(
        pl.kernel, mesh=_sc_mesh(), out_type=jax.ShapeDtypeStruct((n_slots, D_MODEL), F32),
        scratch_types=[pltpu.VMEM((n_ch, 2, SC_CHUNK), jnp.int32), pltpu.VMEM((2, SC_CHUNK, D_MODEL), F32),
                       pltpu.SemaphoreType.DMA((2,)), pltpu.SemaphoreType.DMA((2,))],
        name="sc_dispatch")
    def dispatch(h_hbm, pos_hbm, xs_hbm, idx_v, rows_v, load_sem, scat_sem):
        wid = lax.axis_index("s") * SC_CORES + lax.axis_index("c")
        base = wid * per_w
        pltpu.sync_copy(pos_hbm.at[wid], idx_v)

        def load(c, b):
            return pltpu.make_async_copy(h_hbm.at[pl.ds(base + c * SC_CHUNK, SC_CHUNK)], rows_v.at[b],
                                         load_sem.at[b])

        def scat(c, b, k):
            return pltpu.make_async_copy(rows_v.at[b], xs_hbm.at[idx_v.at[c, k]], scat_sem.at[b])

        load(0, 0).start()

        @pl.loop(0, n_ch, step=2)
        def _(c0):
            for b in range(2):
                c = c0 + b
                load(c, b).wait()
                scat(c, b, 0).start()
                scat(c, b, 1).start()

                @pl.when(c >= 1)
                def _():
                    scat(c - 1, 1 - b, 0).wait()
                    scat(c - 1, 1 - b, 1).wait()

                @pl.when(c + 1 < n_ch)
                def _():
                    load(c + 1, 1 - b).start()

        scat(n_ch - 1, 1, 0).wait()
        scat(n_ch - 1, 1, 1).wait()

    return dispatch(h, pos_w)


def _sc_combine_gather(ys, pos):
    n_tok = pos.shape[1]
    per_w = n_tok // SC_WORKERS
    pos_w, n_ch = _sc_worker_indices(pos, n_tok)
    assert n_ch * SC_CHUNK * SC_WORKERS == n_tok

    @functools.partial(
        pl.kernel, mesh=_sc_mesh(), out_type=jax.ShapeDtypeStruct((2, n_tok, D_MODEL), F32),
        scratch_types=[pltpu.VMEM((n_ch, 2, SC_CHUNK), jnp.int32), pltpu.VMEM((2, SC_CHUNK, D_MODEL), F32),
                       pltpu.SemaphoreType.DMA((2,)), pltpu.SemaphoreType.DMA((2,))],
        name="sc_combine_gather")
    def gather(ys_hbm, pos_hbm, yg_hbm, idx_v, rows_v, gath_sem, write_sem):
        wid = lax.axis_index("s") * SC_CORES + lax.axis_index("c")
        base = wid * per_w
        pltpu.sync_copy(pos_hbm.at[wid], idx_v)

        def gath(c, k):
            return pltpu.make_async_copy(ys_hbm.at[idx_v.at[c, k]], rows_v.at[k], gath_sem.at[k])

        def write(c, k):
            return pltpu.make_async_copy(rows_v.at[k], yg_hbm.at[k, pl.ds(base + c * SC_CHUNK, SC_CHUNK)],
                                         write_sem.at[k])

        gath(0, 0).start()

        @pl.loop(0, n_ch)
        def _(c):
            gath(c, 0).wait()
            write(c, 0).start()

            @pl.when(c >= 1)
            def _():
                write(c - 1, 1).wait()

            gath(c, 1).start()
            gath(c, 1).wait()
            write(c, 1).start()
            write(c, 0).wait()

            @pl.when(c + 1 < n_ch)
            def _():
                gath(c + 1, 0).start()

        write(n_ch - 1, 1).wait()

    return gather(ys, pos_w)


def _combine_kernel(x1_ref, yg_ref, gt_ref, mod_ref, fg_ref, x2_ref, *, final):
    g = gt_ref[...]
    y = g[:, 0:1] * yg_ref[0] + g[:, 1:2] * yg_ref[1]
    x2 = x1_ref[...] + mod_ref[0][5:6] * y
    if final:
        ms = jnp.mean(x2 * x2, axis=-1, keepdims=True)
        x2 = x2 * lax.rsqrt(ms + NORM_EPS) * fg_ref[...]
    x2_ref[...] = x2


def _combine(x1, yg, gates_t, mod_l, final_g, n_tok, *, final):
    tok = lambda i: (i, 0)
    return pl.pallas_call(
        functools.partial(_combine_kernel, final=final),
        grid=(n_tok // TM,),
        in_specs=[
            pl.BlockSpec((TM, D_MODEL), tok),
            pl.BlockSpec((2, TM, D_MODEL), lambda i: (0, i, 0)),
            pl.BlockSpec((TM, 2), tok),
            pl.BlockSpec((1, N_MOD, D_MODEL), lambda i: (_mod_row(i), 0, 0)),
            pl.BlockSpec((1, D_MODEL), lambda i: (0, 0)),
        ],
        out_specs=pl.BlockSpec((TM, D_MODEL), tok),
        out_shape=jax.ShapeDtypeStruct((n_tok, D_MODEL), F32),
        compiler_params=_cparams("parallel"),
        name="moe_combine_final" if final else "moe_combine",
    )(x1, yg, gates_t, mod_l, final_g.reshape(1, D_MODEL))


def _diff_lambda_init(layer_idx):
    return 0.8 - 0.6 * math.exp(-0.3 * (layer_idx - 1))


def kernel(x, c, ctx, c_ctx, mod_w, mod_b, norm_mix, norm_ffn, w_qkv, w_o, na_rpb, diff_lambda, diff_subln,
           router_w, router_b, expert_w_gate, expert_w_up, expert_w_down, final_norm):
    mod = _modulation(c, c_ctx, mod_w, mod_b)
    xa = jnp.concatenate([x.reshape(T_LAT, D_MODEL), ctx.reshape(T_CTX, D_MODEL)], axis=0)
    w_qkv_b = w_qkv.astype(BF16)
    w_o_b = w_o.astype(BF16)
    cs, sn = _rope_tables()
    rw_pad = jnp.zeros((D_MODEL, HB), F32).at[:, :N_EXPERTS].set(router_w)
    rb_col = router_b.reshape(N_EXPERTS, 1).astype(F32)
    tri = (jnp.arange(TM)[:, None] < jnp.arange(TM)[None, :]).astype(BF16)

    for i in range(DEPTH):
        last = i == DEPTH - 1
        j = i // 2
        is_diff = i % 2 == 1
        q, k, v = _qkv(xa, mod[i], norm_mix[i], w_qkv_b[i], cs, sn, rope=is_diff)
        if not is_diff:
            o, o_ctx = _na_attention(_na_pair_table(na_rpb[j]), q, k, v)
        else:
            lam_init = _diff_lambda_init(i + 1)
            sg = diff_subln[j].reshape(HB, 1)
            outs = _diff_attention(diff_lambda[j], sg, q, k, v, lam_init, ctx_out=not last)
            o, o_ctx = outs if not last else (outs[0], None)
        n_tok = T_LAT if last else T_ALL
        x1, h, route_i, route_f, cnt = _oproj_route(o, o_ctx, xa, mod[i], w_o_b[i], norm_ffn[i], rw_pad,
                                                    rb_col, tri, n_tok)
        pos, te, tv, n_tiles = _moe_layout(route_i, cnt[:, 0], n_tok)
        xs = _sc_dispatch(h, pos, n_tiles * TM)
        ys = _grouped_ffn(te, tv, xs, expert_w_gate, expert_w_up, expert_w_down, i)
        yg = _sc_combine_gather(ys, pos)
        xa = _combine(x1, yg, route_f[0:2].T, mod[i], final_norm, n_tok, final=last)
    return xa.reshape(BATCH, SEQ, D_MODEL)
```

```python
import functools
import math

import jax
import jax.numpy as jnp
from jax import lax
from jax.experimental import pallas as pl
from jax.experimental.pallas import tpu as pltpu
from jax.experimental.pallas import tpu_sc as plsc

F32 = jnp.float32
BF16 = jnp.bfloat16

D_MODEL = 1024
BATCH = 8
SEQ = 2048
DEPTH = 4
CTX_LEN = 256
GRID_W = 64
ROWS = SEQ // GRID_W
NA_HEADS = 16
NA_WIN_ROWS = 8
NA_WIN_COLS = 16
DIFF_HEADS = 8
DIFF_HEAD_DIM = 64
ROPE_THETA = 10000.0
N_EXPERTS = 16
N_GROUPS = 4
EXPERTS_PER_GROUP = 4
D_EXPERT = 1024
N_MOD = 6
NORM_EPS = 1e-6

T_LAT = BATCH * SEQ
T_CTX = BATCH * CTX_LEN
T_ALL = T_LAT + T_CTX
TM = 512
LAT_TILES = T_LAT // TM
CTX_TILES = T_CTX // TM
TILES_PER_BATCH = SEQ // TM
CTX_ROW = BATCH
MOD_ROWS = 16
HB = 128
N_HB = D_MODEL // HB
NA_RB = 4
NA_BAND_ROWS = 12
NA_Q = NA_RB * GRID_W
NA_BAND = NA_BAND_ROWS * GRID_W
TQ_DIFF = 512
MASK_VALUE = -1e30
LOG2E = math.log2(math.e)
Q_SCALE = 0.125 * LOG2E
VMEM_LIMIT = 56 * 1024 * 1024


def _cparams(*sem):
    return pltpu.CompilerParams(dimension_semantics=sem, vmem_limit_bytes=VMEM_LIMIT)


def _dot(a, b):
    return jnp.dot(a, b, preferred_element_type=F32)


def _dot_nt(a, b):
    return lax.dot_general(a, b, (((1,), (1,)), ((), ())), preferred_element_type=F32)


def _sigmoid(x):
    return 1.0 / (1.0 + jnp.exp(-x))


D_PACK = D_MODEL // 2


def _pack_bf16_pair(x):
    lo = lax.bitcast_convert_type(x[:, :D_PACK].astype(BF16).astype(F32), jnp.uint32)
    hi = lax.bitcast_convert_type(x[:, D_PACK:].astype(BF16).astype(F32), jnp.uint32)
    return (lo >> 16) | hi


def _unpack_bf16_pair(w):
    lo = lax.bitcast_convert_type(w << 16, F32)
    hi = lax.bitcast_convert_type(w & jnp.uint32(0xFFFF0000), F32)
    return jnp.concatenate([lo, hi], axis=1)


def _mod_row(i):
    return jnp.minimum(i // TILES_PER_BATCH, CTX_ROW)


def _mod_kernel(act_ref, w_ref, b_ref, o_ref):
    a = act_ref[...]
    a = a * _sigmoid(a)
    o_ref[0] = _dot(a.astype(BF16), w_ref[0].astype(BF16)) + b_ref[0]


def _modulation(c, c_ctx, mod_w, mod_b):
    tn = 1536
    act = jnp.zeros((MOD_ROWS, D_MODEL), F32).at[:BATCH].set(c).at[CTX_ROW].set(c_ctx)
    out = pl.pallas_call(
        _mod_kernel,
        grid=(DEPTH, N_MOD * D_MODEL // tn),
        in_specs=[
            pl.BlockSpec((MOD_ROWS, D_MODEL), lambda l, j: (0, 0)),
            pl.BlockSpec((1, D_MODEL, tn), lambda l, j: (l, 0, j)),
            pl.BlockSpec((1, 1, tn), lambda l, j: (l, 0, j)),
        ],
        out_specs=pl.BlockSpec((1, MOD_ROWS, tn), lambda l, j: (l, 0, j)),
        out_shape=jax.ShapeDtypeStruct((DEPTH, MOD_ROWS, N_MOD * D_MODEL), F32),
        compiler_params=_cparams("parallel", "parallel"),
        name="adaln_mod",
    )(act, mod_w, mod_b.reshape(DEPTH, 1, N_MOD * D_MODEL))
    return out.reshape(DEPTH, MOD_ROWS, N_MOD, D_MODEL)


def _rms_modulate(x, g, shift, scale):
    ms = jnp.mean(x * x, axis=-1, keepdims=True)
    y = x * lax.rsqrt(ms + NORM_EPS) * g
    return y * (1.0 + scale) + shift


def _rope_block(xb, cs, sn):
    lane = lax.broadcasted_iota(jnp.int32, xb.shape, 1)
    partner = jnp.where((lane & 63) < 32, pltpu.roll(xb, 96, 1), pltpu.roll(xb, 32, 1))
    return xb * cs + partner * sn


def _qkv_kernel(x_ref, mod_ref, g_ref, wqk_ref, wvt_ref, cs_ref, sn_ref, q_ref, k_ref, vt_ref, *, rope):
    m = mod_ref[0]
    h = _rms_modulate(x_ref[...], g_ref[...], m[0:1], m[1:2]).astype(BF16)
    for idx, out in enumerate((q_ref, k_ref)):
        acc = _dot(h, wqk_ref[:, idx * D_MODEL:(idx + 1) * D_MODEL])
        if rope:
            cs = cs_ref[...]
            sn = sn_ref[...]
            acc = jnp.concatenate(
                [_rope_block(acc[:, j * HB:(j + 1) * HB], cs, sn) for j in range(N_HB)], axis=1)
        if idx == 0:
            acc = acc * Q_SCALE
        out[...] = acc.astype(BF16)
    vt_ref[...] = _dot_nt(wvt_ref[...], h).astype(BF16)


def _qkv(x, mod_l, g, w, cs, sn, *, rope):
    n_tiles = T_ALL // TM
    tok = lambda i: (i, 0)
    const2 = lambda i: (0, 0)
    rope_idx = lambda i: (jnp.where(i < LAT_TILES, i % TILES_PER_BATCH, TILES_PER_BATCH), 0)
    out = jax.ShapeDtypeStruct((T_ALL, D_MODEL), BF16)
    w_qk = w[:, :2 * D_MODEL]
    w_v = w[:, 2 * D_MODEL:].T
    v_spec = pl.BlockSpec((D_MODEL, TM), lambda i: (0, i))
    v_shape = jax.ShapeDtypeStruct((D_MODEL, T_ALL), BF16)
    return pl.pallas_call(
        functools.partial(_qkv_kernel, rope=rope),
        grid=(n_tiles,),
        in_specs=[
            pl.BlockSpec((TM, D_MODEL), tok),
            pl.BlockSpec((1, N_MOD, D_MODEL), lambda i: (_mod_row(i), 0, 0)),
            pl.BlockSpec((1, D_MODEL), const2),
            pl.BlockSpec((D_MODEL, 2 * D_MODEL), const2),
            pl.BlockSpec((D_MODEL, D_MODEL), const2),
            pl.BlockSpec((TM, HB), rope_idx),
            pl.BlockSpec((TM, HB), rope_idx),
        ],
        out_specs=[pl.BlockSpec((TM, D_MODEL), tok), pl.BlockSpec((TM, D_MODEL), tok), v_spec],
        out_shape=[out, out, v_shape],
        compiler_params=_cparams("parallel"),
        name="norm_qkv_rope" if rope else "norm_qkv",
    )(x, mod_l, g.reshape(1, D_MODEL), w_qk, w_v, cs, sn)


def _rope_tables():
    t = jnp.arange(SEQ)
    row = (t // GRID_W).astype(F32)
    col = (t % GRID_W).astype(F32)
    n_freq = DIFF_HEAD_DIM // 4
    inv_freq = ROPE_THETA ** (-jnp.arange(n_freq, dtype=F32) / n_freq)
    ang = jnp.concatenate([row[:, None] * inv_freq, col[:, None] * inv_freq], axis=-1)
    cos, sin = jnp.cos(ang), jnp.sin(ang)
    cs = jnp.concatenate([cos, cos, cos, cos], axis=-1)
    sn = jnp.concatenate([-sin, sin, -sin, sin], axis=-1)
    cs = jnp.concatenate([cs, jnp.ones((TM, HB), F32)], axis=0)
    sn = jnp.concatenate([sn, jnp.zeros((TM, HB), F32)], axis=0)
    return cs, sn


Q_BLK = 256
KEY_BLK = 256
SUBLANES = 8
DIFF_Q_PER_TRIP = 2


def _colwise(reduce_fn, x):
    return reduce_fn(x.reshape(x.shape[0] // SUBLANES, SUBLANES, x.shape[1]), axis=0)


def _key_blocks(k_ref, vt_ref, key_start=0, n_keys=None, bias=None):
    n_keys = k_ref.shape[0] if n_keys is None else n_keys
    blocks = []
    for c, s0 in enumerate(range(0, n_keys, KEY_BLK)):
        blocks.append((
            lambda s0=s0: k_ref[pl.ds(key_start + s0, KEY_BLK), :],
            lambda s0=s0: vt_ref[:, pl.ds(key_start + s0, KEY_BLK)],
            None if bias is None else functools.partial(bias, c),
        ))
    return blocks


def _attention_t(operands, s_scr):
    def score(u, c, mx):
        qm, blocks = operands[u]
        k_tile, _, bias = blocks[c]
        st = _dot_nt(k_tile(), qm)
        if bias is not None:
            st = st + bias()
        s_scr[u % 2, c * KEY_BLK:(c + 1) * KEY_BLK, :] = st
        return jnp.maximum(mx, _colwise(jnp.max, st))

    def expo(u, c, m, acc, den):
        _, vt_tile, _ = operands[u][1][c]
        e = jnp.exp2(s_scr[u % 2, c * KEY_BLK:(c + 1) * KEY_BLK, :] - m)
        o = _dot(vt_tile(), e.astype(BF16))
        return (o if acc is None else acc + o), den + _colwise(jnp.sum, e)

    neg = jnp.full((SUBLANES, Q_BLK), -jnp.inf, F32)
    zero = jnp.zeros((SUBLANES, Q_BLK), F32)
    n_blocks = [len(blocks) for _, blocks in operands]
    mx = neg
    for c in range(n_blocks[0]):
        mx = score(0, c, mx)
    outs = []
    for u in range(len(operands)):
        m = jnp.max(mx, axis=0, keepdims=True)
        acc, den, mx = None, zero, neg
        n_next = n_blocks[u + 1] if u + 1 < len(operands) else 0
        for c in range(max(n_blocks[u], n_next)):
            if c < n_next:
                mx = score(u + 1, c, mx)
            if c < n_blocks[u]:
                acc, den = expo(u, c, m, acc, den)
        outs.append(acc / jnp.sum(den, axis=0, keepdims=True))
    return outs


def _half_masks(shape):
    lane = lax.broadcasted_iota(jnp.int32, shape, 1)
    return lane < 64, lane >= 64


N_DR = 2 * NA_WIN_ROWS - 1
N_DC = 2 * NA_WIN_COLS - 1
PAIR_MASK_FIRST = N_DR - 1
PAIR_MASK_SECOND = N_DR
PAIR_MASKED = N_DR + 1
N_PAIRS = N_DR + 2
NA_Q_PER_TRIP = 2
DR_LO = NA_WIN_ROWS - 1 - NA_WIN_ROWS // 2
DR_HI = DR_LO + NA_WIN_ROWS - 1


def _na_pair_table(rpb):
    h = rpb.shape[0]
    w = GRID_W
    v = jnp.zeros((h, N_DR, 2 * w), F32)
    v = v.at[..., 0:NA_WIN_COLS].set(rpb[..., NA_WIN_COLS - 1:].astype(F32))
    v = v.at[..., 2 * w - (NA_WIN_COLS - 1):].set(rpb[..., :NA_WIN_COLS - 1].astype(F32))
    y = jnp.broadcast_to(v[:, :, None, :], (h, N_DR, w, 2 * w)).reshape(h, N_DR, w * 2 * w)
    t = y[..., :w * (2 * w - 1)].reshape(h, N_DR, w, 2 * w - 1)[..., :w]
    col = jnp.arange(w)
    cstart = jnp.clip(col - NA_WIN_COLS // 2, 0, w - NA_WIN_COLS)
    cvalid = (col[None, :] >= cstart[:, None]) & (col[None, :] < cstart[:, None] + NA_WIN_COLS)
    tt = jnp.swapaxes(jnp.where(cvalid, t * LOG2E, MASK_VALUE), -1, -2)
    masked = jnp.full((h, 1, w, w), MASK_VALUE, F32)
    regular = jnp.concatenate([tt[:, 1:], tt[:, :-1]], axis=-1)
    first = jnp.concatenate([masked, tt[:, DR_HI:DR_HI + 1]], axis=-1)
    second = jnp.concatenate([tt[:, DR_LO:DR_LO + 1], masked], axis=-1)
    return jnp.concatenate([regular, first, second, jnp.concatenate([masked, masked], axis=-1)], axis=1)


def _na_pair_entries(q_row0, band_row0, band_rows):
    entries = []
    for j in range(band_rows):
        kr = band_row0 + j
        row = []
        for a in range(0, NA_RB, 2):
            ok, d = [], []
            for r in (q_row0 + a, q_row0 + a + 1):
                r0 = min(max(r - NA_WIN_ROWS // 2, 0), ROWS - NA_WIN_ROWS)
                ok.append(r0 <= kr < r0 + NA_WIN_ROWS)
                d.append(kr - r + NA_WIN_ROWS - 1)
            if ok[0] and ok[1]:
                row.append(d[0] - 1)
            elif ok[1]:
                assert d[1] == DR_HI
                row.append(PAIR_MASK_FIRST)
            elif ok[0]:
                assert d[0] == DR_LO
                row.append(PAIR_MASK_SECOND)
            else:
                row.append(PAIR_MASKED)
        entries.append(row)
    return entries


def _na_kernel(tab_ref, q_ref, k_ref, vt_ref, qc_ref, kc_ref, vct_ref, o_ref, oc_ref, s_scr):
    n_blocks = ROWS // NA_RB
    rows_per_chunk = KEY_BLK // GRID_W
    ctx_blocks = _key_blocks(kc_ref, vct_ref)

    def operands(q_start, band_start, band_rows, entries):
        q = q_ref[pl.ds(q_start, NA_Q), :]
        ops = []
        for hh, msk in enumerate(_half_masks(q.shape)):
            def bias(c, hh=hh):
                rows = entries[c * rows_per_chunk:(c + 1) * rows_per_chunk]
                return jnp.concatenate(
                    [jnp.concatenate([tab_ref[hh, e] for e in row], axis=1) for row in rows], axis=0)

            band = _key_blocks(k_ref, vt_ref, band_start, band_rows * GRID_W, bias)
            ops.append((jnp.where(msk, q, jnp.zeros_like(q)), band + ctx_blocks))
        return ops

    def merge_heads(o_lo, o_hi):
        feat = lax.broadcasted_iota(jnp.int32, o_lo.shape, 0)
        return jnp.where(feat < HB // 2, o_lo, o_hi).T.astype(BF16)

    def run(blocks):
        outs = _attention_t([op for blk in blocks for op in operands(*blk)], s_scr)
        for i, blk in enumerate(blocks):
            o_ref[pl.ds(blk[0], NA_Q), :] = merge_heads(outs[2 * i], outs[2 * i + 1])

    last_band = ROWS - NA_WIN_ROWS
    run([(0, 0, NA_WIN_ROWS, _na_pair_entries(0, 0, NA_WIN_ROWS)),
         ((n_blocks - 1) * NA_Q, last_band * GRID_W, NA_WIN_ROWS,
          _na_pair_entries(ROWS - NA_RB, last_band, NA_WIN_ROWS))])
    interior = _na_pair_entries(NA_RB, 0, NA_BAND_ROWS)

    def body(t, carry):
        blocks = []
        for i in range(NA_Q_PER_TRIP):
            rb = 1 + t * NA_Q_PER_TRIP + i
            band_start = pl.multiple_of((rb - 1) * NA_Q, NA_Q)
            blocks.append((pl.multiple_of(rb * NA_Q, NA_Q), band_start, NA_BAND_ROWS, interior))
        run(blocks)
        return carry

    lax.fori_loop(0, (n_blocks - 2) // NA_Q_PER_TRIP, body, 0)

    qc = qc_ref[...]
    outs = _attention_t([(jnp.where(msk, qc, jnp.zeros_like(qc)), ctx_blocks) for msk in _half_masks(qc.shape)],
                        s_scr)
    oc_ref[...] = merge_heads(outs[0], outs[1])


def _na_attention(table, q, k, vt):
    assert NA_Q == Q_BLK and CTX_LEN == Q_BLK and (ROWS // NA_RB - 2) % NA_Q_PER_TRIP == 0
    lat = lambda hp, b: (b, hp)
    ctx = lambda hp, b: (T_LAT // CTX_LEN + b, hp)
    return pl.pallas_call(
        _na_kernel,
        grid=(N_HB, BATCH),
        in_specs=[
            pl.BlockSpec((2, N_PAIRS, GRID_W, 2 * GRID_W), lambda hp, b: (hp, 0, 0, 0)),
            pl.BlockSpec((SEQ, HB), lat),
            pl.BlockSpec((SEQ, HB), lat),
            pl.BlockSpec((HB, SEQ), lambda hp, b: (hp, b)),
            pl.BlockSpec((CTX_LEN, HB), ctx),
            pl.BlockSpec((CTX_LEN, HB), ctx),
            pl.BlockSpec((HB, CTX_LEN), lambda hp, b: (hp, T_LAT // CTX_LEN + b)),
        ],
        out_specs=[pl.BlockSpec((SEQ, HB), lat), pl.BlockSpec((CTX_LEN, HB), lat)],
        out_shape=[jax.ShapeDtypeStruct((T_LAT, D_MODEL), BF16), jax.ShapeDtypeStruct((T_CTX, D_MODEL), BF16)],
        scratch_shapes=[pltpu.VMEM((2, NA_BAND + CTX_LEN, Q_BLK), F32)],
        compiler_params=_cparams("parallel", "parallel"),
        name="na_attention",
    )(table, q, k, vt, q, k, vt)


def _diff_lambda(lam_ref, lam_init):
    lam = lam_ref[...]
    a = jnp.sum(lam[0:1] * lam[1:2], axis=-1, keepdims=True)
    b = jnp.sum(lam[2:3] * lam[3:4], axis=-1, keepdims=True)
    return jnp.exp(a) - jnp.exp(b) + lam_init


def _diff_kernel(lam_ref, sg_ref, q_ref, k_ref, vt_ref, qc_ref, kc_ref, vct_ref, o_ref, *rest, lam_init):
    *maybe_oc_ref, s_scr = rest
    lam = _diff_lambda(lam_ref, lam_init)
    sg_col = sg_ref[...]

    def rows(qs, pieces):
        blocks = [blk for k_r, vt_r in pieces for blk in _key_blocks(k_r, vt_r)]
        outs = _attention_t([(jnp.where(msk, q, jnp.zeros_like(q)), blocks)
                             for q in qs for msk in _half_masks(q.shape)], s_scr)
        res = []
        for o1, o2 in zip(outs[0::2], outs[1::2]):
            ot = o1 - lam * o2
            ms = jnp.mean(ot * ot, axis=0, keepdims=True)
            ot = ot * lax.rsqrt(ms + NORM_EPS) * sg_col * (1.0 - lam_init)
            res.append(ot.T.astype(BF16))
        return res

    def body(t, carry):
        starts = [pl.multiple_of((t * DIFF_Q_PER_TRIP + j) * Q_BLK, Q_BLK) for j in range(DIFF_Q_PER_TRIP)]
        outs = rows([q_ref[pl.ds(r0, Q_BLK), :] for r0 in starts], [(k_ref, vt_ref), (kc_ref, vct_ref)])
        for r0, o in zip(starts, outs):
            o_ref[pl.ds(r0, Q_BLK), :] = o
        return carry

    lax.fori_loop(0, SEQ // (Q_BLK * DIFF_Q_PER_TRIP), body, 0)
    if maybe_oc_ref:
        oc_ref, = maybe_oc_ref
        assert CTX_LEN == Q_BLK
        oc_ref[...], = rows([qc_ref[...]], [(kc_ref, vct_ref)])


def _diff_attention(lam, sg, q, k, vt, lam_init, *, ctx_out):
    lat = lambda b, h: (b, h)
    ctx = lambda b, h: (T_LAT // CTX_LEN + b, h)
    lat_t = lambda b, h: (h, b)
    ctx_t = lambda b, h: (h, T_LAT // CTX_LEN + b)
    out_specs = [pl.BlockSpec((SEQ, HB), lat)]
    out_shape = [jax.ShapeDtypeStruct((T_LAT, D_MODEL), BF16)]
    if ctx_out:
        out_specs.append(pl.BlockSpec((CTX_LEN, HB), lat))
        out_shape.append(jax.ShapeDtypeStruct((T_CTX, D_MODEL), BF16))
    return pl.pallas_call(
        functools.partial(_diff_kernel, lam_init=lam_init),
        grid=(BATCH, DIFF_HEADS),
        in_specs=[
            pl.BlockSpec(lam.shape, lambda b, h: (0, 0)),
            pl.BlockSpec(sg.shape, lambda b, h: (0, 0)),
            pl.BlockSpec((SEQ, HB), lat),
            pl.BlockSpec((SEQ, HB), lat),
            pl.BlockSpec((HB, SEQ), lat_t),
            pl.BlockSpec((CTX_LEN, HB), ctx),
            pl.BlockSpec((CTX_LEN, HB), ctx),
            pl.BlockSpec((HB, CTX_LEN), ctx_t),
        ],
        out_specs=out_specs,
        out_shape=out_shape,
        scratch_shapes=[pltpu.VMEM((2, SEQ + CTX_LEN, Q_BLK), F32)],
        compiler_params=_cparams("parallel", "parallel"),
        name="diff_attention",
    )(lam, sg, q, k, vt, q, k, vt)


def _route_rows(logits_t, bias_col):
    s = _sigmoid(logits_t)
    sel = s + bias_col
    sel_r = [sel[e:e + 1] for e in range(N_EXPERTS)]
    s_r = [s[e:e + 1] for e in range(N_EXPERTS)]
    group_scores = []
    for g in range(N_GROUPS):
        v = sel_r[g * EXPERTS_PER_GROUP:(g + 1) * EXPERTS_PER_GROUP]
        pairs = [v[a] + v[b] for a in range(EXPERTS_PER_GROUP) for b in range(a + 1, EXPERTS_PER_GROUP)]
        group_scores.append(functools.reduce(jnp.maximum, pairs))
    best = group_scores[0]
    grp = jnp.zeros(best.shape, jnp.int32)
    for g in range(1, N_GROUPS):
        upd = group_scores[g] > best
        best = jnp.where(upd, group_scores[g], best)
        grp = jnp.where(upd, g, grp)

    def pick(rows, j):
        out = rows[j]
        for g in range(1, N_GROUPS):
            out = jnp.where(grp == g, rows[g * EXPERTS_PER_GROUP + j], out)
        return out

    w = [pick(sel_r, j) for j in range(EXPERTS_PER_GROUP)]
    sc = [pick(s_r, j) for j in range(EXPERTS_PER_GROUP)]

    def argmax_first(vals):
        bv = vals[0]
        bi = jnp.zeros(bv.shape, jnp.int32)
        for j in range(1, len(vals)):
            upd = vals[j] > bv
            bv = jnp.where(upd, vals[j], bv)
            bi = jnp.where(upd, j, bi)
        return bi

    i1 = argmax_first(w)
    i2 = argmax_first([jnp.where(i1 == j, -jnp.inf, w[j]) for j in range(EXPERTS_PER_GROUP)])

    def take(vals, idx):
        out = vals[0]
        for j in range(1, len(vals)):
            out = jnp.where(idx == j, vals[j], out)
        return out

    s1 = take(sc, i1)
    s2 = take(sc, i2)
    den = s1 + s2
    return grp * EXPERTS_PER_GROUP + i1, grp * EXPERTS_PER_GROUP + i2, s1 / den, s2 / den


def _oproj_kernel(o_ref, oc_ref, x_ref, mod_ref, w_ref, g_ref, rw_ref, rb_ref, tri_ref,
                  x1_ref, h_ref, ri_ref, rf_ref, cnt_ref, carry_ref):
    i = pl.program_id(0)

    @pl.when(i == 0)
    def _():
        carry_ref[...] = jnp.zeros_like(carry_ref)

    m = mod_ref[0]
    o = jnp.where(i < LAT_TILES, o_ref[...], oc_ref[...])
    x1 = x_ref[...] + m[2:3] * _dot(o, w_ref[...])
    x1_ref[...] = x1
    h = _rms_modulate(x1, g_ref[...], m[3:4], m[4:5])
    hb = h.astype(BF16)
    h_ref[...] = _pack_bf16_pair(h)

    h_lo = (h - hb.astype(F32)).astype(BF16)
    rw = rw_ref[...]
    rw_hi = rw.astype(BF16)
    rw_lo = (rw - rw_hi.astype(F32)).astype(BF16)
    logits = _dot(hb, rw_hi) + _dot(h_lo, rw_hi) + _dot(hb, rw_lo)
    logits_t = logits.T[:N_EXPERTS]
    e0, e1, g0, g1 = _route_rows(logits_t, rb_ref[...])

    eiota = lax.broadcasted_iota(jnp.int32, (N_EXPERTS, TM), 0)
    oh0 = eiota == e0
    oh1 = eiota == e1
    oh = jnp.where(oh0 | oh1, 1.0, 0.0)
    before = _dot(oh.astype(BF16), tri_ref[...]) + carry_ref[...]
    rank0 = jnp.sum(jnp.where(oh0, before, 0.0), axis=0, keepdims=True)
    rank1 = jnp.sum(jnp.where(oh1, before, 0.0), axis=0, keepdims=True)
    carry_ref[...] = carry_ref[...] + jnp.sum(oh, axis=1, keepdims=True)

    ri_ref[...] = jnp.zeros_like(ri_ref)
    ri_ref[0:1, :] = e0
    ri_ref[1:2, :] = e1
    ri_ref[2:3, :] = rank0.astype(jnp.int32)
    ri_ref[3:4, :] = rank1.astype(jnp.int32)
    rf_ref[...] = jnp.zeros_like(rf_ref)
    rf_ref[0:1, :] = g0
    rf_ref[1:2, :] = g1
    cnt_ref[...] = jnp.broadcast_to(carry_ref[...], cnt_ref.shape)


def _oproj_route(o, o_ctx, x, mod_l, w_o, g, rw_pad, rb_col, tri, n_tok):
    n_tiles = n_tok // TM
    tok = lambda i: (i, 0)
    const2 = lambda i: (0, 0)
    lane_tok = lambda i: (0, i)
    return pl.pallas_call(
        _oproj_kernel,
        grid=(n_tiles,),
        in_specs=[
            pl.BlockSpec((TM, D_MODEL), lambda i: (jnp.minimum(i, LAT_TILES - 1), 0)),
            pl.BlockSpec((TM, D_MODEL), lambda i: (jnp.maximum(i - LAT_TILES, 0), 0)),
            pl.BlockSpec((TM, D_MODEL), tok),
            pl.BlockSpec((1, N_MOD, D_MODEL), lambda i: (_mod_row(i), 0, 0)),
            pl.BlockSpec((D_MODEL, D_MODEL), const2),
            pl.BlockSpec((1, D_MODEL), const2),
            pl.BlockSpec((D_MODEL, HB), const2),
            pl.BlockSpec((N_EXPERTS, 1), const2),
            pl.BlockSpec((TM, TM), const2),
        ],
        out_specs=[
            pl.BlockSpec((TM, D_MODEL), tok),
            pl.BlockSpec((TM, D_PACK), tok),
            pl.BlockSpec((8, TM), lane_tok),
            pl.BlockSpec((8, TM), lane_tok),
            pl.BlockSpec((N_EXPERTS, HB), const2),
        ],
        out_shape=[
            jax.ShapeDtypeStruct((n_tok, D_MODEL), F32),
            jax.ShapeDtypeStruct((n_tok, D_PACK), jnp.uint32),
            jax.ShapeDtypeStruct((8, n_tok), jnp.int32),
            jax.ShapeDtypeStruct((8, n_tok), F32),
            jax.ShapeDtypeStruct((N_EXPERTS, HB), F32),
        ],
        scratch_shapes=[pltpu.VMEM((N_EXPERTS, 1), F32)],
        compiler_params=_cparams("arbitrary"),
        name="oproj_norm_route",
    )(o, o if o_ctx is None else o_ctx, x, mod_l, w_o, g.reshape(1, D_MODEL), rw_pad, rb_col, tri)


def _moe_kernel(te_ref, tv_ref, xs_ref, wg_ref, wu_ref, wd_ref, ys_ref, wgb, wub, wdb):
    i = pl.program_id(0)
    valid = tv_ref[i]
    e = te_ref[i]
    prev = te_ref[jnp.maximum(i - 1, 0)]

    @pl.when((i == 0) | (e != prev))
    def _():
        wgb[...] = wg_ref[0, 0].astype(BF16)
        wub[...] = wu_ref[0, 0].astype(BF16)
        wdb[...] = wd_ref[0, 0].astype(BF16)

    for r0 in range(0, TM, TM // 2):
        rows = pl.ds(r0, TM // 2)

        @pl.when(valid > r0)
        def _():
            row = r0 + lax.broadcasted_iota(jnp.int32, (TM // 2, 1), 0)
            x = xs_ref[rows, :]
            x = _unpack_bf16_pair(jnp.where(row < valid, x, jnp.zeros_like(x))).astype(BF16)
            a = _dot(x, wgb[...])
            u = _dot(x, wub[...])
            hm = (a * _sigmoid(a) * u).astype(BF16)
            ys_ref[rows, :] = _pack_bf16_pair(_dot(hm, wdb[...]))

        @pl.when(valid <= r0)
        def _():
            ys_ref[rows, :] = jnp.zeros((TM // 2, D_PACK), jnp.uint32)


def _grouped_ffn(tile_expert, tile_valid, xs, w_gate, w_up, w_down, layer):
    n_tiles = xs.shape[0] // TM
    wmap = lambda i, te, tv: (layer, te[i], 0, 0)
    tok = lambda i, te, tv: (i, 0)
    return pl.pallas_call(
        _moe_kernel,
        grid_spec=pltpu.PrefetchScalarGridSpec(
            num_scalar_prefetch=2,
            grid=(n_tiles,),
            in_specs=[
                pl.BlockSpec((TM, D_PACK), tok),
                pl.BlockSpec((1, 1, D_MODEL, D_EXPERT), wmap),
                pl.BlockSpec((1, 1, D_MODEL, D_EXPERT), wmap),
                pl.BlockSpec((1, 1, D_EXPERT, D_MODEL), wmap),
            ],
            out_specs=pl.BlockSpec((TM, D_PACK), tok),
            scratch_shapes=[
                pltpu.VMEM((D_MODEL, D_EXPERT), BF16),
                pltpu.VMEM((D_MODEL, D_EXPERT), BF16),
                pltpu.VMEM((D_EXPERT, D_MODEL), BF16),
            ],
        ),
        out_shape=jax.ShapeDtypeStruct((xs.shape[0], D_PACK), jnp.uint32),
        compiler_params=_cparams("arbitrary"),
        name="grouped_ffn",
    )(tile_expert, tile_valid, xs, w_gate, w_up, w_down)


def _moe_layout(route_i, counts, n_tok):
    n_tiles = 2 * n_tok // TM + N_EXPERTS
    counts = counts.astype(jnp.int32)
    tiles_e = (counts + TM - 1) // TM
    tiles_end = jnp.cumsum(tiles_e)
    tile_start = tiles_end - tiles_e
    eid = jnp.arange(N_EXPERTS, dtype=jnp.int32)
    tok_oh = route_i[0:2][:, :, None] == eid
    pos = jnp.sum(jnp.where(tok_oh, tile_start * TM, 0), axis=-1) + route_i[2:4]
    tile_ids = jnp.arange(n_tiles, dtype=jnp.int32)
    n_used = tiles_end[-1]
    te = jnp.sum(tiles_end[None, :] <= jnp.minimum(tile_ids, n_used - 1)[:, None], axis=1).astype(jnp.int32)
    tile_oh = te[:, None] == eid
    cnt_t = jnp.sum(jnp.where(tile_oh, counts, 0), axis=1)
    start_t = jnp.sum(jnp.where(tile_oh, tile_start, 0), axis=1)
    tv = jnp.clip(cnt_t - (tile_ids - start_t) * TM, 0, TM)
    tv = jnp.where(tile_ids < n_used, tv, 0).astype(jnp.int32)
    return pos.astype(jnp.int32), te, tv, n_tiles


SC_CORES = 2
SC_SUBCORES = 16
SC_WORKERS = SC_CORES * SC_SUBCORES
SC_CHUNK = 32


def _sc_mesh():
    return plsc.VectorSubcoreMesh(core_axis_name="c", subcore_axis_name="s")


def _sc_worker_indices(pos, n_tok):
    n_ch = n_tok // SC_WORKERS // SC_CHUNK
    return pos.reshape(2, SC_WORKERS, n_ch, SC_CHUNK).transpose(1, 2, 0, 3), n_ch


def _sc_dispatch(h, pos, n_slots):
    n_tok = h.shape[0]
    per_w = n_tok // SC_WORKERS
    pos_w, n_ch = _sc_worker_indices(pos, n_tok)
    assert n_ch % 2 == 0 and n_ch * SC_CHUNK * SC_WORKERS == n_tok

    @functools.partial(
        pl.kernel, mesh=_sc_mesh(), out_type=jax.ShapeDtypeStruct((n_slots, h.shape[1]), h.dtype),
        scratch_types=[pltpu.VMEM((n_ch, 2, SC_CHUNK), jnp.int32), pltpu.VMEM((2, SC_CHUNK, h.shape[1]), h.dtype),
                       pltpu.SemaphoreType.DMA((2,)), pltpu.SemaphoreType.DMA((2,))],
        name="sc_dispatch")
    def dispatch(h_hbm, pos_hbm, xs_hbm, idx_v, rows_v, load_sem, scat_sem):
        wid = lax.axis_index("s") * SC_CORES + lax.axis_index("c")
        base = wid * per_w
        pltpu.sync_copy(pos_hbm.at[wid], idx_v)

        def load(c, b):
            return pltpu.make_async_copy(h_hbm.at[pl.ds(base + c * SC_CHUNK, SC_CHUNK)], rows_v.at[b],
                                         load_sem.at[b])

        def scat(c, b, k):
            return pltpu.make_async_copy(rows_v.at[b], xs_hbm.at[idx_v.at[c, k]], scat_sem.at[b])

        load(0, 0).start()

        @pl.loop(0, n_ch, step=2)
        def _(c0):
            for b in range(2):
                c = c0 + b
                load(c, b).wait()
                scat(c, b, 0).start()
                scat(c, b, 1).start()

                @pl.when(c >= 1)
                def _():
                    scat(c - 1, 1 - b, 0).wait()
                    scat(c - 1, 1 - b, 1).wait()

                @pl.when(c + 1 < n_ch)
                def _():
                    load(c + 1, 1 - b).start()

        scat(n_ch - 1, 1, 0).wait()
        scat(n_ch - 1, 1, 1).wait()

    return dispatch(h, pos_w)


def _sc_combine_gather(ys, pos):
    n_tok = pos.shape[1]
    per_w = n_tok // SC_WORKERS
    pos_w, n_ch = _sc_worker_indices(pos, n_tok)
    assert n_ch * SC_CHUNK * SC_WORKERS == n_tok

    @functools.partial(
        pl.kernel, mesh=_sc_mesh(), out_type=jax.ShapeDtypeStruct((2, n_tok, ys.shape[1]), ys.dtype),
        scratch_types=[pltpu.VMEM((n_ch, 2, SC_CHUNK), jnp.int32), pltpu.VMEM((2, SC_CHUNK, ys.shape[1]), ys.dtype),
                       pltpu.SemaphoreType.DMA((2,)), pltpu.SemaphoreType.DMA((2,))],
        name="sc_combine_gather")
    def gather(ys_hbm, pos_hbm, yg_hbm, idx_v, rows_v, gath_sem, write_sem):
        wid = lax.axis_index("s") * SC_CORES + lax.axis_index("c")
        base = wid * per_w
        pltpu.sync_copy(pos_hbm.at[wid], idx_v)

        def gath(c, k):
            return pltpu.make_async_copy(ys_hbm.at[idx_v.at[c, k]], rows_v.at[k], gath_sem.at[k])

        def write(c, k):
            return pltpu.make_async_copy(rows_v.at[k], yg_hbm.at[k, pl.ds(base + c * SC_CHUNK, SC_CHUNK)],
                                         write_sem.at[k])

        gath(0, 0).start()

        @pl.loop(0, n_ch)
        def _(c):
            gath(c, 0).wait()
            write(c, 0).start()

            @pl.when(c >= 1)
            def _():
                write(c - 1, 1).wait()

            gath(c, 1).start()
            gath(c, 1).wait()
            write(c, 1).start()
            write(c, 0).wait()

            @pl.when(c + 1 < n_ch)
            def _():
                gath(c + 1, 0).start()

        write(n_ch - 1, 1).wait()

    return gather(ys, pos_w)


def _combine_kernel(x1_ref, yg_ref, gt_ref, mod_ref, fg_ref, x2_ref, *, final):
    g = gt_ref[...]
    y = g[:, 0:1] * _unpack_bf16_pair(yg_ref[0]) + g[:, 1:2] * _unpack_bf16_pair(yg_ref[1])
    x2 = x1_ref[...] + mod_ref[0][5:6] * y
    if final:
        ms = jnp.mean(x2 * x2, axis=-1, keepdims=True)
        x2 = x2 * lax.rsqrt(ms + NORM_EPS) * fg_ref[...]
    x2_ref[...] = x2


def _combine(x1, yg, gates_t, mod_l, final_g, n_tok, *, final):
    tok = lambda i: (i, 0)
    return pl.pallas_call(
        functools.partial(_combine_kernel, final=final),
        grid=(n_tok // TM,),
        in_specs=[
            pl.BlockSpec((TM, D_MODEL), tok),
            pl.BlockSpec((2, TM, D_PACK), lambda i: (0, i, 0)),
            pl.BlockSpec((TM, 2), tok),
            pl.BlockSpec((1, N_MOD, D_MODEL), lambda i: (_mod_row(i), 0, 0)),
            pl.BlockSpec((1, D_MODEL), lambda i: (0, 0)),
        ],
        out_specs=pl.BlockSpec((TM, D_MODEL), tok),
        out_shape=jax.ShapeDtypeStruct((n_tok, D_MODEL), F32),
        compiler_params=_cparams("parallel"),
        name="moe_combine_final" if final else "moe_combine",
    )(x1, yg, gates_t, mod_l, final_g.reshape(1, D_MODEL))


def _diff_lambda_init(layer_idx):
    return 0.8 - 0.6 * math.exp(-0.3 * (layer_idx - 1))


def kernel(x, c, ctx, c_ctx, mod_w, mod_b, norm_mix, norm_ffn, w_qkv, w_o, na_rpb, diff_lambda, diff_subln,
           router_w, router_b, expert_w_gate, expert_w_up, expert_w_down, final_norm):
    mod = _modulation(c, c_ctx, mod_w, mod_b)
    xa = jnp.concatenate([x.reshape(T_LAT, D_MODEL), ctx.reshape(T_CTX, D_MODEL)], axis=0)
    w_qkv_b = w_qkv.astype(BF16)
    w_o_b = w_o.astype(BF16)
    cs, sn = _rope_tables()
    rw_pad = jnp.zeros((D_MODEL, HB), F32).at[:, :N_EXPERTS].set(router_w)
    rb_col = router_b.reshape(N_EXPERTS, 1).astype(F32)
    tri = (jnp.arange(TM)[:, None] < jnp.arange(TM)[None, :]).astype(BF16)

    for i in range(DEPTH):
        last = i == DEPTH - 1
        j = i // 2
        is_diff = i % 2 == 1
        q, k, v = _qkv(xa, mod[i], norm_mix[i], w_qkv_b[i], cs, sn, rope=is_diff)
        if not is_diff:
            o, o_ctx = _na_attention(_na_pair_table(na_rpb[j]), q, k, v)
        else:
            lam_init = _diff_lambda_init(i + 1)
            sg = diff_subln[j].reshape(HB, 1)
            outs = _diff_attention(diff_lambda[j], sg, q, k, v, lam_init, ctx_out=not last)
            o, o_ctx = outs if not last else (outs[0], None)
        n_tok = T_LAT if last else T_ALL
        x1, h, route_i, route_f, cnt = _oproj_route(o, o_ctx, xa, mod[i], w_o_b[i], norm_ffn[i], rw_pad,
                                                    rb_col, tri, n_tok)
        pos, te, tv, n_tiles = _moe_layout(route_i, cnt[:, 0], n_tok)
        xs = _sc_dispatch(h, pos, n_tiles * TM)
        ys = _grouped_ffn(te, tv, xs, expert_w_gate, expert_w_up, expert_w_down, i)
        yg = _sc_combine_gather(ys, pos)
        xa = _combine(x1, yg, route_f[0:2].T, mod[i], final_norm, n_tok, final=last)
    return xa.reshape(BATCH, SEQ, D_MODEL)
```

```python
import functools
import math

import jax
import jax.numpy as jnp
from jax import lax
from jax.experimental import pallas as pl
from jax.experimental.pallas import tpu as pltpu
from jax.experimental.pallas import tpu_sc as plsc

F32 = jnp.float32
BF16 = jnp.bfloat16

D_MODEL = 1024
BATCH = 8
SEQ = 2048
DEPTH = 4
CTX_LEN = 256
GRID_W = 64
ROWS = SEQ // GRID_W
NA_HEADS = 16
NA_WIN_ROWS = 8
NA_WIN_COLS = 16
DIFF_HEADS = 8
DIFF_HEAD_DIM = 64
ROPE_THETA = 10000.0
N_EXPERTS = 16
N_GROUPS = 4
EXPERTS_PER_GROUP = 4
D_EXPERT = 1024
N_MOD = 6
NORM_EPS = 1e-6

T_LAT = BATCH * SEQ
T_CTX = BATCH * CTX_LEN
T_ALL = T_LAT + T_CTX
TM = 512
ROUTE_BLK = TM // 2
LAT_TILES = T_LAT // TM
CTX_TILES = T_CTX // TM
TILES_PER_BATCH = SEQ // TM
CTX_ROW = BATCH
MOD_ROWS = 16
HB = 128
N_HB = D_MODEL // HB
NA_RB = 4
NA_BAND_ROWS = 12
NA_Q = NA_RB * GRID_W
NA_BAND = NA_BAND_ROWS * GRID_W
TQ_DIFF = 512
MASK_VALUE = -1e30
LOG2E = math.log2(math.e)
Q_SCALE = 0.125 * LOG2E
VMEM_LIMIT = 56 * 1024 * 1024


def _cparams(*sem):
    return pltpu.CompilerParams(dimension_semantics=sem, vmem_limit_bytes=VMEM_LIMIT)


def _dot(a, b):
    return jnp.dot(a, b, preferred_element_type=F32)


def _dot_nt(a, b):
    return lax.dot_general(a, b, (((1,), (1,)), ((), ())), preferred_element_type=F32)


def _sigmoid(x):
    return 1.0 / (1.0 + jnp.exp(-x))


D_PACK = D_MODEL // 2


def _pack_bf16_pair(x):
    lo = lax.bitcast_convert_type(x[:, :D_PACK].astype(BF16).astype(F32), jnp.uint32)
    hi = lax.bitcast_convert_type(x[:, D_PACK:].astype(BF16).astype(F32), jnp.uint32)
    return (lo >> 16) | hi


def _unpack_bf16_pair(w):
    lo = lax.bitcast_convert_type(w << 16, F32)
    hi = lax.bitcast_convert_type(w & jnp.uint32(0xFFFF0000), F32)
    return jnp.concatenate([lo, hi], axis=1)


def _mod_row(i):
    return jnp.minimum(i // TILES_PER_BATCH, CTX_ROW)


def _mod_kernel(act_ref, w_ref, b_ref, o_ref):
    a = act_ref[...]
    a = a * _sigmoid(a)
    o_ref[0] = _dot(a.astype(BF16), w_ref[0].astype(BF16)) + b_ref[0]


def _modulation(c, c_ctx, mod_w, mod_b):
    tn = 1536
    act = jnp.zeros((MOD_ROWS, D_MODEL), F32).at[:BATCH].set(c).at[CTX_ROW].set(c_ctx)
    out = pl.pallas_call(
        _mod_kernel,
        grid=(DEPTH, N_MOD * D_MODEL // tn),
        in_specs=[
            pl.BlockSpec((MOD_ROWS, D_MODEL), lambda l, j: (0, 0)),
            pl.BlockSpec((1, D_MODEL, tn), lambda l, j: (l, 0, j)),
            pl.BlockSpec((1, 1, tn), lambda l, j: (l, 0, j)),
        ],
        out_specs=pl.BlockSpec((1, MOD_ROWS, tn), lambda l, j: (l, 0, j)),
        out_shape=jax.ShapeDtypeStruct((DEPTH, MOD_ROWS, N_MOD * D_MODEL), F32),
        compiler_params=_cparams("parallel", "parallel"),
        name="adaln_mod",
    )(act, mod_w, mod_b.reshape(DEPTH, 1, N_MOD * D_MODEL))
    return out.reshape(DEPTH, MOD_ROWS, N_MOD, D_MODEL)


def _rms_modulate(x, g, shift, scale):
    ms = jnp.mean(x * x, axis=-1, keepdims=True)
    y = x * lax.rsqrt(ms + NORM_EPS) * g
    return y * (1.0 + scale) + shift


def _rope_block(xb, cs, sn):
    lane = lax.broadcasted_iota(jnp.int32, xb.shape, 1)
    partner = jnp.where((lane & 63) < 32, pltpu.roll(xb, 96, 1), pltpu.roll(xb, 32, 1))
    return xb * cs + partner * sn


def _moe_residual(x1, yg_ref, gt_ref, gate_ffn):
    g = gt_ref[...]
    y = g[:, 0:1] * _unpack_bf16_pair(yg_ref[0]) + g[:, 1:2] * _unpack_bf16_pair(yg_ref[1])
    return x1 + gate_ffn * y


def _qkv_kernel(*refs, rope, first):
    if first:
        xl_ref, xc_ref, *refs = refs
        x = jnp.where(pl.program_id(0) < LAT_TILES, xl_ref[...], xc_ref[...])
    else:
        x1_ref, yg_ref, gt_ref, modp_ref, *refs = refs
        x = _moe_residual(x1_ref[...], yg_ref, gt_ref, modp_ref[0][5:6])
    mod_ref, g_ref, wqk_ref, wvt_ref, cs_ref, sn_ref, x_ref, q_ref, k_ref, vt_ref = refs
    x_ref[...] = x
    m = mod_ref[0]
    h = _rms_modulate(x, g_ref[...], m[0:1], m[1:2]).astype(BF16)
    for idx, out in enumerate((q_ref, k_ref)):
        acc = _dot(h, wqk_ref[:, idx * D_MODEL:(idx + 1) * D_MODEL])
        if rope:
            cs = cs_ref[...]
            sn = sn_ref[...]
            acc = jnp.concatenate(
                [_rope_block(acc[:, j * HB:(j + 1) * HB], cs, sn) for j in range(N_HB)], axis=1)
        if idx == 0:
            acc = acc * Q_SCALE
        out[...] = acc.astype(BF16)
    vt_ref[...] = _dot_nt(wvt_ref[...], h).astype(BF16)


def _qkv(stream, mod_l, g, w, cs, sn, *, rope):
    first = len(stream) == 2
    n_tiles = T_ALL // TM
    tok = lambda i: (i, 0)
    const2 = lambda i: (0, 0)
    mod_spec = pl.BlockSpec((1, N_MOD, D_MODEL), lambda i: (_mod_row(i), 0, 0))
    rope_idx = lambda i: (jnp.where(i < LAT_TILES, i % TILES_PER_BATCH, TILES_PER_BATCH), 0)
    out = jax.ShapeDtypeStruct((T_ALL, D_MODEL), BF16)
    w_qk = w[:, :2 * D_MODEL]
    w_v = w[:, 2 * D_MODEL:].T
    if first:
        stream_specs = [
            pl.BlockSpec((TM, D_MODEL), lambda i: (jnp.minimum(i, LAT_TILES - 1), 0)),
            pl.BlockSpec((TM, D_MODEL), lambda i: (jnp.maximum(i - LAT_TILES, 0), 0)),
        ]
    else:
        stream_specs = [
            pl.BlockSpec((TM, D_MODEL), tok),
            pl.BlockSpec((2, TM, D_PACK), lambda i: (0, i, 0)),
            pl.BlockSpec((TM, 2), tok),
            mod_spec,
        ]
    return pl.pallas_call(
        functools.partial(_qkv_kernel, rope=rope, first=first),
        grid=(n_tiles,),
        in_specs=stream_specs + [
            mod_spec,
            pl.BlockSpec((1, D_MODEL), const2),
            pl.BlockSpec((D_MODEL, 2 * D_MODEL), const2),
            pl.BlockSpec((D_MODEL, D_MODEL), const2),
            pl.BlockSpec((TM, HB), rope_idx),
            pl.BlockSpec((TM, HB), rope_idx),
        ],
        out_specs=[pl.BlockSpec((TM, D_MODEL), tok), pl.BlockSpec((TM, D_MODEL), tok),
                   pl.BlockSpec((TM, D_MODEL), tok), pl.BlockSpec((D_MODEL, TM), lambda i: (0, i))],
        out_shape=[jax.ShapeDtypeStruct((T_ALL, D_MODEL), F32), out, out,
                   jax.ShapeDtypeStruct((D_MODEL, T_ALL), BF16)],
        compiler_params=_cparams("parallel"),
        name="norm_qkv_rope" if rope else "norm_qkv",
    )(*stream, mod_l, g.reshape(1, D_MODEL), w_qk, w_v, cs, sn)


def _rope_tables():
    t = jnp.arange(SEQ)
    row = (t // GRID_W).astype(F32)
    col = (t % GRID_W).astype(F32)
    n_freq = DIFF_HEAD_DIM // 4
    inv_freq = ROPE_THETA ** (-jnp.arange(n_freq, dtype=F32) / n_freq)
    ang = jnp.concatenate([row[:, None] * inv_freq, col[:, None] * inv_freq], axis=-1)
    cos, sin = jnp.cos(ang), jnp.sin(ang)
    cs = jnp.concatenate([cos, cos, cos, cos], axis=-1)
    sn = jnp.concatenate([-sin, sin, -sin, sin], axis=-1)
    cs = jnp.concatenate([cs, jnp.ones((TM, HB), F32)], axis=0)
    sn = jnp.concatenate([sn, jnp.zeros((TM, HB), F32)], axis=0)
    return cs, sn


Q_BLK = 256
KEY_BLK = 256
SUBLANES = 8
DIFF_Q_PER_TRIP = 2


def _colwise(reduce_fn, x):
    return reduce_fn(x.reshape(x.shape[0] // SUBLANES, SUBLANES, x.shape[1]), axis=0)


def _key_blocks(k_ref, vt_ref, key_start=0, n_keys=None, bias=None):
    n_keys = k_ref.shape[0] if n_keys is None else n_keys
    blocks = []
    for c, s0 in enumerate(range(0, n_keys, KEY_BLK)):
        blocks.append((
            lambda s0=s0: k_ref[pl.ds(key_start + s0, KEY_BLK), :],
            lambda s0=s0: vt_ref[:, pl.ds(key_start + s0, KEY_BLK)],
            None if bias is None else functools.partial(bias, c),
        ))
    return blocks


def _attention_t(operands, s_scr):
    def score(u, c, mx):
        qm, blocks = operands[u]
        k_tile, _, bias = blocks[c]
        st = _dot_nt(k_tile(), qm)
        if bias is not None:
            st = st + bias()
        s_scr[u % 2, c * KEY_BLK:(c + 1) * KEY_BLK, :] = st
        return jnp.maximum(mx, _colwise(jnp.max, st))

    def expo(u, c, m, acc, den):
        _, vt_tile, _ = operands[u][1][c]
        e = jnp.exp2(s_scr[u % 2, c * KEY_BLK:(c + 1) * KEY_BLK, :] - m)
        o = _dot(vt_tile(), e.astype(BF16))
        return (o if acc is None else acc + o), den + _colwise(jnp.sum, e)

    neg = jnp.full((SUBLANES, Q_BLK), -jnp.inf, F32)
    zero = jnp.zeros((SUBLANES, Q_BLK), F32)
    n_blocks = [len(blocks) for _, blocks in operands]
    mx = neg
    for c in range(n_blocks[0]):
        mx = score(0, c, mx)
    outs = []
    for u in range(len(operands)):
        m = jnp.max(mx, axis=0, keepdims=True)
        acc, den, mx = None, zero, neg
        n_next = n_blocks[u + 1] if u + 1 < len(operands) else 0
        for c in range(max(n_blocks[u], n_next)):
            if c < n_next:
                mx = score(u + 1, c, mx)
            if c < n_blocks[u]:
                acc, den = expo(u, c, m, acc, den)
        outs.append(acc / jnp.sum(den, axis=0, keepdims=True))
    return outs


def _half_masks(shape):
    lane = lax.broadcasted_iota(jnp.int32, shape, 1)
    return lane < 64, lane >= 64


N_DR = 2 * NA_WIN_ROWS - 1
N_DC = 2 * NA_WIN_COLS - 1
PAIR_MASK_FIRST = N_DR - 1
PAIR_MASK_SECOND = N_DR
PAIR_MASKED = N_DR + 1
N_PAIRS = N_DR + 2
NA_Q_PER_TRIP = 2
DR_LO = NA_WIN_ROWS - 1 - NA_WIN_ROWS // 2
DR_HI = DR_LO + NA_WIN_ROWS - 1


def _na_pair_table(rpb):
    h = rpb.shape[0]
    w = GRID_W
    v = jnp.zeros((h, N_DR, 2 * w), F32)
    v = v.at[..., 0:NA_WIN_COLS].set(rpb[..., NA_WIN_COLS - 1:].astype(F32))
    v = v.at[..., 2 * w - (NA_WIN_COLS - 1):].set(rpb[..., :NA_WIN_COLS - 1].astype(F32))
    y = jnp.broadcast_to(v[:, :, None, :], (h, N_DR, w, 2 * w)).reshape(h, N_DR, w * 2 * w)
    t = y[..., :w * (2 * w - 1)].reshape(h, N_DR, w, 2 * w - 1)[..., :w]
    col = jnp.arange(w)
    cstart = jnp.clip(col - NA_WIN_COLS // 2, 0, w - NA_WIN_COLS)
    cvalid = (col[None, :] >= cstart[:, None]) & (col[None, :] < cstart[:, None] + NA_WIN_COLS)
    tt = jnp.swapaxes(jnp.where(cvalid, t * LOG2E, MASK_VALUE), -1, -2)
    masked = jnp.full((h, 1, w, w), MASK_VALUE, F32)
    regular = jnp.concatenate([tt[:, 1:], tt[:, :-1]], axis=-1)
    first = jnp.concatenate([masked, tt[:, DR_HI:DR_HI + 1]], axis=-1)
    second = jnp.concatenate([tt[:, DR_LO:DR_LO + 1], masked], axis=-1)
    return jnp.concatenate([regular, first, second, jnp.concatenate([masked, masked], axis=-1)], axis=1)


def _na_pair_entries(q_row0, band_row0, band_rows):
    entries = []
    for j in range(band_rows):
        kr = band_row0 + j
        row = []
        for a in range(0, NA_RB, 2):
            ok, d = [], []
            for r in (q_row0 + a, q_row0 + a + 1):
                r0 = min(max(r - NA_WIN_ROWS // 2, 0), ROWS - NA_WIN_ROWS)
                ok.append(r0 <= kr < r0 + NA_WIN_ROWS)
                d.append(kr - r + NA_WIN_ROWS - 1)
            if ok[0] and ok[1]:
                row.append(d[0] - 1)
            elif ok[1]:
                assert d[1] == DR_HI
                row.append(PAIR_MASK_FIRST)
            elif ok[0]:
                assert d[0] == DR_LO
                row.append(PAIR_MASK_SECOND)
            else:
                row.append(PAIR_MASKED)
        entries.append(row)
    return entries


def _na_kernel(tab_ref, q_ref, k_ref, vt_ref, qc_ref, kc_ref, vct_ref, o_ref, oc_ref, s_scr):
    n_blocks = ROWS // NA_RB
    rows_per_chunk = KEY_BLK // GRID_W
    ctx_blocks = _key_blocks(kc_ref, vct_ref)

    def operands(q_start, band_start, band_rows, entries):
        q = q_ref[pl.ds(q_start, NA_Q), :]
        ops = []
        for hh, msk in enumerate(_half_masks(q.shape)):
            def bias(c, hh=hh):
                rows = entries[c * rows_per_chunk:(c + 1) * rows_per_chunk]
                return jnp.concatenate(
                    [jnp.concatenate([tab_ref[hh, e] for e in row], axis=1) for row in rows], axis=0)

            band = _key_blocks(k_ref, vt_ref, band_start, band_rows * GRID_W, bias)
            ops.append((jnp.where(msk, q, jnp.zeros_like(q)), band + ctx_blocks))
        return ops

    def merge_heads(o_lo, o_hi):
        feat = lax.broadcasted_iota(jnp.int32, o_lo.shape, 0)
        return jnp.where(feat < HB // 2, o_lo, o_hi).T.astype(BF16)

    def run(blocks):
        outs = _attention_t([op for blk in blocks for op in operands(*blk)], s_scr)
        for i, blk in enumerate(blocks):
            o_ref[pl.ds(blk[0], NA_Q), :] = merge_heads(outs[2 * i], outs[2 * i + 1])

    last_band = ROWS - NA_WIN_ROWS
    run([(0, 0, NA_WIN_ROWS, _na_pair_entries(0, 0, NA_WIN_ROWS)),
         ((n_blocks - 1) * NA_Q, last_band * GRID_W, NA_WIN_ROWS,
          _na_pair_entries(ROWS - NA_RB, last_band, NA_WIN_ROWS))])
    interior = _na_pair_entries(NA_RB, 0, NA_BAND_ROWS)

    def body(t, carry):
        blocks = []
        for i in range(NA_Q_PER_TRIP):
            rb = 1 + t * NA_Q_PER_TRIP + i
            band_start = pl.multiple_of((rb - 1) * NA_Q, NA_Q)
            blocks.append((pl.multiple_of(rb * NA_Q, NA_Q), band_start, NA_BAND_ROWS, interior))
        run(blocks)
        return carry

    lax.fori_loop(0, (n_blocks - 2) // NA_Q_PER_TRIP, body, 0)

    qc = qc_ref[...]
    outs = _attention_t([(jnp.where(msk, qc, jnp.zeros_like(qc)), ctx_blocks) for msk in _half_masks(qc.shape)],
                        s_scr)
    oc_ref[...] = merge_heads(outs[0], outs[1])


def _na_attention(table, q, k, vt):
    assert NA_Q == Q_BLK and CTX_LEN == Q_BLK and (ROWS // NA_RB - 2) % NA_Q_PER_TRIP == 0
    lat = lambda hp, b: (b, hp)
    ctx = lambda hp, b: (T_LAT // CTX_LEN + b, hp)
    return pl.pallas_call(
        _na_kernel,
        grid=(N_HB, BATCH),
        in_specs=[
            pl.BlockSpec((2, N_PAIRS, GRID_W, 2 * GRID_W), lambda hp, b: (hp, 0, 0, 0)),
            pl.BlockSpec((SEQ, HB), lat),
            pl.BlockSpec((SEQ, HB), lat),
            pl.BlockSpec((HB, SEQ), lambda hp, b: (hp, b)),
            pl.BlockSpec((CTX_LEN, HB), ctx),
            pl.BlockSpec((CTX_LEN, HB), ctx),
            pl.BlockSpec((HB, CTX_LEN), lambda hp, b: (hp, T_LAT // CTX_LEN + b)),
        ],
        out_specs=[pl.BlockSpec((SEQ, HB), lat), pl.BlockSpec((CTX_LEN, HB), lat)],
        out_shape=[jax.ShapeDtypeStruct((T_LAT, D_MODEL), BF16), jax.ShapeDtypeStruct((T_CTX, D_MODEL), BF16)],
        scratch_shapes=[pltpu.VMEM((2, NA_BAND + CTX_LEN, Q_BLK), F32)],
        compiler_params=_cparams("parallel", "parallel"),
        name="na_attention",
    )(table, q, k, vt, q, k, vt)


def _diff_lambda(lam_ref, lam_init):
    lam = lam_ref[...]
    a = jnp.sum(lam[0:1] * lam[1:2], axis=-1, keepdims=True)
    b = jnp.sum(lam[2:3] * lam[3:4], axis=-1, keepdims=True)
    return jnp.exp(a) - jnp.exp(b) + lam_init


def _diff_kernel(lam_ref, sg_ref, q_ref, k_ref, vt_ref, qc_ref, kc_ref, vct_ref, o_ref, *rest, lam_init):
    *maybe_oc_ref, s_scr = rest
    lam = _diff_lambda(lam_ref, lam_init)
    sg_col = sg_ref[...]

    def rows(qs, pieces):
        blocks = [blk for k_r, vt_r in pieces for blk in _key_blocks(k_r, vt_r)]
        outs = _attention_t([(jnp.where(msk, q, jnp.zeros_like(q)), blocks)
                             for q in qs for msk in _half_masks(q.shape)], s_scr)
        res = []
        for o1, o2 in zip(outs[0::2], outs[1::2]):
            ot = o1 - lam * o2
            ms = jnp.mean(ot * ot, axis=0, keepdims=True)
            ot = ot * lax.rsqrt(ms + NORM_EPS) * sg_col * (1.0 - lam_init)
            res.append(ot.T.astype(BF16))
        return res

    def body(t, carry):
        starts = [pl.multiple_of((t * DIFF_Q_PER_TRIP + j) * Q_BLK, Q_BLK) for j in range(DIFF_Q_PER_TRIP)]
        outs = rows([q_ref[pl.ds(r0, Q_BLK), :] for r0 in starts], [(k_ref, vt_ref), (kc_ref, vct_ref)])
        for r0, o in zip(starts, outs):
            o_ref[pl.ds(r0, Q_BLK), :] = o
        return carry

    lax.fori_loop(0, SEQ // (Q_BLK * DIFF_Q_PER_TRIP), body, 0)
    if maybe_oc_ref:
        oc_ref, = maybe_oc_ref
        assert CTX_LEN == Q_BLK
        oc_ref[...], = rows([qc_ref[...]], [(kc_ref, vct_ref)])


def _diff_attention(lam, sg, q, k, vt, lam_init, *, ctx_out):
    lat = lambda b, h: (b, h)
    ctx = lambda b, h: (T_LAT // CTX_LEN + b, h)
    lat_t = lambda b, h: (h, b)
    ctx_t = lambda b, h: (h, T_LAT // CTX_LEN + b)
    out_specs = [pl.BlockSpec((SEQ, HB), lat)]
    out_shape = [jax.ShapeDtypeStruct((T_LAT, D_MODEL), BF16)]
    if ctx_out:
        out_specs.append(pl.BlockSpec((CTX_LEN, HB), lat))
        out_shape.append(jax.ShapeDtypeStruct((T_CTX, D_MODEL), BF16))
    return pl.pallas_call(
        functools.partial(_diff_kernel, lam_init=lam_init),
        grid=(BATCH, DIFF_HEADS),
        in_specs=[
            pl.BlockSpec(lam.shape, lambda b, h: (0, 0)),
            pl.BlockSpec(sg.shape, lambda b, h: (0, 0)),
            pl.BlockSpec((SEQ, HB), lat),
            pl.BlockSpec((SEQ, HB), lat),
            pl.BlockSpec((HB, SEQ), lat_t),
            pl.BlockSpec((CTX_LEN, HB), ctx),
            pl.BlockSpec((CTX_LEN, HB), ctx),
            pl.BlockSpec((HB, CTX_LEN), ctx_t),
        ],
        out_specs=out_specs,
        out_shape=out_shape,
        scratch_shapes=[pltpu.VMEM((2, SEQ + CTX_LEN, Q_BLK), F32)],
        compiler_params=_cparams("parallel", "parallel"),
        name="diff_attention",
    )(lam, sg, q, k, vt, q, k, vt)


def _route_rows(logits_t, bias_col):
    s = _sigmoid(logits_t)
    sel = s + bias_col
    sel_r = [sel[e:e + 1] for e in range(N_EXPERTS)]
    s_r = [s[e:e + 1] for e in range(N_EXPERTS)]
    group_scores = []
    for g in range(N_GROUPS):
        v = sel_r[g * EXPERTS_PER_GROUP:(g + 1) * EXPERTS_PER_GROUP]
        pairs = [v[a] + v[b] for a in range(EXPERTS_PER_GROUP) for b in range(a + 1, EXPERTS_PER_GROUP)]
        group_scores.append(functools.reduce(jnp.maximum, pairs))
    best = group_scores[0]
    grp = jnp.zeros(best.shape, jnp.int32)
    for g in range(1, N_GROUPS):
        upd = group_scores[g] > best
        best = jnp.where(upd, group_scores[g], best)
        grp = jnp.where(upd, g, grp)

    def pick(rows, j):
        out = rows[j]
        for g in range(1, N_GROUPS):
            out = jnp.where(grp == g, rows[g * EXPERTS_PER_GROUP + j], out)
        return out

    w = [pick(sel_r, j) for j in range(EXPERTS_PER_GROUP)]
    sc = [pick(s_r, j) for j in range(EXPERTS_PER_GROUP)]

    def argmax_first(vals):
        bv = vals[0]
        bi = jnp.zeros(bv.shape, jnp.int32)
        for j in range(1, len(vals)):
            upd = vals[j] > bv
            bv = jnp.where(upd, vals[j], bv)
            bi = jnp.where(upd, j, bi)
        return bi

    i1 = argmax_first(w)
    i2 = argmax_first([jnp.where(i1 == j, -jnp.inf, w[j]) for j in range(EXPERTS_PER_GROUP)])

    def take(vals, idx):
        out = vals[0]
        for j in range(1, len(vals)):
            out = jnp.where(idx == j, vals[j], out)
        return out

    s1 = take(sc, i1)
    s2 = take(sc, i2)
    den = s1 + s2
    return grp * EXPERTS_PER_GROUP + i1, grp * EXPERTS_PER_GROUP + i2, s1 / den, s2 / den


def _oproj_kernel(o_ref, oc_ref, x_ref, mod_ref, w_ref, g_ref, rw_ref, rb_ref, tri_ref,
                  x1_ref, h_ref, ri_ref, rf_ref, cnt_ref, carry_ref):
    i = pl.program_id(0)

    @pl.when(i == 0)
    def _():
        carry_ref[...] = jnp.zeros_like(carry_ref)

    m = mod_ref[0]
    rw = rw_ref[...]
    rw_hi = rw.astype(BF16)
    rw_lo = (rw - rw_hi.astype(F32)).astype(BF16)
    ri_ref[...] = jnp.zeros_like(ri_ref)
    rf_ref[...] = jnp.zeros_like(rf_ref)
    carry = carry_ref[...]
    for r0 in range(0, TM, ROUTE_BLK):
        rows = pl.ds(r0, ROUTE_BLK)
        o = jnp.where(i < LAT_TILES, o_ref[rows, :], oc_ref[rows, :])
        x1 = x_ref[rows, :] + m[2:3] * _dot(o, w_ref[...])
        x1_ref[rows, :] = x1
        h = _rms_modulate(x1, g_ref[...], m[3:4], m[4:5])
        hb = h.astype(BF16)
        h_ref[rows, :] = _pack_bf16_pair(h)

        h_lo = (h - hb.astype(F32)).astype(BF16)
        logits = _dot(hb, rw_hi) + _dot(h_lo, rw_hi) + _dot(hb, rw_lo)
        logits_t = logits.T[:N_EXPERTS]
        e0, e1, g0, g1 = _route_rows(logits_t, rb_ref[...])

        eiota = lax.broadcasted_iota(jnp.int32, (N_EXPERTS, ROUTE_BLK), 0)
        oh0 = eiota == e0
        oh1 = eiota == e1
        oh = jnp.where(oh0 | oh1, 1.0, 0.0)
        before = _dot(oh.astype(BF16), tri_ref[...]) + carry
        rank0 = jnp.sum(jnp.where(oh0, before, 0.0), axis=0, keepdims=True)
        rank1 = jnp.sum(jnp.where(oh1, before, 0.0), axis=0, keepdims=True)
        carry = carry + jnp.sum(oh, axis=1, keepdims=True)

        ri_ref[0:1, rows] = e0
        ri_ref[1:2, rows] = e1
        ri_ref[2:3, rows] = rank0.astype(jnp.int32)
        ri_ref[3:4, rows] = rank1.astype(jnp.int32)
        rf_ref[0:1, rows] = g0
        rf_ref[1:2, rows] = g1
    carry_ref[...] = carry
    cnt_ref[...] = jnp.broadcast_to(carry, cnt_ref.shape)


def _oproj_route(o, o_ctx, x, mod_l, w_o, g, rw_pad, rb_col, tri, n_tok):
    n_tiles = n_tok // TM
    tok = lambda i: (i, 0)
    const2 = lambda i: (0, 0)
    lane_tok = lambda i: (0, i)
    return pl.pallas_call(
        _oproj_kernel,
        grid=(n_tiles,),
        in_specs=[
            pl.BlockSpec((TM, D_MODEL), lambda i: (jnp.minimum(i, LAT_TILES - 1), 0)),
            pl.BlockSpec((TM, D_MODEL), lambda i: (jnp.maximum(i - LAT_TILES, 0), 0)),
            pl.BlockSpec((TM, D_MODEL), tok),
            pl.BlockSpec((1, N_MOD, D_MODEL), lambda i: (_mod_row(i), 0, 0)),
            pl.BlockSpec((D_MODEL, D_MODEL), const2),
            pl.BlockSpec((1, D_MODEL), const2),
            pl.BlockSpec((D_MODEL, HB), const2),
            pl.BlockSpec((N_EXPERTS, 1), const2),
            pl.BlockSpec((ROUTE_BLK, ROUTE_BLK), const2),
        ],
        out_specs=[
            pl.BlockSpec((TM, D_MODEL), tok),
            pl.BlockSpec((TM, D_PACK), tok),
            pl.BlockSpec((8, TM), lane_tok),
            pl.BlockSpec((8, TM), lane_tok),
            pl.BlockSpec((N_EXPERTS, HB), const2),
        ],
        out_shape=[
            jax.ShapeDtypeStruct((n_tok, D_MODEL), F32),
            jax.ShapeDtypeStruct((n_tok, D_PACK), jnp.uint32),
            jax.ShapeDtypeStruct((8, n_tok), jnp.int32),
            jax.ShapeDtypeStruct((8, n_tok), F32),
            jax.ShapeDtypeStruct((N_EXPERTS, HB), F32),
        ],
        scratch_shapes=[pltpu.VMEM((N_EXPERTS, 1), F32)],
        compiler_params=_cparams("arbitrary"),
        name="oproj_norm_route",
    )(o, o if o_ctx is None else o_ctx, x, mod_l, w_o, g.reshape(1, D_MODEL), rw_pad, rb_col, tri)


def _moe_kernel(te_ref, tv_ref, xs_ref, wg_ref, wu_ref, wd_ref, ys_ref, wgub, wdb):
    i = pl.program_id(0)
    valid = tv_ref[i]
    e = te_ref[i]
    prev = te_ref[jnp.maximum(i - 1, 0)]
    n_col = D_EXPERT // HB

    @pl.when((i == 0) | (e != prev))
    def _():
        for j in range(n_col):
            wgub[:, (2 * j) * HB:(2 * j + 1) * HB] = wg_ref[0, 0, :, j * HB:(j + 1) * HB].astype(BF16)
            wgub[:, (2 * j + 1) * HB:(2 * j + 2) * HB] = wu_ref[0, 0, :, j * HB:(j + 1) * HB].astype(BF16)
        wdb[...] = wd_ref[0, 0].astype(BF16)

    for r0 in range(0, TM, TM // 2):
        rows = pl.ds(r0, TM // 2)

        @pl.when(valid > r0)
        def _():
            row = r0 + lax.broadcasted_iota(jnp.int32, (TM // 2, 1), 0)
            x = xs_ref[rows, :]
            x = _unpack_bf16_pair(jnp.where(row < valid, x, jnp.zeros_like(x))).astype(BF16)
            au = _dot(x, wgub[...])
            hm = []
            for j in range(n_col):
                a = au[:, (2 * j) * HB:(2 * j + 1) * HB]
                u = au[:, (2 * j + 1) * HB:(2 * j + 2) * HB]
                hm.append((a * _sigmoid(a) * u).astype(BF16))
            ys_ref[rows, :] = _pack_bf16_pair(_dot(jnp.concatenate(hm, axis=1), wdb[...]))

        @pl.when(valid <= r0)
        def _():
            ys_ref[rows, :] = jnp.zeros((TM // 2, D_PACK), jnp.uint32)


def _grouped_ffn(tile_expert, tile_valid, xs, w_gate, w_up, w_down, layer):
    n_tiles = xs.shape[0] // TM
    wmap = lambda i, te, tv: (layer, te[i], 0, 0)
    tok = lambda i, te, tv: (i, 0)
    return pl.pallas_call(
        _moe_kernel,
        grid_spec=pltpu.PrefetchScalarGridSpec(
            num_scalar_prefetch=2,
            grid=(n_tiles,),
            in_specs=[
                pl.BlockSpec((TM, D_PACK), tok),
                pl.BlockSpec((1, 1, D_MODEL, D_EXPERT), wmap),
                pl.BlockSpec((1, 1, D_MODEL, D_EXPERT), wmap),
                pl.BlockSpec((1, 1, D_EXPERT, D_MODEL), wmap),
            ],
            out_specs=pl.BlockSpec((TM, D_PACK), tok),
            scratch_shapes=[
                pltpu.VMEM((D_MODEL, 2 * D_EXPERT), BF16),
                pltpu.VMEM((D_EXPERT, D_MODEL), BF16),
            ],
        ),
        out_shape=jax.ShapeDtypeStruct((xs.shape[0], D_PACK), jnp.uint32),
        compiler_params=_cparams("arbitrary"),
        name="grouped_ffn",
    )(tile_expert, tile_valid, xs, w_gate, w_up, w_down)


def _moe_layout(route_i, counts, n_tok):
    n_tiles = 2 * n_tok // TM + N_EXPERTS
    counts = counts.astype(jnp.int32)
    tiles_e = (counts + TM - 1) // TM
    tiles_end = jnp.cumsum(tiles_e)
    tile_start = tiles_end - tiles_e
    eid = jnp.arange(N_EXPERTS, dtype=jnp.int32)
    tok_oh = route_i[0:2][:, :, None] == eid
    pos = jnp.sum(jnp.where(tok_oh, tile_start * TM, 0), axis=-1) + route_i[2:4]
    tile_ids = jnp.arange(n_tiles, dtype=jnp.int32)
    n_used = tiles_end[-1]
    te = jnp.sum(tiles_end[None, :] <= jnp.minimum(tile_ids, n_used - 1)[:, None], axis=1).astype(jnp.int32)
    tile_oh = te[:, None] == eid
    cnt_t = jnp.sum(jnp.where(tile_oh, counts, 0), axis=1)
    start_t = jnp.sum(jnp.where(tile_oh, tile_start, 0), axis=1)
    tv = jnp.clip(cnt_t - (tile_ids - start_t) * TM, 0, TM)
    tv = jnp.where(tile_ids < n_used, tv, 0).astype(jnp.int32)
    return pos.astype(jnp.int32), te, tv, n_tiles


SC_CORES = 2
SC_SUBCORES = 16
SC_WORKERS = SC_CORES * SC_SUBCORES
SC_CHUNK = 32


def _sc_mesh():
    return plsc.VectorSubcoreMesh(core_axis_name="c", subcore_axis_name="s")


def _sc_worker_indices(pos, n_tok):
    n_ch = n_tok // SC_WORKERS // SC_CHUNK
    return pos.reshape(2, SC_WORKERS, n_ch, SC_CHUNK).transpose(1, 2, 0, 3), n_ch


def _sc_dispatch(h, pos, n_slots):
    n_tok = h.shape[0]
    per_w = n_tok // SC_WORKERS
    pos_w, n_ch = _sc_worker_indices(pos, n_tok)
    assert n_ch % 2 == 0 and n_ch * SC_CHUNK * SC_WORKERS == n_tok

    @functools.partial(
        pl.kernel, mesh=_sc_mesh(), out_type=jax.ShapeDtypeStruct((n_slots, h.shape[1]), h.dtype),
        scratch_types=[pltpu.VMEM((n_ch, 2, SC_CHUNK), jnp.int32), pltpu.VMEM((2, SC_CHUNK, h.shape[1]), h.dtype),
                       pltpu.SemaphoreType.DMA((2,)), pltpu.SemaphoreType.DMA((2,))],
        name="sc_dispatch")
    def dispatch(h_hbm, pos_hbm, xs_hbm, idx_v, rows_v, load_sem, scat_sem):
        wid = lax.axis_index("s") * SC_CORES + lax.axis_index("c")
        base = wid * per_w
        pltpu.sync_copy(pos_hbm.at[wid], idx_v)

        def load(c, b):
            return pltpu.make_async_copy(h_hbm.at[pl.ds(base + c * SC_CHUNK, SC_CHUNK)], rows_v.at[b],
                                         load_sem.at[b])

        def scat(c, b, k):
            return pltpu.make_async_copy(rows_v.at[b], xs_hbm.at[idx_v.at[c, k]], scat_sem.at[b])

        load(0, 0).start()

        @pl.loop(0, n_ch, step=2)
        def _(c0):
            for b in range(2):
                c = c0 + b
                load(c, b).wait()
                scat(c, b, 0).start()
                scat(c, b, 1).start()

                @pl.when(c >= 1)
                def _():
                    scat(c - 1, 1 - b, 0).wait()
                    scat(c - 1, 1 - b, 1).wait()

                @pl.when(c + 1 < n_ch)
                def _():
                    load(c + 1, 1 - b).start()

        scat(n_ch - 1, 1, 0).wait()
        scat(n_ch - 1, 1, 1).wait()

    return dispatch(h, pos_w)


def _sc_combine_gather(ys, pos):
    n_tok = pos.shape[1]
    per_w = n_tok // SC_WORKERS
    pos_w, n_ch = _sc_worker_indices(pos, n_tok)
    assert n_ch * SC_CHUNK * SC_WORKERS == n_tok

    @functools.partial(
        pl.kernel, mesh=_sc_mesh(), out_type=jax.ShapeDtypeStruct((2, n_tok, ys.shape[1]), ys.dtype),
        scratch_types=[pltpu.VMEM((n_ch, 2, SC_CHUNK), jnp.int32), pltpu.VMEM((2, SC_CHUNK, ys.shape[1]), ys.dtype),
                       pltpu.SemaphoreType.DMA((2,)), pltpu.SemaphoreType.DMA((2,))],
        name="sc_combine_gather")
    def gather(ys_hbm, pos_hbm, yg_hbm, idx_v, rows_v, gath_sem, write_sem):
        wid = lax.axis_index("s") * SC_CORES + lax.axis_index("c")
        base = wid * per_w
        pltpu.sync_copy(pos_hbm.at[wid], idx_v)

        def gath(c, k):
            return pltpu.make_async_copy(ys_hbm.at[idx_v.at[c, k]], rows_v.at[k], gath_sem.at[k])

        def write(c, k):
            return pltpu.make_async_copy(rows_v.at[k], yg_hbm.at[k, pl.ds(base + c * SC_CHUNK, SC_CHUNK)],
                                         write_sem.at[k])

        gath(0, 0).start()

        @pl.loop(0, n_ch)
        def _(c):
            gath(c, 0).wait()
            write(c, 0).start()

            @pl.when(c >= 1)
            def _():
                write(c - 1, 1).wait()

            gath(c, 1).start()
            gath(c, 1).wait()
            write(c, 1).start()
            write(c, 0).wait()

            @pl.when(c + 1 < n_ch)
            def _():
                gath(c + 1, 0).start()

        write(n_ch - 1, 1).wait()

    return gather(ys, pos_w)


def _final_kernel(x1_ref, yg_ref, gt_ref, mod_ref, fg_ref, out_ref):
    x2 = _moe_residual(x1_ref[...], yg_ref, gt_ref, mod_ref[0][5:6])
    ms = jnp.mean(x2 * x2, axis=-1, keepdims=True)
    out_ref[...] = x2 * lax.rsqrt(ms + NORM_EPS) * fg_ref[...]


def _combine_final(x1, yg, gates_t, mod_l, final_g, n_tok):
    tok = lambda i: (i, 0)
    return pl.pallas_call(
        _final_kernel,
        grid=(n_tok // TM,),
        in_specs=[
            pl.BlockSpec((TM, D_MODEL), tok),
            pl.BlockSpec((2, TM, D_PACK), lambda i: (0, i, 0)),
            pl.BlockSpec((TM, 2), tok),
            pl.BlockSpec((1, N_MOD, D_MODEL), lambda i: (_mod_row(i), 0, 0)),
            pl.BlockSpec((1, D_MODEL), lambda i: (0, 0)),
        ],
        out_specs=pl.BlockSpec((TM, D_MODEL), tok),
        out_shape=jax.ShapeDtypeStruct((n_tok, D_MODEL), F32),
        compiler_params=_cparams("parallel"),
        name="moe_combine_final",
    )(x1, yg, gates_t, mod_l, final_g.reshape(1, D_MODEL))


def _diff_lambda_init(layer_idx):
    return 0.8 - 0.6 * math.exp(-0.3 * (layer_idx - 1))


def kernel(x, c, ctx, c_ctx, mod_w, mod_b, norm_mix, norm_ffn, w_qkv, w_o, na_rpb, diff_lambda, diff_subln,
           router_w, router_b, expert_w_gate, expert_w_up, expert_w_down, final_norm):
    mod = _modulation(c, c_ctx, mod_w, mod_b)
    stream = (x.reshape(T_LAT, D_MODEL), ctx.reshape(T_CTX, D_MODEL))
    w_qkv_b = w_qkv.astype(BF16)
    w_o_b = w_o.astype(BF16)
    cs, sn = _rope_tables()
    rw_pad = jnp.zeros((D_MODEL, HB), F32).at[:, :N_EXPERTS].set(router_w)
    rb_col = router_b.reshape(N_EXPERTS, 1).astype(F32)
    blk = jnp.arange(ROUTE_BLK)
    tri = (blk[:, None] < blk[None, :]).astype(BF16)

    for i in range(DEPTH):
        last = i == DEPTH - 1
        j = i // 2
        is_diff = i % 2 == 1
        xa, q, k, v = _qkv(stream, mod[i], norm_mix[i], w_qkv_b[i], cs, sn, rope=is_diff)
        if not is_diff:
            o, o_ctx = _na_attention(_na_pair_table(na_rpb[j]), q, k, v)
        else:
            lam_init = _diff_lambda_init(i + 1)
            sg = diff_subln[j].reshape(HB, 1)
            outs = _diff_attention(diff_lambda[j], sg, q, k, v, lam_init, ctx_out=not last)
            o, o_ctx = outs if not last else (outs[0], None)
        n_tok = T_LAT if last else T_ALL
        x1, h, route_i, route_f, cnt = _oproj_route(o, o_ctx, xa, mod[i], w_o_b[i], norm_ffn[i], rw_pad,
                                                    rb_col, tri, n_tok)
        pos, te, tv, n_tiles = _moe_layout(route_i, cnt[:, 0], n_tok)
        xs = _sc_dispatch(h, pos, n_tiles * TM)
        ys = _grouped_ffn(te, tv, xs, expert_w_gate, expert_w_up, expert_w_down, i)
        yg = _sc_combine_gather(ys, pos)
        stream = (x1, yg, route_f[0:2].T, mod[i])
    out = _combine_final(*stream, final_norm, T_LAT)
    return out.reshape(BATCH, SEQ, D_MODEL)
```

```python
import functools
import math

import jax
import jax.numpy as jnp
from jax import lax
from jax.experimental import pallas as pl
from jax.experimental.pallas import tpu as pltpu
from jax.experimental.pallas import tpu_sc as plsc

F32 = jnp.float32
BF16 = jnp.bfloat16

D_MODEL = 1024
BATCH = 8
SEQ = 2048
DEPTH = 4
CTX_LEN = 256
GRID_W = 64
ROWS = SEQ // GRID_W
NA_HEADS = 16
NA_WIN_ROWS = 8
NA_WIN_COLS = 16
DIFF_HEADS = 8
DIFF_HEAD_DIM = 64
ROPE_THETA = 10000.0
N_EXPERTS = 16
N_GROUPS = 4
EXPERTS_PER_GROUP = 4
D_EXPERT = 1024
N_MOD = 6
NORM_EPS = 1e-6

T_LAT = BATCH * SEQ
T_CTX = BATCH * CTX_LEN
T_ALL = T_LAT + T_CTX
TM = 512
ROUTE_BLK = TM // 2
LAT_TILES = T_LAT // TM
CTX_TILES = T_CTX // TM
TILES_PER_BATCH = SEQ // TM
CTX_ROW = BATCH
MOD_ROWS = 16
HB = 128
N_HB = D_MODEL // HB
NA_RB = 4
NA_BAND_ROWS = 12
NA_Q = NA_RB * GRID_W
NA_BAND = NA_BAND_ROWS * GRID_W
TQ_DIFF = 512
MASK_VALUE = -1e30
LOG2E = math.log2(math.e)
Q_SCALE = 0.125 * LOG2E
VMEM_LIMIT = 56 * 1024 * 1024


def _cparams(*sem):
    return pltpu.CompilerParams(dimension_semantics=sem, vmem_limit_bytes=VMEM_LIMIT)


def _dot(a, b):
    return jnp.dot(a, b, preferred_element_type=F32)


def _dot_nt(a, b):
    return lax.dot_general(a, b, (((1,), (1,)), ((), ())), preferred_element_type=F32)


def _sigmoid(x):
    return 1.0 / (1.0 + jnp.exp(-x))


D_PACK = D_MODEL // 2


def _pack_bf16_pair(x):
    lo = lax.bitcast_convert_type(x[:, :D_PACK].astype(BF16).astype(F32), jnp.uint32)
    hi = lax.bitcast_convert_type(x[:, D_PACK:].astype(BF16).astype(F32), jnp.uint32)
    return (lo >> 16) | hi


def _unpack_bf16_pair(w):
    lo = lax.bitcast_convert_type(w << 16, F32)
    hi = lax.bitcast_convert_type(w & jnp.uint32(0xFFFF0000), F32)
    return jnp.concatenate([lo, hi], axis=1)


def _mod_row(i):
    return jnp.minimum(i // TILES_PER_BATCH, CTX_ROW)


def _mod_kernel(act_ref, w_ref, b_ref, o_ref):
    a = act_ref[...]
    a = a * _sigmoid(a)
    o_ref[0] = _dot(a.astype(BF16), w_ref[0].astype(BF16)) + b_ref[0]


def _modulation(c, c_ctx, mod_w, mod_b):
    tn = 1536
    act = jnp.zeros((MOD_ROWS, D_MODEL), F32).at[:BATCH].set(c).at[CTX_ROW].set(c_ctx)
    out = pl.pallas_call(
        _mod_kernel,
        grid=(DEPTH, N_MOD * D_MODEL // tn),
        in_specs=[
            pl.BlockSpec((MOD_ROWS, D_MODEL), lambda l, j: (0, 0)),
            pl.BlockSpec((1, D_MODEL, tn), lambda l, j: (l, 0, j)),
            pl.BlockSpec((1, 1, tn), lambda l, j: (l, 0, j)),
        ],
        out_specs=pl.BlockSpec((1, MOD_ROWS, tn), lambda l, j: (l, 0, j)),
        out_shape=jax.ShapeDtypeStruct((DEPTH, MOD_ROWS, N_MOD * D_MODEL), F32),
        compiler_params=_cparams("parallel", "parallel"),
        name="adaln_mod",
    )(act, mod_w, mod_b.reshape(DEPTH, 1, N_MOD * D_MODEL))
    return out.reshape(DEPTH, MOD_ROWS, N_MOD, D_MODEL)


def _rms_modulate(x, g, shift, scale):
    ms = jnp.mean(x * x, axis=-1, keepdims=True)
    y = x * lax.rsqrt(ms + NORM_EPS) * g
    return y * (1.0 + scale) + shift


def _rope_block(xb, cs, sn):
    lane = lax.broadcasted_iota(jnp.int32, xb.shape, 1)
    partner = jnp.where((lane & 63) < 32, pltpu.roll(xb, 96, 1), pltpu.roll(xb, 32, 1))
    return xb * cs + partner * sn


def _moe_residual(x1, yg_ref, gt_ref, gate_ffn):
    g = gt_ref[...]
    y = g[:, 0:1] * _unpack_bf16_pair(yg_ref[0]) + g[:, 1:2] * _unpack_bf16_pair(yg_ref[1])
    return x1 + gate_ffn * y


def _qkv_kernel(*refs, rope, first):
    if first:
        xl_ref, xc_ref, *refs = refs
        x = jnp.where(pl.program_id(0) < LAT_TILES, xl_ref[...], xc_ref[...])
    else:
        x1_ref, yg_ref, gt_ref, modp_ref, *refs = refs
        x = _moe_residual(x1_ref[...], yg_ref, gt_ref, modp_ref[0][5:6])
    mod_ref, g_ref, wqk_ref, wvt_ref, cs_ref, sn_ref, x_ref, q_ref, k_ref, vt_ref = refs
    x_ref[...] = x
    m = mod_ref[0]
    h = _rms_modulate(x, g_ref[...], m[0:1], m[1:2]).astype(BF16)
    for idx, out in enumerate((q_ref, k_ref)):
        acc = _dot(h, wqk_ref[0, :, idx * D_MODEL:(idx + 1) * D_MODEL])
        if rope:
            cs = cs_ref[...]
            sn = sn_ref[...]
            acc = jnp.concatenate(
                [_rope_block(acc[:, j * HB:(j + 1) * HB], cs, sn) for j in range(N_HB)], axis=1)
        if idx == 0:
            acc = acc * Q_SCALE
        out[...] = acc.astype(BF16)
    vt_ref[...] = _dot_nt(wvt_ref[0], h).astype(BF16)


def _qkv(stream, mod_l, g, w_qk, w_vt, layer, cs, sn, *, rope):
    first = len(stream) == 2
    n_tiles = T_ALL // TM
    tok = lambda i: (i, 0)
    const2 = lambda i: (0, 0)
    mod_spec = pl.BlockSpec((1, N_MOD, D_MODEL), lambda i: (_mod_row(i), 0, 0))
    rope_idx = lambda i: (jnp.where(i < LAT_TILES, i % TILES_PER_BATCH, TILES_PER_BATCH), 0)
    out = jax.ShapeDtypeStruct((T_ALL, D_MODEL), BF16)
    wmap = lambda i: (layer, 0, 0)
    if first:
        stream_specs = [
            pl.BlockSpec((TM, D_MODEL), lambda i: (jnp.minimum(i, LAT_TILES - 1), 0)),
            pl.BlockSpec((TM, D_MODEL), lambda i: (jnp.maximum(i - LAT_TILES, 0), 0)),
        ]
    else:
        stream_specs = [
            pl.BlockSpec((TM, D_MODEL), tok),
            pl.BlockSpec((2, TM, D_PACK), lambda i: (0, i, 0)),
            pl.BlockSpec((TM, 2), tok),
            mod_spec,
        ]
    return pl.pallas_call(
        functools.partial(_qkv_kernel, rope=rope, first=first),
        grid=(n_tiles,),
        in_specs=stream_specs + [
            mod_spec,
            pl.BlockSpec((1, D_MODEL), const2),
            pl.BlockSpec((1, D_MODEL, 2 * D_MODEL), wmap),
            pl.BlockSpec((1, D_MODEL, D_MODEL), wmap),
            pl.BlockSpec((TM, HB), rope_idx),
            pl.BlockSpec((TM, HB), rope_idx),
        ],
        out_specs=[pl.BlockSpec((TM, D_MODEL), tok), pl.BlockSpec((TM, D_MODEL), tok),
                   pl.BlockSpec((TM, D_MODEL), tok), pl.BlockSpec((D_MODEL, TM), lambda i: (0, i))],
        out_shape=[jax.ShapeDtypeStruct((T_ALL, D_MODEL), F32), out, out,
                   jax.ShapeDtypeStruct((D_MODEL, T_ALL), BF16)],
        compiler_params=_cparams("parallel"),
        name="norm_qkv_rope" if rope else "norm_qkv",
    )(*stream, mod_l, g.reshape(1, D_MODEL), w_qk, w_vt, cs, sn)


def _rope_tables():
    t = jnp.arange(SEQ)
    row = (t // GRID_W).astype(F32)
    col = (t % GRID_W).astype(F32)
    n_freq = DIFF_HEAD_DIM // 4
    inv_freq = ROPE_THETA ** (-jnp.arange(n_freq, dtype=F32) / n_freq)
    ang = jnp.concatenate([row[:, None] * inv_freq, col[:, None] * inv_freq], axis=-1)
    cos, sin = jnp.cos(ang), jnp.sin(ang)
    cs = jnp.concatenate([cos, cos, cos, cos], axis=-1)
    sn = jnp.concatenate([-sin, sin, -sin, sin], axis=-1)
    cs = jnp.concatenate([cs, jnp.ones((TM, HB), F32)], axis=0)
    sn = jnp.concatenate([sn, jnp.zeros((TM, HB), F32)], axis=0)
    return cs, sn


Q_BLK = 256
KEY_BLK = 256
SUBLANES = 8
DIFF_Q_PER_TRIP = 4


def _colwise(reduce_fn, x):
    return reduce_fn(x.reshape(x.shape[0] // SUBLANES, SUBLANES, x.shape[1]), axis=0)


def _key_blocks(k_ref, vt_ref, key_start=0, n_keys=None, bias=None):
    n_keys = k_ref.shape[0] if n_keys is None else n_keys
    blocks = []
    for c, s0 in enumerate(range(0, n_keys, KEY_BLK)):
        blocks.append((
            lambda s0=s0: k_ref[pl.ds(key_start + s0, KEY_BLK), :],
            lambda s0=s0: vt_ref[:, pl.ds(key_start + s0, KEY_BLK)],
            None if bias is None else functools.partial(bias, c),
        ))
    return blocks


def _attention_t(operands, s_scr):
    def score(u, c, mx):
        qm, blocks = operands[u]
        k_tile, _, bias = blocks[c]
        st = _dot_nt(k_tile(), qm)
        if bias is not None:
            st = st + bias()
        s_scr[u % 2, c * KEY_BLK:(c + 1) * KEY_BLK, :] = st
        return jnp.maximum(mx, _colwise(jnp.max, st))

    def expo(u, c, m, acc, den):
        _, vt_tile, _ = operands[u][1][c]
        e = jnp.exp2(s_scr[u % 2, c * KEY_BLK:(c + 1) * KEY_BLK, :] - m)
        o = _dot(vt_tile(), e.astype(BF16))
        return (o if acc is None else acc + o), den + _colwise(jnp.sum, e)

    neg = jnp.full((SUBLANES, Q_BLK), -jnp.inf, F32)
    zero = jnp.zeros((SUBLANES, Q_BLK), F32)
    n_blocks = [len(blocks) for _, blocks in operands]
    mx = neg
    for c in range(n_blocks[0]):
        mx = score(0, c, mx)
    outs = []
    for u in range(len(operands)):
        m = jnp.max(mx, axis=0, keepdims=True)
        acc, den, mx = None, zero, neg
        n_next = n_blocks[u + 1] if u + 1 < len(operands) else 0
        for c in range(max(n_blocks[u], n_next)):
            if c < n_next:
                mx = score(u + 1, c, mx)
            if c < n_blocks[u]:
                acc, den = expo(u, c, m, acc, den)
        outs.append(acc / jnp.sum(den, axis=0, keepdims=True))
    return outs


def _half_masks(shape):
    lane = lax.broadcasted_iota(jnp.int32, shape, 1)
    return lane < 64, lane >= 64


N_DR = 2 * NA_WIN_ROWS - 1
N_DC = 2 * NA_WIN_COLS - 1
PAIR_MASK_FIRST = N_DR - 1
PAIR_MASK_SECOND = N_DR
PAIR_MASKED = N_DR + 1
N_PAIRS = N_DR + 2
NA_Q_PER_TRIP = 2
DR_LO = NA_WIN_ROWS - 1 - NA_WIN_ROWS // 2
DR_HI = DR_LO + NA_WIN_ROWS - 1


def _na_pair_table(rpb):
    h = rpb.shape[0]
    w = GRID_W
    v = jnp.zeros((h, N_DR, 2 * w), F32)
    v = v.at[..., 0:NA_WIN_COLS].set(rpb[..., NA_WIN_COLS - 1:].astype(F32))
    v = v.at[..., 2 * w - (NA_WIN_COLS - 1):].set(rpb[..., :NA_WIN_COLS - 1].astype(F32))
    y = jnp.broadcast_to(v[:, :, None, :], (h, N_DR, w, 2 * w)).reshape(h, N_DR, w * 2 * w)
    t = y[..., :w * (2 * w - 1)].reshape(h, N_DR, w, 2 * w - 1)[..., :w]
    col = jnp.arange(w)
    cstart = jnp.clip(col - NA_WIN_COLS // 2, 0, w - NA_WIN_COLS)
    cvalid = (col[None, :] >= cstart[:, None]) & (col[None, :] < cstart[:, None] + NA_WIN_COLS)
    tt = jnp.swapaxes(jnp.where(cvalid, t * LOG2E, MASK_VALUE), -1, -2)
    masked = jnp.full((h, 1, w, w), MASK_VALUE, F32)
    regular = jnp.concatenate([tt[:, 1:], tt[:, :-1]], axis=-1)
    first = jnp.concatenate([masked, tt[:, DR_HI:DR_HI + 1]], axis=-1)
    second = jnp.concatenate([tt[:, DR_LO:DR_LO + 1], masked], axis=-1)
    return jnp.concatenate([regular, first, second, jnp.concatenate([masked, masked], axis=-1)], axis=1)


def _na_pair_entries(q_row0, band_row0, band_rows):
    entries = []
    for j in range(band_rows):
        kr = band_row0 + j
        row = []
        for a in range(0, NA_RB, 2):
            ok, d = [], []
            for r in (q_row0 + a, q_row0 + a + 1):
                r0 = min(max(r - NA_WIN_ROWS // 2, 0), ROWS - NA_WIN_ROWS)
                ok.append(r0 <= kr < r0 + NA_WIN_ROWS)
                d.append(kr - r + NA_WIN_ROWS - 1)
            if ok[0] and ok[1]:
                row.append(d[0] - 1)
            elif ok[1]:
                assert d[1] == DR_HI
                row.append(PAIR_MASK_FIRST)
            elif ok[0]:
                assert d[0] == DR_LO
                row.append(PAIR_MASK_SECOND)
            else:
                row.append(PAIR_MASKED)
        entries.append(row)
    return entries


def _na_kernel(tab_ref, q_ref, k_ref, vt_ref, qc_ref, kc_ref, vct_ref, o_ref, oc_ref, s_scr):
    n_blocks = ROWS // NA_RB
    rows_per_chunk = KEY_BLK // GRID_W
    ctx_blocks = _key_blocks(kc_ref, vct_ref)

    def operands(q_start, band_start, band_rows, entries):
        q = q_ref[pl.ds(q_start, NA_Q), :]
        ops = []
        for hh, msk in enumerate(_half_masks(q.shape)):
            def bias(c, hh=hh):
                rows = entries[c * rows_per_chunk:(c + 1) * rows_per_chunk]
                return jnp.concatenate(
                    [jnp.concatenate([tab_ref[hh, e] for e in row], axis=1) for row in rows], axis=0)

            band = _key_blocks(k_ref, vt_ref, band_start, band_rows * GRID_W, bias)
            ops.append((jnp.where(msk, q, jnp.zeros_like(q)), band + ctx_blocks))
        return ops

    def merge_heads(o_lo, o_hi):
        feat = lax.broadcasted_iota(jnp.int32, o_lo.shape, 0)
        return jnp.where(feat < HB // 2, o_lo, o_hi).T.astype(BF16)

    def run(blocks):
        outs = _attention_t([op for blk in blocks for op in operands(*blk)], s_scr)
        for i, blk in enumerate(blocks):
            o_ref[pl.ds(blk[0], NA_Q), :] = merge_heads(outs[2 * i], outs[2 * i + 1])

    last_band = ROWS - NA_WIN_ROWS
    run([(0, 0, NA_WIN_ROWS, _na_pair_entries(0, 0, NA_WIN_ROWS)),
         ((n_blocks - 1) * NA_Q, last_band * GRID_W, NA_WIN_ROWS,
          _na_pair_entries(ROWS - NA_RB, last_band, NA_WIN_ROWS))])
    interior = _na_pair_entries(NA_RB, 0, NA_BAND_ROWS)

    def body(t, carry):
        blocks = []
        for i in range(NA_Q_PER_TRIP):
            rb = 1 + t * NA_Q_PER_TRIP + i
            band_start = pl.multiple_of((rb - 1) * NA_Q, NA_Q)
            blocks.append((pl.multiple_of(rb * NA_Q, NA_Q), band_start, NA_BAND_ROWS, interior))
        run(blocks)
        return carry

    lax.fori_loop(0, (n_blocks - 2) // NA_Q_PER_TRIP, body, 0)

    qc = qc_ref[...]
    outs = _attention_t([(jnp.where(msk, qc, jnp.zeros_like(qc)), ctx_blocks) for msk in _half_masks(qc.shape)],
                        s_scr)
    oc_ref[...] = merge_heads(outs[0], outs[1])


def _na_attention(tables, layer_na, q, k, vt):
    assert NA_Q == Q_BLK and CTX_LEN == Q_BLK and (ROWS // NA_RB - 2) % NA_Q_PER_TRIP == 0
    lat = lambda hp, b: (b, hp)
    ctx = lambda hp, b: (T_LAT // CTX_LEN + b, hp)
    return pl.pallas_call(
        _na_kernel,
        grid=(N_HB, BATCH),
        in_specs=[
            pl.BlockSpec((2, N_PAIRS, GRID_W, 2 * GRID_W), lambda hp, b: (layer_na * N_HB + hp, 0, 0, 0)),
            pl.BlockSpec((SEQ, HB), lat),
            pl.BlockSpec((SEQ, HB), lat),
            pl.BlockSpec((HB, SEQ), lambda hp, b: (hp, b)),
            pl.BlockSpec((CTX_LEN, HB), ctx),
            pl.BlockSpec((CTX_LEN, HB), ctx),
            pl.BlockSpec((HB, CTX_LEN), lambda hp, b: (hp, T_LAT // CTX_LEN + b)),
        ],
        out_specs=[pl.BlockSpec((SEQ, HB), lat), pl.BlockSpec((CTX_LEN, HB), lat)],
        out_shape=[jax.ShapeDtypeStruct((T_LAT, D_MODEL), BF16), jax.ShapeDtypeStruct((T_CTX, D_MODEL), BF16)],
        scratch_shapes=[pltpu.VMEM((2, NA_BAND + CTX_LEN, Q_BLK), F32)],
        compiler_params=_cparams("parallel", "parallel"),
        name="na_attention",
    )(tables, q, k, vt, q, k, vt)


def _diff_lambda(lam_ref, lam_init):
    lam = lam_ref[...]
    a = jnp.sum(lam[0:1] * lam[1:2], axis=-1, keepdims=True)
    b = jnp.sum(lam[2:3] * lam[3:4], axis=-1, keepdims=True)
    return jnp.exp(a) - jnp.exp(b) + lam_init


def _diff_kernel(lam_ref, sg_ref, q_ref, k_ref, vt_ref, qc_ref, kc_ref, vct_ref, o_ref, *rest, lam_init):
    *maybe_oc_ref, s_scr = rest
    lam = _diff_lambda(lam_ref, lam_init)
    sg_col = sg_ref[...]

    def rows(qs, pieces):
        blocks = [blk for k_r, vt_r in pieces for blk in _key_blocks(k_r, vt_r)]
        outs = _attention_t([(jnp.where(msk, q, jnp.zeros_like(q)), blocks)
                             for q in qs for msk in _half_masks(q.shape)], s_scr)
        res = []
        for o1, o2 in zip(outs[0::2], outs[1::2]):
            ot = o1 - lam * o2
            ms = jnp.mean(ot * ot, axis=0, keepdims=True)
            ot = ot * lax.rsqrt(ms + NORM_EPS) * sg_col * (1.0 - lam_init)
            res.append(ot.T.astype(BF16))
        return res

    def body(t, carry):
        starts = [pl.multiple_of((t * DIFF_Q_PER_TRIP + j) * Q_BLK, Q_BLK) for j in range(DIFF_Q_PER_TRIP)]
        outs = rows([q_ref[pl.ds(r0, Q_BLK), :] for r0 in starts], [(k_ref, vt_ref), (kc_ref, vct_ref)])
        for r0, o in zip(starts, outs):
            o_ref[pl.ds(r0, Q_BLK), :] = o
        return carry

    lax.fori_loop(0, SEQ // (Q_BLK * DIFF_Q_PER_TRIP), body, 0)
    if maybe_oc_ref:
        oc_ref, = maybe_oc_ref
        assert CTX_LEN == Q_BLK
        oc_ref[...], = rows([qc_ref[...]], [(kc_ref, vct_ref)])


def _diff_attention(lam, sg, q, k, vt, lam_init, *, ctx_out):
    lat = lambda b, h: (b, h)
    ctx = lambda b, h: (T_LAT // CTX_LEN + b, h)
    lat_t = lambda b, h: (h, b)
    ctx_t = lambda b, h: (h, T_LAT // CTX_LEN + b)
    out_specs = [pl.BlockSpec((SEQ, HB), lat)]
    out_shape = [jax.ShapeDtypeStruct((T_LAT, D_MODEL), BF16)]
    if ctx_out:
        out_specs.append(pl.BlockSpec((CTX_LEN, HB), lat))
        out_shape.append(jax.ShapeDtypeStruct((T_CTX, D_MODEL), BF16))
    return pl.pallas_call(
        functools.partial(_diff_kernel, lam_init=lam_init),
        grid=(BATCH, DIFF_HEADS),
        in_specs=[
            pl.BlockSpec(lam.shape, lambda b, h: (0, 0)),
            pl.BlockSpec(sg.shape, lambda b, h: (0, 0)),
            pl.BlockSpec((SEQ, HB), lat),
            pl.BlockSpec((SEQ, HB), lat),
            pl.BlockSpec((HB, SEQ), lat_t),
            pl.BlockSpec((CTX_LEN, HB), ctx),
            pl.BlockSpec((CTX_LEN, HB), ctx),
            pl.BlockSpec((HB, CTX_LEN), ctx_t),
        ],
        out_specs=out_specs,
        out_shape=out_shape,
        scratch_shapes=[pltpu.VMEM((2, SEQ + CTX_LEN, Q_BLK), F32)],
        compiler_params=_cparams("parallel", "parallel"),
        name="diff_attention",
    )(lam, sg, q, k, vt, q, k, vt)


def _route_rows(logits_t, bias_col):
    s = _sigmoid(logits_t)
    sel = s + bias_col
    sel_r = [sel[e:e + 1] for e in range(N_EXPERTS)]
    s_r = [s[e:e + 1] for e in range(N_EXPERTS)]
    group_scores = []
    for g in range(N_GROUPS):
        v = sel_r[g * EXPERTS_PER_GROUP:(g + 1) * EXPERTS_PER_GROUP]
        pairs = [v[a] + v[b] for a in range(EXPERTS_PER_GROUP) for b in range(a + 1, EXPERTS_PER_GROUP)]
        group_scores.append(functools.reduce(jnp.maximum, pairs))
    best = group_scores[0]
    grp = jnp.zeros(best.shape, jnp.int32)
    for g in range(1, N_GROUPS):
        upd = group_scores[g] > best
        best = jnp.where(upd, group_scores[g], best)
        grp = jnp.where(upd, g, grp)

    def pick(rows, j):
        out = rows[j]
        for g in range(1, N_GROUPS):
            out = jnp.where(grp == g, rows[g * EXPERTS_PER_GROUP + j], out)
        return out

    w = [pick(sel_r, j) for j in range(EXPERTS_PER_GROUP)]
    sc = [pick(s_r, j) for j in range(EXPERTS_PER_GROUP)]

    def argmax_first(vals):
        bv = vals[0]
        bi = jnp.zeros(bv.shape, jnp.int32)
        for j in range(1, len(vals)):
            upd = vals[j] > bv
            bv = jnp.where(upd, vals[j], bv)
            bi = jnp.where(upd, j, bi)
        return bi

    i1 = argmax_first(w)
    i2 = argmax_first([jnp.where(i1 == j, -jnp.inf, w[j]) for j in range(EXPERTS_PER_GROUP)])

    def take(vals, idx):
        out = vals[0]
        for j in range(1, len(vals)):
            out = jnp.where(idx == j, vals[j], out)
        return out

    s1 = take(sc, i1)
    s2 = take(sc, i2)
    den = s1 + s2
    return grp * EXPERTS_PER_GROUP + i1, grp * EXPERTS_PER_GROUP + i2, s1 / den, s2 / den


def _oproj_kernel(o_ref, oc_ref, x_ref, mod_ref, w_ref, g_ref, rw_ref, rb_ref, tri_ref,
                  x1_ref, h_ref, ri_ref, rf_ref, cnt_ref, carry_ref):
    i = pl.program_id(0)

    @pl.when(i == 0)
    def _():
        carry_ref[...] = jnp.zeros_like(carry_ref)

    m = mod_ref[0]
    rw = rw_ref[...]
    rw_hi = rw.astype(BF16)
    rw_lo = (rw - rw_hi.astype(F32)).astype(BF16)
    ri_ref[...] = jnp.zeros_like(ri_ref)
    rf_ref[...] = jnp.zeros_like(rf_ref)
    carry = carry_ref[...]
    for r0 in range(0, TM, ROUTE_BLK):
        rows = pl.ds(r0, ROUTE_BLK)
        o = jnp.where(i < LAT_TILES, o_ref[rows, :], oc_ref[rows, :])
        x1 = x_ref[rows, :] + m[2:3] * _dot(o, w_ref[0])
        x1_ref[rows, :] = x1
        h = _rms_modulate(x1, g_ref[...], m[3:4], m[4:5])
        hb = h.astype(BF16)
        h_ref[rows, :] = _pack_bf16_pair(h)

        h_lo = (h - hb.astype(F32)).astype(BF16)
        logits = _dot(hb, rw_hi) + _dot(h_lo, rw_hi) + _dot(hb, rw_lo)
        logits_t = logits.T[:N_EXPERTS]
        e0, e1, g0, g1 = _route_rows(logits_t, rb_ref[...])

        eiota = lax.broadcasted_iota(jnp.int32, (N_EXPERTS, ROUTE_BLK), 0)
        oh0 = eiota == e0
        oh1 = eiota == e1
        oh = jnp.where(oh0 | oh1, 1.0, 0.0)
        before = _dot(oh.astype(BF16), tri_ref[...]) + carry
        rank0 = jnp.sum(jnp.where(oh0, before, 0.0), axis=0, keepdims=True)
        rank1 = jnp.sum(jnp.where(oh1, before, 0.0), axis=0, keepdims=True)
        carry = carry + jnp.sum(oh, axis=1, keepdims=True)

        ri_ref[0:1, rows] = e0
        ri_ref[1:2, rows] = e1
        ri_ref[2:3, rows] = rank0.astype(jnp.int32)
        ri_ref[3:4, rows] = rank1.astype(jnp.int32)
        rf_ref[0:1, rows] = g0
        rf_ref[1:2, rows] = g1
    carry_ref[...] = carry
    cnt_ref[...] = jnp.broadcast_to(carry, cnt_ref.shape)


def _oproj_route(o, o_ctx, x, mod_l, w_o, layer, g, rw_pad, rb_col, tri, n_tok):
    n_tiles = n_tok // TM
    tok = lambda i: (i, 0)
    const2 = lambda i: (0, 0)
    lane_tok = lambda i: (0, i)
    return pl.pallas_call(
        _oproj_kernel,
        grid=(n_tiles,),
        in_specs=[
            pl.BlockSpec((TM, D_MODEL), lambda i: (jnp.minimum(i, LAT_TILES - 1), 0)),
            pl.BlockSpec((TM, D_MODEL), lambda i: (jnp.maximum(i - LAT_TILES, 0), 0)),
            pl.BlockSpec((TM, D_MODEL), tok),
            pl.BlockSpec((1, N_MOD, D_MODEL), lambda i: (_mod_row(i), 0, 0)),
            pl.BlockSpec((1, D_MODEL, D_MODEL), lambda i: (layer, 0, 0)),
            pl.BlockSpec((1, D_MODEL), const2),
            pl.BlockSpec((D_MODEL, HB), const2),
            pl.BlockSpec((N_EXPERTS, 1), const2),
            pl.BlockSpec((ROUTE_BLK, ROUTE_BLK), const2),
        ],
        out_specs=[
            pl.BlockSpec((TM, D_MODEL), tok),
            pl.BlockSpec((TM, D_PACK), tok),
            pl.BlockSpec((8, TM), lane_tok),
            pl.BlockSpec((8, TM), lane_tok),
            pl.BlockSpec((N_EXPERTS, HB), const2),
        ],
        out_shape=[
            jax.ShapeDtypeStruct((n_tok, D_MODEL), F32),
            jax.ShapeDtypeStruct((n_tok, D_PACK), jnp.uint32),
            jax.ShapeDtypeStruct((8, n_tok), jnp.int32),
            jax.ShapeDtypeStruct((8, n_tok), F32),
            jax.ShapeDtypeStruct((N_EXPERTS, HB), F32),
        ],
        scratch_shapes=[pltpu.VMEM((N_EXPERTS, 1), F32)],
        compiler_params=_cparams("arbitrary"),
        name="oproj_norm_route",
    )(o, o if o_ctx is None else o_ctx, x, mod_l, w_o, g.reshape(1, D_MODEL), rw_pad, rb_col, tri)


def _moe_kernel(te_ref, tv_ref, xs_ref, wg_ref, wu_ref, wd_ref, ys_ref, wgb, wub, wdb):
    i = pl.program_id(0)
    valid = tv_ref[i]
    e = te_ref[i]
    prev = te_ref[jnp.maximum(i - 1, 0)]
    half = TM // 2

    @pl.when((i == 0) | (e != prev))
    def _():
        wgb[...] = wg_ref[0, 0].astype(BF16)
        wub[...] = wu_ref[0, 0].astype(BF16)
        wdb[...] = wd_ref[0, 0].astype(BF16)

    def ffn(rows, x):
        x = _unpack_bf16_pair(x).astype(BF16)
        a = _dot(x, wgb[...])
        u = _dot(x, wub[...])
        hm = (a * _sigmoid(a) * u).astype(BF16)
        ys_ref[rows, :] = _pack_bf16_pair(_dot(hm, wdb[...]))

    for r0 in range(0, TM, half):
        rows = pl.ds(r0, half)

        @pl.when(valid >= r0 + half)
        def _():
            ffn(rows, xs_ref[rows, :])

        @pl.when((valid > r0) & (valid < r0 + half))
        def _():
            row = r0 + lax.broadcasted_iota(jnp.int32, (half, 1), 0)
            x = xs_ref[rows, :]
            ffn(rows, jnp.where(row < valid, x, jnp.zeros_like(x)))

        @pl.when(valid <= r0)
        def _():
            ys_ref[rows, :] = jnp.zeros((half, D_PACK), jnp.uint32)


def _grouped_ffn(tile_expert, tile_valid, xs, w_gate, w_up, w_down, layer):
    n_tiles = xs.shape[0] // TM
    wmap = lambda i, te, tv: (layer, te[i], 0, 0)
    tok = lambda i, te, tv: (i, 0)
    return pl.pallas_call(
        _moe_kernel,
        grid_spec=pltpu.PrefetchScalarGridSpec(
            num_scalar_prefetch=2,
            grid=(n_tiles,),
            in_specs=[
                pl.BlockSpec((TM, D_PACK), tok),
                pl.BlockSpec((1, 1, D_MODEL, D_EXPERT), wmap),
                pl.BlockSpec((1, 1, D_MODEL, D_EXPERT), wmap),
                pl.BlockSpec((1, 1, D_EXPERT, D_MODEL), wmap),
            ],
            out_specs=pl.BlockSpec((TM, D_PACK), tok),
            scratch_shapes=[
                pltpu.VMEM((D_MODEL, D_EXPERT), BF16),
                pltpu.VMEM((D_MODEL, D_EXPERT), BF16),
                pltpu.VMEM((D_EXPERT, D_MODEL), BF16),
            ],
        ),
        out_shape=jax.ShapeDtypeStruct((xs.shape[0], D_PACK), jnp.uint32),
        compiler_params=_cparams("arbitrary"),
        name="grouped_ffn",
    )(tile_expert, tile_valid, xs, w_gate, w_up, w_down)


def _moe_layout(route_i, counts, n_tok):
    n_tiles = 2 * n_tok // TM + N_EXPERTS
    counts = counts.astype(jnp.int32)
    tiles_e = (counts + TM - 1) // TM
    tiles_end = jnp.cumsum(tiles_e)
    tile_start = tiles_end - tiles_e
    eid = jnp.arange(N_EXPERTS, dtype=jnp.int32)
    tok_oh = route_i[0:2][:, :, None] == eid
    pos = jnp.sum(jnp.where(tok_oh, tile_start * TM, 0), axis=-1) + route_i[2:4]
    tile_ids = jnp.arange(n_tiles, dtype=jnp.int32)
    n_used = tiles_end[-1]
    te = jnp.sum(tiles_end[None, :] <= jnp.minimum(tile_ids, n_used - 1)[:, None], axis=1).astype(jnp.int32)
    tile_oh = te[:, None] == eid
    cnt_t = jnp.sum(jnp.where(tile_oh, counts, 0), axis=1)
    start_t = jnp.sum(jnp.where(tile_oh, tile_start, 0), axis=1)
    tv = jnp.clip(cnt_t - (tile_ids - start_t) * TM, 0, TM)
    tv = jnp.where(tile_ids < n_used, tv, 0).astype(jnp.int32)
    return pos.astype(jnp.int32), te, tv, n_tiles


SC_CORES = 2
SC_SUBCORES = 16
SC_WORKERS = SC_CORES * SC_SUBCORES
SC_CHUNK = 32


def _sc_mesh():
    return plsc.VectorSubcoreMesh(core_axis_name="c", subcore_axis_name="s")


def _sc_worker_indices(pos, n_tok):
    n_ch = n_tok // SC_WORKERS // SC_CHUNK
    return pos.reshape(2, SC_WORKERS, n_ch, SC_CHUNK).transpose(1, 2, 0, 3), n_ch


def _sc_dispatch(h, pos, n_slots):
    n_tok = h.shape[0]
    per_w = n_tok // SC_WORKERS
    pos_w, n_ch = _sc_worker_indices(pos, n_tok)
    assert n_ch % 2 == 0 and n_ch * SC_CHUNK * SC_WORKERS == n_tok

    @functools.partial(
        pl.kernel, mesh=_sc_mesh(), out_type=jax.ShapeDtypeStruct((n_slots, h.shape[1]), h.dtype),
        scratch_types=[pltpu.VMEM((n_ch, 2, SC_CHUNK), jnp.int32), pltpu.VMEM((2, SC_CHUNK, h.shape[1]), h.dtype),
                       pltpu.SemaphoreType.DMA((2,)), pltpu.SemaphoreType.DMA((2,))],
        name="sc_dispatch")
    def dispatch(h_hbm, pos_hbm, xs_hbm, idx_v, rows_v, load_sem, scat_sem):
        wid = lax.axis_index("s") * SC_CORES + lax.axis_index("c")
        base = wid * per_w
        pltpu.sync_copy(pos_hbm.at[wid], idx_v)

        def load(c, b):
            return pltpu.make_async_copy(h_hbm.at[pl.ds(base + c * SC_CHUNK, SC_CHUNK)], rows_v.at[b],
                                         load_sem.at[b])

        def scat(c, b, k):
            return pltpu.make_async_copy(rows_v.at[b], xs_hbm.at[idx_v.at[c, k]], scat_sem.at[b])

        load(0, 0).start()

        @pl.loop(0, n_ch, step=2)
        def _(c0):
            for b in range(2):
                c = c0 + b
                load(c, b).wait()
                scat(c, b, 0).start()
                scat(c, b, 1).start()

                @pl.when(c >= 1)
                def _():
                    scat(c - 1, 1 - b, 0).wait()
                    scat(c - 1, 1 - b, 1).wait()

                @pl.when(c + 1 < n_ch)
                def _():
                    load(c + 1, 1 - b).start()

        scat(n_ch - 1, 1, 0).wait()
        scat(n_ch - 1, 1, 1).wait()

    return dispatch(h, pos_w)


def _sc_combine_gather(ys, pos):
    n_tok = pos.shape[1]
    per_w = n_tok // SC_WORKERS
    pos_w, n_ch = _sc_worker_indices(pos, n_tok)
    assert n_ch * SC_CHUNK * SC_WORKERS == n_tok

    @functools.partial(
        pl.kernel, mesh=_sc_mesh(), out_type=jax.ShapeDtypeStruct((2, n_tok, ys.shape[1]), ys.dtype),
        scratch_types=[pltpu.VMEM((n_ch, 2, SC_CHUNK), jnp.int32), pltpu.VMEM((2, SC_CHUNK, ys.shape[1]), ys.dtype),
                       pltpu.SemaphoreType.DMA((2,)), pltpu.SemaphoreType.DMA((2,))],
        name="sc_combine_gather")
    def gather(ys_hbm, pos_hbm, yg_hbm, idx_v, rows_v, gath_sem, write_sem):
        wid = lax.axis_index("s") * SC_CORES + lax.axis_index("c")
        base = wid * per_w
        pltpu.sync_copy(pos_hbm.at[wid], idx_v)

        def gath(c, k):
            return pltpu.make_async_copy(ys_hbm.at[idx_v.at[c, k]], rows_v.at[k], gath_sem.at[k])

        def write(c, k):
            return pltpu.make_async_copy(rows_v.at[k], yg_hbm.at[k, pl.ds(base + c * SC_CHUNK, SC_CHUNK)],
                                         write_sem.at[k])

        gath(0, 0).start()

        @pl.loop(0, n_ch)
        def _(c):
            gath(c, 0).wait()
            write(c, 0).start()

            @pl.when(c >= 1)
            def _():
                write(c - 1, 1).wait()

            gath(c, 1).start()
            gath(c, 1).wait()
            write(c, 1).start()
            write(c, 0).wait()

            @pl.when(c + 1 < n_ch)
            def _():
                gath(c + 1, 0).start()

        write(n_ch - 1, 1).wait()

    return gather(ys, pos_w)


def _final_kernel(x1_ref, yg_ref, gt_ref, mod_ref, fg_ref, out_ref):
    x2 = _moe_residual(x1_ref[...], yg_ref, gt_ref, mod_ref[0][5:6])
    ms = jnp.mean(x2 * x2, axis=-1, keepdims=True)
    out_ref[...] = x2 * lax.rsqrt(ms + NORM_EPS) * fg_ref[...]


def _combine_final(x1, yg, gates_t, mod_l, final_g, n_tok):
    tok = lambda i: (i, 0)
    return pl.pallas_call(
        _final_kernel,
        grid=(n_tok // TM,),
        in_specs=[
            pl.BlockSpec((TM, D_MODEL), tok),
            pl.BlockSpec((2, TM, D_PACK), lambda i: (0, i, 0)),
            pl.BlockSpec((TM, 2), tok),
            pl.BlockSpec((1, N_MOD, D_MODEL), lambda i: (_mod_row(i), 0, 0)),
            pl.BlockSpec((1, D_MODEL), lambda i: (0, 0)),
        ],
        out_specs=pl.BlockSpec((TM, D_MODEL), tok),
        out_shape=jax.ShapeDtypeStruct((n_tok, D_MODEL), F32),
        compiler_params=_cparams("parallel"),
        name="moe_combine_final",
    )(x1, yg, gates_t, mod_l, final_g.reshape(1, D_MODEL))


def _diff_lambda_init(layer_idx):
    return 0.8 - 0.6 * math.exp(-0.3 * (layer_idx - 1))


def kernel(x, c, ctx, c_ctx, mod_w, mod_b, norm_mix, norm_ffn, w_qkv, w_o, na_rpb, diff_lambda, diff_subln,
           router_w, router_b, expert_w_gate, expert_w_up, expert_w_down, final_norm):
    mod = _modulation(c, c_ctx, mod_w, mod_b)
    stream = (x.reshape(T_LAT, D_MODEL), ctx.reshape(T_CTX, D_MODEL))
    w_qk_b = w_qkv[:, :, :2 * D_MODEL].astype(BF16)
    w_vt_b = jnp.swapaxes(w_qkv[:, :, 2 * D_MODEL:], 1, 2).astype(BF16)
    w_o_b = w_o.astype(BF16)
    na_tables = _na_pair_table(na_rpb.reshape(-1, N_DR, N_DC))
    cs, sn = _rope_tables()
    rw_pad = jnp.zeros((D_MODEL, HB), F32).at[:, :N_EXPERTS].set(router_w)
    rb_col = router_b.reshape(N_EXPERTS, 1).astype(F32)
    blk = jnp.arange(ROUTE_BLK)
    tri = (blk[:, None] < blk[None, :]).astype(BF16)

    for i in range(DEPTH):
        last = i == DEPTH - 1
        j = i // 2
        is_diff = i % 2 == 1
        xa, q, k, v = _qkv(stream, mod[i], norm_mix[i], w_qk_b, w_vt_b, i, cs, sn, rope=is_diff)
        if not is_diff:
            o, o_ctx = _na_attention(na_tables, j, q, k, v)
        else:
            lam_init = _diff_lambda_init(i + 1)
            sg = diff_subln[j].reshape(HB, 1)
            outs = _diff_attention(diff_lambda[j], sg, q, k, v, lam_init, ctx_out=not last)
            o, o_ctx = outs if not last else (outs[0], None)
        n_tok = T_LAT if last else T_ALL
        x1, h, route_i, route_f, cnt = _oproj_route(o, o_ctx, xa, mod[i], w_o_b, i, norm_ffn[i], rw_pad,
                                                    rb_col, tri, n_tok)
        pos, te, tv, n_tiles = _moe_layout(route_i, cnt[:, 0], n_tok)
        xs = _sc_dispatch(h, pos, n_tiles * TM)
        ys = _grouped_ffn(te, tv, xs, expert_w_gate, expert_w_up, expert_w_down, i)
        yg = _sc_combine_gather(ys, pos)
        stream = (x1, yg, route_f[0:2].T, mod[i])
    out = _combine_final(*stream, final_norm, T_LAT)
    return out.reshape(BATCH, SEQ, D_MODEL)
```

```python
import functools
import math

import jax
import jax.numpy as jnp
from jax import lax
from jax.experimental import pallas as pl
from jax.experimental.pallas import tpu as pltpu
from jax.experimental.pallas import tpu_sc as plsc

F32 = jnp.float32
BF16 = jnp.bfloat16

D_MODEL = 1024
BATCH = 8
SEQ = 2048
DEPTH = 4
CTX_LEN = 256
GRID_W = 64
ROWS = SEQ // GRID_W
NA_HEADS = 16
NA_WIN_ROWS = 8
NA_WIN_COLS = 16
DIFF_HEADS = 8
DIFF_HEAD_DIM = 64
ROPE_THETA = 10000.0
N_EXPERTS = 16
N_GROUPS = 4
EXPERTS_PER_GROUP = 4
D_EXPERT = 1024
N_MOD = 6
NORM_EPS = 1e-6

T_LAT = BATCH * SEQ
T_CTX = BATCH * CTX_LEN
T_ALL = T_LAT + T_CTX
TM = 512
ROUTE_BLK = TM // 2
LAT_TILES = T_LAT // TM
CTX_TILES = T_CTX // TM
TILES_PER_BATCH = SEQ // TM
CTX_ROW = BATCH
MOD_ROWS = 16
HB = 128
N_HB = D_MODEL // HB
NA_RB = 4
NA_BAND_ROWS = 12
NA_Q = NA_RB * GRID_W
NA_BAND = NA_BAND_ROWS * GRID_W
TQ_DIFF = 512
MASK_VALUE = -1e30
LOG2E = math.log2(math.e)
Q_SCALE = 0.125 * LOG2E
VMEM_LIMIT = 56 * 1024 * 1024


def _cparams(*sem):
    return pltpu.CompilerParams(dimension_semantics=sem, vmem_limit_bytes=VMEM_LIMIT)


def _dot(a, b):
    return jnp.dot(a, b, preferred_element_type=F32)


def _dot_nt(a, b):
    return lax.dot_general(a, b, (((1,), (1,)), ((), ())), preferred_element_type=F32)


def _sigmoid(x):
    return 1.0 / (1.0 + jnp.exp(-x))


D_PACK = D_MODEL // 2


def _pack_bf16_pair(x):
    lo = lax.bitcast_convert_type(x[:, :D_PACK].astype(BF16).astype(F32), jnp.uint32)
    hi = lax.bitcast_convert_type(x[:, D_PACK:].astype(BF16).astype(F32), jnp.uint32)
    return (lo >> 16) | hi


def _unpack_bf16_pair(w):
    lo = lax.bitcast_convert_type(w << 16, F32)
    hi = lax.bitcast_convert_type(w & jnp.uint32(0xFFFF0000), F32)
    return jnp.concatenate([lo, hi], axis=1)


def _mod_row(i):
    return jnp.minimum(i // TILES_PER_BATCH, CTX_ROW)


def _mod_kernel(act_ref, w_ref, b_ref, o_ref):
    a = act_ref[...]
    a = a * _sigmoid(a)
    o_ref[0] = _dot(a.astype(BF16), w_ref[0].astype(BF16)) + b_ref[0]


def _modulation(c, c_ctx, mod_w, mod_b):
    tn = 1536
    act = jnp.zeros((MOD_ROWS, D_MODEL), F32).at[:BATCH].set(c).at[CTX_ROW].set(c_ctx)
    out = pl.pallas_call(
        _mod_kernel,
        grid=(DEPTH, N_MOD * D_MODEL // tn),
        in_specs=[
            pl.BlockSpec((MOD_ROWS, D_MODEL), lambda l, j: (0, 0)),
            pl.BlockSpec((1, D_MODEL, tn), lambda l, j: (l, 0, j)),
            pl.BlockSpec((1, 1, tn), lambda l, j: (l, 0, j)),
        ],
        out_specs=pl.BlockSpec((1, MOD_ROWS, tn), lambda l, j: (l, 0, j)),
        out_shape=jax.ShapeDtypeStruct((DEPTH, MOD_ROWS, N_MOD * D_MODEL), F32),
        compiler_params=_cparams("parallel", "parallel"),
        name="adaln_mod",
    )(act, mod_w, mod_b.reshape(DEPTH, 1, N_MOD * D_MODEL))
    return out.reshape(DEPTH, MOD_ROWS, N_MOD, D_MODEL)


def _rms_modulate(x, g, shift, scale):
    ms = jnp.mean(x * x, axis=-1, keepdims=True)
    y = x * lax.rsqrt(ms + NORM_EPS) * g
    return y * (1.0 + scale) + shift


def _rope_block(xb, cs, sn):
    lane = lax.broadcasted_iota(jnp.int32, xb.shape, 1)
    partner = jnp.where((lane & 63) < 32, pltpu.roll(xb, 96, 1), pltpu.roll(xb, 32, 1))
    return xb * cs + partner * sn


def _moe_residual(x1, yg_ref, gt_ref, gate_ffn):
    g = gt_ref[...]
    y = g[:, 0:1] * _unpack_bf16_pair(yg_ref[0]) + g[:, 1:2] * _unpack_bf16_pair(yg_ref[1])
    return x1 + gate_ffn * y


def _qkv_kernel(*refs, rope, first):
    if first:
        xl_ref, xc_ref, *refs = refs
        x = jnp.where(pl.program_id(0) < LAT_TILES, xl_ref[...], xc_ref[...])
    else:
        x1_ref, yg_ref, gt_ref, modp_ref, *refs = refs
        x = _moe_residual(x1_ref[...], yg_ref, gt_ref, modp_ref[0][5:6])
    mod_ref, g_ref, wqk_ref, wvt_ref, cs_ref, sn_ref, x_ref, q_ref, k_ref, vt_ref = refs
    x_ref[...] = x
    m = mod_ref[0]
    h = _rms_modulate(x, g_ref[...], m[0:1], m[1:2]).astype(BF16)
    for idx, out in enumerate((q_ref, k_ref)):
        acc = _dot(h, wqk_ref[0, :, idx * D_MODEL:(idx + 1) * D_MODEL])
        if rope:
            cs = cs_ref[...]
            sn = sn_ref[...]
            acc = jnp.concatenate(
                [_rope_block(acc[:, j * HB:(j + 1) * HB], cs, sn) for j in range(N_HB)], axis=1)
        if idx == 0:
            acc = acc * Q_SCALE
        out[...] = acc.astype(BF16)
    vt_ref[...] = _dot_nt(wvt_ref[0], h).astype(BF16)


def _qkv(stream, mod_l, g, w_qk, w_vt, layer, cs, sn, *, rope):
    first = len(stream) == 2
    n_tiles = T_ALL // TM
    tok = lambda i: (i, 0)
    const2 = lambda i: (0, 0)
    mod_spec = pl.BlockSpec((1, N_MOD, D_MODEL), lambda i: (_mod_row(i), 0, 0))
    rope_idx = lambda i: (jnp.where(i < LAT_TILES, i % TILES_PER_BATCH, TILES_PER_BATCH), 0)
    out = jax.ShapeDtypeStruct((T_ALL, D_MODEL), BF16)
    wmap = lambda i: (layer, 0, 0)
    if first:
        stream_specs = [
            pl.BlockSpec((TM, D_MODEL), lambda i: (jnp.minimum(i, LAT_TILES - 1), 0)),
            pl.BlockSpec((TM, D_MODEL), lambda i: (jnp.maximum(i - LAT_TILES, 0), 0)),
        ]
    else:
        stream_specs = [
            pl.BlockSpec((TM, D_MODEL), tok),
            pl.BlockSpec((2, TM, D_PACK), lambda i: (0, i, 0)),
            pl.BlockSpec((TM, 2), tok),
            mod_spec,
        ]
    return pl.pallas_call(
        functools.partial(_qkv_kernel, rope=rope, first=first),
        grid=(n_tiles,),
        in_specs=stream_specs + [
            mod_spec,
            pl.BlockSpec((1, D_MODEL), const2),
            pl.BlockSpec((1, D_MODEL, 2 * D_MODEL), wmap),
            pl.BlockSpec((1, D_MODEL, D_MODEL), wmap),
            pl.BlockSpec((TM, HB), rope_idx),
            pl.BlockSpec((TM, HB), rope_idx),
        ],
        out_specs=[pl.BlockSpec((TM, D_MODEL), tok), pl.BlockSpec((TM, D_MODEL), tok),
                   pl.BlockSpec((TM, D_MODEL), tok), pl.BlockSpec((D_MODEL, TM), lambda i: (0, i))],
        out_shape=[jax.ShapeDtypeStruct((T_ALL, D_MODEL), F32), out, out,
                   jax.ShapeDtypeStruct((D_MODEL, T_ALL), BF16)],
        compiler_params=_cparams("parallel"),
        name="norm_qkv_rope" if rope else "norm_qkv",
    )(*stream, mod_l, g.reshape(1, D_MODEL), w_qk, w_vt, cs, sn)


def _rope_tables():
    t = jnp.arange(SEQ)
    row = (t // GRID_W).astype(F32)
    col = (t % GRID_W).astype(F32)
    n_freq = DIFF_HEAD_DIM // 4
    inv_freq = ROPE_THETA ** (-jnp.arange(n_freq, dtype=F32) / n_freq)
    ang = jnp.concatenate([row[:, None] * inv_freq, col[:, None] * inv_freq], axis=-1)
    cos, sin = jnp.cos(ang), jnp.sin(ang)
    cs = jnp.concatenate([cos, cos, cos, cos], axis=-1)
    sn = jnp.concatenate([-sin, sin, -sin, sin], axis=-1)
    cs = jnp.concatenate([cs, jnp.ones((TM, HB), F32)], axis=0)
    sn = jnp.concatenate([sn, jnp.zeros((TM, HB), F32)], axis=0)
    return cs, sn


Q_BLK = 256
KEY_BLK = 256
SUBLANES = 8
DIFF_Q_PER_TRIP = 4


def _colwise(reduce_fn, x):
    return reduce_fn(x.reshape(x.shape[0] // SUBLANES, SUBLANES, x.shape[1]), axis=0)


def _key_blocks(k_ref, vt_ref, key_start=0, n_keys=None, bias=None):
    n_keys = k_ref.shape[0] if n_keys is None else n_keys
    blocks = []
    for c, s0 in enumerate(range(0, n_keys, KEY_BLK)):
        blocks.append((
            lambda s0=s0: k_ref[pl.ds(key_start + s0, KEY_BLK), :],
            lambda s0=s0: vt_ref[:, pl.ds(key_start + s0, KEY_BLK)],
            None if bias is None else functools.partial(bias, c),
        ))
    return blocks


def _attention_t(operands, s_scr):
    def score(u, c, mx):
        qm, blocks = operands[u]
        k_tile, _, bias = blocks[c]
        st = _dot_nt(k_tile(), qm)
        if bias is not None:
            st = st + bias()
        s_scr[u % 2, c * KEY_BLK:(c + 1) * KEY_BLK, :] = st
        return jnp.maximum(mx, _colwise(jnp.max, st))

    def expo(u, c, m, acc, den):
        _, vt_tile, _ = operands[u][1][c]
        e = jnp.exp2(s_scr[u % 2, c * KEY_BLK:(c + 1) * KEY_BLK, :] - m)
        o = _dot(vt_tile(), e.astype(BF16))
        return (o if acc is None else acc + o), den + _colwise(jnp.sum, e)

    neg = jnp.full((SUBLANES, Q_BLK), -jnp.inf, F32)
    zero = jnp.zeros((SUBLANES, Q_BLK), F32)
    n_blocks = [len(blocks) for _, blocks in operands]
    mx = neg
    for c in range(n_blocks[0]):
        mx = score(0, c, mx)
    outs = []
    for u in range(len(operands)):
        m = jnp.max(mx, axis=0, keepdims=True)
        acc, den, mx = None, zero, neg
        n_next = n_blocks[u + 1] if u + 1 < len(operands) else 0
        for c in range(max(n_blocks[u], n_next)):
            if c < n_next:
                mx = score(u + 1, c, mx)
            if c < n_blocks[u]:
                acc, den = expo(u, c, m, acc, den)
        outs.append(acc / jnp.sum(den, axis=0, keepdims=True))
    return outs


def _half_masks(shape):
    lane = lax.broadcasted_iota(jnp.int32, shape, 1)
    return lane < 64, lane >= 64


N_DR = 2 * NA_WIN_ROWS - 1
N_DC = 2 * NA_WIN_COLS - 1
PAIR_MASK_FIRST = N_DR - 1
PAIR_MASK_SECOND = N_DR
PAIR_MASKED = N_DR + 1
N_PAIRS = N_DR + 2
NA_Q_PER_TRIP = 3
DR_LO = NA_WIN_ROWS - 1 - NA_WIN_ROWS // 2
DR_HI = DR_LO + NA_WIN_ROWS - 1


def _na_pair_table(rpb):
    h = rpb.shape[0]
    w = GRID_W
    v = jnp.zeros((h, N_DR, 2 * w), F32)
    v = v.at[..., 0:NA_WIN_COLS].set(rpb[..., NA_WIN_COLS - 1:].astype(F32))
    v = v.at[..., 2 * w - (NA_WIN_COLS - 1):].set(rpb[..., :NA_WIN_COLS - 1].astype(F32))
    y = jnp.broadcast_to(v[:, :, None, :], (h, N_DR, w, 2 * w)).reshape(h, N_DR, w * 2 * w)
    t = y[..., :w * (2 * w - 1)].reshape(h, N_DR, w, 2 * w - 1)[..., :w]
    col = jnp.arange(w)
    cstart = jnp.clip(col - NA_WIN_COLS // 2, 0, w - NA_WIN_COLS)
    cvalid = (col[None, :] >= cstart[:, None]) & (col[None, :] < cstart[:, None] + NA_WIN_COLS)
    tt = jnp.swapaxes(jnp.where(cvalid, t * LOG2E, MASK_VALUE), -1, -2)
    masked = jnp.full((h, 1, w, w), MASK_VALUE, F32)
    regular = jnp.concatenate([tt[:, 1:], tt[:, :-1]], axis=-1)
    first = jnp.concatenate([masked, tt[:, DR_HI:DR_HI + 1]], axis=-1)
    second = jnp.concatenate([tt[:, DR_LO:DR_LO + 1], masked], axis=-1)
    return jnp.concatenate([regular, first, second, jnp.concatenate([masked, masked], axis=-1)], axis=1)


def _na_pair_entries(q_row0, band_row0, band_rows):
    entries = []
    for j in range(band_rows):
        kr = band_row0 + j
        row = []
        for a in range(0, NA_RB, 2):
            ok, d = [], []
            for r in (q_row0 + a, q_row0 + a + 1):
                r0 = min(max(r - NA_WIN_ROWS // 2, 0), ROWS - NA_WIN_ROWS)
                ok.append(r0 <= kr < r0 + NA_WIN_ROWS)
                d.append(kr - r + NA_WIN_ROWS - 1)
            if ok[0] and ok[1]:
                row.append(d[0] - 1)
            elif ok[1]:
                assert d[1] == DR_HI
                row.append(PAIR_MASK_FIRST)
            elif ok[0]:
                assert d[0] == DR_LO
                row.append(PAIR_MASK_SECOND)
            else:
                row.append(PAIR_MASKED)
        entries.append(row)
    return entries


def _na_kernel(tab_ref, q_ref, k_ref, vt_ref, qc_ref, kc_ref, vct_ref, o_ref, oc_ref, s_scr):
    n_blocks = ROWS // NA_RB
    rows_per_chunk = KEY_BLK // GRID_W
    ctx_blocks = _key_blocks(kc_ref, vct_ref)

    def operands(q_start, band_start, band_rows, entries):
        q = q_ref[pl.ds(q_start, NA_Q), :]
        ops = []
        for hh, msk in enumerate(_half_masks(q.shape)):
            def bias(c, hh=hh):
                rows = entries[c * rows_per_chunk:(c + 1) * rows_per_chunk]
                return jnp.concatenate(
                    [jnp.concatenate([tab_ref[hh, e] for e in row], axis=1) for row in rows], axis=0)

            band = _key_blocks(k_ref, vt_ref, band_start, band_rows * GRID_W, bias)
            ops.append((jnp.where(msk, q, jnp.zeros_like(q)), band + ctx_blocks))
        return ops

    def merge_heads(o_lo, o_hi):
        feat = lax.broadcasted_iota(jnp.int32, o_lo.shape, 0)
        return jnp.where(feat < HB // 2, o_lo, o_hi).T.astype(BF16)

    def run(blocks):
        outs = _attention_t([op for blk in blocks for op in operands(*blk)], s_scr)
        for i, blk in enumerate(blocks):
            o_ref[pl.ds(blk[0], NA_Q), :] = merge_heads(outs[2 * i], outs[2 * i + 1])

    last_band = ROWS - NA_WIN_ROWS
    run([(0, 0, NA_WIN_ROWS, _na_pair_entries(0, 0, NA_WIN_ROWS)),
         ((n_blocks - 1) * NA_Q, last_band * GRID_W, NA_WIN_ROWS,
          _na_pair_entries(ROWS - NA_RB, last_band, NA_WIN_ROWS))])
    interior = _na_pair_entries(NA_RB, 0, NA_BAND_ROWS)

    def body(t, carry):
        blocks = []
        for i in range(NA_Q_PER_TRIP):
            rb = 1 + t * NA_Q_PER_TRIP + i
            band_start = pl.multiple_of((rb - 1) * NA_Q, NA_Q)
            blocks.append((pl.multiple_of(rb * NA_Q, NA_Q), band_start, NA_BAND_ROWS, interior))
        run(blocks)
        return carry

    lax.fori_loop(0, (n_blocks - 2) // NA_Q_PER_TRIP, body, 0)

    qc = qc_ref[...]
    outs = _attention_t([(jnp.where(msk, qc, jnp.zeros_like(qc)), ctx_blocks) for msk in _half_masks(qc.shape)],
                        s_scr)
    oc_ref[...] = merge_heads(outs[0], outs[1])


def _na_attention(tables, layer_na, q, k, vt):
    assert NA_Q == Q_BLK and CTX_LEN == Q_BLK and (ROWS // NA_RB - 2) % NA_Q_PER_TRIP == 0
    lat = lambda hp, b: (b, hp)
    ctx = lambda hp, b: (T_LAT // CTX_LEN + b, hp)
    return pl.pallas_call(
        _na_kernel,
        grid=(N_HB, BATCH),
        in_specs=[
            pl.BlockSpec((2, N_PAIRS, GRID_W, 2 * GRID_W), lambda hp, b: (layer_na * N_HB + hp, 0, 0, 0)),
            pl.BlockSpec((SEQ, HB), lat),
            pl.BlockSpec((SEQ, HB), lat),
            pl.BlockSpec((HB, SEQ), lambda hp, b: (hp, b)),
            pl.BlockSpec((CTX_LEN, HB), ctx),
            pl.BlockSpec((CTX_LEN, HB), ctx),
            pl.BlockSpec((HB, CTX_LEN), lambda hp, b: (hp, T_LAT // CTX_LEN + b)),
        ],
        out_specs=[pl.BlockSpec((SEQ, HB), lat), pl.BlockSpec((CTX_LEN, HB), lat)],
        out_shape=[jax.ShapeDtypeStruct((T_LAT, D_MODEL), BF16), jax.ShapeDtypeStruct((T_CTX, D_MODEL), BF16)],
        scratch_shapes=[pltpu.VMEM((2, NA_BAND + CTX_LEN, Q_BLK), F32)],
        compiler_params=_cparams("parallel", "parallel"),
        name="na_attention",
    )(tables, q, k, vt, q, k, vt)


def _diff_lambda(lam_ref, lam_init):
    lam = lam_ref[...]
    a = jnp.sum(lam[0:1] * lam[1:2], axis=-1, keepdims=True)
    b = jnp.sum(lam[2:3] * lam[3:4], axis=-1, keepdims=True)
    return jnp.exp(a) - jnp.exp(b) + lam_init


def _diff_kernel(lam_ref, sg_ref, q_ref, k_ref, vt_ref, qc_ref, kc_ref, vct_ref, o_ref, *rest, lam_init):
    *maybe_oc_ref, s_scr = rest
    lam = _diff_lambda(lam_ref, lam_init)
    sg_col = sg_ref[...]

    def rows(qs, pieces):
        blocks = [blk for k_r, vt_r in pieces for blk in _key_blocks(k_r, vt_r)]
        outs = _attention_t([(jnp.where(msk, q, jnp.zeros_like(q)), blocks)
                             for q in qs for msk in _half_masks(q.shape)], s_scr)
        res = []
        for o1, o2 in zip(outs[0::2], outs[1::2]):
            ot = o1 - lam * o2
            ms = jnp.mean(ot * ot, axis=0, keepdims=True)
            ot = ot * lax.rsqrt(ms + NORM_EPS) * sg_col * (1.0 - lam_init)
            res.append(ot.T.astype(BF16))
        return res

    def body(t, carry):
        starts = [pl.multiple_of((t * DIFF_Q_PER_TRIP + j) * Q_BLK, Q_BLK) for j in range(DIFF_Q_PER_TRIP)]
        outs = rows([q_ref[pl.ds(r0, Q_BLK), :] for r0 in starts], [(k_ref, vt_ref), (kc_ref, vct_ref)])
        for r0, o in zip(starts, outs):
            o_ref[pl.ds(r0, Q_BLK), :] = o
        return carry

    lax.fori_loop(0, SEQ // (Q_BLK * DIFF_Q_PER_TRIP), body, 0)
    if maybe_oc_ref:
        oc_ref, = maybe_oc_ref
        assert CTX_LEN == Q_BLK
        oc_ref[...], = rows([qc_ref[...]], [(kc_ref, vct_ref)])


def _diff_attention(lam, sg, q, k, vt, lam_init, *, ctx_out):
    lat = lambda b, h: (b, h)
    ctx = lambda b, h: (T_LAT // CTX_LEN + b, h)
    lat_t = lambda b, h: (h, b)
    ctx_t = lambda b, h: (h, T_LAT // CTX_LEN + b)
    out_specs = [pl.BlockSpec((SEQ, HB), lat)]
    out_shape = [jax.ShapeDtypeStruct((T_LAT, D_MODEL), BF16)]
    if ctx_out:
        out_specs.append(pl.BlockSpec((CTX_LEN, HB), lat))
        out_shape.append(jax.ShapeDtypeStruct((T_CTX, D_MODEL), BF16))
    return pl.pallas_call(
        functools.partial(_diff_kernel, lam_init=lam_init),
        grid=(BATCH, DIFF_HEADS),
        in_specs=[
            pl.BlockSpec(lam.shape, lambda b, h: (0, 0)),
            pl.BlockSpec(sg.shape, lambda b, h: (0, 0)),
            pl.BlockSpec((SEQ, HB), lat),
            pl.BlockSpec((SEQ, HB), lat),
            pl.BlockSpec((HB, SEQ), lat_t),
            pl.BlockSpec((CTX_LEN, HB), ctx),
            pl.BlockSpec((CTX_LEN, HB), ctx),
            pl.BlockSpec((HB, CTX_LEN), ctx_t),
        ],
        out_specs=out_specs,
        out_shape=out_shape,
        scratch_shapes=[pltpu.VMEM((2, SEQ + CTX_LEN, Q_BLK), F32)],
        compiler_params=_cparams("parallel", "parallel"),
        name="diff_attention",
    )(lam, sg, q, k, vt, q, k, vt)


def _route_rows(logits_t, bias_col):
    s = _sigmoid(logits_t)
    sel = s + bias_col
    sel_r = [sel[e:e + 1] for e in range(N_EXPERTS)]
    s_r = [s[e:e + 1] for e in range(N_EXPERTS)]
    group_scores = []
    for g in range(N_GROUPS):
        v = sel_r[g * EXPERTS_PER_GROUP:(g + 1) * EXPERTS_PER_GROUP]
        pairs = [v[a] + v[b] for a in range(EXPERTS_PER_GROUP) for b in range(a + 1, EXPERTS_PER_GROUP)]
        group_scores.append(functools.reduce(jnp.maximum, pairs))
    best = group_scores[0]
    grp = jnp.zeros(best.shape, jnp.int32)
    for g in range(1, N_GROUPS):
        upd = group_scores[g] > best
        best = jnp.where(upd, group_scores[g], best)
        grp = jnp.where(upd, g, grp)

    def pick(rows, j):
        out = rows[j]
        for g in range(1, N_GROUPS):
            out = jnp.where(grp == g, rows[g * EXPERTS_PER_GROUP + j], out)
        return out

    w = [pick(sel_r, j) for j in range(EXPERTS_PER_GROUP)]
    sc = [pick(s_r, j) for j in range(EXPERTS_PER_GROUP)]

    def argmax_first(vals):
        bv = vals[0]
        bi = jnp.zeros(bv.shape, jnp.int32)
        for j in range(1, len(vals)):
            upd = vals[j] > bv
            bv = jnp.where(upd, vals[j], bv)
            bi = jnp.where(upd, j, bi)
        return bi

    i1 = argmax_first(w)
    i2 = argmax_first([jnp.where(i1 == j, -jnp.inf, w[j]) for j in range(EXPERTS_PER_GROUP)])

    def take(vals, idx):
        out = vals[0]
        for j in range(1, len(vals)):
            out = jnp.where(idx == j, vals[j], out)
        return out

    s1 = take(sc, i1)
    s2 = take(sc, i2)
    den = s1 + s2
    return grp * EXPERTS_PER_GROUP + i1, grp * EXPERTS_PER_GROUP + i2, s1 / den, s2 / den


def _oproj_kernel(o_ref, oc_ref, x_ref, mod_ref, w_ref, g_ref, rw_ref, rb_ref, tri_ref,
                  x1_ref, h_ref, ri_ref, rf_ref, cnt_ref, carry_ref):
    i = pl.program_id(0)

    @pl.when(i == 0)
    def _():
        carry_ref[...] = jnp.zeros_like(carry_ref)

    m = mod_ref[0]
    rw = rw_ref[...].astype(BF16)
    ri_ref[...] = jnp.zeros_like(ri_ref)
    rf_ref[...] = jnp.zeros_like(rf_ref)
    carry = carry_ref[...]
    for r0 in range(0, TM, ROUTE_BLK):
        rows = pl.ds(r0, ROUTE_BLK)
        o = jnp.where(i < LAT_TILES, o_ref[rows, :], oc_ref[rows, :])
        x1 = x_ref[rows, :] + m[2:3] * _dot(o, w_ref[0])
        x1_ref[rows, :] = x1
        h = _rms_modulate(x1, g_ref[...], m[3:4], m[4:5])
        hb = h.astype(BF16)
        h_ref[rows, :] = _pack_bf16_pair(h)

        logits = _dot(hb, rw)
        logits_t = logits.T[:N_EXPERTS]
        e0, e1, g0, g1 = _route_rows(logits_t, rb_ref[...])

        eiota = lax.broadcasted_iota(jnp.int32, (N_EXPERTS, ROUTE_BLK), 0)
        oh0 = eiota == e0
        oh1 = eiota == e1
        oh = jnp.where(oh0 | oh1, 1.0, 0.0)
        before = _dot(oh.astype(BF16), tri_ref[...]) + carry
        rank0 = jnp.sum(jnp.where(oh0, before, 0.0), axis=0, keepdims=True)
        rank1 = jnp.sum(jnp.where(oh1, before, 0.0), axis=0, keepdims=True)
        carry = carry + jnp.sum(oh, axis=1, keepdims=True)

        ri_ref[0:1, rows] = e0
        ri_ref[1:2, rows] = e1
        ri_ref[2:3, rows] = rank0.astype(jnp.int32)
        ri_ref[3:4, rows] = rank1.astype(jnp.int32)
        rf_ref[0:1, rows] = g0
        rf_ref[1:2, rows] = g1
    carry_ref[...] = carry
    cnt_ref[...] = jnp.broadcast_to(carry, cnt_ref.shape)


def _oproj_route(o, o_ctx, x, mod_l, w_o, layer, g, rw_pad, rb_col, tri, n_tok):
    n_tiles = n_tok // TM
    tok = lambda i: (i, 0)
    const2 = lambda i: (0, 0)
    lane_tok = lambda i: (0, i)
    return pl.pallas_call(
        _oproj_kernel,
        grid=(n_tiles,),
        in_specs=[
            pl.BlockSpec((TM, D_MODEL), lambda i: (jnp.minimum(i, LAT_TILES - 1), 0)),
            pl.BlockSpec((TM, D_MODEL), lambda i: (jnp.maximum(i - LAT_TILES, 0), 0)),
            pl.BlockSpec((TM, D_MODEL), tok),
            pl.BlockSpec((1, N_MOD, D_MODEL), lambda i: (_mod_row(i), 0, 0)),
            pl.BlockSpec((1, D_MODEL, D_MODEL), lambda i: (layer, 0, 0)),
            pl.BlockSpec((1, D_MODEL), const2),
            pl.BlockSpec((D_MODEL, HB), const2),
            pl.BlockSpec((N_EXPERTS, 1), const2),
            pl.BlockSpec((ROUTE_BLK, ROUTE_BLK), const2),
        ],
        out_specs=[
            pl.BlockSpec((TM, D_MODEL), tok),
            pl.BlockSpec((TM, D_PACK), tok),
            pl.BlockSpec((8, TM), lane_tok),
            pl.BlockSpec((8, TM), lane_tok),
            pl.BlockSpec((N_EXPERTS, HB), const2),
        ],
        out_shape=[
            jax.ShapeDtypeStruct((n_tok, D_MODEL), F32),
            jax.ShapeDtypeStruct((n_tok, D_PACK), jnp.uint32),
            jax.ShapeDtypeStruct((8, n_tok), jnp.int32),
            jax.ShapeDtypeStruct((8, n_tok), F32),
            jax.ShapeDtypeStruct((N_EXPERTS, HB), F32),
        ],
        scratch_shapes=[pltpu.VMEM((N_EXPERTS, 1), F32)],
        compiler_params=_cparams("arbitrary"),
        name="oproj_norm_route",
    )(o, o if o_ctx is None else o_ctx, x, mod_l, w_o, g.reshape(1, D_MODEL), rw_pad, rb_col, tri)


def _moe_kernel(te_ref, tv_ref, xs_ref, wg_ref, wu_ref, wd_ref, ys_ref, wgb, wub, wdb):
    i = pl.program_id(0)
    valid = tv_ref[i]
    e = te_ref[i]
    prev = te_ref[jnp.maximum(i - 1, 0)]
    half = TM // 2

    @pl.when((i == 0) | (e != prev))
    def _():
        wgb[...] = wg_ref[0, 0].astype(BF16)
        wub[...] = wu_ref[0, 0].astype(BF16)
        wdb[...] = wd_ref[0, 0].astype(BF16)

    def ffn(rows, x):
        x = _unpack_bf16_pair(x).astype(BF16)
        a = _dot(x, wgb[...])
        u = _dot(x, wub[...])
        hm = (a * _sigmoid(a) * u).astype(BF16)
        ys_ref[rows, :] = _pack_bf16_pair(_dot(hm, wdb[...]))

    for r0 in range(0, TM, half):
        rows = pl.ds(r0, half)

        @pl.when(valid >= r0 + half)
        def _():
            ffn(rows, xs_ref[rows, :])

        @pl.when((valid > r0) & (valid < r0 + half))
        def _():
            row = r0 + lax.broadcasted_iota(jnp.int32, (half, 1), 0)
            x = xs_ref[rows, :]
            ffn(rows, jnp.where(row < valid, x, jnp.zeros_like(x)))

        @pl.when(valid <= r0)
        def _():
            ys_ref[rows, :] = jnp.zeros((half, D_PACK), jnp.uint32)


def _grouped_ffn(tile_expert, tile_valid, xs, w_gate, w_up, w_down, layer):
    n_tiles = xs.shape[0] // TM
    wmap = lambda i, te, tv: (layer, te[i], 0, 0)
    tok = lambda i, te, tv: (i, 0)
    return pl.pallas_call(
        _moe_kernel,
        grid_spec=pltpu.PrefetchScalarGridSpec(
            num_scalar_prefetch=2,
            grid=(n_tiles,),
            in_specs=[
                pl.BlockSpec((TM, D_PACK), tok),
                pl.BlockSpec((1, 1, D_MODEL, D_EXPERT), wmap),
                pl.BlockSpec((1, 1, D_MODEL, D_EXPERT), wmap),
                pl.BlockSpec((1, 1, D_EXPERT, D_MODEL), wmap),
            ],
            out_specs=pl.BlockSpec((TM, D_PACK), tok),
            scratch_shapes=[
                pltpu.VMEM((D_MODEL, D_EXPERT), BF16),
                pltpu.VMEM((D_MODEL, D_EXPERT), BF16),
                pltpu.VMEM((D_EXPERT, D_MODEL), BF16),
            ],
        ),
        out_shape=jax.ShapeDtypeStruct((xs.shape[0], D_PACK), jnp.uint32),
        compiler_params=_cparams("arbitrary"),
        name="grouped_ffn",
    )(tile_expert, tile_valid, xs, w_gate, w_up, w_down)


def _moe_layout(route_i, counts, n_tok):
    n_tiles = 2 * n_tok // TM + N_EXPERTS
    counts = counts.astype(jnp.int32)
    tiles_e = (counts + TM - 1) // TM
    tiles_end = jnp.cumsum(tiles_e)
    tile_start = tiles_end - tiles_e
    eid = jnp.arange(N_EXPERTS, dtype=jnp.int32)
    tok_oh = route_i[0:2][:, :, None] == eid
    pos = jnp.sum(jnp.where(tok_oh, tile_start * TM, 0), axis=-1) + route_i[2:4]
    tile_ids = jnp.arange(n_tiles, dtype=jnp.int32)
    n_used = tiles_end[-1]
    te = jnp.sum(tiles_end[None, :] <= jnp.minimum(tile_ids, n_used - 1)[:, None], axis=1).astype(jnp.int32)
    tile_oh = te[:, None] == eid
    cnt_t = jnp.sum(jnp.where(tile_oh, counts, 0), axis=1)
    start_t = jnp.sum(jnp.where(tile_oh, tile_start, 0), axis=1)
    tv = jnp.clip(cnt_t - (tile_ids - start_t) * TM, 0, TM)
    tv = jnp.where(tile_ids < n_used, tv, 0).astype(jnp.int32)
    return pos.astype(jnp.int32), te, tv, n_tiles


SC_CORES = 2
SC_SUBCORES = 16
SC_WORKERS = SC_CORES * SC_SUBCORES
SC_CHUNK = 32


def _sc_mesh():
    return plsc.VectorSubcoreMesh(core_axis_name="c", subcore_axis_name="s")


def _sc_worker_indices(pos, n_tok):
    n_ch = n_tok // SC_WORKERS // SC_CHUNK
    return pos.reshape(2, SC_WORKERS, n_ch, SC_CHUNK).transpose(1, 2, 0, 3), n_ch


def _sc_dispatch(h, pos, n_slots):
    n_tok = h.shape[0]
    per_w = n_tok // SC_WORKERS
    pos_w, n_ch = _sc_worker_indices(pos, n_tok)
    assert n_ch % 2 == 0 and n_ch * SC_CHUNK * SC_WORKERS == n_tok

    @functools.partial(
        pl.kernel, mesh=_sc_mesh(), out_type=jax.ShapeDtypeStruct((n_slots, h.shape[1]), h.dtype),
        scratch_types=[pltpu.VMEM((n_ch, 2, SC_CHUNK), jnp.int32), pltpu.VMEM((2, SC_CHUNK, h.shape[1]), h.dtype),
                       pltpu.SemaphoreType.DMA((2,)), pltpu.SemaphoreType.DMA((2,))],
        name="sc_dispatch")
    def dispatch(h_hbm, pos_hbm, xs_hbm, idx_v, rows_v, load_sem, scat_sem):
        wid = lax.axis_index("s") * SC_CORES + lax.axis_index("c")
        base = wid * per_w
        pltpu.sync_copy(pos_hbm.at[wid], idx_v)

        def load(c, b):
            return pltpu.make_async_copy(h_hbm.at[pl.ds(base + c * SC_CHUNK, SC_CHUNK)], rows_v.at[b],
                                         load_sem.at[b])

        def scat(c, b, k):
            return pltpu.make_async_copy(rows_v.at[b], xs_hbm.at[idx_v.at[c, k]], scat_sem.at[b])

        load(0, 0).start()

        @pl.loop(0, n_ch, step=2)
        def _(c0):
            for b in range(2):
                c = c0 + b
                load(c, b).wait()
                scat(c, b, 0).start()
                scat(c, b, 1).start()

                @pl.when(c >= 1)
                def _():
                    scat(c - 1, 1 - b, 0).wait()
                    scat(c - 1, 1 - b, 1).wait()

                @pl.when(c + 1 < n_ch)
                def _():
                    load(c + 1, 1 - b).start()

        scat(n_ch - 1, 1, 0).wait()
        scat(n_ch - 1, 1, 1).wait()

    return dispatch(h, pos_w)


def _sc_combine_gather(ys, pos):
    n_tok = pos.shape[1]
    per_w = n_tok // SC_WORKERS
    pos_w, n_ch = _sc_worker_indices(pos, n_tok)
    assert n_ch * SC_CHUNK * SC_WORKERS == n_tok

    @functools.partial(
        pl.kernel, mesh=_sc_mesh(), out_type=jax.ShapeDtypeStruct((2, n_tok, ys.shape[1]), ys.dtype),
        scratch_types=[pltpu.VMEM((n_ch, 2, SC_CHUNK), jnp.int32), pltpu.VMEM((2, SC_CHUNK, ys.shape[1]), ys.dtype),
                       pltpu.SemaphoreType.DMA((2,)), pltpu.SemaphoreType.DMA((2,))],
        name="sc_combine_gather")
    def gather(ys_hbm, pos_hbm, yg_hbm, idx_v, rows_v, gath_sem, write_sem):
        wid = lax.axis_index("s") * SC_CORES + lax.axis_index("c")
        base = wid * per_w
        pltpu.sync_copy(pos_hbm.at[wid], idx_v)

        def gath(c, k):
            return pltpu.make_async_copy(ys_hbm.at[idx_v.at[c, k]], rows_v.at[k], gath_sem.at[k])

        def write(c, k):
            return pltpu.make_async_copy(rows_v.at[k], yg_hbm.at[k, pl.ds(base + c * SC_CHUNK, SC_CHUNK)],
                                         write_sem.at[k])

        gath(0, 0).start()

        @pl.loop(0, n_ch)
        def _(c):
            gath(c, 0).wait()
            write(c, 0).start()

            @pl.when(c >= 1)
            def _():
                write(c - 1, 1).wait()

            gath(c, 1).start()
            gath(c, 1).wait()
            write(c, 1).start()
            write(c, 0).wait()

            @pl.when(c + 1 < n_ch)
            def _():
                gath(c + 1, 0).start()

        write(n_ch - 1, 1).wait()

    return gather(ys, pos_w)


def _final_kernel(x1_ref, yg_ref, gt_ref, mod_ref, fg_ref, out_ref):
    x2 = _moe_residual(x1_ref[...], yg_ref, gt_ref, mod_ref[0][5:6])
    ms = jnp.mean(x2 * x2, axis=-1, keepdims=True)
    out_ref[...] = x2 * lax.rsqrt(ms + NORM_EPS) * fg_ref[...]


def _combine_final(x1, yg, gates_t, mod_l, final_g, n_tok):
    tok = lambda i: (i, 0)
    return pl.pallas_call(
        _final_kernel,
        grid=(n_tok // TM,),
        in_specs=[
            pl.BlockSpec((TM, D_MODEL), tok),
            pl.BlockSpec((2, TM, D_PACK), lambda i: (0, i, 0)),
            pl.BlockSpec((TM, 2), tok),
            pl.BlockSpec((1, N_MOD, D_MODEL), lambda i: (_mod_row(i), 0, 0)),
            pl.BlockSpec((1, D_MODEL), lambda i: (0, 0)),
        ],
        out_specs=pl.BlockSpec((TM, D_MODEL), tok),
        out_shape=jax.ShapeDtypeStruct((n_tok, D_MODEL), F32),
        compiler_params=_cparams("parallel"),
        name="moe_combine_final",
    )(x1, yg, gates_t, mod_l, final_g.reshape(1, D_MODEL))


def _diff_lambda_init(layer_idx):
    return 0.8 - 0.6 * math.exp(-0.3 * (layer_idx - 1))


def kernel(x, c, ctx, c_ctx, mod_w, mod_b, norm_mix, norm_ffn, w_qkv, w_o, na_rpb, diff_lambda, diff_subln,
           router_w, router_b, expert_w_gate, expert_w_up, expert_w_down, final_norm):
    mod = _modulation(c, c_ctx, mod_w, mod_b)
    stream = (x.reshape(T_LAT, D_MODEL), ctx.reshape(T_CTX, D_MODEL))
    w_qk_b = w_qkv[:, :, :2 * D_MODEL].astype(BF16)
    w_vt_b = jnp.swapaxes(w_qkv[:, :, 2 * D_MODEL:], 1, 2).astype(BF16)
    w_o_b = w_o.astype(BF16)
    na_tables = _na_pair_table(na_rpb.reshape(-1, N_DR, N_DC))
    cs, sn = _rope_tables()
    rw_pad = jnp.zeros((D_MODEL, HB), F32).at[:, :N_EXPERTS].set(router_w)
    rb_col = router_b.reshape(N_EXPERTS, 1).astype(F32)
    blk = jnp.arange(ROUTE_BLK)
    tri = (blk[:, None] < blk[None, :]).astype(BF16)

    for i in range(DEPTH):
        last = i == DEPTH - 1
        j = i // 2
        is_diff = i % 2 == 1
        xa, q, k, v = _qkv(stream, mod[i], norm_mix[i], w_qk_b, w_vt_b, i, cs, sn, rope=is_diff)
        if not is_diff:
            o, o_ctx = _na_attention(na_tables, j, q, k, v)
        else:
            lam_init = _diff_lambda_init(i + 1)
            sg = diff_subln[j].reshape(HB, 1)
            outs = _diff_attention(diff_lambda[j], sg, q, k, v, lam_init, ctx_out=not last)
            o, o_ctx = outs if not last else (outs[0], None)
        n_tok = T_LAT if last else T_ALL
        x1, h, route_i, route_f, cnt = _oproj_route(o, o_ctx, xa, mod[i], w_o_b, i, norm_ffn[i], rw_pad,
                                                    rb_col, tri, n_tok)
        pos, te, tv, n_tiles = _moe_layout(route_i, cnt[:, 0], n_tok)
        xs = _sc_dispatch(h, pos, n_tiles * TM)
        ys = _grouped_ffn(te, tv, xs, expert_w_gate, expert_w_up, expert_w_down, i)
        yg = _sc_combine_gather(ys, pos)
        stream = (x1, yg, route_f[0:2].T, mod[i])
    out = _combine_final(*stream, final_norm, T_LAT)
    return out.reshape(BATCH, SEQ, D_MODEL)
```

```python
import functools
import math

import jax
import jax.numpy as jnp
from jax import lax
from jax.experimental import pallas as pl
from jax.experimental.pallas import tpu as pltpu
from jax.experimental.pallas import tpu_sc as plsc

F32 = jnp.float32
BF16 = jnp.bfloat16

D_MODEL = 1024
BATCH = 8
SEQ = 2048
DEPTH = 4
CTX_LEN = 256
GRID_W = 64
ROWS = SEQ // GRID_W
NA_HEADS = 16
NA_WIN_ROWS = 8
NA_WIN_COLS = 16
DIFF_HEADS = 8
DIFF_HEAD_DIM = 64
ROPE_THETA = 10000.0
N_EXPERTS = 16
N_GROUPS = 4
EXPERTS_PER_GROUP = 4
D_EXPERT = 1024
N_MOD = 6
NORM_EPS = 1e-6

T_LAT = BATCH * SEQ
T_CTX = BATCH * CTX_LEN
T_ALL = T_LAT + T_CTX
TM = 512
ROUTE_BLK = TM // 2
LAT_TILES = T_LAT // TM
CTX_TILES = T_CTX // TM
TILES_PER_BATCH = SEQ // TM
CTX_ROW = BATCH
MOD_ROWS = 16
HB = 128
N_HB = D_MODEL // HB
NA_RB = 4
NA_BAND_ROWS = 12
NA_Q = NA_RB * GRID_W
NA_BAND = NA_BAND_ROWS * GRID_W
TQ_DIFF = 512
MASK_VALUE = -1e30
LOG2E = math.log2(math.e)
Q_SCALE = 0.125 * LOG2E
VMEM_LIMIT = 56 * 1024 * 1024


def _cparams(*sem):
    return pltpu.CompilerParams(dimension_semantics=sem, vmem_limit_bytes=VMEM_LIMIT)


def _dot(a, b):
    return jnp.dot(a, b, preferred_element_type=F32)


def _dot_nt(a, b):
    return lax.dot_general(a, b, (((1,), (1,)), ((), ())), preferred_element_type=F32)


def _sigmoid(x):
    return 1.0 / (1.0 + jnp.exp(-x))


D_PACK = D_MODEL // 2


def _pack_bf16_pair(x):
    lo = lax.bitcast_convert_type(x[:, :D_PACK].astype(BF16).astype(F32), jnp.uint32)
    hi = lax.bitcast_convert_type(x[:, D_PACK:].astype(BF16).astype(F32), jnp.uint32)
    return (lo >> 16) | hi


def _unpack_bf16_pair(w):
    lo = lax.bitcast_convert_type(w << 16, F32)
    hi = lax.bitcast_convert_type(w & jnp.uint32(0xFFFF0000), F32)
    return jnp.concatenate([lo, hi], axis=1)


def _mod_row(i):
    return jnp.minimum(i // TILES_PER_BATCH, CTX_ROW)


def _mod_kernel(act_ref, w_ref, b_ref, o_ref):
    a = act_ref[...]
    a = a * _sigmoid(a)
    o_ref[0] = _dot(a.astype(BF16), w_ref[0].astype(BF16)) + b_ref[0]


def _modulation(c, c_ctx, mod_w, mod_b):
    tn = 1536
    act = jnp.zeros((MOD_ROWS, D_MODEL), F32).at[:BATCH].set(c).at[CTX_ROW].set(c_ctx)
    out = pl.pallas_call(
        _mod_kernel,
        grid=(DEPTH, N_MOD * D_MODEL // tn),
        in_specs=[
            pl.BlockSpec((MOD_ROWS, D_MODEL), lambda l, j: (0, 0)),
            pl.BlockSpec((1, D_MODEL, tn), lambda l, j: (l, 0, j)),
            pl.BlockSpec((1, 1, tn), lambda l, j: (l, 0, j)),
        ],
        out_specs=pl.BlockSpec((1, MOD_ROWS, tn), lambda l, j: (l, 0, j)),
        out_shape=jax.ShapeDtypeStruct((DEPTH, MOD_ROWS, N_MOD * D_MODEL), F32),
        compiler_params=_cparams("parallel", "parallel"),
        name="adaln_mod",
    )(act, mod_w, mod_b.reshape(DEPTH, 1, N_MOD * D_MODEL))
    return out.reshape(DEPTH, MOD_ROWS, N_MOD, D_MODEL)


def _rms_modulate(x, g, shift, scale):
    ms = jnp.mean(x * x, axis=-1, keepdims=True)
    y = x * lax.rsqrt(ms + NORM_EPS) * g
    return y * (1.0 + scale) + shift


def _rope_block(xb, cs, sn):
    lane = lax.broadcasted_iota(jnp.int32, xb.shape, 1)
    partner = jnp.where((lane & 63) < 32, pltpu.roll(xb, 96, 1), pltpu.roll(xb, 32, 1))
    return xb * cs + partner * sn


def _moe_residual(x1, y0, y1, g, gate_ffn):
    y = g[:, 0:1] * _unpack_bf16_pair(y0) + g[:, 1:2] * _unpack_bf16_pair(y1)
    return x1 + gate_ffn * y


def _qkv_kernel(*refs, rope, first):
    n_stream = 2 if first else 4
    stream_refs, refs = refs[:n_stream], refs[n_stream:]
    mod_ref, g_ref, wqk_ref, wvt_ref, cs_ref, sn_ref, x_ref, q_ref, k_ref, vt_ref = refs
    m = mod_ref[0]
    for r0 in range(0, TM, ROUTE_BLK):
        rows = pl.ds(r0, ROUTE_BLK)
        if first:
            xl_ref, xc_ref = stream_refs
            x = jnp.where(pl.program_id(0) < LAT_TILES, xl_ref[rows, :], xc_ref[rows, :])
        else:
            x1_ref, yg_ref, gt_ref, modp_ref = stream_refs
            x = _moe_residual(x1_ref[rows, :], yg_ref[0, rows, :], yg_ref[1, rows, :], gt_ref[rows, :],
                              modp_ref[0][5:6])
        x_ref[rows, :] = x
        h = _rms_modulate(x, g_ref[...], m[0:1], m[1:2]).astype(BF16)
        for idx, out in enumerate((q_ref, k_ref)):
            acc = _dot(h, wqk_ref[0, :, idx * D_MODEL:(idx + 1) * D_MODEL])
            if rope:
                cs = cs_ref[rows, :]
                sn = sn_ref[rows, :]
                acc = jnp.concatenate(
                    [_rope_block(acc[:, j * HB:(j + 1) * HB], cs, sn) for j in range(N_HB)], axis=1)
            if idx == 0:
                acc = acc * Q_SCALE
            out[rows, :] = acc.astype(BF16)
        vt_ref[:, rows] = _dot_nt(wvt_ref[0], h).astype(BF16)


def _qkv(stream, mod_l, g, w_qk, w_vt, layer, cs, sn, *, rope):
    first = len(stream) == 2
    n_tiles = T_ALL // TM
    tok = lambda i: (i, 0)
    const2 = lambda i: (0, 0)
    mod_spec = pl.BlockSpec((1, N_MOD, D_MODEL), lambda i: (_mod_row(i), 0, 0))
    rope_idx = lambda i: (jnp.where(i < LAT_TILES, i % TILES_PER_BATCH, TILES_PER_BATCH), 0)
    out = jax.ShapeDtypeStruct((T_ALL, D_MODEL), BF16)
    wmap = lambda i: (layer, 0, 0)
    if first:
        stream_specs = [
            pl.BlockSpec((TM, D_MODEL), lambda i: (jnp.minimum(i, LAT_TILES - 1), 0)),
            pl.BlockSpec((TM, D_MODEL), lambda i: (jnp.maximum(i - LAT_TILES, 0), 0)),
        ]
    else:
        stream_specs = [
            pl.BlockSpec((TM, D_MODEL), tok),
            pl.BlockSpec((2, TM, D_PACK), lambda i: (0, i, 0)),
            pl.BlockSpec((TM, 2), tok),
            mod_spec,
        ]
    return pl.pallas_call(
        functools.partial(_qkv_kernel, rope=rope, first=first),
        grid=(n_tiles,),
        in_specs=stream_specs + [
            mod_spec,
            pl.BlockSpec((1, D_MODEL), const2),
            pl.BlockSpec((1, D_MODEL, 2 * D_MODEL), wmap),
            pl.BlockSpec((1, D_MODEL, D_MODEL), wmap),
            pl.BlockSpec((TM, HB), rope_idx),
            pl.BlockSpec((TM, HB), rope_idx),
        ],
        out_specs=[pl.BlockSpec((TM, D_MODEL), tok), pl.BlockSpec((TM, D_MODEL), tok),
                   pl.BlockSpec((TM, D_MODEL), tok), pl.BlockSpec((D_MODEL, TM), lambda i: (0, i))],
        out_shape=[jax.ShapeDtypeStruct((T_ALL, D_MODEL), F32), out, out,
                   jax.ShapeDtypeStruct((D_MODEL, T_ALL), BF16)],
        compiler_params=_cparams("parallel"),
        name="norm_qkv_rope" if rope else "norm_qkv",
    )(*stream, mod_l, g.reshape(1, D_MODEL), w_qk, w_vt, cs, sn)


def _rope_tables():
    t = jnp.arange(SEQ)
    row = (t // GRID_W).astype(F32)
    col = (t % GRID_W).astype(F32)
    n_freq = DIFF_HEAD_DIM // 4
    inv_freq = ROPE_THETA ** (-jnp.arange(n_freq, dtype=F32) / n_freq)
    ang = jnp.concatenate([row[:, None] * inv_freq, col[:, None] * inv_freq], axis=-1)
    cos, sin = jnp.cos(ang), jnp.sin(ang)
    cs = jnp.concatenate([cos, cos, cos, cos], axis=-1)
    sn = jnp.concatenate([-sin, sin, -sin, sin], axis=-1)
    cs = jnp.concatenate([cs, jnp.ones((TM, HB), F32)], axis=0)
    sn = jnp.concatenate([sn, jnp.zeros((TM, HB), F32)], axis=0)
    return cs, sn


Q_BLK = 256
KEY_BLK = 256
SUBLANES = 8
DIFF_Q_PER_TRIP = 4


def _colwise(reduce_fn, x):
    return reduce_fn(x.reshape(x.shape[0] // SUBLANES, SUBLANES, x.shape[1]), axis=0)


def _key_blocks(k_ref, vt_ref, key_start=0, n_keys=None, bias=None):
    n_keys = k_ref.shape[0] if n_keys is None else n_keys
    blocks = []
    for c, s0 in enumerate(range(0, n_keys, KEY_BLK)):
        blocks.append((
            lambda s0=s0: k_ref[pl.ds(key_start + s0, KEY_BLK), :],
            lambda s0=s0: vt_ref[:, pl.ds(key_start + s0, KEY_BLK)],
            None if bias is None else functools.partial(bias, c),
        ))
    return blocks


def _attention_t(operands, s_scr):
    def score(u, c, mx):
        qm, blocks = operands[u]
        k_tile, _, bias = blocks[c]
        st = _dot_nt(k_tile(), qm)
        if bias is not None:
            st = st + bias()
        s_scr[u % 2, c * KEY_BLK:(c + 1) * KEY_BLK, :] = st
        return jnp.maximum(mx, _colwise(jnp.max, st))

    def expo(u, c, m, acc, den):
        _, vt_tile, _ = operands[u][1][c]
        e = jnp.exp2(s_scr[u % 2, c * KEY_BLK:(c + 1) * KEY_BLK, :] - m)
        o = _dot(vt_tile(), e.astype(BF16))
        return (o if acc is None else acc + o), den + _colwise(jnp.sum, e)

    neg = jnp.full((SUBLANES, Q_BLK), -jnp.inf, F32)
    zero = jnp.zeros((SUBLANES, Q_BLK), F32)
    n_blocks = [len(blocks) for _, blocks in operands]
    mx = neg
    for c in range(n_blocks[0]):
        mx = score(0, c, mx)
    outs = []
    for u in range(len(operands)):
        m = jnp.max(mx, axis=0, keepdims=True)
        acc, den, mx = None, zero, neg
        n_next = n_blocks[u + 1] if u + 1 < len(operands) else 0
        for c in range(max(n_blocks[u], n_next)):
            if c < n_next:
                mx = score(u + 1, c, mx)
            if c < n_blocks[u]:
                acc, den = expo(u, c, m, acc, den)
        outs.append(acc / jnp.sum(den, axis=0, keepdims=True))
    return outs


def _half_masks(shape):
    lane = lax.broadcasted_iota(jnp.int32, shape, 1)
    return lane < 64, lane >= 64


N_DR = 2 * NA_WIN_ROWS - 1
N_DC = 2 * NA_WIN_COLS - 1
PAIR_MASK_FIRST = N_DR - 1
PAIR_MASK_SECOND = N_DR
PAIR_MASKED = N_DR + 1
N_PAIRS = N_DR + 2
NA_Q_PER_TRIP = 3
DR_LO = NA_WIN_ROWS - 1 - NA_WIN_ROWS // 2
DR_HI = DR_LO + NA_WIN_ROWS - 1


def _na_pair_table(rpb):
    h = rpb.shape[0]
    w = GRID_W
    v = jnp.zeros((h, N_DR, 2 * w), F32)
    v = v.at[..., 0:NA_WIN_COLS].set(rpb[..., NA_WIN_COLS - 1:].astype(F32))
    v = v.at[..., 2 * w - (NA_WIN_COLS - 1):].set(rpb[..., :NA_WIN_COLS - 1].astype(F32))
    y = jnp.broadcast_to(v[:, :, None, :], (h, N_DR, w, 2 * w)).reshape(h, N_DR, w * 2 * w)
    t = y[..., :w * (2 * w - 1)].reshape(h, N_DR, w, 2 * w - 1)[..., :w]
    col = jnp.arange(w)
    cstart = jnp.clip(col - NA_WIN_COLS // 2, 0, w - NA_WIN_COLS)
    cvalid = (col[None, :] >= cstart[:, None]) & (col[None, :] < cstart[:, None] + NA_WIN_COLS)
    tt = jnp.swapaxes(jnp.where(cvalid, t * LOG2E, MASK_VALUE), -1, -2)
    masked = jnp.full((h, 1, w, w), MASK_VALUE, F32)
    regular = jnp.concatenate([tt[:, 1:], tt[:, :-1]], axis=-1)
    first = jnp.concatenate([masked, tt[:, DR_HI:DR_HI + 1]], axis=-1)
    second = jnp.concatenate([tt[:, DR_LO:DR_LO + 1], masked], axis=-1)
    return jnp.concatenate([regular, first, second, jnp.concatenate([masked, masked], axis=-1)], axis=1)


def _na_pair_entries(q_row0, band_row0, band_rows):
    entries = []
    for j in range(band_rows):
        kr = band_row0 + j
        row = []
        for a in range(0, NA_RB, 2):
            ok, d = [], []
            for r in (q_row0 + a, q_row0 + a + 1):
                r0 = min(max(r - NA_WIN_ROWS // 2, 0), ROWS - NA_WIN_ROWS)
                ok.append(r0 <= kr < r0 + NA_WIN_ROWS)
                d.append(kr - r + NA_WIN_ROWS - 1)
            if ok[0] and ok[1]:
                row.append(d[0] - 1)
            elif ok[1]:
                assert d[1] == DR_HI
                row.append(PAIR_MASK_FIRST)
            elif ok[0]:
                assert d[0] == DR_LO
                row.append(PAIR_MASK_SECOND)
            else:
                row.append(PAIR_MASKED)
        entries.append(row)
    return entries


def _na_kernel(tab_ref, q_ref, k_ref, vt_ref, qc_ref, kc_ref, vct_ref, o_ref, oc_ref, s_scr):
    n_blocks = ROWS // NA_RB
    rows_per_chunk = KEY_BLK // GRID_W
    ctx_blocks = _key_blocks(kc_ref, vct_ref)

    def operands(q_start, band_start, band_rows, entries):
        q = q_ref[pl.ds(q_start, NA_Q), :]
        ops = []
        for hh, msk in enumerate(_half_masks(q.shape)):
            def bias(c, hh=hh):
                rows = entries[c * rows_per_chunk:(c + 1) * rows_per_chunk]
                return jnp.concatenate(
                    [jnp.concatenate([tab_ref[hh, e] for e in row], axis=1) for row in rows], axis=0)

            band = _key_blocks(k_ref, vt_ref, band_start, band_rows * GRID_W, bias)
            ops.append((jnp.where(msk, q, jnp.zeros_like(q)), band + ctx_blocks))
        return ops

    def merge_heads(o_lo, o_hi):
        feat = lax.broadcasted_iota(jnp.int32, o_lo.shape, 0)
        return jnp.where(feat < HB // 2, o_lo, o_hi).T.astype(BF16)

    def run(blocks):
        outs = _attention_t([op for blk in blocks for op in operands(*blk)], s_scr)
        for i, blk in enumerate(blocks):
            o_ref[pl.ds(blk[0], NA_Q), :] = merge_heads(outs[2 * i], outs[2 * i + 1])

    last_band = ROWS - NA_WIN_ROWS
    run([(0, 0, NA_WIN_ROWS, _na_pair_entries(0, 0, NA_WIN_ROWS)),
         ((n_blocks - 1) * NA_Q, last_band * GRID_W, NA_WIN_ROWS,
          _na_pair_entries(ROWS - NA_RB, last_band, NA_WIN_ROWS))])
    interior = _na_pair_entries(NA_RB, 0, NA_BAND_ROWS)

    def body(t, carry):
        blocks = []
        for i in range(NA_Q_PER_TRIP):
            rb = 1 + t * NA_Q_PER_TRIP + i
            band_start = pl.multiple_of((rb - 1) * NA_Q, NA_Q)
            blocks.append((pl.multiple_of(rb * NA_Q, NA_Q), band_start, NA_BAND_ROWS, interior))
        run(blocks)
        return carry

    lax.fori_loop(0, (n_blocks - 2) // NA_Q_PER_TRIP, body, 0)

    qc = qc_ref[...]
    outs = _attention_t([(jnp.where(msk, qc, jnp.zeros_like(qc)), ctx_blocks) for msk in _half_masks(qc.shape)],
                        s_scr)
    oc_ref[...] = merge_heads(outs[0], outs[1])


def _na_attention(tables, layer_na, q, k, vt):
    assert NA_Q == Q_BLK and CTX_LEN == Q_BLK and (ROWS // NA_RB - 2) % NA_Q_PER_TRIP == 0
    lat = lambda hp, b: (b, hp)
    ctx = lambda hp, b: (T_LAT // CTX_LEN + b, hp)
    return pl.pallas_call(
        _na_kernel,
        grid=(N_HB, BATCH),
        in_specs=[
            pl.BlockSpec((2, N_PAIRS, GRID_W, 2 * GRID_W), lambda hp, b: (layer_na * N_HB + hp, 0, 0, 0)),
            pl.BlockSpec((SEQ, HB), lat),
            pl.BlockSpec((SEQ, HB), lat),
            pl.BlockSpec((HB, SEQ), lambda hp, b: (hp, b)),
            pl.BlockSpec((CTX_LEN, HB), ctx),
            pl.BlockSpec((CTX_LEN, HB), ctx),
            pl.BlockSpec((HB, CTX_LEN), lambda hp, b: (hp, T_LAT // CTX_LEN + b)),
        ],
        out_specs=[pl.BlockSpec((SEQ, HB), lat), pl.BlockSpec((CTX_LEN, HB), lat)],
        out_shape=[jax.ShapeDtypeStruct((T_LAT, D_MODEL), BF16), jax.ShapeDtypeStruct((T_CTX, D_MODEL), BF16)],
        scratch_shapes=[pltpu.VMEM((2, NA_BAND + CTX_LEN, Q_BLK), F32)],
        compiler_params=_cparams("parallel", "parallel"),
        name="na_attention",
    )(tables, q, k, vt, q, k, vt)


def _diff_lambda(lam_ref, lam_init):
    lam = lam_ref[...]
    a = jnp.sum(lam[0:1] * lam[1:2], axis=-1, keepdims=True)
    b = jnp.sum(lam[2:3] * lam[3:4], axis=-1, keepdims=True)
    return jnp.exp(a) - jnp.exp(b) + lam_init


def _diff_kernel(lam_ref, sg_ref, q_ref, k_ref, vt_ref, qc_ref, kc_ref, vct_ref, o_ref, *rest, lam_init):
    *maybe_oc_ref, s_scr = rest
    lam = _diff_lambda(lam_ref, lam_init)
    sg_col = sg_ref[...]

    def rows(qs, pieces):
        blocks = [blk for k_r, vt_r in pieces for blk in _key_blocks(k_r, vt_r)]
        outs = _attention_t([(jnp.where(msk, q, jnp.zeros_like(q)), blocks)
                             for q in qs for msk in _half_masks(q.shape)], s_scr)
        res = []
        for o1, o2 in zip(outs[0::2], outs[1::2]):
            ot = o1 - lam * o2
            ms = jnp.mean(ot * ot, axis=0, keepdims=True)
            ot = ot * lax.rsqrt(ms + NORM_EPS) * sg_col * (1.0 - lam_init)
            res.append(ot.T.astype(BF16))
        return res

    def body(t, carry):
        starts = [pl.multiple_of((t * DIFF_Q_PER_TRIP + j) * Q_BLK, Q_BLK) for j in range(DIFF_Q_PER_TRIP)]
        outs = rows([q_ref[pl.ds(r0, Q_BLK), :] for r0 in starts], [(k_ref, vt_ref), (kc_ref, vct_ref)])
        for r0, o in zip(starts, outs):
            o_ref[pl.ds(r0, Q_BLK), :] = o
        return carry

    lax.fori_loop(0, SEQ // (Q_BLK * DIFF_Q_PER_TRIP), body, 0)
    if maybe_oc_ref:
        oc_ref, = maybe_oc_ref
        assert CTX_LEN == Q_BLK
        oc_ref[...], = rows([qc_ref[...]], [(kc_ref, vct_ref)])


def _diff_attention(lam, sg, q, k, vt, lam_init, *, ctx_out):
    lat = lambda b, h: (b, h)
    ctx = lambda b, h: (T_LAT // CTX_LEN + b, h)
    lat_t = lambda b, h: (h, b)
    ctx_t = lambda b, h: (h, T_LAT // CTX_LEN + b)
    out_specs = [pl.BlockSpec((SEQ, HB), lat)]
    out_shape = [jax.ShapeDtypeStruct((T_LAT, D_MODEL), BF16)]
    if ctx_out:
        out_specs.append(pl.BlockSpec((CTX_LEN, HB), lat))
        out_shape.append(jax.ShapeDtypeStruct((T_CTX, D_MODEL), BF16))
    return pl.pallas_call(
        functools.partial(_diff_kernel, lam_init=lam_init),
        grid=(BATCH, DIFF_HEADS),
        in_specs=[
            pl.BlockSpec(lam.shape, lambda b, h: (0, 0)),
            pl.BlockSpec(sg.shape, lambda b, h: (0, 0)),
            pl.BlockSpec((SEQ, HB), lat),
            pl.BlockSpec((SEQ, HB), lat),
            pl.BlockSpec((HB, SEQ), lat_t),
            pl.BlockSpec((CTX_LEN, HB), ctx),
            pl.BlockSpec((CTX_LEN, HB), ctx),
            pl.BlockSpec((HB, CTX_LEN), ctx_t),
        ],
        out_specs=out_specs,
        out_shape=out_shape,
        scratch_shapes=[pltpu.VMEM((2, SEQ + CTX_LEN, Q_BLK), F32)],
        compiler_params=_cparams("parallel", "parallel"),
        name="diff_attention",
    )(lam, sg, q, k, vt, q, k, vt)


def _route_rows(logits_t, bias_col):
    s = _sigmoid(logits_t)
    sel = s + bias_col
    sel_r = [sel[e:e + 1] for e in range(N_EXPERTS)]
    s_r = [s[e:e + 1] for e in range(N_EXPERTS)]
    group_scores = []
    for g in range(N_GROUPS):
        v = sel_r[g * EXPERTS_PER_GROUP:(g + 1) * EXPERTS_PER_GROUP]
        pairs = [v[a] + v[b] for a in range(EXPERTS_PER_GROUP) for b in range(a + 1, EXPERTS_PER_GROUP)]
        group_scores.append(functools.reduce(jnp.maximum, pairs))
    best = group_scores[0]
    grp = jnp.zeros(best.shape, jnp.int32)
    for g in range(1, N_GROUPS):
        upd = group_scores[g] > best
        best = jnp.where(upd, group_scores[g], best)
        grp = jnp.where(upd, g, grp)

    def pick(rows, j):
        out = rows[j]
        for g in range(1, N_GROUPS):
            out = jnp.where(grp == g, rows[g * EXPERTS_PER_GROUP + j], out)
        return out

    w = [pick(sel_r, j) for j in range(EXPERTS_PER_GROUP)]
    sc = [pick(s_r, j) for j in range(EXPERTS_PER_GROUP)]

    def argmax_first(vals):
        bv = vals[0]
        bi = jnp.zeros(bv.shape, jnp.int32)
        for j in range(1, len(vals)):
            upd = vals[j] > bv
            bv = jnp.where(upd, vals[j], bv)
            bi = jnp.where(upd, j, bi)
        return bi

    i1 = argmax_first(w)
    i2 = argmax_first([jnp.where(i1 == j, -jnp.inf, w[j]) for j in range(EXPERTS_PER_GROUP)])

    def take(vals, idx):
        out = vals[0]
        for j in range(1, len(vals)):
            out = jnp.where(idx == j, vals[j], out)
        return out

    s1 = take(sc, i1)
    s2 = take(sc, i2)
    den = s1 + s2
    return grp * EXPERTS_PER_GROUP + i1, grp * EXPERTS_PER_GROUP + i2, s1 / den, s2 / den


def _oproj_kernel(o_ref, oc_ref, x_ref, mod_ref, w_ref, g_ref, rw_ref, rb_ref, tri_ref,
                  x1_ref, h_ref, ri_ref, rf_ref, cnt_ref, carry_ref):
    i = pl.program_id(0)

    @pl.when(i == 0)
    def _():
        carry_ref[...] = jnp.zeros_like(carry_ref)

    m = mod_ref[0]
    rw = rw_ref[...].astype(BF16)
    ri_ref[...] = jnp.zeros_like(ri_ref)
    rf_ref[...] = jnp.zeros_like(rf_ref)
    carry = carry_ref[...]
    for r0 in range(0, TM, ROUTE_BLK):
        rows = pl.ds(r0, ROUTE_BLK)
        o = jnp.where(i < LAT_TILES, o_ref[rows, :], oc_ref[rows, :])
        x1 = x_ref[rows, :] + m[2:3] * _dot(o, w_ref[0])
        x1_ref[rows, :] = x1
        h = _rms_modulate(x1, g_ref[...], m[3:4], m[4:5])
        hb = h.astype(BF16)
        h_ref[rows, :] = _pack_bf16_pair(h)

        logits = _dot(hb, rw)
        logits_t = logits.T[:N_EXPERTS]
        e0, e1, g0, g1 = _route_rows(logits_t, rb_ref[...])

        eiota = lax.broadcasted_iota(jnp.int32, (N_EXPERTS, ROUTE_BLK), 0)
        oh0 = eiota == e0
        oh1 = eiota == e1
        oh = jnp.where(oh0 | oh1, 1.0, 0.0)
        before = _dot(oh.astype(BF16), tri_ref[...]) + carry
        rank0 = jnp.sum(jnp.where(oh0, before, 0.0), axis=0, keepdims=True)
        rank1 = jnp.sum(jnp.where(oh1, before, 0.0), axis=0, keepdims=True)
        carry = carry + jnp.sum(oh, axis=1, keepdims=True)

        ri_ref[0:1, rows] = e0
        ri_ref[1:2, rows] = e1
        ri_ref[2:3, rows] = rank0.astype(jnp.int32)
        ri_ref[3:4, rows] = rank1.astype(jnp.int32)
        rf_ref[0:1, rows] = g0
        rf_ref[1:2, rows] = g1
    carry_ref[...] = carry
    cnt_ref[...] = jnp.broadcast_to(carry, cnt_ref.shape)


def _oproj_route(o, o_ctx, x, mod_l, w_o, layer, g, rw_pad, rb_col, tri, n_tok):
    n_tiles = n_tok // TM
    tok = lambda i: (i, 0)
    const2 = lambda i: (0, 0)
    lane_tok = lambda i: (0, i)
    return pl.pallas_call(
        _oproj_kernel,
        grid=(n_tiles,),
        in_specs=[
            pl.BlockSpec((TM, D_MODEL), lambda i: (jnp.minimum(i, LAT_TILES - 1), 0)),
            pl.BlockSpec((TM, D_MODEL), lambda i: (jnp.maximum(i - LAT_TILES, 0), 0)),
            pl.BlockSpec((TM, D_MODEL), tok),
            pl.BlockSpec((1, N_MOD, D_MODEL), lambda i: (_mod_row(i), 0, 0)),
            pl.BlockSpec((1, D_MODEL, D_MODEL), lambda i: (layer, 0, 0)),
            pl.BlockSpec((1, D_MODEL), const2),
            pl.BlockSpec((D_MODEL, HB), const2),
            pl.BlockSpec((N_EXPERTS, 1), const2),
            pl.BlockSpec((ROUTE_BLK, ROUTE_BLK), const2),
        ],
        out_specs=[
            pl.BlockSpec((TM, D_MODEL), tok),
            pl.BlockSpec((TM, D_PACK), tok),
            pl.BlockSpec((8, TM), lane_tok),
            pl.BlockSpec((8, TM), lane_tok),
            pl.BlockSpec((N_EXPERTS, HB), const2),
        ],
        out_shape=[
            jax.ShapeDtypeStruct((n_tok, D_MODEL), F32),
            jax.ShapeDtypeStruct((n_tok, D_PACK), jnp.uint32),
            jax.ShapeDtypeStruct((8, n_tok), jnp.int32),
            jax.ShapeDtypeStruct((8, n_tok), F32),
            jax.ShapeDtypeStruct((N_EXPERTS, HB), F32),
        ],
        scratch_shapes=[pltpu.VMEM((N_EXPERTS, 1), F32)],
        compiler_params=_cparams("arbitrary"),
        name="oproj_norm_route",
    )(o, o if o_ctx is None else o_ctx, x, mod_l, w_o, g.reshape(1, D_MODEL), rw_pad, rb_col, tri)


def _moe_kernel(te_ref, tv_ref, xs_ref, wg_ref, wu_ref, wd_ref, ys_ref, wgb, wub, wdb):
    i = pl.program_id(0)
    valid = tv_ref[i]
    e = te_ref[i]
    prev = te_ref[jnp.maximum(i - 1, 0)]
    half = TM // 2

    @pl.when((i == 0) | (e != prev))
    def _():
        wgb[...] = wg_ref[0, 0].astype(BF16)
        wub[...] = wu_ref[0, 0].astype(BF16)
        wdb[...] = wd_ref[0, 0].astype(BF16)

    def ffn(rows, x):
        x = _unpack_bf16_pair(x).astype(BF16)
        a = _dot(x, wgb[...])
        u = _dot(x, wub[...])
        hm = (a * _sigmoid(a) * u).astype(BF16)
        ys_ref[rows, :] = _pack_bf16_pair(_dot(hm, wdb[...]))

    @pl.when(valid == TM)
    def _():
        for r0 in range(0, TM, half):
            ffn(pl.ds(r0, half), xs_ref[pl.ds(r0, half), :])

    for r0 in range(0, TM, half):
        rows = pl.ds(r0, half)

        @pl.when((valid < TM) & (valid >= r0 + half))
        def _():
            ffn(rows, xs_ref[rows, :])

        @pl.when((valid > r0) & (valid < r0 + half))
        def _():
            row = r0 + lax.broadcasted_iota(jnp.int32, (half, 1), 0)
            x = xs_ref[rows, :]
            ffn(rows, jnp.where(row < valid, x, jnp.zeros_like(x)))

        @pl.when(valid <= r0)
        def _():
            ys_ref[rows, :] = jnp.zeros((half, D_PACK), jnp.uint32)


def _grouped_ffn(tile_expert, tile_valid, xs, w_gate, w_up, w_down, layer):
    n_tiles = xs.shape[0] // TM
    wmap = lambda i, te, tv: (layer, te[i], 0, 0)
    tok = lambda i, te, tv: (i, 0)
    return pl.pallas_call(
        _moe_kernel,
        grid_spec=pltpu.PrefetchScalarGridSpec(
            num_scalar_prefetch=2,
            grid=(n_tiles,),
            in_specs=[
                pl.BlockSpec((TM, D_PACK), tok),
                pl.BlockSpec((1, 1, D_MODEL, D_EXPERT), wmap),
                pl.BlockSpec((1, 1, D_MODEL, D_EXPERT), wmap),
                pl.BlockSpec((1, 1, D_EXPERT, D_MODEL), wmap),
            ],
            out_specs=pl.BlockSpec((TM, D_PACK), tok),
            scratch_shapes=[
                pltpu.VMEM((D_MODEL, D_EXPERT), BF16),
                pltpu.VMEM((D_MODEL, D_EXPERT), BF16),
                pltpu.VMEM((D_EXPERT, D_MODEL), BF16),
            ],
        ),
        out_shape=jax.ShapeDtypeStruct((xs.shape[0], D_PACK), jnp.uint32),
        compiler_params=_cparams("arbitrary"),
        name="grouped_ffn",
    )(tile_expert, tile_valid, xs, w_gate, w_up, w_down)


def _moe_layout(route_i, counts, n_tok):
    n_tiles = 2 * n_tok // TM + N_EXPERTS
    counts = counts.astype(jnp.int32)
    tiles_e = (counts + TM - 1) // TM
    tiles_end = jnp.cumsum(tiles_e)
    tile_start = tiles_end - tiles_e
    eid = jnp.arange(N_EXPERTS, dtype=jnp.int32)
    tok_oh = route_i[0:2][:, :, None] == eid
    pos = jnp.sum(jnp.where(tok_oh, tile_start * TM, 0), axis=-1) + route_i[2:4]
    tile_ids = jnp.arange(n_tiles, dtype=jnp.int32)
    n_used = tiles_end[-1]
    te = jnp.sum(tiles_end[None, :] <= jnp.minimum(tile_ids, n_used - 1)[:, None], axis=1).astype(jnp.int32)
    tile_oh = te[:, None] == eid
    cnt_t = jnp.sum(jnp.where(tile_oh, counts, 0), axis=1)
    start_t = jnp.sum(jnp.where(tile_oh, tile_start, 0), axis=1)
    tv = jnp.clip(cnt_t - (tile_ids - start_t) * TM, 0, TM)
    tv = jnp.where(tile_ids < n_used, tv, 0).astype(jnp.int32)
    return pos.astype(jnp.int32), te, tv, n_tiles


SC_CORES = 2
SC_SUBCORES = 16
SC_WORKERS = SC_CORES * SC_SUBCORES
SC_CHUNK = 32


def _sc_mesh():
    return plsc.VectorSubcoreMesh(core_axis_name="c", subcore_axis_name="s")


def _sc_worker_indices(pos, n_tok):
    n_ch = n_tok // SC_WORKERS // SC_CHUNK
    return pos.reshape(2, SC_WORKERS, n_ch, SC_CHUNK).transpose(1, 2, 0, 3), n_ch


def _sc_dispatch(h, pos, n_slots):
    n_tok = h.shape[0]
    per_w = n_tok // SC_WORKERS
    pos_w, n_ch = _sc_worker_indices(pos, n_tok)
    assert n_ch % 2 == 0 and n_ch * SC_CHUNK * SC_WORKERS == n_tok

    @functools.partial(
        pl.kernel, mesh=_sc_mesh(), out_type=jax.ShapeDtypeStruct((n_slots, h.shape[1]), h.dtype),
        scratch_types=[pltpu.VMEM((n_ch, 2, SC_CHUNK), jnp.int32), pltpu.VMEM((2, SC_CHUNK, h.shape[1]), h.dtype),
                       pltpu.SemaphoreType.DMA((2,)), pltpu.SemaphoreType.DMA((2,))],
        name="sc_dispatch")
    def dispatch(h_hbm, pos_hbm, xs_hbm, idx_v, rows_v, load_sem, scat_sem):
        wid = lax.axis_index("s") * SC_CORES + lax.axis_index("c")
        base = wid * per_w
        pltpu.sync_copy(pos_hbm.at[wid], idx_v)

        def load(c, b):
            return pltpu.make_async_copy(h_hbm.at[pl.ds(base + c * SC_CHUNK, SC_CHUNK)], rows_v.at[b],
                                         load_sem.at[b])

        def scat(c, b, k):
            return pltpu.make_async_copy(rows_v.at[b], xs_hbm.at[idx_v.at[c, k]], scat_sem.at[b])

        load(0, 0).start()

        @pl.loop(0, n_ch, step=2)
        def _(c0):
            for b in range(2):
                c = c0 + b
                load(c, b).wait()
                scat(c, b, 0).start()
                scat(c, b, 1).start()

                @pl.when(c >= 1)
                def _():
                    scat(c - 1, 1 - b, 0).wait()
                    scat(c - 1, 1 - b, 1).wait()

                @pl.when(c + 1 < n_ch)
                def _():
                    load(c + 1, 1 - b).start()

        scat(n_ch - 1, 1, 0).wait()
        scat(n_ch - 1, 1, 1).wait()

    return dispatch(h, pos_w)


def _sc_combine_gather(ys, pos):
    n_tok = pos.shape[1]
    per_w = n_tok // SC_WORKERS
    pos_w, n_ch = _sc_worker_indices(pos, n_tok)
    assert n_ch * SC_CHUNK * SC_WORKERS == n_tok

    @functools.partial(
        pl.kernel, mesh=_sc_mesh(), out_type=jax.ShapeDtypeStruct((2, n_tok, ys.shape[1]), ys.dtype),
        scratch_types=[pltpu.VMEM((n_ch, 2, SC_CHUNK), jnp.int32), pltpu.VMEM((2, SC_CHUNK, ys.shape[1]), ys.dtype),
                       pltpu.SemaphoreType.DMA((2,)), pltpu.SemaphoreType.DMA((2,))],
        name="sc_combine_gather")
    def gather(ys_hbm, pos_hbm, yg_hbm, idx_v, rows_v, gath_sem, write_sem):
        wid = lax.axis_index("s") * SC_CORES + lax.axis_index("c")
        base = wid * per_w
        pltpu.sync_copy(pos_hbm.at[wid], idx_v)

        def gath(c, k):
            return pltpu.make_async_copy(ys_hbm.at[idx_v.at[c, k]], rows_v.at[k], gath_sem.at[k])

        def write(c, k):
            return pltpu.make_async_copy(rows_v.at[k], yg_hbm.at[k, pl.ds(base + c * SC_CHUNK, SC_CHUNK)],
                                         write_sem.at[k])

        gath(0, 0).start()

        @pl.loop(0, n_ch)
        def _(c):
            gath(c, 0).wait()
            write(c, 0).start()

            @pl.when(c >= 1)
            def _():
                write(c - 1, 1).wait()

            gath(c, 1).start()
            gath(c, 1).wait()
            write(c, 1).start()
            write(c, 0).wait()

            @pl.when(c + 1 < n_ch)
            def _():
                gath(c + 1, 0).start()

        write(n_ch - 1, 1).wait()

    return gather(ys, pos_w)


def _final_kernel(x1_ref, yg_ref, gt_ref, mod_ref, fg_ref, out_ref):
    x2 = _moe_residual(x1_ref[...], yg_ref[0], yg_ref[1], gt_ref[...], mod_ref[0][5:6])
    ms = jnp.mean(x2 * x2, axis=-1, keepdims=True)
    out_ref[...] = x2 * lax.rsqrt(ms + NORM_EPS) * fg_ref[...]


def _combine_final(x1, yg, gates_t, mod_l, final_g, n_tok):
    tok = lambda i: (i, 0)
    return pl.pallas_call(
        _final_kernel,
        grid=(n_tok // TM,),
        in_specs=[
            pl.BlockSpec((TM, D_MODEL), tok),
            pl.BlockSpec((2, TM, D_PACK), lambda i: (0, i, 0)),
            pl.BlockSpec((TM, 2), tok),
            pl.BlockSpec((1, N_MOD, D_MODEL), lambda i: (_mod_row(i), 0, 0)),
            pl.BlockSpec((1, D_MODEL), lambda i: (0, 0)),
        ],
        out_specs=pl.BlockSpec((TM, D_MODEL), tok),
        out_shape=jax.ShapeDtypeStruct((n_tok, D_MODEL), F32),
        compiler_params=_cparams("parallel"),
        name="moe_combine_final",
    )(x1, yg, gates_t, mod_l, final_g.reshape(1, D_MODEL))


def _diff_lambda_init(layer_idx):
    return 0.8 - 0.6 * math.exp(-0.3 * (layer_idx - 1))


def kernel(x, c, ctx, c_ctx, mod_w, mod_b, norm_mix, norm_ffn, w_qkv, w_o, na_rpb, diff_lambda, diff_subln,
           router_w, router_b, expert_w_gate, expert_w_up, expert_w_down, final_norm):
    mod = _modulation(c, c_ctx, mod_w, mod_b)
    stream = (x.reshape(T_LAT, D_MODEL), ctx.reshape(T_CTX, D_MODEL))
    w_qk_b = w_qkv[:, :, :2 * D_MODEL].astype(BF16)
    w_vt_b = jnp.swapaxes(w_qkv[:, :, 2 * D_MODEL:], 1, 2).astype(BF16)
    w_o_b = w_o.astype(BF16)
    na_tables = _na_pair_table(na_rpb.reshape(-1, N_DR, N_DC))
    cs, sn = _rope_tables()
    rw_pad = jnp.zeros((D_MODEL, HB), F32).at[:, :N_EXPERTS].set(router_w)
    rb_col = router_b.reshape(N_EXPERTS, 1).astype(F32)
    blk = jnp.arange(ROUTE_BLK)
    tri = (blk[:, None] < blk[None, :]).astype(BF16)

    for i in range(DEPTH):
        last = i == DEPTH - 1
        j = i // 2
        is_diff = i % 2 == 1
        xa, q, k, v = _qkv(stream, mod[i], norm_mix[i], w_qk_b, w_vt_b, i, cs, sn, rope=is_diff)
        if not is_diff:
            o, o_ctx = _na_attention(na_tables, j, q, k, v)
        else:
            lam_init = _diff_lambda_init(i + 1)
            sg = diff_subln[j].reshape(HB, 1)
            outs = _diff_attention(diff_lambda[j], sg, q, k, v, lam_init, ctx_out=not last)
            o, o_ctx = outs if not last else (outs[0], None)
        n_tok = T_LAT if last else T_ALL
        x1, h, route_i, route_f, cnt = _oproj_route(o, o_ctx, xa, mod[i], w_o_b, i, norm_ffn[i], rw_pad,
                                                    rb_col, tri, n_tok)
        pos, te, tv, n_tiles = _moe_layout(route_i, cnt[:, 0], n_tok)
        xs = _sc_dispatch(h, pos, n_tiles * TM)
        ys = _grouped_ffn(te, tv, xs, expert_w_gate, expert_w_up, expert_w_down, i)
        yg = _sc_combine_gather(ys, pos)
        stream = (x1, yg, route_f[0:2].T, mod[i])
    out = _combine_final(*stream, final_norm, T_LAT)
    return out.reshape(BATCH, SEQ, D_MODEL)
```

```python
import functools
import math

import jax
import jax.numpy as jnp
from jax import lax
from jax.experimental import pallas as pl
from jax.experimental.pallas import tpu as pltpu
from jax.experimental.pallas import tpu_sc as plsc

F32 = jnp.float32
BF16 = jnp.bfloat16

D_MODEL = 1024
BATCH = 8
SEQ = 2048
DEPTH = 4
CTX_LEN = 256
GRID_W = 64
ROWS = SEQ // GRID_W
NA_HEADS = 16
NA_WIN_ROWS = 8
NA_WIN_COLS = 16
DIFF_HEADS = 8
DIFF_HEAD_DIM = 64
ROPE_THETA = 10000.0
N_EXPERTS = 16
N_GROUPS = 4
EXPERTS_PER_GROUP = 4
D_EXPERT = 1024
N_MOD = 6
NORM_EPS = 1e-6

T_LAT = BATCH * SEQ
T_CTX = BATCH * CTX_LEN
T_ALL = T_LAT + T_CTX
TM = 512
ROUTE_BLK = TM // 2
LAT_TILES = T_LAT // TM
CTX_TILES = T_CTX // TM
TILES_PER_BATCH = SEQ // TM
CTX_ROW = BATCH
MOD_ROWS = 16
HB = 128
N_HB = D_MODEL // HB
NA_RB = 4
NA_BAND_ROWS = 12
NA_Q = NA_RB * GRID_W
NA_BAND = NA_BAND_ROWS * GRID_W
TQ_DIFF = 512
MASK_VALUE = -1e30
LOG2E = math.log2(math.e)
Q_SCALE = 0.125 * LOG2E
VMEM_LIMIT = 56 * 1024 * 1024


def _cparams(*sem):
    return pltpu.CompilerParams(dimension_semantics=sem, vmem_limit_bytes=VMEM_LIMIT)


def _dot(a, b):
    return jnp.dot(a, b, preferred_element_type=F32)


def _dot_nt(a, b):
    return lax.dot_general(a, b, (((1,), (1,)), ((), ())), preferred_element_type=F32)


def _sigmoid(x):
    return 1.0 / (1.0 + jnp.exp(-x))


D_PACK = D_MODEL // 2


def _pack_bf16_pair(x):
    lo = lax.bitcast_convert_type(x[:, :D_PACK].astype(BF16).astype(F32), jnp.uint32)
    hi = lax.bitcast_convert_type(x[:, D_PACK:].astype(BF16).astype(F32), jnp.uint32)
    return (lo >> 16) | hi


def _unpack_bf16_pair(w):
    lo = lax.bitcast_convert_type(w << 16, F32)
    hi = lax.bitcast_convert_type(w & jnp.uint32(0xFFFF0000), F32)
    return jnp.concatenate([lo, hi], axis=1)


def _mod_row(i):
    return jnp.minimum(i // TILES_PER_BATCH, CTX_ROW)


def _mod_kernel(act_ref, w_ref, b_ref, o_ref):
    a = act_ref[...]
    a = a * _sigmoid(a)
    o_ref[0] = _dot(a.astype(BF16), w_ref[0].astype(BF16)) + b_ref[0]


def _modulation(c, c_ctx, mod_w, mod_b):
    tn = 1536
    act = jnp.zeros((MOD_ROWS, D_MODEL), F32).at[:BATCH].set(c).at[CTX_ROW].set(c_ctx)
    out = pl.pallas_call(
        _mod_kernel,
        grid=(DEPTH, N_MOD * D_MODEL // tn),
        in_specs=[
            pl.BlockSpec((MOD_ROWS, D_MODEL), lambda l, j: (0, 0)),
            pl.BlockSpec((1, D_MODEL, tn), lambda l, j: (l, 0, j)),
            pl.BlockSpec((1, 1, tn), lambda l, j: (l, 0, j)),
        ],
        out_specs=pl.BlockSpec((1, MOD_ROWS, tn), lambda l, j: (l, 0, j)),
        out_shape=jax.ShapeDtypeStruct((DEPTH, MOD_ROWS, N_MOD * D_MODEL), F32),
        compiler_params=_cparams("parallel", "parallel"),
        name="adaln_mod",
    )(act, mod_w, mod_b.reshape(DEPTH, 1, N_MOD * D_MODEL))
    return out.reshape(DEPTH, MOD_ROWS, N_MOD, D_MODEL)


def _rms_modulate(x, g, shift, scale):
    ms = jnp.mean(x * x, axis=-1, keepdims=True)
    y = x * lax.rsqrt(ms + NORM_EPS) * g
    return y * (1.0 + scale) + shift


def _rope_block(xb, cs, sn):
    lane = lax.broadcasted_iota(jnp.int32, xb.shape, 1)
    partner = jnp.where((lane & 63) < 32, pltpu.roll(xb, 96, 1), pltpu.roll(xb, 32, 1))
    return xb * cs + partner * sn


def _moe_residual(x1, y0, y1, g, gate_ffn):
    y = g[:, 0:1] * _unpack_bf16_pair(y0) + g[:, 1:2] * _unpack_bf16_pair(y1)
    return x1 + gate_ffn * y


def _qkv_kernel(*refs, rope, first):
    n_stream = 2 if first else 4
    stream_refs, refs = refs[:n_stream], refs[n_stream:]
    mod_ref, g_ref, wqk_ref, wvt_ref, cs_ref, sn_ref, x_ref, q_ref, k_ref, vt_ref = refs
    m = mod_ref[0]
    for r0 in range(0, TM, ROUTE_BLK):
        rows = pl.ds(r0, ROUTE_BLK)
        if first:
            xl_ref, xc_ref = stream_refs
            x = jnp.where(pl.program_id(0) < LAT_TILES, xl_ref[rows, :], xc_ref[rows, :])
        else:
            x1_ref, yg_ref, gt_ref, modp_ref = stream_refs
            x = _moe_residual(x1_ref[rows, :], yg_ref[0, rows, :], yg_ref[1, rows, :], gt_ref[rows, :],
                              modp_ref[0][5:6])
        x_ref[rows, :] = x
        h = _rms_modulate(x, g_ref[...], m[0:1], m[1:2]).astype(BF16)
        for idx, out in enumerate((q_ref, k_ref)):
            acc = _dot(h, wqk_ref[0, :, idx * D_MODEL:(idx + 1) * D_MODEL])
            if rope:
                cs = cs_ref[rows, :]
                sn = sn_ref[rows, :]
                acc = jnp.concatenate(
                    [_rope_block(acc[:, j * HB:(j + 1) * HB], cs, sn) for j in range(N_HB)], axis=1)
            if idx == 0:
                acc = acc * Q_SCALE
            out[rows, :] = acc.astype(BF16)
        vt_ref[:, rows] = _dot_nt(wvt_ref[0], h).astype(BF16)


def _qkv(stream, mod_l, g, w_qk, w_vt, layer, cs, sn, *, rope):
    first = len(stream) == 2
    n_tiles = T_ALL // TM
    tok = lambda i: (i, 0)
    const2 = lambda i: (0, 0)
    mod_spec = pl.BlockSpec((1, N_MOD, D_MODEL), lambda i: (_mod_row(i), 0, 0))
    rope_idx = lambda i: (jnp.where(i < LAT_TILES, i % TILES_PER_BATCH, TILES_PER_BATCH), 0)
    out = jax.ShapeDtypeStruct((T_ALL, D_MODEL), BF16)
    wmap = lambda i: (layer, 0, 0)
    if first:
        stream_specs = [
            pl.BlockSpec((TM, D_MODEL), lambda i: (jnp.minimum(i, LAT_TILES - 1), 0)),
            pl.BlockSpec((TM, D_MODEL), lambda i: (jnp.maximum(i - LAT_TILES, 0), 0)),
        ]
    else:
        stream_specs = [
            pl.BlockSpec((TM, D_MODEL), tok),
            pl.BlockSpec((2, TM, D_PACK), lambda i: (0, i, 0)),
            pl.BlockSpec((TM, 2), tok),
            mod_spec,
        ]
    return pl.pallas_call(
        functools.partial(_qkv_kernel, rope=rope, first=first),
        grid=(n_tiles,),
        in_specs=stream_specs + [
            mod_spec,
            pl.BlockSpec((1, D_MODEL), const2),
            pl.BlockSpec((1, D_MODEL, 2 * D_MODEL), wmap),
            pl.BlockSpec((1, D_MODEL, D_MODEL), wmap),
            pl.BlockSpec((TM, HB), rope_idx),
            pl.BlockSpec((TM, HB), rope_idx),
        ],
        out_specs=[pl.BlockSpec((TM, D_MODEL), tok), pl.BlockSpec((TM, D_MODEL), tok),
                   pl.BlockSpec((TM, D_MODEL), tok), pl.BlockSpec((D_MODEL, TM), lambda i: (0, i))],
        out_shape=[jax.ShapeDtypeStruct((T_ALL, D_MODEL), F32), out, out,
                   jax.ShapeDtypeStruct((D_MODEL, T_ALL), BF16)],
        compiler_params=_cparams("parallel"),
        name="norm_qkv_rope" if rope else "norm_qkv",
    )(*stream, mod_l, g.reshape(1, D_MODEL), w_qk, w_vt, cs, sn)


def _rope_tables():
    t = jnp.arange(SEQ)
    row = (t // GRID_W).astype(F32)
    col = (t % GRID_W).astype(F32)
    n_freq = DIFF_HEAD_DIM // 4
    inv_freq = ROPE_THETA ** (-jnp.arange(n_freq, dtype=F32) / n_freq)
    ang = jnp.concatenate([row[:, None] * inv_freq, col[:, None] * inv_freq], axis=-1)
    cos, sin = jnp.cos(ang), jnp.sin(ang)
    cs = jnp.concatenate([cos, cos, cos, cos], axis=-1)
    sn = jnp.concatenate([-sin, sin, -sin, sin], axis=-1)
    cs = jnp.concatenate([cs, jnp.ones((TM, HB), F32)], axis=0)
    sn = jnp.concatenate([sn, jnp.zeros((TM, HB), F32)], axis=0)
    return cs, sn


Q_BLK = 256
KEY_BLK = 256
SUBLANES = 8
DIFF_Q_PER_TRIP = 4
DIFF_KEY_BLK = 1024


def _colwise(reduce_fn, x):
    return reduce_fn(x.reshape(x.shape[0] // SUBLANES, SUBLANES, x.shape[1]), axis=0)


def _key_blocks(k_ref, vt_ref, key_start=0, n_keys=None, bias=None, blk=KEY_BLK):
    n_keys = k_ref.shape[0] if n_keys is None else n_keys
    blk = min(blk, n_keys)
    blocks = []
    for c, s0 in enumerate(range(0, n_keys, blk)):
        blocks.append((
            lambda s0=s0: k_ref[pl.ds(key_start + s0, blk), :],
            lambda s0=s0: vt_ref[:, pl.ds(key_start + s0, blk)],
            None if bias is None else functools.partial(bias, c),
            blk,
        ))
    return blocks


def _attention_t(operands, s_scr):
    def rows_of(u, c):
        sizes = [blk[3] for blk in operands[u][1]]
        return slice(sum(sizes[:c]), sum(sizes[:c + 1]))

    def score(u, c, mx):
        qm, blocks = operands[u]
        k_tile, _, bias, _ = blocks[c]
        st = _dot_nt(k_tile(), qm)
        if bias is not None:
            st = st + bias()
        s_scr[u % 2, rows_of(u, c), :] = st
        return jnp.maximum(mx, _colwise(jnp.max, st))

    def expo(u, c, m, acc, den):
        vt_tile = operands[u][1][c][1]
        e = jnp.exp2(s_scr[u % 2, rows_of(u, c), :] - m)
        o = _dot(vt_tile(), e.astype(BF16))
        return (o if acc is None else acc + o), den + _colwise(jnp.sum, e)

    neg = jnp.full((SUBLANES, Q_BLK), -jnp.inf, F32)
    zero = jnp.zeros((SUBLANES, Q_BLK), F32)
    n_blocks = [len(blocks) for _, blocks in operands]
    mx = neg
    for c in range(n_blocks[0]):
        mx = score(0, c, mx)
    outs = []
    for u in range(len(operands)):
        m = jnp.max(mx, axis=0, keepdims=True)
        acc, den, mx = None, zero, neg
        n_next = n_blocks[u + 1] if u + 1 < len(operands) else 0
        for c in range(max(n_blocks[u], n_next)):
            if c < n_next:
                mx = score(u + 1, c, mx)
            if c < n_blocks[u]:
                acc, den = expo(u, c, m, acc, den)
        outs.append(acc / jnp.sum(den, axis=0, keepdims=True))
    return outs


def _half_masks(shape):
    lane = lax.broadcasted_iota(jnp.int32, shape, 1)
    return lane < 64, lane >= 64


N_DR = 2 * NA_WIN_ROWS - 1
N_DC = 2 * NA_WIN_COLS - 1
PAIR_MASK_FIRST = N_DR - 1
PAIR_MASK_SECOND = N_DR
PAIR_MASKED = N_DR + 1
N_PAIRS = N_DR + 2
NA_KEY_BLK = 768
NA_Q_PER_TRIP = 3
DR_LO = NA_WIN_ROWS - 1 - NA_WIN_ROWS // 2
DR_HI = DR_LO + NA_WIN_ROWS - 1


def _na_pair_table(rpb):
    h = rpb.shape[0]
    w = GRID_W
    v = jnp.zeros((h, N_DR, 2 * w), F32)
    v = v.at[..., 0:NA_WIN_COLS].set(rpb[..., NA_WIN_COLS - 1:].astype(F32))
    v = v.at[..., 2 * w - (NA_WIN_COLS - 1):].set(rpb[..., :NA_WIN_COLS - 1].astype(F32))
    y = jnp.broadcast_to(v[:, :, None, :], (h, N_DR, w, 2 * w)).reshape(h, N_DR, w * 2 * w)
    t = y[..., :w * (2 * w - 1)].reshape(h, N_DR, w, 2 * w - 1)[..., :w]
    col = jnp.arange(w)
    cstart = jnp.clip(col - NA_WIN_COLS // 2, 0, w - NA_WIN_COLS)
    cvalid = (col[None, :] >= cstart[:, None]) & (col[None, :] < cstart[:, None] + NA_WIN_COLS)
    tt = jnp.swapaxes(jnp.where(cvalid, t * LOG2E, MASK_VALUE), -1, -2)
    masked = jnp.full((h, 1, w, w), MASK_VALUE, F32)
    regular = jnp.concatenate([tt[:, 1:], tt[:, :-1]], axis=-1)
    first = jnp.concatenate([masked, tt[:, DR_HI:DR_HI + 1]], axis=-1)
    second = jnp.concatenate([tt[:, DR_LO:DR_LO + 1], masked], axis=-1)
    return jnp.concatenate([regular, first, second, jnp.concatenate([masked, masked], axis=-1)], axis=1)


def _na_pair_entries(q_row0, band_row0, band_rows):
    entries = []
    for j in range(band_rows):
        kr = band_row0 + j
        row = []
        for a in range(0, NA_RB, 2):
            ok, d = [], []
            for r in (q_row0 + a, q_row0 + a + 1):
                r0 = min(max(r - NA_WIN_ROWS // 2, 0), ROWS - NA_WIN_ROWS)
                ok.append(r0 <= kr < r0 + NA_WIN_ROWS)
                d.append(kr - r + NA_WIN_ROWS - 1)
            if ok[0] and ok[1]:
                row.append(d[0] - 1)
            elif ok[1]:
                assert d[1] == DR_HI
                row.append(PAIR_MASK_FIRST)
            elif ok[0]:
                assert d[0] == DR_LO
                row.append(PAIR_MASK_SECOND)
            else:
                row.append(PAIR_MASKED)
        entries.append(row)
    return entries


def _na_kernel(tab_ref, q_ref, k_ref, vt_ref, qc_ref, kc_ref, vct_ref, o_ref, oc_ref, s_scr):
    n_blocks = ROWS // NA_RB
    ctx_blocks = _key_blocks(kc_ref, vct_ref)

    def operands(q_start, band_start, band_rows, entries):
        q = q_ref[pl.ds(q_start, NA_Q), :]
        n_band = band_rows * GRID_W
        blk = NA_KEY_BLK if n_band % NA_KEY_BLK == 0 else n_band
        rows_per_chunk = blk // GRID_W
        ops = []
        for hh, msk in enumerate(_half_masks(q.shape)):
            def bias(c, hh=hh):
                rows = entries[c * rows_per_chunk:(c + 1) * rows_per_chunk]
                return jnp.concatenate(
                    [jnp.concatenate([tab_ref[hh, e] for e in row], axis=1) for row in rows], axis=0)

            band = _key_blocks(k_ref, vt_ref, band_start, n_band, bias, blk)
            ops.append((jnp.where(msk, q, jnp.zeros_like(q)), band + ctx_blocks))
        return ops

    def merge_heads(o_lo, o_hi):
        feat = lax.broadcasted_iota(jnp.int32, o_lo.shape, 0)
        return jnp.where(feat < HB // 2, o_lo, o_hi).T.astype(BF16)

    def run(blocks):
        outs = _attention_t([op for blk in blocks for op in operands(*blk)], s_scr)
        for i, blk in enumerate(blocks):
            o_ref[pl.ds(blk[0], NA_Q), :] = merge_heads(outs[2 * i], outs[2 * i + 1])

    last_band = ROWS - NA_WIN_ROWS
    run([(0, 0, NA_WIN_ROWS, _na_pair_entries(0, 0, NA_WIN_ROWS)),
         ((n_blocks - 1) * NA_Q, last_band * GRID_W, NA_WIN_ROWS,
          _na_pair_entries(ROWS - NA_RB, last_band, NA_WIN_ROWS))])
    interior = _na_pair_entries(NA_RB, 0, NA_BAND_ROWS)

    def body(t, carry):
        blocks = []
        for i in range(NA_Q_PER_TRIP):
            rb = 1 + t * NA_Q_PER_TRIP + i
            band_start = pl.multiple_of((rb - 1) * NA_Q, NA_Q)
            blocks.append((pl.multiple_of(rb * NA_Q, NA_Q), band_start, NA_BAND_ROWS, interior))
        run(blocks)
        return carry

    lax.fori_loop(0, (n_blocks - 2) // NA_Q_PER_TRIP, body, 0)

    qc = qc_ref[...]
    outs = _attention_t([(jnp.where(msk, qc, jnp.zeros_like(qc)), ctx_blocks) for msk in _half_masks(qc.shape)],
                        s_scr)
    oc_ref[...] = merge_heads(outs[0], outs[1])


def _na_attention(tables, layer_na, q, k, vt):
    assert NA_Q == Q_BLK and CTX_LEN == Q_BLK and (ROWS // NA_RB - 2) % NA_Q_PER_TRIP == 0
    lat = lambda hp, b: (b, hp)
    ctx = lambda hp, b: (T_LAT // CTX_LEN + b, hp)
    return pl.pallas_call(
        _na_kernel,
        grid=(N_HB, BATCH),
        in_specs=[
            pl.BlockSpec((2, N_PAIRS, GRID_W, 2 * GRID_W), lambda hp, b: (layer_na * N_HB + hp, 0, 0, 0)),
            pl.BlockSpec((SEQ, HB), lat),
            pl.BlockSpec((SEQ, HB), lat),
            pl.BlockSpec((HB, SEQ), lambda hp, b: (hp, b)),
            pl.BlockSpec((CTX_LEN, HB), ctx),
            pl.BlockSpec((CTX_LEN, HB), ctx),
            pl.BlockSpec((HB, CTX_LEN), lambda hp, b: (hp, T_LAT // CTX_LEN + b)),
        ],
        out_specs=[pl.BlockSpec((SEQ, HB), lat), pl.BlockSpec((CTX_LEN, HB), lat)],
        out_shape=[jax.ShapeDtypeStruct((T_LAT, D_MODEL), BF16), jax.ShapeDtypeStruct((T_CTX, D_MODEL), BF16)],
        scratch_shapes=[pltpu.VMEM((2, NA_BAND + CTX_LEN, Q_BLK), F32)],
        compiler_params=_cparams("parallel", "parallel"),
        name="na_attention",
    )(tables, q, k, vt, q, k, vt)


def _diff_lambda(lam_ref, lam_init):
    lam = lam_ref[...]
    a = jnp.sum(lam[0:1] * lam[1:2], axis=-1, keepdims=True)
    b = jnp.sum(lam[2:3] * lam[3:4], axis=-1, keepdims=True)
    return jnp.exp(a) - jnp.exp(b) + lam_init


def _diff_kernel(lam_ref, sg_ref, q_ref, k_ref, vt_ref, qc_ref, kc_ref, vct_ref, o_ref, *rest, lam_init):
    *maybe_oc_ref, s_scr = rest
    lam = _diff_lambda(lam_ref, lam_init)
    sg_col = sg_ref[...]

    def rows(qs, pieces):
        blocks = [blk for k_r, vt_r in pieces for blk in _key_blocks(k_r, vt_r, blk=DIFF_KEY_BLK)]
        outs = _attention_t([(jnp.where(msk, q, jnp.zeros_like(q)), blocks)
                             for q in qs for msk in _half_masks(q.shape)], s_scr)
        res = []
        for o1, o2 in zip(outs[0::2], outs[1::2]):
            ot = o1 - lam * o2
            ms = jnp.mean(ot * ot, axis=0, keepdims=True)
            ot = ot * lax.rsqrt(ms + NORM_EPS) * sg_col * (1.0 - lam_init)
            res.append(ot.T.astype(BF16))
        return res

    def body(t, carry):
        starts = [pl.multiple_of((t * DIFF_Q_PER_TRIP + j) * Q_BLK, Q_BLK) for j in range(DIFF_Q_PER_TRIP)]
        outs = rows([q_ref[pl.ds(r0, Q_BLK), :] for r0 in starts], [(k_ref, vt_ref), (kc_ref, vct_ref)])
        for r0, o in zip(starts, outs):
            o_ref[pl.ds(r0, Q_BLK), :] = o
        return carry

    lax.fori_loop(0, SEQ // (Q_BLK * DIFF_Q_PER_TRIP), body, 0)
    if maybe_oc_ref:
        oc_ref, = maybe_oc_ref
        assert CTX_LEN == Q_BLK
        oc_ref[...], = rows([qc_ref[...]], [(kc_ref, vct_ref)])


def _diff_attention(lam, sg, q, k, vt, lam_init, *, ctx_out):
    lat = lambda b, h: (b, h)
    ctx = lambda b, h: (T_LAT // CTX_LEN + b, h)
    lat_t = lambda b, h: (h, b)
    ctx_t = lambda b, h: (h, T_LAT // CTX_LEN + b)
    out_specs = [pl.BlockSpec((SEQ, HB), lat)]
    out_shape = [jax.ShapeDtypeStruct((T_LAT, D_MODEL), BF16)]
    if ctx_out:
        out_specs.append(pl.BlockSpec((CTX_LEN, HB), lat))
        out_shape.append(jax.ShapeDtypeStruct((T_CTX, D_MODEL), BF16))
    return pl.pallas_call(
        functools.partial(_diff_kernel, lam_init=lam_init),
        grid=(BATCH, DIFF_HEADS),
        in_specs=[
            pl.BlockSpec(lam.shape, lambda b, h: (0, 0)),
            pl.BlockSpec(sg.shape, lambda b, h: (0, 0)),
            pl.BlockSpec((SEQ, HB), lat),
            pl.BlockSpec((SEQ, HB), lat),
            pl.BlockSpec((HB, SEQ), lat_t),
            pl.BlockSpec((CTX_LEN, HB), ctx),
            pl.BlockSpec((CTX_LEN, HB), ctx),
            pl.BlockSpec((HB, CTX_LEN), ctx_t),
        ],
        out_specs=out_specs,
        out_shape=out_shape,
        scratch_shapes=[pltpu.VMEM((2, SEQ + CTX_LEN, Q_BLK), F32)],
        compiler_params=_cparams("parallel", "parallel"),
        name="diff_attention",
    )(lam, sg, q, k, vt, q, k, vt)


def _route_rows(logits_t, bias_col):
    s = _sigmoid(logits_t)
    sel = s + bias_col
    sel_r = [sel[e:e + 1] for e in range(N_EXPERTS)]
    s_r = [s[e:e + 1] for e in range(N_EXPERTS)]
    group_scores = []
    for g in range(N_GROUPS):
        v = sel_r[g * EXPERTS_PER_GROUP:(g + 1) * EXPERTS_PER_GROUP]
        pairs = [v[a] + v[b] for a in range(EXPERTS_PER_GROUP) for b in range(a + 1, EXPERTS_PER_GROUP)]
        group_scores.append(functools.reduce(jnp.maximum, pairs))
    best = group_scores[0]
    grp = jnp.zeros(best.shape, jnp.int32)
    for g in range(1, N_GROUPS):
        upd = group_scores[g] > best
        best = jnp.where(upd, group_scores[g], best)
        grp = jnp.where(upd, g, grp)

    def pick(rows, j):
        out = rows[j]
        for g in range(1, N_GROUPS):
            out = jnp.where(grp == g, rows[g * EXPERTS_PER_GROUP + j], out)
        return out

    w = [pick(sel_r, j) for j in range(EXPERTS_PER_GROUP)]
    sc = [pick(s_r, j) for j in range(EXPERTS_PER_GROUP)]

    def argmax_first(vals):
        bv = vals[0]
        bi = jnp.zeros(bv.shape, jnp.int32)
        for j in range(1, len(vals)):
            upd = vals[j] > bv
            bv = jnp.where(upd, vals[j], bv)
            bi = jnp.where(upd, j, bi)
        return bi

    i1 = argmax_first(w)
    i2 = argmax_first([jnp.where(i1 == j, -jnp.inf, w[j]) for j in range(EXPERTS_PER_GROUP)])

    def take(vals, idx):
        out = vals[0]
        for j in range(1, len(vals)):
            out = jnp.where(idx == j, vals[j], out)
        return out

    s1 = take(sc, i1)
    s2 = take(sc, i2)
    den = s1 + s2
    return grp * EXPERTS_PER_GROUP + i1, grp * EXPERTS_PER_GROUP + i2, s1 / den, s2 / den


def _oproj_kernel(o_ref, oc_ref, x_ref, mod_ref, w_ref, g_ref, rw_ref, rb_ref, tri_ref,
                  x1_ref, h_ref, ri_ref, rf_ref, cnt_ref, carry_ref):
    i = pl.program_id(0)

    @pl.when(i == 0)
    def _():
        carry_ref[...] = jnp.zeros_like(carry_ref)

    m = mod_ref[0]
    rw = rw_ref[...].astype(BF16)
    ri_ref[...] = jnp.zeros_like(ri_ref)
    rf_ref[...] = jnp.zeros_like(rf_ref)
    carry = carry_ref[...]
    for r0 in range(0, TM, ROUTE_BLK):
        rows = pl.ds(r0, ROUTE_BLK)
        o = jnp.where(i < LAT_TILES, o_ref[rows, :], oc_ref[rows, :])
        x1 = x_ref[rows, :] + m[2:3] * _dot(o, w_ref[0])
        x1_ref[rows, :] = x1
        h = _rms_modulate(x1, g_ref[...], m[3:4], m[4:5])
        hb = h.astype(BF16)
        h_ref[rows, :] = _pack_bf16_pair(h)

        logits = _dot(hb, rw)
        logits_t = logits.T[:N_EXPERTS]
        e0, e1, g0, g1 = _route_rows(logits_t, rb_ref[...])

        eiota = lax.broadcasted_iota(jnp.int32, (N_EXPERTS, ROUTE_BLK), 0)
        oh0 = eiota == e0
        oh1 = eiota == e1
        oh = jnp.where(oh0 | oh1, 1.0, 0.0)
        before = _dot(oh.astype(BF16), tri_ref[...]) + carry
        rank0 = jnp.sum(jnp.where(oh0, before, 0.0), axis=0, keepdims=True)
        rank1 = jnp.sum(jnp.where(oh1, before, 0.0), axis=0, keepdims=True)
        carry = carry + jnp.sum(oh, axis=1, keepdims=True)

        ri_ref[0:1, rows] = e0
        ri_ref[1:2, rows] = e1
        ri_ref[2:3, rows] = rank0.astype(jnp.int32)
        ri_ref[3:4, rows] = rank1.astype(jnp.int32)
        rf_ref[0:1, rows] = g0
        rf_ref[1:2, rows] = g1
    carry_ref[...] = carry
    cnt_ref[...] = jnp.broadcast_to(carry, cnt_ref.shape)


def _oproj_route(o, o_ctx, x, mod_l, w_o, layer, g, rw_pad, rb_col, tri, n_tok):
    n_tiles = n_tok // TM
    tok = lambda i: (i, 0)
    const2 = lambda i: (0, 0)
    lane_tok = lambda i: (0, i)
    return pl.pallas_call(
        _oproj_kernel,
        grid=(n_tiles,),
        in_specs=[
            pl.BlockSpec((TM, D_MODEL), lambda i: (jnp.minimum(i, LAT_TILES - 1), 0)),
            pl.BlockSpec((TM, D_MODEL), lambda i: (jnp.maximum(i - LAT_TILES, 0), 0)),
            pl.BlockSpec((TM, D_MODEL), tok),
            pl.BlockSpec((1, N_MOD, D_MODEL), lambda i: (_mod_row(i), 0, 0)),
            pl.BlockSpec((1, D_MODEL, D_MODEL), lambda i: (layer, 0, 0)),
            pl.BlockSpec((1, D_MODEL), const2),
            pl.BlockSpec((D_MODEL, HB), const2),
            pl.BlockSpec((N_EXPERTS, 1), const2),
            pl.BlockSpec((ROUTE_BLK, ROUTE_BLK), const2),
        ],
        out_specs=[
            pl.BlockSpec((TM, D_MODEL), tok),
            pl.BlockSpec((TM, D_PACK), tok),
            pl.BlockSpec((8, TM), lane_tok),
            pl.BlockSpec((8, TM), lane_tok),
            pl.BlockSpec((N_EXPERTS, HB), const2),
        ],
        out_shape=[
            jax.ShapeDtypeStruct((n_tok, D_MODEL), F32),
            jax.ShapeDtypeStruct((n_tok, D_PACK), jnp.uint32),
            jax.ShapeDtypeStruct((8, n_tok), jnp.int32),
            jax.ShapeDtypeStruct((8, n_tok), F32),
            jax.ShapeDtypeStruct((N_EXPERTS, HB), F32),
        ],
        scratch_shapes=[pltpu.VMEM((N_EXPERTS, 1), F32)],
        compiler_params=_cparams("arbitrary"),
        name="oproj_norm_route",
    )(o, o if o_ctx is None else o_ctx, x, mod_l, w_o, g.reshape(1, D_MODEL), rw_pad, rb_col, tri)


def _moe_kernel(te_ref, tv_ref, xs_ref, wg_ref, wu_ref, wd_ref, ys_ref, wgb, wub, wdb):
    i = pl.program_id(0)
    valid = tv_ref[i]
    e = te_ref[i]
    prev = te_ref[jnp.maximum(i - 1, 0)]
    half = TM // 2

    @pl.when((i == 0) | (e != prev))
    def _():
        wgb[...] = wg_ref[0, 0].astype(BF16)
        wub[...] = wu_ref[0, 0].astype(BF16)
        wdb[...] = wd_ref[0, 0].astype(BF16)

    def ffn(rows, x):
        x = _unpack_bf16_pair(x).astype(BF16)
        a = _dot(x, wgb[...])
        u = _dot(x, wub[...])
        hm = (a * _sigmoid(a) * u).astype(BF16)
        ys_ref[rows, :] = _pack_bf16_pair(_dot(hm, wdb[...]))

    @pl.when(valid == TM)
    def _():
        for r0 in range(0, TM, half):
            ffn(pl.ds(r0, half), xs_ref[pl.ds(r0, half), :])

    for r0 in range(0, TM, half):
        rows = pl.ds(r0, half)

        @pl.when((valid < TM) & (valid >= r0 + half))
        def _():
            ffn(rows, xs_ref[rows, :])

        @pl.when((valid > r0) & (valid < r0 + half))
        def _():
            row = r0 + lax.broadcasted_iota(jnp.int32, (half, 1), 0)
            x = xs_ref[rows, :]
            ffn(rows, jnp.where(row < valid, x, jnp.zeros_like(x)))

        @pl.when(valid <= r0)
        def _():
            ys_ref[rows, :] = jnp.zeros((half, D_PACK), jnp.uint32)


def _grouped_ffn(tile_expert, tile_valid, xs, w_gate, w_up, w_down, layer):
    n_tiles = xs.shape[0] // TM
    wmap = lambda i, te, tv: (layer, te[i], 0, 0)
    tok = lambda i, te, tv: (i, 0)
    return pl.pallas_call(
        _moe_kernel,
        grid_spec=pltpu.PrefetchScalarGridSpec(
            num_scalar_prefetch=2,
            grid=(n_tiles,),
            in_specs=[
                pl.BlockSpec((TM, D_PACK), tok),
                pl.BlockSpec((1, 1, D_MODEL, D_EXPERT), wmap),
                pl.BlockSpec((1, 1, D_MODEL, D_EXPERT), wmap),
                pl.BlockSpec((1, 1, D_EXPERT, D_MODEL), wmap),
            ],
            out_specs=pl.BlockSpec((TM, D_PACK), tok),
            scratch_shapes=[
                pltpu.VMEM((D_MODEL, D_EXPERT), BF16),
                pltpu.VMEM((D_MODEL, D_EXPERT), BF16),
                pltpu.VMEM((D_EXPERT, D_MODEL), BF16),
            ],
        ),
        out_shape=jax.ShapeDtypeStruct((xs.shape[0], D_PACK), jnp.uint32),
        compiler_params=_cparams("arbitrary"),
        name="grouped_ffn",
    )(tile_expert, tile_valid, xs, w_gate, w_up, w_down)


def _moe_layout(route_i, counts, n_tok):
    n_tiles = 2 * n_tok // TM + N_EXPERTS
    counts = counts.astype(jnp.int32)
    tiles_e = (counts + TM - 1) // TM
    tiles_end = jnp.cumsum(tiles_e)
    tile_start = tiles_end - tiles_e
    eid = jnp.arange(N_EXPERTS, dtype=jnp.int32)
    tok_oh = route_i[0:2][:, :, None] == eid
    pos = jnp.sum(jnp.where(tok_oh, tile_start * TM, 0), axis=-1) + route_i[2:4]
    tile_ids = jnp.arange(n_tiles, dtype=jnp.int32)
    n_used = tiles_end[-1]
    te = jnp.sum(tiles_end[None, :] <= jnp.minimum(tile_ids, n_used - 1)[:, None], axis=1).astype(jnp.int32)
    tile_oh = te[:, None] == eid
    cnt_t = jnp.sum(jnp.where(tile_oh, counts, 0), axis=1)
    start_t = jnp.sum(jnp.where(tile_oh, tile_start, 0), axis=1)
    tv = jnp.clip(cnt_t - (tile_ids - start_t) * TM, 0, TM)
    tv = jnp.where(tile_ids < n_used, tv, 0).astype(jnp.int32)
    return pos.astype(jnp.int32), te, tv, n_tiles


SC_CORES = 2
SC_SUBCORES = 16
SC_WORKERS = SC_CORES * SC_SUBCORES
SC_CHUNK = 32


def _sc_mesh():
    return plsc.VectorSubcoreMesh(core_axis_name="c", subcore_axis_name="s")


def _sc_worker_indices(pos, n_tok):
    n_ch = n_tok // SC_WORKERS // SC_CHUNK
    return pos.reshape(2, SC_WORKERS, n_ch, SC_CHUNK).transpose(1, 2, 0, 3), n_ch


def _sc_dispatch(h, pos, n_slots):
    n_tok = h.shape[0]
    per_w = n_tok // SC_WORKERS
    pos_w, n_ch = _sc_worker_indices(pos, n_tok)
    assert n_ch % 2 == 0 and n_ch * SC_CHUNK * SC_WORKERS == n_tok

    @functools.partial(
        pl.kernel, mesh=_sc_mesh(), out_type=jax.ShapeDtypeStruct((n_slots, h.shape[1]), h.dtype),
        scratch_types=[pltpu.VMEM((n_ch, 2, SC_CHUNK), jnp.int32), pltpu.VMEM((2, SC_CHUNK, h.shape[1]), h.dtype),
                       pltpu.SemaphoreType.DMA((2,)), pltpu.SemaphoreType.DMA((2,))],
        name="sc_dispatch")
    def dispatch(h_hbm, pos_hbm, xs_hbm, idx_v, rows_v, load_sem, scat_sem):
        wid = lax.axis_index("s") * SC_CORES + lax.axis_index("c")
        base = wid * per_w
        pltpu.sync_copy(pos_hbm.at[wid], idx_v)

        def load(c, b):
            return pltpu.make_async_copy(h_hbm.at[pl.ds(base + c * SC_CHUNK, SC_CHUNK)], rows_v.at[b],
                                         load_sem.at[b])

        def scat(c, b, k):
            return pltpu.make_async_copy(rows_v.at[b], xs_hbm.at[idx_v.at[c, k]], scat_sem.at[b])

        load(0, 0).start()

        @pl.loop(0, n_ch, step=2)
        def _(c0):
            for b in range(2):
                c = c0 + b
                load(c, b).wait()
                scat(c, b, 0).start()
                scat(c, b, 1).start()

                @pl.when(c >= 1)
                def _():
                    scat(c - 1, 1 - b, 0).wait()
                    scat(c - 1, 1 - b, 1).wait()

                @pl.when(c + 1 < n_ch)
                def _():
                    load(c + 1, 1 - b).start()

        scat(n_ch - 1, 1, 0).wait()
        scat(n_ch - 1, 1, 1).wait()

    return dispatch(h, pos_w)


def _sc_combine_gather(ys, pos):
    n_tok = pos.shape[1]
    per_w = n_tok // SC_WORKERS
    pos_w, n_ch = _sc_worker_indices(pos, n_tok)
    assert n_ch * SC_CHUNK * SC_WORKERS == n_tok

    @functools.partial(
        pl.kernel, mesh=_sc_mesh(), out_type=jax.ShapeDtypeStruct((2, n_tok, ys.shape[1]), ys.dtype),
        scratch_types=[pltpu.VMEM((n_ch, 2, SC_CHUNK), jnp.int32), pltpu.VMEM((2, SC_CHUNK, ys.shape[1]), ys.dtype),
                       pltpu.SemaphoreType.DMA((2,)), pltpu.SemaphoreType.DMA((2,))],
        name="sc_combine_gather")
    def gather(ys_hbm, pos_hbm, yg_hbm, idx_v, rows_v, gath_sem, write_sem):
        wid = lax.axis_index("s") * SC_CORES + lax.axis_index("c")
        base = wid * per_w
        pltpu.sync_copy(pos_hbm.at[wid], idx_v)

        def gath(c, k):
            return pltpu.make_async_copy(ys_hbm.at[idx_v.at[c, k]], rows_v.at[k], gath_sem.at[k])

        def write(c, k):
            return pltpu.make_async_copy(rows_v.at[k], yg_hbm.at[k, pl.ds(base + c * SC_CHUNK, SC_CHUNK)],
                                         write_sem.at[k])

        gath(0, 0).start()

        @pl.loop(0, n_ch)
        def _(c):
            gath(c, 0).wait()
            write(c, 0).start()

            @pl.when(c >= 1)
            def _():
                write(c - 1, 1).wait()

            gath(c, 1).start()
            gath(c, 1).wait()
            write(c, 1).start()
            write(c, 0).wait()

            @pl.when(c + 1 < n_ch)
            def _():
                gath(c + 1, 0).start()

        write(n_ch - 1, 1).wait()

    return gather(ys, pos_w)


def _final_kernel(x1_ref, yg_ref, gt_ref, mod_ref, fg_ref, out_ref):
    x2 = _moe_residual(x1_ref[...], yg_ref[0], yg_ref[1], gt_ref[...], mod_ref[0][5:6])
    ms = jnp.mean(x2 * x2, axis=-1, keepdims=True)
    out_ref[...] = x2 * lax.rsqrt(ms + NORM_EPS) * fg_ref[...]


def _combine_final(x1, yg, gates_t, mod_l, final_g, n_tok):
    tok = lambda i: (i, 0)
    return pl.pallas_call(
        _final_kernel,
        grid=(n_tok // TM,),
        in_specs=[
            pl.BlockSpec((TM, D_MODEL), tok),
            pl.BlockSpec((2, TM, D_PACK), lambda i: (0, i, 0)),
            pl.BlockSpec((TM, 2), tok),
            pl.BlockSpec((1, N_MOD, D_MODEL), lambda i: (_mod_row(i), 0, 0)),
            pl.BlockSpec((1, D_MODEL), lambda i: (0, 0)),
        ],
        out_specs=pl.BlockSpec((TM, D_MODEL), tok),
        out_shape=jax.ShapeDtypeStruct((n_tok, D_MODEL), F32),
        compiler_params=_cparams("parallel"),
        name="moe_combine_final",
    )(x1, yg, gates_t, mod_l, final_g.reshape(1, D_MODEL))


def _diff_lambda_init(layer_idx):
    return 0.8 - 0.6 * math.exp(-0.3 * (layer_idx - 1))


def kernel(x, c, ctx, c_ctx, mod_w, mod_b, norm_mix, norm_ffn, w_qkv, w_o, na_rpb, diff_lambda, diff_subln,
           router_w, router_b, expert_w_gate, expert_w_up, expert_w_down, final_norm):
    mod = _modulation(c, c_ctx, mod_w, mod_b)
    stream = (x.reshape(T_LAT, D_MODEL), ctx.reshape(T_CTX, D_MODEL))
    w_qk_b = w_qkv[:, :, :2 * D_MODEL].astype(BF16)
    w_vt_b = jnp.swapaxes(w_qkv[:, :, 2 * D_MODEL:], 1, 2).astype(BF16)
    w_o_b = w_o.astype(BF16)
    na_tables = _na_pair_table(na_rpb.reshape(-1, N_DR, N_DC))
    cs, sn = _rope_tables()
    rw_pad = jnp.zeros((D_MODEL, HB), F32).at[:, :N_EXPERTS].set(router_w)
    rb_col = router_b.reshape(N_EXPERTS, 1).astype(F32)
    blk = jnp.arange(ROUTE_BLK)
    tri = (blk[:, None] < blk[None, :]).astype(BF16)

    for i in range(DEPTH):
        last = i == DEPTH - 1
        j = i // 2
        is_diff = i % 2 == 1
        xa, q, k, v = _qkv(stream, mod[i], norm_mix[i], w_qk_b, w_vt_b, i, cs, sn, rope=is_diff)
        if not is_diff:
            o, o_ctx = _na_attention(na_tables, j, q, k, v)
        else:
            lam_init = _diff_lambda_init(i + 1)
            sg = diff_subln[j].reshape(HB, 1)
            outs = _diff_attention(diff_lambda[j], sg, q, k, v, lam_init, ctx_out=not last)
            o, o_ctx = outs if not last else (outs[0], None)
        n_tok = T_LAT if last else T_ALL
        x1, h, route_i, route_f, cnt = _oproj_route(o, o_ctx, xa, mod[i], w_o_b, i, norm_ffn[i], rw_pad,
                                                    rb_col, tri, n_tok)
        pos, te, tv, n_tiles = _moe_layout(route_i, cnt[:, 0], n_tok)
        xs = _sc_dispatch(h, pos, n_tiles * TM)
        ys = _grouped_ffn(te, tv, xs, expert_w_gate, expert_w_up, expert_w_down, i)
        yg = _sc_combine_gather(ys, pos)
        stream = (x1, yg, route_f[0:2].T, mod[i])
    out = _combine_final(*stream, final_norm, T_LAT)
    return out.reshape(BATCH, SEQ, D_MODEL)
```

```python
import functools
import math

import jax
import jax.numpy as jnp
from jax import lax
from jax.experimental import pallas as pl
from jax.experimental.pallas import tpu as pltpu
from jax.experimental.pallas import tpu_sc as plsc

F32 = jnp.float32
BF16 = jnp.bfloat16

D_MODEL = 1024
BATCH = 8
SEQ = 2048
DEPTH = 4
CTX_LEN = 256
GRID_W = 64
ROWS = SEQ // GRID_W
NA_HEADS = 16
NA_WIN_ROWS = 8
NA_WIN_COLS = 16
DIFF_HEADS = 8
DIFF_HEAD_DIM = 64
ROPE_THETA = 10000.0
N_EXPERTS = 16
N_GROUPS = 4
EXPERTS_PER_GROUP = 4
D_EXPERT = 1024
N_MOD = 6
NORM_EPS = 1e-6

T_LAT = BATCH * SEQ
T_CTX = BATCH * CTX_LEN
T_ALL = T_LAT + T_CTX
TM = 512
ROUTE_BLK = TM // 2
LAT_TILES = T_LAT // TM
CTX_TILES = T_CTX // TM
TILES_PER_BATCH = SEQ // TM
CTX_ROW = BATCH
MOD_ROWS = 16
HB = 128
N_HB = D_MODEL // HB
NA_RB = 4
NA_BAND_ROWS = 12
NA_Q = NA_RB * GRID_W
NA_BAND = NA_BAND_ROWS * GRID_W
TQ_DIFF = 512
MASK_VALUE = -1e30
LOG2E = math.log2(math.e)
Q_SCALE = 0.125 * LOG2E
VMEM_LIMIT = 56 * 1024 * 1024


def _cparams(*sem):
    return pltpu.CompilerParams(dimension_semantics=sem, vmem_limit_bytes=VMEM_LIMIT)


def _dot(a, b):
    return jnp.dot(a, b, preferred_element_type=F32)


def _dot_nt(a, b):
    return lax.dot_general(a, b, (((1,), (1,)), ((), ())), preferred_element_type=F32)


def _sigmoid(x):
    return 1.0 / (1.0 + jnp.exp(-x))


D_PACK = D_MODEL // 2


def _pack_bf16_pair(x):
    lo = lax.bitcast_convert_type(x[:, :D_PACK].astype(BF16).astype(F32), jnp.uint32)
    hi = lax.bitcast_convert_type(x[:, D_PACK:].astype(BF16).astype(F32), jnp.uint32)
    return (lo >> 16) | hi


def _unpack_bf16_pair(w):
    lo = lax.bitcast_convert_type(w << 16, F32)
    hi = lax.bitcast_convert_type(w & jnp.uint32(0xFFFF0000), F32)
    return jnp.concatenate([lo, hi], axis=1)


def _mod_row(i):
    return jnp.minimum(i // TILES_PER_BATCH, CTX_ROW)


def _mod_kernel(act_ref, w_ref, b_ref, o_ref):
    a = act_ref[...]
    a = a * _sigmoid(a)
    o_ref[0] = _dot(a.astype(BF16), w_ref[0].astype(BF16)) + b_ref[0]


def _modulation(c, c_ctx, mod_w, mod_b):
    tn = 1536
    act = jnp.zeros((MOD_ROWS, D_MODEL), F32).at[:BATCH].set(c).at[CTX_ROW].set(c_ctx)
    out = pl.pallas_call(
        _mod_kernel,
        grid=(DEPTH, N_MOD * D_MODEL // tn),
        in_specs=[
            pl.BlockSpec((MOD_ROWS, D_MODEL), lambda l, j: (0, 0)),
            pl.BlockSpec((1, D_MODEL, tn), lambda l, j: (l, 0, j)),
            pl.BlockSpec((1, 1, tn), lambda l, j: (l, 0, j)),
        ],
        out_specs=pl.BlockSpec((1, MOD_ROWS, tn), lambda l, j: (l, 0, j)),
        out_shape=jax.ShapeDtypeStruct((DEPTH, MOD_ROWS, N_MOD * D_MODEL), F32),
        compiler_params=_cparams("parallel", "parallel"),
        name="adaln_mod",
    )(act, mod_w, mod_b.reshape(DEPTH, 1, N_MOD * D_MODEL))
    return out.reshape(DEPTH, MOD_ROWS, N_MOD, D_MODEL)


def _rms_modulate(x, g, shift, scale):
    ms = jnp.mean(x * x, axis=-1, keepdims=True)
    y = x * lax.rsqrt(ms + NORM_EPS) * g
    return y * (1.0 + scale) + shift


def _rope_block(xb, cs, sn):
    lane = lax.broadcasted_iota(jnp.int32, xb.shape, 1)
    partner = jnp.where((lane & 63) < 32, pltpu.roll(xb, 96, 1), pltpu.roll(xb, 32, 1))
    return xb * cs + partner * sn


def _moe_residual(x1, y0, y1, g, gate_ffn):
    y = g[:, 0:1] * _unpack_bf16_pair(y0) + g[:, 1:2] * _unpack_bf16_pair(y1)
    return x1 + gate_ffn * y


def _qkv_kernel(*refs, rope, first):
    n_stream = 2 if first else 4
    stream_refs, refs = refs[:n_stream], refs[n_stream:]
    mod_ref, g_ref, wqk_ref, wvt_ref, cs_ref, sn_ref, x_ref, q_ref, k_ref, vt_ref = refs
    m = mod_ref[0]
    for r0 in range(0, TM, ROUTE_BLK):
        rows = pl.ds(r0, ROUTE_BLK)
        if first:
            xl_ref, xc_ref = stream_refs
            x = jnp.where(pl.program_id(0) < LAT_TILES, xl_ref[rows, :], xc_ref[rows, :])
        else:
            x1_ref, yg_ref, gt_ref, modp_ref = stream_refs
            x = _moe_residual(x1_ref[rows, :], yg_ref[0, rows, :], yg_ref[1, rows, :], gt_ref[rows, :],
                              modp_ref[0][5:6])
        x_ref[rows, :] = x
        h = _rms_modulate(x, g_ref[...], m[0:1], m[1:2]).astype(BF16)
        for idx, out in enumerate((q_ref, k_ref)):
            acc = _dot(h, wqk_ref[0, :, idx * D_MODEL:(idx + 1) * D_MODEL])
            if rope:
                cs = cs_ref[rows, :]
                sn = sn_ref[rows, :]
                acc = jnp.concatenate(
                    [_rope_block(acc[:, j * HB:(j + 1) * HB], cs, sn) for j in range(N_HB)], axis=1)
            if idx == 0:
                acc = acc * Q_SCALE
            out[rows, :] = acc.astype(BF16)
        vt_ref[:, rows] = _dot_nt(wvt_ref[0], h).astype(BF16)


def _qkv(stream, mod_l, g, w_qk, w_vt, layer, cs, sn, *, rope):
    first = len(stream) == 2
    n_tiles = T_ALL // TM
    tok = lambda i: (i, 0)
    const2 = lambda i: (0, 0)
    mod_spec = pl.BlockSpec((1, N_MOD, D_MODEL), lambda i: (_mod_row(i), 0, 0))
    rope_idx = lambda i: (jnp.where(i < LAT_TILES, i % TILES_PER_BATCH, TILES_PER_BATCH), 0)
    out = jax.ShapeDtypeStruct((T_ALL, D_MODEL), BF16)
    wmap = lambda i: (layer, 0, 0)
    if first:
        stream_specs = [
            pl.BlockSpec((TM, D_MODEL), lambda i: (jnp.minimum(i, LAT_TILES - 1), 0)),
            pl.BlockSpec((TM, D_MODEL), lambda i: (jnp.maximum(i - LAT_TILES, 0), 0)),
        ]
    else:
        stream_specs = [
            pl.BlockSpec((TM, D_MODEL), tok),
            pl.BlockSpec((2, TM, D_PACK), lambda i: (0, i, 0)),
            pl.BlockSpec((TM, 2), tok),
            mod_spec,
        ]
    return pl.pallas_call(
        functools.partial(_qkv_kernel, rope=rope, first=first),
        grid=(n_tiles,),
        in_specs=stream_specs + [
            mod_spec,
            pl.BlockSpec((1, D_MODEL), const2),
            pl.BlockSpec((1, D_MODEL, 2 * D_MODEL), wmap),
            pl.BlockSpec((1, D_MODEL, D_MODEL), wmap),
            pl.BlockSpec((TM, HB), rope_idx),
            pl.BlockSpec((TM, HB), rope_idx),
        ],
        out_specs=[pl.BlockSpec((TM, D_MODEL), tok), pl.BlockSpec((TM, D_MODEL), tok),
                   pl.BlockSpec((TM, D_MODEL), tok), pl.BlockSpec((D_MODEL, TM), lambda i: (0, i))],
        out_shape=[jax.ShapeDtypeStruct((T_ALL, D_MODEL), F32), out, out,
                   jax.ShapeDtypeStruct((D_MODEL, T_ALL), BF16)],
        compiler_params=_cparams("parallel"),
        name="norm_qkv_rope" if rope else "norm_qkv",
    )(*stream, mod_l, g.reshape(1, D_MODEL), w_qk, w_vt, cs, sn)


def _rope_tables():
    t = jnp.arange(SEQ)
    row = (t // GRID_W).astype(F32)
    col = (t % GRID_W).astype(F32)
    n_freq = DIFF_HEAD_DIM // 4
    inv_freq = ROPE_THETA ** (-jnp.arange(n_freq, dtype=F32) / n_freq)
    ang = jnp.concatenate([row[:, None] * inv_freq, col[:, None] * inv_freq], axis=-1)
    cos, sin = jnp.cos(ang), jnp.sin(ang)
    cs = jnp.concatenate([cos, cos, cos, cos], axis=-1)
    sn = jnp.concatenate([-sin, sin, -sin, sin], axis=-1)
    cs = jnp.concatenate([cs, jnp.ones((TM, HB), F32)], axis=0)
    sn = jnp.concatenate([sn, jnp.zeros((TM, HB), F32)], axis=0)
    return cs, sn


Q_BLK = 256
KEY_BLK = 256
SUBLANES = 8
DIFF_Q_BLK = 256
DIFF_Q_PER_TRIP = 8
DIFF_KEY_BLK = 1024


def _colwise(reduce_fn, x):
    return reduce_fn(x.reshape(x.shape[0] // SUBLANES, SUBLANES, x.shape[1]), axis=0)


def _key_blocks(k_ref, vt_ref, key_start=0, n_keys=None, bias=None, blk=KEY_BLK):
    n_keys = k_ref.shape[0] if n_keys is None else n_keys
    blk = min(blk, n_keys)
    blocks = []
    for c, s0 in enumerate(range(0, n_keys, blk)):
        blocks.append((
            lambda s0=s0: k_ref[pl.ds(key_start + s0, blk), :],
            lambda s0=s0: vt_ref[:, pl.ds(key_start + s0, blk)],
            None if bias is None else functools.partial(bias, c),
            blk,
        ))
    return blocks


def _attention_t(operands, s_scr):
    def rows_of(u, c):
        sizes = [blk[3] for blk in operands[u][1]]
        return slice(sum(sizes[:c]), sum(sizes[:c + 1]))

    def score(u, c, mx):
        qm, blocks = operands[u]
        k_tile, _, bias, _ = blocks[c]
        st = _dot_nt(k_tile(), qm)
        if bias is not None:
            st = st + bias()
        s_scr[u % 2, rows_of(u, c), :n_q] = st
        return jnp.maximum(mx, _colwise(jnp.max, st))

    def expo(u, c, m, acc, den):
        vt_tile = operands[u][1][c][1]
        e = jnp.exp2(s_scr[u % 2, rows_of(u, c), :n_q] - m)
        o = _dot(vt_tile(), e.astype(BF16))
        return (o if acc is None else acc + o), den + _colwise(jnp.sum, e)

    n_q = operands[0][0].shape[0]
    neg = jnp.full((SUBLANES, n_q), -jnp.inf, F32)
    zero = jnp.zeros((SUBLANES, n_q), F32)
    n_blocks = [len(blocks) for _, blocks in operands]
    mx = neg
    for c in range(n_blocks[0]):
        mx = score(0, c, mx)
    outs = []
    for u in range(len(operands)):
        m = jnp.max(mx, axis=0, keepdims=True)
        acc, den, mx = None, zero, neg
        n_next = n_blocks[u + 1] if u + 1 < len(operands) else 0
        for c in range(max(n_blocks[u], n_next)):
            if c < n_next:
                mx = score(u + 1, c, mx)
            if c < n_blocks[u]:
                acc, den = expo(u, c, m, acc, den)
        outs.append(acc / jnp.sum(den, axis=0, keepdims=True))
    return outs


def _half_masks(shape):
    lane = lax.broadcasted_iota(jnp.int32, shape, 1)
    return lane < 64, lane >= 64


N_DR = 2 * NA_WIN_ROWS - 1
N_DC = 2 * NA_WIN_COLS - 1
PAIR_MASK_FIRST = N_DR - 1
PAIR_MASK_SECOND = N_DR
PAIR_MASKED = N_DR + 1
N_PAIRS = N_DR + 2
NA_KEY_BLK = 768
DR_LO = NA_WIN_ROWS - 1 - NA_WIN_ROWS // 2
DR_HI = DR_LO + NA_WIN_ROWS - 1


def _na_pair_table(rpb):
    h = rpb.shape[0]
    w = GRID_W
    v = jnp.zeros((h, N_DR, 2 * w), F32)
    v = v.at[..., 0:NA_WIN_COLS].set(rpb[..., NA_WIN_COLS - 1:].astype(F32))
    v = v.at[..., 2 * w - (NA_WIN_COLS - 1):].set(rpb[..., :NA_WIN_COLS - 1].astype(F32))
    y = jnp.broadcast_to(v[:, :, None, :], (h, N_DR, w, 2 * w)).reshape(h, N_DR, w * 2 * w)
    t = y[..., :w * (2 * w - 1)].reshape(h, N_DR, w, 2 * w - 1)[..., :w]
    col = jnp.arange(w)
    cstart = jnp.clip(col - NA_WIN_COLS // 2, 0, w - NA_WIN_COLS)
    cvalid = (col[None, :] >= cstart[:, None]) & (col[None, :] < cstart[:, None] + NA_WIN_COLS)
    tt = jnp.swapaxes(jnp.where(cvalid, t * LOG2E, MASK_VALUE), -1, -2)
    masked = jnp.full((h, 1, w, w), MASK_VALUE, F32)
    regular = jnp.concatenate([tt[:, 1:], tt[:, :-1]], axis=-1)
    first = jnp.concatenate([masked, tt[:, DR_HI:DR_HI + 1]], axis=-1)
    second = jnp.concatenate([tt[:, DR_LO:DR_LO + 1], masked], axis=-1)
    return jnp.concatenate([regular, first, second, jnp.concatenate([masked, masked], axis=-1)], axis=1)


def _na_pair_entries(q_row0, band_row0, band_rows):
    entries = []
    for j in range(band_rows):
        kr = band_row0 + j
        row = []
        for a in range(0, NA_RB, 2):
            ok, d = [], []
            for r in (q_row0 + a, q_row0 + a + 1):
                r0 = min(max(r - NA_WIN_ROWS // 2, 0), ROWS - NA_WIN_ROWS)
                ok.append(r0 <= kr < r0 + NA_WIN_ROWS)
                d.append(kr - r + NA_WIN_ROWS - 1)
            if ok[0] and ok[1]:
                row.append(d[0] - 1)
            elif ok[1]:
                assert d[1] == DR_HI
                row.append(PAIR_MASK_FIRST)
            elif ok[0]:
                assert d[0] == DR_LO
                row.append(PAIR_MASK_SECOND)
            else:
                row.append(PAIR_MASKED)
        entries.append(row)
    return entries


def _na_kernel(tab_ref, q_ref, k_ref, vt_ref, qc_ref, kc_ref, vct_ref, o_ref, oc_ref, s_scr):
    n_blocks = ROWS // NA_RB
    ctx_blocks = _key_blocks(kc_ref, vct_ref)

    def operands(q_start, band_start, band_rows, entries):
        q = q_ref[pl.ds(q_start, NA_Q), :]
        n_band = band_rows * GRID_W
        blk = NA_KEY_BLK if n_band % NA_KEY_BLK == 0 else n_band
        rows_per_chunk = blk // GRID_W
        ops = []
        for hh, msk in enumerate(_half_masks(q.shape)):
            def bias(c, hh=hh):
                rows = entries[c * rows_per_chunk:(c + 1) * rows_per_chunk]
                return jnp.concatenate(
                    [jnp.concatenate([tab_ref[hh, e] for e in row], axis=1) for row in rows], axis=0)

            band = _key_blocks(k_ref, vt_ref, band_start, n_band, bias, blk)
            ops.append((jnp.where(msk, q, jnp.zeros_like(q)), band + ctx_blocks))
        return ops

    def merge_heads(o_lo, o_hi):
        feat = lax.broadcasted_iota(jnp.int32, o_lo.shape, 0)
        return jnp.where(feat < HB // 2, o_lo, o_hi).T.astype(BF16)

    last_band = ROWS - NA_WIN_ROWS
    interior = _na_pair_entries(NA_RB, 0, NA_BAND_ROWS)
    blocks = [(0, 0, NA_WIN_ROWS, _na_pair_entries(0, 0, NA_WIN_ROWS))]
    blocks += [(rb * NA_Q, (rb - 1) * NA_Q, NA_BAND_ROWS, interior) for rb in range(1, n_blocks - 1)]
    blocks.append(((n_blocks - 1) * NA_Q, last_band * GRID_W, NA_WIN_ROWS,
                   _na_pair_entries(ROWS - NA_RB, last_band, NA_WIN_ROWS)))
    qc = qc_ref[...]
    ctx_ops = [(jnp.where(msk, qc, jnp.zeros_like(qc)), ctx_blocks) for msk in _half_masks(qc.shape)]
    outs = _attention_t([op for blk in blocks for op in operands(*blk)] + ctx_ops, s_scr)
    for i, blk in enumerate(blocks):
        o_ref[pl.ds(blk[0], NA_Q), :] = merge_heads(outs[2 * i], outs[2 * i + 1])
    oc_ref[...] = merge_heads(outs[-2], outs[-1])


def _na_attention(tables, layer_na, q, k, vt):
    assert NA_Q == Q_BLK and CTX_LEN == Q_BLK
    lat = lambda hp, b: (b, hp)
    ctx = lambda hp, b: (T_LAT // CTX_LEN + b, hp)
    return pl.pallas_call(
        _na_kernel,
        grid=(N_HB, BATCH),
        in_specs=[
            pl.BlockSpec((2, N_PAIRS, GRID_W, 2 * GRID_W), lambda hp, b: (layer_na * N_HB + hp, 0, 0, 0)),
            pl.BlockSpec((SEQ, HB), lat),
            pl.BlockSpec((SEQ, HB), lat),
            pl.BlockSpec((HB, SEQ), lambda hp, b: (hp, b)),
            pl.BlockSpec((CTX_LEN, HB), ctx),
            pl.BlockSpec((CTX_LEN, HB), ctx),
            pl.BlockSpec((HB, CTX_LEN), lambda hp, b: (hp, T_LAT // CTX_LEN + b)),
        ],
        out_specs=[pl.BlockSpec((SEQ, HB), lat), pl.BlockSpec((CTX_LEN, HB), lat)],
        out_shape=[jax.ShapeDtypeStruct((T_LAT, D_MODEL), BF16), jax.ShapeDtypeStruct((T_CTX, D_MODEL), BF16)],
        scratch_shapes=[pltpu.VMEM((2, NA_BAND + CTX_LEN, Q_BLK), F32)],
        compiler_params=_cparams("parallel", "parallel"),
        name="na_attention",
    )(tables, q, k, vt, q, k, vt)


def _diff_lambda(lam_ref, lam_init):
    lam = lam_ref[...]
    a = jnp.sum(lam[0:1] * lam[1:2], axis=-1, keepdims=True)
    b = jnp.sum(lam[2:3] * lam[3:4], axis=-1, keepdims=True)
    return jnp.exp(a) - jnp.exp(b) + lam_init


def _diff_kernel(lam_ref, sg_ref, q_ref, k_ref, vt_ref, qc_ref, kc_ref, vct_ref, o_ref, *rest, lam_init):
    *maybe_oc_ref, s_scr = rest
    lam = _diff_lambda(lam_ref, lam_init)
    sg_col = sg_ref[...]

    def rows(qs, pieces):
        blocks = [blk for k_r, vt_r in pieces for blk in _key_blocks(k_r, vt_r, blk=DIFF_KEY_BLK)]
        outs = _attention_t([(jnp.where(msk, q, jnp.zeros_like(q)), blocks)
                             for q in qs for msk in _half_masks(q.shape)], s_scr)
        res = []
        for o1, o2 in zip(outs[0::2], outs[1::2]):
            ot = o1 - lam * o2
            ms = jnp.mean(ot * ot, axis=0, keepdims=True)
            ot = ot * lax.rsqrt(ms + NORM_EPS) * sg_col * (1.0 - lam_init)
            res.append(ot.T.astype(BF16))
        return res

    def body(t, carry):
        starts = [pl.multiple_of((t * DIFF_Q_PER_TRIP + j) * DIFF_Q_BLK, DIFF_Q_BLK)
                  for j in range(DIFF_Q_PER_TRIP)]
        outs = rows([q_ref[pl.ds(r0, DIFF_Q_BLK), :] for r0 in starts], [(k_ref, vt_ref), (kc_ref, vct_ref)])
        for r0, o in zip(starts, outs):
            o_ref[pl.ds(r0, DIFF_Q_BLK), :] = o
        return carry

    lax.fori_loop(0, SEQ // (DIFF_Q_BLK * DIFF_Q_PER_TRIP), body, 0)
    if maybe_oc_ref:
        oc_ref, = maybe_oc_ref
        oc_ref[...], = rows([qc_ref[...]], [(kc_ref, vct_ref)])


def _diff_attention(lam, sg, q, k, vt, lam_init, *, ctx_out):
    lat = lambda b, h: (b, h)
    ctx = lambda b, h: (T_LAT // CTX_LEN + b, h)
    lat_t = lambda b, h: (h, b)
    ctx_t = lambda b, h: (h, T_LAT // CTX_LEN + b)
    out_specs = [pl.BlockSpec((SEQ, HB), lat)]
    out_shape = [jax.ShapeDtypeStruct((T_LAT, D_MODEL), BF16)]
    if ctx_out:
        out_specs.append(pl.BlockSpec((CTX_LEN, HB), lat))
        out_shape.append(jax.ShapeDtypeStruct((T_CTX, D_MODEL), BF16))
    return pl.pallas_call(
        functools.partial(_diff_kernel, lam_init=lam_init),
        grid=(BATCH, DIFF_HEADS),
        in_specs=[
            pl.BlockSpec(lam.shape, lambda b, h: (0, 0)),
            pl.BlockSpec(sg.shape, lambda b, h: (0, 0)),
            pl.BlockSpec((SEQ, HB), lat),
            pl.BlockSpec((SEQ, HB), lat),
            pl.BlockSpec((HB, SEQ), lat_t),
            pl.BlockSpec((CTX_LEN, HB), ctx),
            pl.BlockSpec((CTX_LEN, HB), ctx),
            pl.BlockSpec((HB, CTX_LEN), ctx_t),
        ],
        out_specs=out_specs,
        out_shape=out_shape,
        scratch_shapes=[pltpu.VMEM((2, SEQ + CTX_LEN, DIFF_Q_BLK), F32)],
        compiler_params=_cparams("parallel", "parallel"),
        name="diff_attention",
    )(lam, sg, q, k, vt, q, k, vt)


def _route_rows(logits_t, bias_col):
    s = _sigmoid(logits_t)
    sel = s + bias_col
    sel_r = [sel[e:e + 1] for e in range(N_EXPERTS)]
    s_r = [s[e:e + 1] for e in range(N_EXPERTS)]
    group_scores = []
    for g in range(N_GROUPS):
        v = sel_r[g * EXPERTS_PER_GROUP:(g + 1) * EXPERTS_PER_GROUP]
        pairs = [v[a] + v[b] for a in range(EXPERTS_PER_GROUP) for b in range(a + 1, EXPERTS_PER_GROUP)]
        group_scores.append(functools.reduce(jnp.maximum, pairs))
    best = group_scores[0]
    grp = jnp.zeros(best.shape, jnp.int32)
    for g in range(1, N_GROUPS):
        upd = group_scores[g] > best
        best = jnp.where(upd, group_scores[g], best)
        grp = jnp.where(upd, g, grp)

    def pick(rows, j):
        out = rows[j]
        for g in range(1, N_GROUPS):
            out = jnp.where(grp == g, rows[g * EXPERTS_PER_GROUP + j], out)
        return out

    w = [pick(sel_r, j) for j in range(EXPERTS_PER_GROUP)]
    sc = [pick(s_r, j) for j in range(EXPERTS_PER_GROUP)]

    def argmax_first(vals):
        bv = vals[0]
        bi = jnp.zeros(bv.shape, jnp.int32)
        for j in range(1, len(vals)):
            upd = vals[j] > bv
            bv = jnp.where(upd, vals[j], bv)
            bi = jnp.where(upd, j, bi)
        return bi

    i1 = argmax_first(w)
    i2 = argmax_first([jnp.where(i1 == j, -jnp.inf, w[j]) for j in range(EXPERTS_PER_GROUP)])

    def take(vals, idx):
        out = vals[0]
        for j in range(1, len(vals)):
            out = jnp.where(idx == j, vals[j], out)
        return out

    s1 = take(sc, i1)
    s2 = take(sc, i2)
    den = s1 + s2
    return grp * EXPERTS_PER_GROUP + i1, grp * EXPERTS_PER_GROUP + i2, s1 / den, s2 / den


def _oproj_kernel(o_ref, oc_ref, x_ref, mod_ref, w_ref, g_ref, rw_ref, rb_ref, tri_ref,
                  x1_ref, h_ref, ri_ref, rf_ref, cnt_ref, carry_ref):
    i = pl.program_id(0)

    @pl.when(i == 0)
    def _():
        carry_ref[...] = jnp.zeros_like(carry_ref)

    m = mod_ref[0]
    rw = rw_ref[...].astype(BF16)
    ri_ref[...] = jnp.zeros_like(ri_ref)
    rf_ref[...] = jnp.zeros_like(rf_ref)
    carry = carry_ref[...]
    for r0 in range(0, TM, ROUTE_BLK):
        rows = pl.ds(r0, ROUTE_BLK)
        o = jnp.where(i < LAT_TILES, o_ref[rows, :], oc_ref[rows, :])
        x1 = x_ref[rows, :] + m[2:3] * _dot(o, w_ref[0])
        x1_ref[rows, :] = x1
        h = _rms_modulate(x1, g_ref[...], m[3:4], m[4:5])
        hb = h.astype(BF16)
        h_ref[rows, :] = _pack_bf16_pair(h)

        logits = _dot(hb, rw)
        logits_t = logits.T[:N_EXPERTS]
        e0, e1, g0, g1 = _route_rows(logits_t, rb_ref[...])

        eiota = lax.broadcasted_iota(jnp.int32, (N_EXPERTS, ROUTE_BLK), 0)
        oh0 = eiota == e0
        oh1 = eiota == e1
        oh = jnp.where(oh0 | oh1, 1.0, 0.0)
        before = _dot(oh.astype(BF16), tri_ref[...]) + carry
        rank0 = jnp.sum(jnp.where(oh0, before, 0.0), axis=0, keepdims=True)
        rank1 = jnp.sum(jnp.where(oh1, before, 0.0), axis=0, keepdims=True)
        carry = carry + jnp.sum(oh, axis=1, keepdims=True)

        ri_ref[0:1, rows] = e0
        ri_ref[1:2, rows] = e1
        ri_ref[2:3, rows] = rank0.astype(jnp.int32)
        ri_ref[3:4, rows] = rank1.astype(jnp.int32)
        rf_ref[0:1, rows] = g0
        rf_ref[1:2, rows] = g1
    carry_ref[...] = carry
    cnt_ref[...] = jnp.broadcast_to(carry, cnt_ref.shape)


def _oproj_route(o, o_ctx, x, mod_l, w_o, layer, g, rw_pad, rb_col, tri, n_tok):
    n_tiles = n_tok // TM
    tok = lambda i: (i, 0)
    const2 = lambda i: (0, 0)
    lane_tok = lambda i: (0, i)
    return pl.pallas_call(
        _oproj_kernel,
        grid=(n_tiles,),
        in_specs=[
            pl.BlockSpec((TM, D_MODEL), lambda i: (jnp.minimum(i, LAT_TILES - 1), 0)),
            pl.BlockSpec((TM, D_MODEL), lambda i: (jnp.maximum(i - LAT_TILES, 0), 0)),
            pl.BlockSpec((TM, D_MODEL), tok),
            pl.BlockSpec((1, N_MOD, D_MODEL), lambda i: (_mod_row(i), 0, 0)),
            pl.BlockSpec((1, D_MODEL, D_MODEL), lambda i: (layer, 0, 0)),
            pl.BlockSpec((1, D_MODEL), const2),
            pl.BlockSpec((D_MODEL, HB), const2),
            pl.BlockSpec((N_EXPERTS, 1), const2),
            pl.BlockSpec((ROUTE_BLK, ROUTE_BLK), const2),
        ],
        out_specs=[
            pl.BlockSpec((TM, D_MODEL), tok),
            pl.BlockSpec((TM, D_PACK), tok),
            pl.BlockSpec((8, TM), lane_tok),
            pl.BlockSpec((8, TM), lane_tok),
            pl.BlockSpec((N_EXPERTS, HB), const2),
        ],
        out_shape=[
            jax.ShapeDtypeStruct((n_tok, D_MODEL), F32),
            jax.ShapeDtypeStruct((n_tok, D_PACK), jnp.uint32),
            jax.ShapeDtypeStruct((8, n_tok), jnp.int32),
            jax.ShapeDtypeStruct((8, n_tok), F32),
            jax.ShapeDtypeStruct((N_EXPERTS, HB), F32),
        ],
        scratch_shapes=[pltpu.VMEM((N_EXPERTS, 1), F32)],
        compiler_params=_cparams("arbitrary"),
        name="oproj_norm_route",
    )(o, o if o_ctx is None else o_ctx, x, mod_l, w_o, g.reshape(1, D_MODEL), rw_pad, rb_col, tri)


def _moe_kernel(te_ref, tv_ref, xs_ref, wg_ref, wu_ref, wd_ref, ys_ref, wgb, wub, wdb):
    i = pl.program_id(0)
    valid = tv_ref[i]
    e = te_ref[i]
    prev = te_ref[jnp.maximum(i - 1, 0)]
    half = TM // 2

    @pl.when((i == 0) | (e != prev))
    def _():
        wgb[...] = wg_ref[0, 0].astype(BF16)
        wub[...] = wu_ref[0, 0].astype(BF16)
        wdb[...] = wd_ref[0, 0].astype(BF16)

    def ffn(rows, x):
        x = _unpack_bf16_pair(x).astype(BF16)
        a = _dot(x, wgb[...])
        u = _dot(x, wub[...])
        hm = (a * _sigmoid(a) * u).astype(BF16)
        ys_ref[rows, :] = _pack_bf16_pair(_dot(hm, wdb[...]))

    @pl.when(valid == TM)
    def _():
        for r0 in range(0, TM, half):
            ffn(pl.ds(r0, half), xs_ref[pl.ds(r0, half), :])

    for r0 in range(0, TM, half):
        rows = pl.ds(r0, half)

        @pl.when((valid < TM) & (valid >= r0 + half))
        def _():
            ffn(rows, xs_ref[rows, :])

        @pl.when((valid > r0) & (valid < r0 + half))
        def _():
            row = r0 + lax.broadcasted_iota(jnp.int32, (half, 1), 0)
            x = xs_ref[rows, :]
            ffn(rows, jnp.where(row < valid, x, jnp.zeros_like(x)))

        @pl.when(valid <= r0)
        def _():
            ys_ref[rows, :] = jnp.zeros((half, D_PACK), jnp.uint32)


def _grouped_ffn(tile_expert, tile_valid, xs, w_gate, w_up, w_down, layer):
    n_tiles = xs.shape[0] // TM
    wmap = lambda i, te, tv: (layer, te[i], 0, 0)
    tok = lambda i, te, tv: (i, 0)
    return pl.pallas_call(
        _moe_kernel,
        grid_spec=pltpu.PrefetchScalarGridSpec(
            num_scalar_prefetch=2,
            grid=(n_tiles,),
            in_specs=[
                pl.BlockSpec((TM, D_PACK), tok),
                pl.BlockSpec((1, 1, D_MODEL, D_EXPERT), wmap),
                pl.BlockSpec((1, 1, D_MODEL, D_EXPERT), wmap),
                pl.BlockSpec((1, 1, D_EXPERT, D_MODEL), wmap),
            ],
            out_specs=pl.BlockSpec((TM, D_PACK), tok),
            scratch_shapes=[
                pltpu.VMEM((D_MODEL, D_EXPERT), BF16),
                pltpu.VMEM((D_MODEL, D_EXPERT), BF16),
                pltpu.VMEM((D_EXPERT, D_MODEL), BF16),
            ],
        ),
        out_shape=jax.ShapeDtypeStruct((xs.shape[0], D_PACK), jnp.uint32),
        compiler_params=_cparams("arbitrary"),
        name="grouped_ffn",
    )(tile_expert, tile_valid, xs, w_gate, w_up, w_down)


def _moe_layout(route_i, counts, n_tok):
    n_tiles = 2 * n_tok // TM + N_EXPERTS
    counts = counts.astype(jnp.int32)
    tiles_e = (counts + TM - 1) // TM
    tiles_end = jnp.cumsum(tiles_e)
    tile_start = tiles_end - tiles_e
    eid = jnp.arange(N_EXPERTS, dtype=jnp.int32)
    tok_oh = route_i[0:2][:, :, None] == eid
    pos = jnp.sum(jnp.where(tok_oh, tile_start * TM, 0), axis=-1) + route_i[2:4]
    tile_ids = jnp.arange(n_tiles, dtype=jnp.int32)
    n_used = tiles_end[-1]
    te = jnp.sum(tiles_end[None, :] <= jnp.minimum(tile_ids, n_used - 1)[:, None], axis=1).astype(jnp.int32)
    tile_oh = te[:, None] == eid
    cnt_t = jnp.sum(jnp.where(tile_oh, counts, 0), axis=1)
    start_t = jnp.sum(jnp.where(tile_oh, tile_start, 0), axis=1)
    tv = jnp.clip(cnt_t - (tile_ids - start_t) * TM, 0, TM)
    tv = jnp.where(tile_ids < n_used, tv, 0).astype(jnp.int32)
    return pos.astype(jnp.int32), te, tv, n_tiles


SC_CORES = 2
SC_SUBCORES = 16
SC_WORKERS = SC_CORES * SC_SUBCORES
SC_CHUNK = 32


def _sc_mesh():
    return plsc.VectorSubcoreMesh(core_axis_name="c", subcore_axis_name="s")


def _sc_worker_indices(pos, n_tok):
    n_ch = n_tok // SC_WORKERS // SC_CHUNK
    return pos.reshape(2, SC_WORKERS, n_ch, SC_CHUNK).transpose(1, 2, 0, 3), n_ch


def _sc_dispatch(h, pos, n_slots):
    n_tok = h.shape[0]
    per_w = n_tok // SC_WORKERS
    pos_w, n_ch = _sc_worker_indices(pos, n_tok)
    assert n_ch % 2 == 0 and n_ch * SC_CHUNK * SC_WORKERS == n_tok

    @functools.partial(
        pl.kernel, mesh=_sc_mesh(), out_type=jax.ShapeDtypeStruct((n_slots, h.shape[1]), h.dtype),
        scratch_types=[pltpu.VMEM((n_ch, 2, SC_CHUNK), jnp.int32), pltpu.VMEM((2, SC_CHUNK, h.shape[1]), h.dtype),
                       pltpu.SemaphoreType.DMA((2,)), pltpu.SemaphoreType.DMA((2,))],
        name="sc_dispatch")
    def dispatch(h_hbm, pos_hbm, xs_hbm, idx_v, rows_v, load_sem, scat_sem):
        wid = lax.axis_index("s") * SC_CORES + lax.axis_index("c")
        base = wid * per_w
        pltpu.sync_copy(pos_hbm.at[wid], idx_v)

        def load(c, b):
            return pltpu.make_async_copy(h_hbm.at[pl.ds(base + c * SC_CHUNK, SC_CHUNK)], rows_v.at[b],
                                         load_sem.at[b])

        def scat(c, b, k):
            return pltpu.make_async_copy(rows_v.at[b], xs_hbm.at[idx_v.at[c, k]], scat_sem.at[b])

        load(0, 0).start()

        @pl.loop(0, n_ch, step=2)
        def _(c0):
            for b in range(2):
                c = c0 + b
                load(c, b).wait()
                scat(c, b, 0).start()
                scat(c, b, 1).start()

                @pl.when(c >= 1)
                def _():
                    scat(c - 1, 1 - b, 0).wait()
                    scat(c - 1, 1 - b, 1).wait()

                @pl.when(c + 1 < n_ch)
                def _():
                    load(c + 1, 1 - b).start()

        scat(n_ch - 1, 1, 0).wait()
        scat(n_ch - 1, 1, 1).wait()

    return dispatch(h, pos_w)


def _sc_combine_gather(ys, pos):
    n_tok = pos.shape[1]
    per_w = n_tok // SC_WORKERS
    pos_w, n_ch = _sc_worker_indices(pos, n_tok)
    assert n_ch * SC_CHUNK * SC_WORKERS == n_tok

    @functools.partial(
        pl.kernel, mesh=_sc_mesh(), out_type=jax.ShapeDtypeStruct((2, n_tok, ys.shape[1]), ys.dtype),
        scratch_types=[pltpu.VMEM((n_ch, 2, SC_CHUNK), jnp.int32), pltpu.VMEM((2, SC_CHUNK, ys.shape[1]), ys.dtype),
                       pltpu.SemaphoreType.DMA((2,)), pltpu.SemaphoreType.DMA((2,))],
        name="sc_combine_gather")
    def gather(ys_hbm, pos_hbm, yg_hbm, idx_v, rows_v, gath_sem, write_sem):
        wid = lax.axis_index("s") * SC_CORES + lax.axis_index("c")
        base = wid * per_w
        pltpu.sync_copy(pos_hbm.at[wid], idx_v)

        def gath(c, k):
            return pltpu.make_async_copy(ys_hbm.at[idx_v.at[c, k]], rows_v.at[k], gath_sem.at[k])

        def write(c, k):
            return pltpu.make_async_copy(rows_v.at[k], yg_hbm.at[k, pl.ds(base + c * SC_CHUNK, SC_CHUNK)],
                                         write_sem.at[k])

        gath(0, 0).start()

        @pl.loop(0, n_ch)
        def _(c):
            gath(c, 0).wait()
            write(c, 0).start()

            @pl.when(c >= 1)
            def _():
                write(c - 1, 1).wait()

            gath(c, 1).start()
            gath(c, 1).wait()
            write(c, 1).start()
            write(c, 0).wait()

            @pl.when(c + 1 < n_ch)
            def _():
                gath(c + 1, 0).start()

        write(n_ch - 1, 1).wait()

    return gather(ys, pos_w)


def _final_kernel(x1_ref, yg_ref, gt_ref, mod_ref, fg_ref, out_ref):
    x2 = _moe_residual(x1_ref[...], yg_ref[0], yg_ref[1], gt_ref[...], mod_ref[0][5:6])
    ms = jnp.mean(x2 * x2, axis=-1, keepdims=True)
    out_ref[...] = x2 * lax.rsqrt(ms + NORM_EPS) * fg_ref[...]


def _combine_final(x1, yg, gates_t, mod_l, final_g, n_tok):
    tok = lambda i: (i, 0)
    return pl.pallas_call(
        _final_kernel,
        grid=(n_tok // TM,),
        in_specs=[
            pl.BlockSpec((TM, D_MODEL), tok),
            pl.BlockSpec((2, TM, D_PACK), lambda i: (0, i, 0)),
            pl.BlockSpec((TM, 2), tok),
            pl.BlockSpec((1, N_MOD, D_MODEL), lambda i: (_mod_row(i), 0, 0)),
            pl.BlockSpec((1, D_MODEL), lambda i: (0, 0)),
        ],
        out_specs=pl.BlockSpec((TM, D_MODEL), tok),
        out_shape=jax.ShapeDtypeStruct((n_tok, D_MODEL), F32),
        compiler_params=_cparams("parallel"),
        name="moe_combine_final",
    )(x1, yg, gates_t, mod_l, final_g.reshape(1, D_MODEL))


def _diff_lambda_init(layer_idx):
    return 0.8 - 0.6 * math.exp(-0.3 * (layer_idx - 1))


def kernel(x, c, ctx, c_ctx, mod_w, mod_b, norm_mix, norm_ffn, w_qkv, w_o, na_rpb, diff_lambda, diff_subln,
           router_w, router_b, expert_w_gate, expert_w_up, expert_w_down, final_norm):
    mod = _modulation(c, c_ctx, mod_w, mod_b)
    stream = (x.reshape(T_LAT, D_MODEL), ctx.reshape(T_CTX, D_MODEL))
    w_qk_b = w_qkv[:, :, :2 * D_MODEL].astype(BF16)
    w_vt_b = jnp.swapaxes(w_qkv[:, :, 2 * D_MODEL:], 1, 2).astype(BF16)
    w_o_b = w_o.astype(BF16)
    na_tables = _na_pair_table(na_rpb.reshape(-1, N_DR, N_DC))
    cs, sn = _rope_tables()
    rw_pad = jnp.zeros((D_MODEL, HB), F32).at[:, :N_EXPERTS].set(router_w)
    rb_col = router_b.reshape(N_EXPERTS, 1).astype(F32)
    blk = jnp.arange(ROUTE_BLK)
    tri = (blk[:, None] < blk[None, :]).astype(BF16)

    for i in range(DEPTH):
        last = i == DEPTH - 1
        j = i // 2
        is_diff = i % 2 == 1
        xa, q, k, v = _qkv(stream, mod[i], norm_mix[i], w_qk_b, w_vt_b, i, cs, sn, rope=is_diff)
        if not is_diff:
            o, o_ctx = _na_attention(na_tables, j, q, k, v)
        else:
            lam_init = _diff_lambda_init(i + 1)
            sg = diff_subln[j].reshape(HB, 1)
            outs = _diff_attention(diff_lambda[j], sg, q, k, v, lam_init, ctx_out=not last)
            o, o_ctx = outs if not last else (outs[0], None)
        n_tok = T_LAT if last else T_ALL
        x1, h, route_i, route_f, cnt = _oproj_route(o, o_ctx, xa, mod[i], w_o_b, i, norm_ffn[i], rw_pad,
                                                    rb_col, tri, n_tok)
        pos, te, tv, n_tiles = _moe_layout(route_i, cnt[:, 0], n_tok)
        xs = _sc_dispatch(h, pos, n_tiles * TM)
        ys = _grouped_ffn(te, tv, xs, expert_w_gate, expert_w_up, expert_w_down, i)
        yg = _sc_combine_gather(ys, pos)
        stream = (x1, yg, route_f[0:2].T, mod[i])
    out = _combine_final(*stream, final_norm, T_LAT)
    return out.reshape(BATCH, SEQ, D_MODEL)
```

```python
import functools
import math

import jax
import jax.numpy as jnp
from jax import lax
from jax.experimental import pallas as pl
from jax.experimental.pallas import tpu as pltpu
from jax.experimental.pallas import tpu_sc as plsc

F32 = jnp.float32
BF16 = jnp.bfloat16

D_MODEL = 1024
BATCH = 8
SEQ = 2048
DEPTH = 4
CTX_LEN = 256
GRID_W = 64
ROWS = SEQ // GRID_W
NA_HEADS = 16
NA_WIN_ROWS = 8
NA_WIN_COLS = 16
DIFF_HEADS = 8
DIFF_HEAD_DIM = 64
ROPE_THETA = 10000.0
N_EXPERTS = 16
N_GROUPS = 4
EXPERTS_PER_GROUP = 4
D_EXPERT = 1024
N_MOD = 6
NORM_EPS = 1e-6

T_LAT = BATCH * SEQ
T_CTX = BATCH * CTX_LEN
T_ALL = T_LAT + T_CTX
TM = 512
ROUTE_BLK = TM // 2
LAT_TILES = T_LAT // TM
CTX_TILES = T_CTX // TM
TILES_PER_BATCH = SEQ // TM
CTX_ROW = BATCH
MOD_ROWS = 16
HB = 128
N_HB = D_MODEL // HB
NA_RB = 4
NA_BAND_ROWS = 12
NA_Q = NA_RB * GRID_W
NA_BAND = NA_BAND_ROWS * GRID_W
TQ_DIFF = 512
MASK_VALUE = -1e30
LOG2E = math.log2(math.e)
Q_SCALE = 0.125 * LOG2E
VMEM_LIMIT = 56 * 1024 * 1024


def _cparams(*sem):
    return pltpu.CompilerParams(dimension_semantics=sem, vmem_limit_bytes=VMEM_LIMIT)


def _dot(a, b):
    return jnp.dot(a, b, preferred_element_type=F32)


def _dot_nt(a, b):
    return lax.dot_general(a, b, (((1,), (1,)), ((), ())), preferred_element_type=F32)


def _sigmoid(x):
    return 1.0 / (1.0 + jnp.exp(-x))


D_PACK = D_MODEL // 2


def _pack_bf16_pair(x):
    lo = lax.bitcast_convert_type(x[:, :D_PACK].astype(BF16).astype(F32), jnp.uint32)
    hi = lax.bitcast_convert_type(x[:, D_PACK:].astype(BF16).astype(F32), jnp.uint32)
    return (lo >> 16) | hi


def _unpack_bf16_pair(w):
    lo = lax.bitcast_convert_type(w << 16, F32)
    hi = lax.bitcast_convert_type(w & jnp.uint32(0xFFFF0000), F32)
    return jnp.concatenate([lo, hi], axis=1)


def _mod_row(i):
    return jnp.minimum(i // TILES_PER_BATCH, CTX_ROW)


def _mod_kernel(act_ref, w_ref, b_ref, o_ref):
    a = act_ref[...]
    a = a * _sigmoid(a)
    o_ref[0] = _dot(a.astype(BF16), w_ref[0].astype(BF16)) + b_ref[0]


def _modulation(c, c_ctx, mod_w, mod_b):
    tn = 1536
    act = jnp.zeros((MOD_ROWS, D_MODEL), F32).at[:BATCH].set(c).at[CTX_ROW].set(c_ctx)
    out = pl.pallas_call(
        _mod_kernel,
        grid=(DEPTH, N_MOD * D_MODEL // tn),
        in_specs=[
            pl.BlockSpec((MOD_ROWS, D_MODEL), lambda l, j: (0, 0)),
            pl.BlockSpec((1, D_MODEL, tn), lambda l, j: (l, 0, j)),
            pl.BlockSpec((1, 1, tn), lambda l, j: (l, 0, j)),
        ],
        out_specs=pl.BlockSpec((1, MOD_ROWS, tn), lambda l, j: (l, 0, j)),
        out_shape=jax.ShapeDtypeStruct((DEPTH, MOD_ROWS, N_MOD * D_MODEL), F32),
        compiler_params=_cparams("parallel", "parallel"),
        name="adaln_mod",
    )(act, mod_w, mod_b.reshape(DEPTH, 1, N_MOD * D_MODEL))
    return out.reshape(DEPTH, MOD_ROWS, N_MOD, D_MODEL)


def _rms_modulate(x, g, shift, scale):
    ms = jnp.mean(x * x, axis=-1, keepdims=True)
    y = x * lax.rsqrt(ms + NORM_EPS) * g
    return y * (1.0 + scale) + shift


def _rope_block(xb, cs, sn):
    lane = lax.broadcasted_iota(jnp.int32, xb.shape, 1)
    partner = jnp.where((lane & 63) < 32, pltpu.roll(xb, 96, 1), pltpu.roll(xb, 32, 1))
    return xb * cs + partner * sn


def _moe_residual(x1, y0, y1, g, gate_ffn):
    y = g[:, 0:1] * _unpack_bf16_pair(y0) + g[:, 1:2] * _unpack_bf16_pair(y1)
    return x1 + gate_ffn * y


def _qkv_kernel(*refs, rope, first):
    n_stream = 2 if first else 4
    stream_refs, refs = refs[:n_stream], refs[n_stream:]
    mod_ref, g_ref, wqk_ref, wvt_ref, cs_ref, sn_ref, x_ref, q_ref, k_ref, vt_ref = refs
    m = mod_ref[0]
    for r0 in range(0, TM, ROUTE_BLK):
        rows = pl.ds(r0, ROUTE_BLK)
        if first:
            xl_ref, xc_ref = stream_refs
            x = jnp.where(pl.program_id(0) < LAT_TILES, xl_ref[rows, :], xc_ref[rows, :])
        else:
            x1_ref, yg_ref, gt_ref, modp_ref = stream_refs
            x = _moe_residual(x1_ref[rows, :], yg_ref[0, rows, :], yg_ref[1, rows, :], gt_ref[rows, :],
                              modp_ref[0][5:6])
        x_ref[rows, :] = x
        h = _rms_modulate(x, g_ref[...], m[0:1], m[1:2]).astype(BF16)
        for idx, out in enumerate((q_ref, k_ref)):
            acc = _dot(h, wqk_ref[0, :, idx * D_MODEL:(idx + 1) * D_MODEL])
            if rope:
                cs = cs_ref[rows, :]
                sn = sn_ref[rows, :]
                acc = jnp.concatenate(
                    [_rope_block(acc[:, j * HB:(j + 1) * HB], cs, sn) for j in range(N_HB)], axis=1)
            if idx == 0:
                acc = acc * Q_SCALE
            out[rows, :] = acc.astype(BF16)
        vt_ref[:, rows] = _dot_nt(wvt_ref[0], h).astype(BF16)


def _qkv(stream, mod_l, g, w_qk, w_vt, layer, cs, sn, *, rope):
    first = len(stream) == 2
    n_tiles = T_ALL // TM
    tok = lambda i: (i, 0)
    const2 = lambda i: (0, 0)
    mod_spec = pl.BlockSpec((1, N_MOD, D_MODEL), lambda i: (_mod_row(i), 0, 0))
    rope_idx = lambda i: (jnp.where(i < LAT_TILES, i % TILES_PER_BATCH, TILES_PER_BATCH), 0)
    out = jax.ShapeDtypeStruct((T_ALL, D_MODEL), BF16)
    wmap = lambda i: (layer, 0, 0)
    if first:
        stream_specs = [
            pl.BlockSpec((TM, D_MODEL), lambda i: (jnp.minimum(i, LAT_TILES - 1), 0)),
            pl.BlockSpec((TM, D_MODEL), lambda i: (jnp.maximum(i - LAT_TILES, 0), 0)),
        ]
    else:
        stream_specs = [
            pl.BlockSpec((TM, D_MODEL), tok),
            pl.BlockSpec((2, TM, D_PACK), lambda i: (0, i, 0)),
            pl.BlockSpec((TM, 2), tok),
            mod_spec,
        ]
    return pl.pallas_call(
        functools.partial(_qkv_kernel, rope=rope, first=first),
        grid=(n_tiles,),
        in_specs=stream_specs + [
            mod_spec,
            pl.BlockSpec((1, D_MODEL), const2),
            pl.BlockSpec((1, D_MODEL, 2 * D_MODEL), wmap),
            pl.BlockSpec((1, D_MODEL, D_MODEL), wmap),
            pl.BlockSpec((TM, HB), rope_idx),
            pl.BlockSpec((TM, HB), rope_idx),
        ],
        out_specs=[pl.BlockSpec((TM, D_MODEL), tok), pl.BlockSpec((TM, D_MODEL), tok),
                   pl.BlockSpec((TM, D_MODEL), tok), pl.BlockSpec((D_MODEL, TM), lambda i: (0, i))],
        out_shape=[jax.ShapeDtypeStruct((T_ALL, D_MODEL), F32), out, out,
                   jax.ShapeDtypeStruct((D_MODEL, T_ALL), BF16)],
        compiler_params=_cparams("parallel"),
        name="norm_qkv_rope" if rope else "norm_qkv",
    )(*stream, mod_l, g.reshape(1, D_MODEL), w_qk, w_vt, cs, sn)


def _rope_tables():
    t = jnp.arange(SEQ)
    row = (t // GRID_W).astype(F32)
    col = (t % GRID_W).astype(F32)
    n_freq = DIFF_HEAD_DIM // 4
    inv_freq = ROPE_THETA ** (-jnp.arange(n_freq, dtype=F32) / n_freq)
    ang = jnp.concatenate([row[:, None] * inv_freq, col[:, None] * inv_freq], axis=-1)
    cos, sin = jnp.cos(ang), jnp.sin(ang)
    cs = jnp.concatenate([cos, cos, cos, cos], axis=-1)
    sn = jnp.concatenate([-sin, sin, -sin, sin], axis=-1)
    cs = jnp.concatenate([cs, jnp.ones((TM, HB), F32)], axis=0)
    sn = jnp.concatenate([sn, jnp.zeros((TM, HB), F32)], axis=0)
    return cs, sn


Q_BLK = 256
KEY_BLK = 256
SUBLANES = 8
DIFF_Q_BLK = 256
DIFF_Q_PER_TRIP = 8
DIFF_KEY_BLK = 1024


def _colwise(reduce_fn, x):
    return reduce_fn(x.reshape(x.shape[0] // SUBLANES, SUBLANES, x.shape[1]), axis=0)


def _key_blocks(k_ref, vt_ref, key_start=0, n_keys=None, bias=None, blk=KEY_BLK):
    n_keys = k_ref.shape[0] if n_keys is None else n_keys
    blk = min(blk, n_keys)
    blocks = []
    for c, s0 in enumerate(range(0, n_keys, blk)):
        blocks.append((
            lambda s0=s0: k_ref[pl.ds(key_start + s0, blk), :],
            lambda s0=s0: vt_ref[:, pl.ds(key_start + s0, blk)],
            None if bias is None else functools.partial(bias, c),
            blk,
        ))
    return blocks


def _attention_t(operands, s_scr):
    def rows_of(u, c):
        sizes = [blk[3] for blk in operands[u][1]]
        return slice(sum(sizes[:c]), sum(sizes[:c + 1]))

    def score(u, c, mx):
        qm, blocks = operands[u]
        k_tile, _, bias, _ = blocks[c]
        st = _dot_nt(k_tile(), qm)
        if bias is not None:
            st = st + bias()
        s_scr[u % 2, rows_of(u, c), :n_q] = st
        return jnp.maximum(mx, _colwise(jnp.max, st))

    def expo(u, c, m, acc, den):
        vt_tile = operands[u][1][c][1]
        e = jnp.exp2(s_scr[u % 2, rows_of(u, c), :n_q] - m)
        o = _dot(vt_tile(), e.astype(BF16))
        return (o if acc is None else acc + o), den + _colwise(jnp.sum, e)

    n_q = operands[0][0].shape[0]
    neg = jnp.full((SUBLANES, n_q), -jnp.inf, F32)
    zero = jnp.zeros((SUBLANES, n_q), F32)
    n_blocks = [len(blocks) for _, blocks in operands]
    mx = neg
    for c in range(n_blocks[0]):
        mx = score(0, c, mx)
    outs = []
    for u in range(len(operands)):
        m = jnp.max(mx, axis=0, keepdims=True)
        acc, den, mx = None, zero, neg
        n_next = n_blocks[u + 1] if u + 1 < len(operands) else 0
        for c in range(max(n_blocks[u], n_next)):
            if c < n_next:
                mx = score(u + 1, c, mx)
            if c < n_blocks[u]:
                acc, den = expo(u, c, m, acc, den)
        outs.append(acc / jnp.sum(den, axis=0, keepdims=True))
    return outs


def _half_masks(shape):
    lane = lax.broadcasted_iota(jnp.int32, shape, 1)
    return lane < 64, lane >= 64


N_DR = 2 * NA_WIN_ROWS - 1
N_DC = 2 * NA_WIN_COLS - 1
PAIR_MASK_FIRST = N_DR - 1
PAIR_MASK_SECOND = N_DR
PAIR_MASKED = N_DR + 1
N_PAIRS = N_DR + 2
NA_KEY_BLK = 768
DR_LO = NA_WIN_ROWS - 1 - NA_WIN_ROWS // 2
DR_HI = DR_LO + NA_WIN_ROWS - 1


def _na_pair_table(rpb):
    h = rpb.shape[0]
    w = GRID_W
    col = jnp.arange(w)
    dc = jnp.clip(col[:, None] - col[None, :] + NA_WIN_COLS - 1, 0, N_DC - 1)
    onehot = (dc[:, :, None] == jnp.arange(N_DC)).astype(F32)
    t = jnp.einsum("hdc,kqc->hdkq", rpb.astype(F32), onehot, precision=lax.Precision.HIGHEST)
    cstart = jnp.clip(col - NA_WIN_COLS // 2, 0, w - NA_WIN_COLS)
    cvalid = (col[:, None] >= cstart[None, :]) & (col[:, None] < cstart[None, :] + NA_WIN_COLS)
    tt = jnp.where(cvalid, t * LOG2E, MASK_VALUE)
    masked = jnp.full((h, 1, w, w), MASK_VALUE, F32)
    regular = jnp.concatenate([tt[:, 1:], tt[:, :-1]], axis=-1)
    first = jnp.concatenate([masked, tt[:, DR_HI:DR_HI + 1]], axis=-1)
    second = jnp.concatenate([tt[:, DR_LO:DR_LO + 1], masked], axis=-1)
    return jnp.concatenate([regular, first, second, jnp.concatenate([masked, masked], axis=-1)], axis=1)


def _na_pair_entries(q_row0, band_row0, band_rows):
    entries = []
    for j in range(band_rows):
        kr = band_row0 + j
        row = []
        for a in range(0, NA_RB, 2):
            ok, d = [], []
            for r in (q_row0 + a, q_row0 + a + 1):
                r0 = min(max(r - NA_WIN_ROWS // 2, 0), ROWS - NA_WIN_ROWS)
                ok.append(r0 <= kr < r0 + NA_WIN_ROWS)
                d.append(kr - r + NA_WIN_ROWS - 1)
            if ok[0] and ok[1]:
                row.append(d[0] - 1)
            elif ok[1]:
                assert d[1] == DR_HI
                row.append(PAIR_MASK_FIRST)
            elif ok[0]:
                assert d[0] == DR_LO
                row.append(PAIR_MASK_SECOND)
            else:
                row.append(PAIR_MASKED)
        entries.append(row)
    return entries


def _na_kernel(tab_ref, q_ref, k_ref, vt_ref, qc_ref, kc_ref, vct_ref, o_ref, oc_ref, s_scr):
    n_blocks = ROWS // NA_RB
    ctx_blocks = _key_blocks(kc_ref, vct_ref)

    def operands(q_start, band_start, band_rows, entries):
        q = q_ref[pl.ds(q_start, NA_Q), :]
        n_band = band_rows * GRID_W
        blk = NA_KEY_BLK if n_band % NA_KEY_BLK == 0 else n_band
        rows_per_chunk = blk // GRID_W
        ops = []
        for hh, msk in enumerate(_half_masks(q.shape)):
            def bias(c, hh=hh):
                rows = entries[c * rows_per_chunk:(c + 1) * rows_per_chunk]
                return jnp.concatenate(
                    [jnp.concatenate([tab_ref[hh, e] for e in row], axis=1) for row in rows], axis=0)

            band = _key_blocks(k_ref, vt_ref, band_start, n_band, bias, blk)
            ops.append((jnp.where(msk, q, jnp.zeros_like(q)), band + ctx_blocks))
        return ops

    def merge_heads(o_lo, o_hi):
        feat = lax.broadcasted_iota(jnp.int32, o_lo.shape, 0)
        return jnp.where(feat < HB // 2, o_lo, o_hi).T.astype(BF16)

    last_band = ROWS - NA_WIN_ROWS
    interior = _na_pair_entries(NA_RB, 0, NA_BAND_ROWS)
    blocks = [(0, 0, NA_WIN_ROWS, _na_pair_entries(0, 0, NA_WIN_ROWS))]
    blocks += [(rb * NA_Q, (rb - 1) * NA_Q, NA_BAND_ROWS, interior) for rb in range(1, n_blocks - 1)]
    blocks.append(((n_blocks - 1) * NA_Q, last_band * GRID_W, NA_WIN_ROWS,
                   _na_pair_entries(ROWS - NA_RB, last_band, NA_WIN_ROWS)))
    qc = qc_ref[...]
    ctx_ops = [(jnp.where(msk, qc, jnp.zeros_like(qc)), ctx_blocks) for msk in _half_masks(qc.shape)]
    outs = _attention_t([op for blk in blocks for op in operands(*blk)] + ctx_ops, s_scr)
    for i, blk in enumerate(blocks):
        o_ref[pl.ds(blk[0], NA_Q), :] = merge_heads(outs[2 * i], outs[2 * i + 1])
    oc_ref[...] = merge_heads(outs[-2], outs[-1])


def _na_attention(tables, layer_na, q, k, vt):
    assert NA_Q == Q_BLK and CTX_LEN == Q_BLK
    lat = lambda hp, b: (b, hp)
    ctx = lambda hp, b: (T_LAT // CTX_LEN + b, hp)
    return pl.pallas_call(
        _na_kernel,
        grid=(N_HB, BATCH),
        in_specs=[
            pl.BlockSpec((2, N_PAIRS, GRID_W, 2 * GRID_W), lambda hp, b: (layer_na * N_HB + hp, 0, 0, 0)),
            pl.BlockSpec((SEQ, HB), lat),
            pl.BlockSpec((SEQ, HB), lat),
            pl.BlockSpec((HB, SEQ), lambda hp, b: (hp, b)),
            pl.BlockSpec((CTX_LEN, HB), ctx),
            pl.BlockSpec((CTX_LEN, HB), ctx),
            pl.BlockSpec((HB, CTX_LEN), lambda hp, b: (hp, T_LAT // CTX_LEN + b)),
        ],
        out_specs=[pl.BlockSpec((SEQ, HB), lat), pl.BlockSpec((CTX_LEN, HB), lat)],
        out_shape=[jax.ShapeDtypeStruct((T_LAT, D_MODEL), BF16), jax.ShapeDtypeStruct((T_CTX, D_MODEL), BF16)],
        scratch_shapes=[pltpu.VMEM((2, NA_BAND + CTX_LEN, Q_BLK), F32)],
        compiler_params=_cparams("parallel", "parallel"),
        name="na_attention",
    )(tables, q, k, vt, q, k, vt)


def _diff_lambda(lam_ref, lam_init):
    lam = lam_ref[...]
    a = jnp.sum(lam[0:1] * lam[1:2], axis=-1, keepdims=True)
    b = jnp.sum(lam[2:3] * lam[3:4], axis=-1, keepdims=True)
    return jnp.exp(a) - jnp.exp(b) + lam_init


def _diff_kernel(lam_ref, sg_ref, q_ref, k_ref, vt_ref, qc_ref, kc_ref, vct_ref, o_ref, *rest, lam_init):
    *maybe_oc_ref, s_scr = rest
    lam = _diff_lambda(lam_ref, lam_init)
    sg_col = sg_ref[...]

    def rows(qs, pieces):
        blocks = [blk for k_r, vt_r in pieces for blk in _key_blocks(k_r, vt_r, blk=DIFF_KEY_BLK)]
        outs = _attention_t([(jnp.where(msk, q, jnp.zeros_like(q)), blocks)
                             for q in qs for msk in _half_masks(q.shape)], s_scr)
        res = []
        for o1, o2 in zip(outs[0::2], outs[1::2]):
            ot = o1 - lam * o2
            ms = jnp.mean(ot * ot, axis=0, keepdims=True)
            ot = ot * lax.rsqrt(ms + NORM_EPS) * sg_col * (1.0 - lam_init)
            res.append(ot.T.astype(BF16))
        return res

    def body(t, carry):
        starts = [pl.multiple_of((t * DIFF_Q_PER_TRIP + j) * DIFF_Q_BLK, DIFF_Q_BLK)
                  for j in range(DIFF_Q_PER_TRIP)]
        outs = rows([q_ref[pl.ds(r0, DIFF_Q_BLK), :] for r0 in starts], [(k_ref, vt_ref), (kc_ref, vct_ref)])
        for r0, o in zip(starts, outs):
            o_ref[pl.ds(r0, DIFF_Q_BLK), :] = o
        return carry

    lax.fori_loop(0, SEQ // (DIFF_Q_BLK * DIFF_Q_PER_TRIP), body, 0)
    if maybe_oc_ref:
        oc_ref, = maybe_oc_ref
        oc_ref[...], = rows([qc_ref[...]], [(kc_ref, vct_ref)])


def _diff_attention(lam, sg, q, k, vt, lam_init, *, ctx_out):
    lat = lambda b, h: (b, h)
    ctx = lambda b, h: (T_LAT // CTX_LEN + b, h)
    lat_t = lambda b, h: (h, b)
    ctx_t = lambda b, h: (h, T_LAT // CTX_LEN + b)
    out_specs = [pl.BlockSpec((SEQ, HB), lat)]
    out_shape = [jax.ShapeDtypeStruct((T_LAT, D_MODEL), BF16)]
    if ctx_out:
        out_specs.append(pl.BlockSpec((CTX_LEN, HB), lat))
        out_shape.append(jax.ShapeDtypeStruct((T_CTX, D_MODEL), BF16))
    return pl.pallas_call(
        functools.partial(_diff_kernel, lam_init=lam_init),
        grid=(BATCH, DIFF_HEADS),
        in_specs=[
            pl.BlockSpec(lam.shape, lambda b, h: (0, 0)),
            pl.BlockSpec(sg.shape, lambda b, h: (0, 0)),
            pl.BlockSpec((SEQ, HB), lat),
            pl.BlockSpec((SEQ, HB), lat),
            pl.BlockSpec((HB, SEQ), lat_t),
            pl.BlockSpec((CTX_LEN, HB), ctx),
            pl.BlockSpec((CTX_LEN, HB), ctx),
            pl.BlockSpec((HB, CTX_LEN), ctx_t),
        ],
        out_specs=out_specs,
        out_shape=out_shape,
        scratch_shapes=[pltpu.VMEM((2, SEQ + CTX_LEN, DIFF_Q_BLK), F32)],
        compiler_params=_cparams("parallel", "parallel"),
        name="diff_attention",
    )(lam, sg, q, k, vt, q, k, vt)


def _route_rows(logits_t, bias_col):
    s = _sigmoid(logits_t)
    sel = s + bias_col
    sel_r = [sel[e:e + 1] for e in range(N_EXPERTS)]
    s_r = [s[e:e + 1] for e in range(N_EXPERTS)]
    group_scores = []
    for g in range(N_GROUPS):
        v = sel_r[g * EXPERTS_PER_GROUP:(g + 1) * EXPERTS_PER_GROUP]
        pairs = [v[a] + v[b] for a in range(EXPERTS_PER_GROUP) for b in range(a + 1, EXPERTS_PER_GROUP)]
        group_scores.append(functools.reduce(jnp.maximum, pairs))
    best = group_scores[0]
    grp = jnp.zeros(best.shape, jnp.int32)
    for g in range(1, N_GROUPS):
        upd = group_scores[g] > best
        best = jnp.where(upd, group_scores[g], best)
        grp = jnp.where(upd, g, grp)

    def pick(rows, j):
        out = rows[j]
        for g in range(1, N_GROUPS):
            out = jnp.where(grp == g, rows[g * EXPERTS_PER_GROUP + j], out)
        return out

    w = [pick(sel_r, j) for j in range(EXPERTS_PER_GROUP)]
    sc = [pick(s_r, j) for j in range(EXPERTS_PER_GROUP)]

    def argmax_first(vals):
        bv = vals[0]
        bi = jnp.zeros(bv.shape, jnp.int32)
        for j in range(1, len(vals)):
            upd = vals[j] > bv
            bv = jnp.where(upd, vals[j], bv)
            bi = jnp.where(upd, j, bi)
        return bi

    i1 = argmax_first(w)
    i2 = argmax_first([jnp.where(i1 == j, -jnp.inf, w[j]) for j in range(EXPERTS_PER_GROUP)])

    def take(vals, idx):
        out = vals[0]
        for j in range(1, len(vals)):
            out = jnp.where(idx == j, vals[j], out)
        return out

    s1 = take(sc, i1)
    s2 = take(sc, i2)
    den = s1 + s2
    return grp * EXPERTS_PER_GROUP + i1, grp * EXPERTS_PER_GROUP + i2, s1 / den, s2 / den


def _oproj_kernel(o_ref, oc_ref, x_ref, mod_ref, w_ref, g_ref, rw_ref, rb_ref, tri_ref,
                  x1_ref, h_ref, ri_ref, rf_ref, cnt_ref, carry_ref):
    i = pl.program_id(0)

    @pl.when(i == 0)
    def _():
        carry_ref[...] = jnp.zeros_like(carry_ref)

    m = mod_ref[0]
    rw = rw_ref[...].astype(BF16)
    ri_ref[...] = jnp.zeros_like(ri_ref)
    rf_ref[...] = jnp.zeros_like(rf_ref)
    carry = carry_ref[...]
    for r0 in range(0, TM, ROUTE_BLK):
        rows = pl.ds(r0, ROUTE_BLK)
        o = jnp.where(i < LAT_TILES, o_ref[rows, :], oc_ref[rows, :])
        x1 = x_ref[rows, :] + m[2:3] * _dot(o, w_ref[0])
        x1_ref[rows, :] = x1
        h = _rms_modulate(x1, g_ref[...], m[3:4], m[4:5])
        hb = h.astype(BF16)
        h_ref[rows, :] = _pack_bf16_pair(h)

        logits = _dot(hb, rw)
        logits_t = logits.T[:N_EXPERTS]
        e0, e1, g0, g1 = _route_rows(logits_t, rb_ref[...])

        eiota = lax.broadcasted_iota(jnp.int32, (N_EXPERTS, ROUTE_BLK), 0)
        oh0 = eiota == e0
        oh1 = eiota == e1
        oh = jnp.where(oh0 | oh1, 1.0, 0.0)
        before = _dot(oh.astype(BF16), tri_ref[...]) + carry
        rank0 = jnp.sum(jnp.where(oh0, before, 0.0), axis=0, keepdims=True)
        rank1 = jnp.sum(jnp.where(oh1, before, 0.0), axis=0, keepdims=True)
        carry = carry + jnp.sum(oh, axis=1, keepdims=True)

        ri_ref[0:1, rows] = e0
        ri_ref[1:2, rows] = e1
        ri_ref[2:3, rows] = rank0.astype(jnp.int32)
        ri_ref[3:4, rows] = rank1.astype(jnp.int32)
        rf_ref[0:1, rows] = g0
        rf_ref[1:2, rows] = g1
    carry_ref[...] = carry
    cnt_ref[...] = jnp.broadcast_to(carry, cnt_ref.shape)


def _oproj_route(o, o_ctx, x, mod_l, w_o, layer, g, rw_pad, rb_col, tri, n_tok):
    n_tiles = n_tok // TM
    tok = lambda i: (i, 0)
    const2 = lambda i: (0, 0)
    lane_tok = lambda i: (0, i)
    return pl.pallas_call(
        _oproj_kernel,
        grid=(n_tiles,),
        in_specs=[
            pl.BlockSpec((TM, D_MODEL), lambda i: (jnp.minimum(i, LAT_TILES - 1), 0)),
            pl.BlockSpec((TM, D_MODEL), lambda i: (jnp.maximum(i - LAT_TILES, 0), 0)),
            pl.BlockSpec((TM, D_MODEL), tok),
            pl.BlockSpec((1, N_MOD, D_MODEL), lambda i: (_mod_row(i), 0, 0)),
            pl.BlockSpec((1, D_MODEL, D_MODEL), lambda i: (layer, 0, 0)),
            pl.BlockSpec((1, D_MODEL), const2),
            pl.BlockSpec((D_MODEL, HB), const2),
            pl.BlockSpec((N_EXPERTS, 1), const2),
            pl.BlockSpec((ROUTE_BLK, ROUTE_BLK), const2),
        ],
        out_specs=[
            pl.BlockSpec((TM, D_MODEL), tok),
            pl.BlockSpec((TM, D_PACK), tok),
            pl.BlockSpec((8, TM), lane_tok),
            pl.BlockSpec((8, TM), lane_tok),
            pl.BlockSpec((N_EXPERTS, HB), const2),
        ],
        out_shape=[
            jax.ShapeDtypeStruct((n_tok, D_MODEL), F32),
            jax.ShapeDtypeStruct((n_tok, D_PACK), jnp.uint32),
            jax.ShapeDtypeStruct((8, n_tok), jnp.int32),
            jax.ShapeDtypeStruct((8, n_tok), F32),
            jax.ShapeDtypeStruct((N_EXPERTS, HB), F32),
        ],
        scratch_shapes=[pltpu.VMEM((N_EXPERTS, 1), F32)],
        compiler_params=_cparams("arbitrary"),
        name="oproj_norm_route",
    )(o, o if o_ctx is None else o_ctx, x, mod_l, w_o, g.reshape(1, D_MODEL), rw_pad, rb_col, tri)


def _moe_kernel(te_ref, tv_ref, xs_ref, wg_ref, wu_ref, wd_ref, ys_ref, wgb, wub, wdb):
    i = pl.program_id(0)
    valid = tv_ref[i]
    e = te_ref[i]
    prev = te_ref[jnp.maximum(i - 1, 0)]
    half = TM // 2

    @pl.when((i == 0) | (e != prev))
    def _():
        wgb[...] = wg_ref[0, 0].astype(BF16)
        wub[...] = wu_ref[0, 0].astype(BF16)
        wdb[...] = wd_ref[0, 0].astype(BF16)

    def ffn(rows, x):
        x = _unpack_bf16_pair(x).astype(BF16)
        a = _dot(x, wgb[...])
        u = _dot(x, wub[...])
        hm = (a * _sigmoid(a) * u).astype(BF16)
        ys_ref[rows, :] = _pack_bf16_pair(_dot(hm, wdb[...]))

    @pl.when(valid == TM)
    def _():
        for r0 in range(0, TM, half):
            ffn(pl.ds(r0, half), xs_ref[pl.ds(r0, half), :])

    for r0 in range(0, TM, half):
        rows = pl.ds(r0, half)

        @pl.when((valid < TM) & (valid >= r0 + half))
        def _():
            ffn(rows, xs_ref[rows, :])

        @pl.when((valid > r0) & (valid < r0 + half))
        def _():
            row = r0 + lax.broadcasted_iota(jnp.int32, (half, 1), 0)
            x = xs_ref[rows, :]
            ffn(rows, jnp.where(row < valid, x, jnp.zeros_like(x)))

        @pl.when(valid <= r0)
        def _():
            ys_ref[rows, :] = jnp.zeros((half, D_PACK), jnp.uint32)


def _grouped_ffn(tile_expert, tile_valid, xs, w_gate, w_up, w_down, layer):
    n_tiles = xs.shape[0] // TM
    wmap = lambda i, te, tv: (layer, te[i], 0, 0)
    tok = lambda i, te, tv: (i, 0)
    return pl.pallas_call(
        _moe_kernel,
        grid_spec=pltpu.PrefetchScalarGridSpec(
            num_scalar_prefetch=2,
            grid=(n_tiles,),
            in_specs=[
                pl.BlockSpec((TM, D_PACK), tok),
                pl.BlockSpec((1, 1, D_MODEL, D_EXPERT), wmap),
                pl.BlockSpec((1, 1, D_MODEL, D_EXPERT), wmap),
                pl.BlockSpec((1, 1, D_EXPERT, D_MODEL), wmap),
            ],
            out_specs=pl.BlockSpec((TM, D_PACK), tok),
            scratch_shapes=[
                pltpu.VMEM((D_MODEL, D_EXPERT), BF16),
                pltpu.VMEM((D_MODEL, D_EXPERT), BF16),
                pltpu.VMEM((D_EXPERT, D_MODEL), BF16),
            ],
        ),
        out_shape=jax.ShapeDtypeStruct((xs.shape[0], D_PACK), jnp.uint32),
        compiler_params=_cparams("arbitrary"),
        name="grouped_ffn",
    )(tile_expert, tile_valid, xs, w_gate, w_up, w_down)


def _moe_layout(route_i, counts, n_tok):
    n_tiles = 2 * n_tok // TM + N_EXPERTS
    counts = counts.astype(jnp.int32)
    tiles_e = (counts + TM - 1) // TM
    tiles_end = jnp.cumsum(tiles_e)
    tile_start = tiles_end - tiles_e
    eid = jnp.arange(N_EXPERTS, dtype=jnp.int32)
    tok_oh = route_i[0:2][:, :, None] == eid
    pos = jnp.sum(jnp.where(tok_oh, tile_start * TM, 0), axis=-1) + route_i[2:4]
    tile_ids = jnp.arange(n_tiles, dtype=jnp.int32)
    n_used = tiles_end[-1]
    te = jnp.sum(tiles_end[None, :] <= jnp.minimum(tile_ids, n_used - 1)[:, None], axis=1).astype(jnp.int32)
    tile_oh = te[:, None] == eid
    cnt_t = jnp.sum(jnp.where(tile_oh, counts, 0), axis=1)
    start_t = jnp.sum(jnp.where(tile_oh, tile_start, 0), axis=1)
    tv = jnp.clip(cnt_t - (tile_ids - start_t) * TM, 0, TM)
    tv = jnp.where(tile_ids < n_used, tv, 0).astype(jnp.int32)
    return pos.astype(jnp.int32), te, tv, n_tiles


SC_CORES = 2
SC_SUBCORES = 16
SC_WORKERS = SC_CORES * SC_SUBCORES
SC_CHUNK = 32


def _sc_mesh():
    return plsc.VectorSubcoreMesh(core_axis_name="c", subcore_axis_name="s")


def _sc_worker_indices(pos, n_tok):
    n_ch = n_tok // SC_WORKERS // SC_CHUNK
    return pos.reshape(2, SC_WORKERS, n_ch, SC_CHUNK).transpose(1, 2, 0, 3), n_ch


def _sc_dispatch(h, pos, n_slots):
    n_tok = h.shape[0]
    per_w = n_tok // SC_WORKERS
    pos_w, n_ch = _sc_worker_indices(pos, n_tok)
    assert n_ch % 2 == 0 and n_ch * SC_CHUNK * SC_WORKERS == n_tok

    @functools.partial(
        pl.kernel, mesh=_sc_mesh(), out_type=jax.ShapeDtypeStruct((n_slots, h.shape[1]), h.dtype),
        scratch_types=[pltpu.VMEM((n_ch, 2, SC_CHUNK), jnp.int32), pltpu.VMEM((2, SC_CHUNK, h.shape[1]), h.dtype),
                       pltpu.SemaphoreType.DMA((2,)), pltpu.SemaphoreType.DMA((2,))],
        name="sc_dispatch")
    def dispatch(h_hbm, pos_hbm, xs_hbm, idx_v, rows_v, load_sem, scat_sem):
        wid = lax.axis_index("s") * SC_CORES + lax.axis_index("c")
        base = wid * per_w
        pltpu.sync_copy(pos_hbm.at[wid], idx_v)

        def load(c, b):
            return pltpu.make_async_copy(h_hbm.at[pl.ds(base + c * SC_CHUNK, SC_CHUNK)], rows_v.at[b],
                                         load_sem.at[b])

        def scat(c, b, k):
            return pltpu.make_async_copy(rows_v.at[b], xs_hbm.at[idx_v.at[c, k]], scat_sem.at[b])

        load(0, 0).start()

        @pl.loop(0, n_ch, step=2)
        def _(c0):
            for b in range(2):
                c = c0 + b
                load(c, b).wait()
                scat(c, b, 0).start()
                scat(c, b, 1).start()

                @pl.when(c >= 1)
                def _():
                    scat(c - 1, 1 - b, 0).wait()
                    scat(c - 1, 1 - b, 1).wait()

                @pl.when(c + 1 < n_ch)
                def _():
                    load(c + 1, 1 - b).start()

        scat(n_ch - 1, 1, 0).wait()
        scat(n_ch - 1, 1, 1).wait()

    return dispatch(h, pos_w)


def _sc_combine_gather(ys, pos):
    n_tok = pos.shape[1]
    per_w = n_tok // SC_WORKERS
    pos_w, n_ch = _sc_worker_indices(pos, n_tok)
    assert n_ch * SC_CHUNK * SC_WORKERS == n_tok

    @functools.partial(
        pl.kernel, mesh=_sc_mesh(), out_type=jax.ShapeDtypeStruct((2, n_tok, ys.shape[1]), ys.dtype),
        scratch_types=[pltpu.VMEM((n_ch, 2, SC_CHUNK), jnp.int32), pltpu.VMEM((2, SC_CHUNK, ys.shape[1]), ys.dtype),
                       pltpu.SemaphoreType.DMA((2,)), pltpu.SemaphoreType.DMA((2,))],
        name="sc_combine_gather")
    def gather(ys_hbm, pos_hbm, yg_hbm, idx_v, rows_v, gath_sem, write_sem):
        wid = lax.axis_index("s") * SC_CORES + lax.axis_index("c")
        base = wid * per_w
        pltpu.sync_copy(pos_hbm.at[wid], idx_v)

        def gath(c, k):
            return pltpu.make_async_copy(ys_hbm.at[idx_v.at[c, k]], rows_v.at[k], gath_sem.at[k])

        def write(c, k):
            return pltpu.make_async_copy(rows_v.at[k], yg_hbm.at[k, pl.ds(base + c * SC_CHUNK, SC_CHUNK)],
                                         write_sem.at[k])

        gath(0, 0).start()

        @pl.loop(0, n_ch)
        def _(c):
            gath(c, 0).wait()
            write(c, 0).start()

            @pl.when(c >= 1)
            def _():
                write(c - 1, 1).wait()

            gath(c, 1).start()
            gath(c, 1).wait()
            write(c, 1).start()
            write(c, 0).wait()

            @pl.when(c + 1 < n_ch)
            def _():
                gath(c + 1, 0).start()

        write(n_ch - 1, 1).wait()

    return gather(ys, pos_w)


def _final_kernel(x1_ref, yg_ref, gt_ref, mod_ref, fg_ref, out_ref):
    x2 = _moe_residual(x1_ref[...], yg_ref[0], yg_ref[1], gt_ref[...], mod_ref[0][5:6])
    ms = jnp.mean(x2 * x2, axis=-1, keepdims=True)
    out_ref[...] = x2 * lax.rsqrt(ms + NORM_EPS) * fg_ref[...]


def _combine_final(x1, yg, gates_t, mod_l, final_g, n_tok):
    tok = lambda i: (i, 0)
    return pl.pallas_call(
        _final_kernel,
        grid=(n_tok // TM,),
        in_specs=[
            pl.BlockSpec((TM, D_MODEL), tok),
            pl.BlockSpec((2, TM, D_PACK), lambda i: (0, i, 0)),
            pl.BlockSpec((TM, 2), tok),
            pl.BlockSpec((1, N_MOD, D_MODEL), lambda i: (_mod_row(i), 0, 0)),
            pl.BlockSpec((1, D_MODEL), lambda i: (0, 0)),
        ],
        out_specs=pl.BlockSpec((TM, D_MODEL), tok),
        out_shape=jax.ShapeDtypeStruct((n_tok, D_MODEL), F32),
        compiler_params=_cparams("parallel"),
        name="moe_combine_final",
    )(x1, yg, gates_t, mod_l, final_g.reshape(1, D_MODEL))


def _diff_lambda_init(layer_idx):
    return 0.8 - 0.6 * math.exp(-0.3 * (layer_idx - 1))


def kernel(x, c, ctx, c_ctx, mod_w, mod_b, norm_mix, norm_ffn, w_qkv, w_o, na_rpb, diff_lambda, diff_subln,
           router_w, router_b, expert_w_gate, expert_w_up, expert_w_down, final_norm):
    mod = _modulation(c, c_ctx, mod_w, mod_b)
    stream = (x.reshape(T_LAT, D_MODEL), ctx.reshape(T_CTX, D_MODEL))
    w_qk_b = w_qkv[:, :, :2 * D_MODEL].astype(BF16)
    w_vt_b = jnp.swapaxes(w_qkv[:, :, 2 * D_MODEL:], 1, 2).astype(BF16)
    w_o_b = w_o.astype(BF16)
    na_tables = _na_pair_table(na_rpb.reshape(-1, N_DR, N_DC))
    cs, sn = _rope_tables()
    rw_pad = jnp.zeros((D_MODEL, HB), F32).at[:, :N_EXPERTS].set(router_w)
    rb_col = router_b.reshape(N_EXPERTS, 1).astype(F32)
    blk = jnp.arange(ROUTE_BLK)
    tri = (blk[:, None] < blk[None, :]).astype(BF16)

    for i in range(DEPTH):
        last = i == DEPTH - 1
        j = i // 2
        is_diff = i % 2 == 1
        xa, q, k, v = _qkv(stream, mod[i], norm_mix[i], w_qk_b, w_vt_b, i, cs, sn, rope=is_diff)
        if not is_diff:
            o, o_ctx = _na_attention(na_tables, j, q, k, v)
        else:
            lam_init = _diff_lambda_init(i + 1)
            sg = diff_subln[j].reshape(HB, 1)
            outs = _diff_attention(diff_lambda[j], sg, q, k, v, lam_init, ctx_out=not last)
            o, o_ctx = outs if not last else (outs[0], None)
        n_tok = T_LAT if last else T_ALL
        x1, h, route_i, route_f, cnt = _oproj_route(o, o_ctx, xa, mod[i], w_o_b, i, norm_ffn[i], rw_pad,
                                                    rb_col, tri, n_tok)
        pos, te, tv, n_tiles = _moe_layout(route_i, cnt[:, 0], n_tok)
        xs = _sc_dispatch(h, pos, n_tiles * TM)
        ys = _grouped_ffn(te, tv, xs, expert_w_gate, expert_w_up, expert_w_down, i)
        yg = _sc_combine_gather(ys, pos)
        stream = (x1, yg, route_f[0:2].T, mod[i])
    out = _combine_final(*stream, final_norm, T_LAT)
    return out.reshape(BATCH, SEQ, D_MODEL)
```

```python
import functools
import math

import jax
import jax.numpy as jnp
from jax import lax
from jax.experimental import pallas as pl
from jax.experimental.pallas import tpu as pltpu
from jax.experimental.pallas import tpu_sc as plsc

F32 = jnp.float32
BF16 = jnp.bfloat16

D_MODEL = 1024
BATCH = 8
SEQ = 2048
DEPTH = 4
CTX_LEN = 256
GRID_W = 64
ROWS = SEQ // GRID_W
NA_HEADS = 16
NA_WIN_ROWS = 8
NA_WIN_COLS = 16
DIFF_HEADS = 8
DIFF_HEAD_DIM = 64
ROPE_THETA = 10000.0
N_EXPERTS = 16
N_GROUPS = 4
EXPERTS_PER_GROUP = 4
D_EXPERT = 1024
N_MOD = 6
NORM_EPS = 1e-6

T_LAT = BATCH * SEQ
T_CTX = BATCH * CTX_LEN
T_ALL = T_LAT + T_CTX
TM = 512
ROUTE_BLK = TM // 2
LAT_TILES = T_LAT // TM
CTX_TILES = T_CTX // TM
TILES_PER_BATCH = SEQ // TM
CTX_ROW = BATCH
MOD_ROWS = 16
HB = 128
N_HB = D_MODEL // HB
NA_RB = 4
NA_BAND_ROWS = 12
NA_Q = NA_RB * GRID_W
NA_BAND = NA_BAND_ROWS * GRID_W
TQ_DIFF = 512
MASK_VALUE = -1e30
LOG2E = math.log2(math.e)
Q_SCALE = 0.125 * LOG2E
VMEM_LIMIT = 56 * 1024 * 1024


def _cparams(*sem):
    return pltpu.CompilerParams(dimension_semantics=sem, vmem_limit_bytes=VMEM_LIMIT)


def _dot(a, b):
    return jnp.dot(a, b, preferred_element_type=F32)


def _dot_nt(a, b):
    return lax.dot_general(a, b, (((1,), (1,)), ((), ())), preferred_element_type=F32)


def _sigmoid(x):
    return 1.0 / (1.0 + jnp.exp(-x))


D_PACK = D_MODEL // 2


def _pack_bf16_pair(x):
    lo = lax.bitcast_convert_type(x[:, :D_PACK].astype(BF16).astype(F32), jnp.uint32)
    hi = lax.bitcast_convert_type(x[:, D_PACK:].astype(BF16).astype(F32), jnp.uint32)
    return (lo >> 16) | hi


def _unpack_bf16_pair(w):
    lo = lax.bitcast_convert_type(w << 16, F32)
    hi = lax.bitcast_convert_type(w & jnp.uint32(0xFFFF0000), F32)
    return jnp.concatenate([lo, hi], axis=1)


def _mod_row(i):
    return jnp.minimum(i // TILES_PER_BATCH, CTX_ROW)


def _mod_kernel(act_ref, w_ref, b_ref, o_ref):
    a = act_ref[...]
    a = a * _sigmoid(a)
    o_ref[0] = _dot(a.astype(BF16), w_ref[0].astype(BF16)) + b_ref[0]


def _modulation(c, c_ctx, mod_w, mod_b):
    tn = 1536
    act = jnp.zeros((MOD_ROWS, D_MODEL), F32).at[:BATCH].set(c).at[CTX_ROW].set(c_ctx)
    out = pl.pallas_call(
        _mod_kernel,
        grid=(DEPTH, N_MOD * D_MODEL // tn),
        in_specs=[
            pl.BlockSpec((MOD_ROWS, D_MODEL), lambda l, j: (0, 0)),
            pl.BlockSpec((1, D_MODEL, tn), lambda l, j: (l, 0, j)),
            pl.BlockSpec((1, 1, tn), lambda l, j: (l, 0, j)),
        ],
        out_specs=pl.BlockSpec((1, MOD_ROWS, tn), lambda l, j: (l, 0, j)),
        out_shape=jax.ShapeDtypeStruct((DEPTH, MOD_ROWS, N_MOD * D_MODEL), F32),
        compiler_params=_cparams("parallel", "parallel"),
        name="adaln_mod",
    )(act, mod_w, mod_b.reshape(DEPTH, 1, N_MOD * D_MODEL))
    return out.reshape(DEPTH, MOD_ROWS, N_MOD, D_MODEL)


def _rms_modulate(x, g, shift, scale):
    ms = jnp.mean(x * x, axis=-1, keepdims=True)
    y = x * lax.rsqrt(ms + NORM_EPS) * g
    return y * (1.0 + scale) + shift


def _rope_block(xb, cs, sn):
    lane = lax.broadcasted_iota(jnp.int32, xb.shape, 1)
    partner = jnp.where((lane & 63) < 32, pltpu.roll(xb, 96, 1), pltpu.roll(xb, 32, 1))
    return xb * cs + partner * sn


def _moe_residual(x1, y0, y1, g, gate_ffn):
    y = g[:, 0:1] * _unpack_bf16_pair(y0) + g[:, 1:2] * _unpack_bf16_pair(y1)
    return x1 + gate_ffn * y


def _qkv_kernel(*refs, rope, first):
    n_stream = 2 if first else 4
    stream_refs, refs = refs[:n_stream], refs[n_stream:]
    mod_ref, g_ref, wqk_ref, wvt_ref, cs_ref, sn_ref, x_ref, q_ref, k_ref, vt_ref = refs
    m = mod_ref[0]
    for r0 in range(0, TM, ROUTE_BLK):
        rows = pl.ds(r0, ROUTE_BLK)
        if first:
            xl_ref, xc_ref = stream_refs
            x = jnp.where(pl.program_id(0) < LAT_TILES, xl_ref[rows, :], xc_ref[rows, :])
        else:
            x1_ref, yg_ref, gt_ref, modp_ref = stream_refs
            x = _moe_residual(x1_ref[rows, :], yg_ref[0, rows, :], yg_ref[1, rows, :], gt_ref[rows, :],
                              modp_ref[0][5:6])
        x_ref[rows, :] = x
        h = _rms_modulate(x, g_ref[...], m[0:1], m[1:2]).astype(BF16)
        for idx, out in enumerate((q_ref, k_ref)):
            acc = _dot(h, wqk_ref[0, :, idx * D_MODEL:(idx + 1) * D_MODEL])
            if rope:
                cs = cs_ref[rows, :]
                sn = sn_ref[rows, :]
                acc = jnp.concatenate(
                    [_rope_block(acc[:, j * HB:(j + 1) * HB], cs, sn) for j in range(N_HB)], axis=1)
            if idx == 0:
                acc = acc * Q_SCALE
            out[rows, :] = acc.astype(BF16)
        vt_ref[:, rows] = _dot_nt(wvt_ref[0], h).astype(BF16)


def _qkv(stream, mod_l, g, w_qk, w_vt, layer, cs, sn, *, rope):
    first = len(stream) == 2
    n_tiles = T_ALL // TM
    tok = lambda i: (i, 0)
    const2 = lambda i: (0, 0)
    mod_spec = pl.BlockSpec((1, N_MOD, D_MODEL), lambda i: (_mod_row(i), 0, 0))
    rope_idx = lambda i: (jnp.where(i < LAT_TILES, i % TILES_PER_BATCH, TILES_PER_BATCH), 0)
    out = jax.ShapeDtypeStruct((T_ALL, D_MODEL), BF16)
    wmap = lambda i: (layer, 0, 0)
    if first:
        stream_specs = [
            pl.BlockSpec((TM, D_MODEL), lambda i: (jnp.minimum(i, LAT_TILES - 1), 0)),
            pl.BlockSpec((TM, D_MODEL), lambda i: (jnp.maximum(i - LAT_TILES, 0), 0)),
        ]
    else:
        stream_specs = [
            pl.BlockSpec((TM, D_MODEL), tok),
            pl.BlockSpec((2, TM, D_PACK), lambda i: (0, i, 0)),
            pl.BlockSpec((TM, 2), tok),
            mod_spec,
        ]
    return pl.pallas_call(
        functools.partial(_qkv_kernel, rope=rope, first=first),
        grid=(n_tiles,),
        in_specs=stream_specs + [
            mod_spec,
            pl.BlockSpec((1, D_MODEL), const2),
            pl.BlockSpec((1, D_MODEL, 2 * D_MODEL), wmap),
            pl.BlockSpec((1, D_MODEL, D_MODEL), wmap),
            pl.BlockSpec((TM, HB), rope_idx),
            pl.BlockSpec((TM, HB), rope_idx),
        ],
        out_specs=[pl.BlockSpec((TM, D_MODEL), tok), pl.BlockSpec((TM, D_MODEL), tok),
                   pl.BlockSpec((TM, D_MODEL), tok), pl.BlockSpec((D_MODEL, TM), lambda i: (0, i))],
        out_shape=[jax.ShapeDtypeStruct((T_ALL, D_MODEL), F32), out, out,
                   jax.ShapeDtypeStruct((D_MODEL, T_ALL), BF16)],
        compiler_params=_cparams("parallel"),
        name="norm_qkv_rope" if rope else "norm_qkv",
    )(*stream, mod_l, g.reshape(1, D_MODEL), w_qk, w_vt, cs, sn)


def _rope_tables():
    t = jnp.arange(SEQ)
    row = (t // GRID_W).astype(F32)
    col = (t % GRID_W).astype(F32)
    n_freq = DIFF_HEAD_DIM // 4
    inv_freq = ROPE_THETA ** (-jnp.arange(n_freq, dtype=F32) / n_freq)
    ang = jnp.concatenate([row[:, None] * inv_freq, col[:, None] * inv_freq], axis=-1)
    cos, sin = jnp.cos(ang), jnp.sin(ang)
    cs = jnp.concatenate([cos, cos, cos, cos], axis=-1)
    sn = jnp.concatenate([-sin, sin, -sin, sin], axis=-1)
    cs = jnp.concatenate([cs, jnp.ones((TM, HB), F32)], axis=0)
    sn = jnp.concatenate([sn, jnp.zeros((TM, HB), F32)], axis=0)
    return cs, sn


Q_BLK = 256
KEY_BLK = 256
SUBLANES = 8
DIFF_Q_BLK = 256
DIFF_Q_PER_TRIP = 8
DIFF_KEY_BLK = 1024


def _colwise(reduce_fn, x):
    return reduce_fn(x.reshape(x.shape[0] // SUBLANES, SUBLANES, x.shape[1]), axis=0)


def _key_blocks(k_ref, vt_ref, key_start=0, n_keys=None, bias=None, blk=KEY_BLK):
    n_keys = k_ref.shape[0] if n_keys is None else n_keys
    blk = min(blk, n_keys)
    blocks = []
    for c, s0 in enumerate(range(0, n_keys, blk)):
        blocks.append((
            lambda s0=s0: k_ref[pl.ds(key_start + s0, blk), :],
            lambda s0=s0: vt_ref[:, pl.ds(key_start + s0, blk)],
            None if bias is None else functools.partial(bias, c),
            blk,
        ))
    return blocks


def _attention_t(operands, s_scr):
    def rows_of(u, c):
        sizes = [blk[3] for blk in operands[u][1]]
        return slice(sum(sizes[:c]), sum(sizes[:c + 1]))

    def score(u, c, mx):
        qm, blocks = operands[u]
        k_tile, _, bias, _ = blocks[c]
        st = _dot_nt(k_tile(), qm)
        if bias is not None:
            st = st + bias()
        s_scr[u % 2, rows_of(u, c), :n_q] = st
        return jnp.maximum(mx, _colwise(jnp.max, st))

    def expo(u, c, m, acc, den):
        vt_tile = operands[u][1][c][1]
        e = jnp.exp2(s_scr[u % 2, rows_of(u, c), :n_q] - m)
        o = _dot(vt_tile(), e.astype(BF16))
        return (o if acc is None else acc + o), den + _colwise(jnp.sum, e)

    n_q = operands[0][0].shape[0]
    neg = jnp.full((SUBLANES, n_q), -jnp.inf, F32)
    zero = jnp.zeros((SUBLANES, n_q), F32)
    n_blocks = [len(blocks) for _, blocks in operands]
    mx = neg
    for c in range(n_blocks[0]):
        mx = score(0, c, mx)
    outs = []
    for u in range(len(operands)):
        m = jnp.max(mx, axis=0, keepdims=True)
        acc, den, mx = None, zero, neg
        n_next = n_blocks[u + 1] if u + 1 < len(operands) else 0
        for c in range(max(n_blocks[u], n_next)):
            if c < n_next:
                mx = score(u + 1, c, mx)
            if c < n_blocks[u]:
                acc, den = expo(u, c, m, acc, den)
        outs.append(acc / jnp.sum(den, axis=0, keepdims=True))
    return outs


def _half_masks(shape):
    lane = lax.broadcasted_iota(jnp.int32, shape, 1)
    return lane < 64, lane >= 64


N_DR = 2 * NA_WIN_ROWS - 1
N_DC = 2 * NA_WIN_COLS - 1
PAIR_MASK_FIRST = N_DR - 1
PAIR_MASK_SECOND = N_DR
PAIR_MASKED = N_DR + 1
N_PAIRS = N_DR + 2
NA_KEY_BLK = 768
DR_LO = NA_WIN_ROWS - 1 - NA_WIN_ROWS // 2
DR_HI = DR_LO + NA_WIN_ROWS - 1


def _na_pair_table(rpb):
    h = rpb.shape[0]
    w = GRID_W
    col = jnp.arange(w)
    dc = jnp.clip(col[:, None] - col[None, :] + NA_WIN_COLS - 1, 0, N_DC - 1)
    onehot = (dc[:, :, None] == jnp.arange(N_DC)).astype(F32)
    t = jnp.einsum("hdc,kqc->hdkq", rpb.astype(F32), onehot, precision=lax.Precision.HIGHEST)
    cstart = jnp.clip(col - NA_WIN_COLS // 2, 0, w - NA_WIN_COLS)
    cvalid = (col[:, None] >= cstart[None, :]) & (col[:, None] < cstart[None, :] + NA_WIN_COLS)
    tt = jnp.where(cvalid, t * LOG2E, MASK_VALUE)
    masked = jnp.full((h, 1, w, w), MASK_VALUE, F32)
    regular = jnp.concatenate([tt[:, 1:], tt[:, :-1]], axis=-1)
    first = jnp.concatenate([masked, tt[:, DR_HI:DR_HI + 1]], axis=-1)
    second = jnp.concatenate([tt[:, DR_LO:DR_LO + 1], masked], axis=-1)
    return jnp.concatenate([regular, first, second, jnp.concatenate([masked, masked], axis=-1)], axis=1)


def _na_pair_entries(q_row0, band_row0, band_rows):
    entries = []
    for j in range(band_rows):
        kr = band_row0 + j
        row = []
        for a in range(0, NA_RB, 2):
            ok, d = [], []
            for r in (q_row0 + a, q_row0 + a + 1):
                r0 = min(max(r - NA_WIN_ROWS // 2, 0), ROWS - NA_WIN_ROWS)
                ok.append(r0 <= kr < r0 + NA_WIN_ROWS)
                d.append(kr - r + NA_WIN_ROWS - 1)
            if ok[0] and ok[1]:
                row.append(d[0] - 1)
            elif ok[1]:
                assert d[1] == DR_HI
                row.append(PAIR_MASK_FIRST)
            elif ok[0]:
                assert d[0] == DR_LO
                row.append(PAIR_MASK_SECOND)
            else:
                row.append(PAIR_MASKED)
        entries.append(row)
    return entries


def _na_kernel(tab_ref, q_ref, k_ref, vt_ref, qc_ref, kc_ref, vct_ref, o_ref, oc_ref, s_scr):
    n_blocks = ROWS // NA_RB
    ctx_blocks = _key_blocks(kc_ref, vct_ref)

    def operands(q_start, band_start, band_rows, entries):
        q = q_ref[pl.ds(q_start, NA_Q), :]
        n_band = band_rows * GRID_W
        blk = NA_KEY_BLK if n_band % NA_KEY_BLK == 0 else n_band
        rows_per_chunk = blk // GRID_W
        ops = []
        for hh, msk in enumerate(_half_masks(q.shape)):
            def bias(c, hh=hh):
                rows = entries[c * rows_per_chunk:(c + 1) * rows_per_chunk]
                return jnp.concatenate(
                    [jnp.concatenate([tab_ref[hh, e] for e in row], axis=1) for row in rows], axis=0)

            band = _key_blocks(k_ref, vt_ref, band_start, n_band, bias, blk)
            ops.append((jnp.where(msk, q, jnp.zeros_like(q)), band + ctx_blocks))
        return ops

    def merge_heads(o_lo, o_hi):
        feat = lax.broadcasted_iota(jnp.int32, o_lo.shape, 0)
        return jnp.where(feat < HB // 2, o_lo, o_hi).T.astype(BF16)

    last_band = ROWS - NA_WIN_ROWS
    interior = _na_pair_entries(NA_RB, 0, NA_BAND_ROWS)
    blocks = [(0, 0, NA_WIN_ROWS, _na_pair_entries(0, 0, NA_WIN_ROWS))]
    blocks += [(rb * NA_Q, (rb - 1) * NA_Q, NA_BAND_ROWS, interior) for rb in range(1, n_blocks - 1)]
    blocks.append(((n_blocks - 1) * NA_Q, last_band * GRID_W, NA_WIN_ROWS,
                   _na_pair_entries(ROWS - NA_RB, last_band, NA_WIN_ROWS)))
    qc = qc_ref[...]
    ctx_ops = [(jnp.where(msk, qc, jnp.zeros_like(qc)), ctx_blocks) for msk in _half_masks(qc.shape)]
    outs = _attention_t([op for blk in blocks for op in operands(*blk)] + ctx_ops, s_scr)
    for i, blk in enumerate(blocks):
        o_ref[pl.ds(blk[0], NA_Q), :] = merge_heads(outs[2 * i], outs[2 * i + 1])
    oc_ref[...] = merge_heads(outs[-2], outs[-1])


def _na_attention(tables, layer_na, q, k, vt):
    assert NA_Q == Q_BLK and CTX_LEN == Q_BLK
    lat = lambda hp, b: (b, hp)
    ctx = lambda hp, b: (T_LAT // CTX_LEN + b, hp)
    return pl.pallas_call(
        _na_kernel,
        grid=(N_HB, BATCH),
        in_specs=[
            pl.BlockSpec((2, N_PAIRS, GRID_W, 2 * GRID_W), lambda hp, b: (layer_na * N_HB + hp, 0, 0, 0)),
            pl.BlockSpec((SEQ, HB), lat),
            pl.BlockSpec((SEQ, HB), lat),
            pl.BlockSpec((HB, SEQ), lambda hp, b: (hp, b)),
            pl.BlockSpec((CTX_LEN, HB), ctx),
            pl.BlockSpec((CTX_LEN, HB), ctx),
            pl.BlockSpec((HB, CTX_LEN), lambda hp, b: (hp, T_LAT // CTX_LEN + b)),
        ],
        out_specs=[pl.BlockSpec((SEQ, HB), lat), pl.BlockSpec((CTX_LEN, HB), lat)],
        out_shape=[jax.ShapeDtypeStruct((T_LAT, D_MODEL), BF16), jax.ShapeDtypeStruct((T_CTX, D_MODEL), BF16)],
        scratch_shapes=[pltpu.VMEM((2, NA_BAND + CTX_LEN, Q_BLK), F32)],
        compiler_params=_cparams("parallel", "parallel"),
        name="na_attention",
    )(tables, q, k, vt, q, k, vt)


def _diff_lambda(lam_ref, lam_init):
    lam = lam_ref[...]
    a = jnp.sum(lam[0:1] * lam[1:2], axis=-1, keepdims=True)
    b = jnp.sum(lam[2:3] * lam[3:4], axis=-1, keepdims=True)
    return jnp.exp(a) - jnp.exp(b) + lam_init


def _diff_kernel(lam_ref, sg_ref, q_ref, k_ref, vt_ref, qc_ref, kc_ref, vct_ref, o_ref, *rest, lam_init):
    *maybe_oc_ref, s_scr = rest
    lam = _diff_lambda(lam_ref, lam_init)
    sg_col = sg_ref[...]

    def rows(qs, pieces):
        blocks = [blk for k_r, vt_r in pieces for blk in _key_blocks(k_r, vt_r, blk=DIFF_KEY_BLK)]
        outs = _attention_t([(jnp.where(msk, q, jnp.zeros_like(q)), blocks)
                             for q in qs for msk in _half_masks(q.shape)], s_scr)
        res = []
        for o1, o2 in zip(outs[0::2], outs[1::2]):
            ot = o1 - lam * o2
            ms = jnp.mean(ot * ot, axis=0, keepdims=True)
            ot = ot * lax.rsqrt(ms + NORM_EPS) * sg_col * (1.0 - lam_init)
            res.append(ot.T.astype(BF16))
        return res

    def body(t, carry):
        starts = [pl.multiple_of((t * DIFF_Q_PER_TRIP + j) * DIFF_Q_BLK, DIFF_Q_BLK)
                  for j in range(DIFF_Q_PER_TRIP)]
        outs = rows([q_ref[pl.ds(r0, DIFF_Q_BLK), :] for r0 in starts], [(k_ref, vt_ref), (kc_ref, vct_ref)])
        for r0, o in zip(starts, outs):
            o_ref[pl.ds(r0, DIFF_Q_BLK), :] = o
        return carry

    lax.fori_loop(0, SEQ // (DIFF_Q_BLK * DIFF_Q_PER_TRIP), body, 0)
    if maybe_oc_ref:
        oc_ref, = maybe_oc_ref
        oc_ref[...], = rows([qc_ref[...]], [(kc_ref, vct_ref)])


def _diff_attention(lam, sg, q, k, vt, lam_init, *, ctx_out):
    lat = lambda b, h: (b, h)
    ctx = lambda b, h: (T_LAT // CTX_LEN + b, h)
    lat_t = lambda b, h: (h, b)
    ctx_t = lambda b, h: (h, T_LAT // CTX_LEN + b)
    out_specs = [pl.BlockSpec((SEQ, HB), lat)]
    out_shape = [jax.ShapeDtypeStruct((T_LAT, D_MODEL), BF16)]
    if ctx_out:
        out_specs.append(pl.BlockSpec((CTX_LEN, HB), lat))
        out_shape.append(jax.ShapeDtypeStruct((T_CTX, D_MODEL), BF16))
    return pl.pallas_call(
        functools.partial(_diff_kernel, lam_init=lam_init),
        grid=(BATCH, DIFF_HEADS),
        in_specs=[
            pl.BlockSpec(lam.shape, lambda b, h: (0, 0)),
            pl.BlockSpec(sg.shape, lambda b, h: (0, 0)),
            pl.BlockSpec((SEQ, HB), lat),
            pl.BlockSpec((SEQ, HB), lat),
            pl.BlockSpec((HB, SEQ), lat_t),
            pl.BlockSpec((CTX_LEN, HB), ctx),
            pl.BlockSpec((CTX_LEN, HB), ctx),
            pl.BlockSpec((HB, CTX_LEN), ctx_t),
        ],
        out_specs=out_specs,
        out_shape=out_shape,
        scratch_shapes=[pltpu.VMEM((2, SEQ + CTX_LEN, DIFF_Q_BLK), F32)],
        compiler_params=_cparams("parallel", "parallel"),
        name="diff_attention",
    )(lam, sg, q, k, vt, q, k, vt)


def _route_rows(logits_t, bias_col):
    s = _sigmoid(logits_t)
    sel = s + bias_col
    sel_r = [sel[e:e + 1] for e in range(N_EXPERTS)]
    s_r = [s[e:e + 1] for e in range(N_EXPERTS)]
    group_scores = []
    for g in range(N_GROUPS):
        v = sel_r[g * EXPERTS_PER_GROUP:(g + 1) * EXPERTS_PER_GROUP]
        pairs = [v[a] + v[b] for a in range(EXPERTS_PER_GROUP) for b in range(a + 1, EXPERTS_PER_GROUP)]
        group_scores.append(functools.reduce(jnp.maximum, pairs))
    best = group_scores[0]
    grp = jnp.zeros(best.shape, jnp.int32)
    for g in range(1, N_GROUPS):
        upd = group_scores[g] > best
        best = jnp.where(upd, group_scores[g], best)
        grp = jnp.where(upd, g, grp)

    def pick(rows, j):
        out = rows[j]
        for g in range(1, N_GROUPS):
            out = jnp.where(grp == g, rows[g * EXPERTS_PER_GROUP + j], out)
        return out

    w = [pick(sel_r, j) for j in range(EXPERTS_PER_GROUP)]
    sc = [pick(s_r, j) for j in range(EXPERTS_PER_GROUP)]

    def argmax_first(vals):
        bv = vals[0]
        bi = jnp.zeros(bv.shape, jnp.int32)
        for j in range(1, len(vals)):
            upd = vals[j] > bv
            bv = jnp.where(upd, vals[j], bv)
            bi = jnp.where(upd, j, bi)
        return bi

    i1 = argmax_first(w)
    i2 = argmax_first([jnp.where(i1 == j, -jnp.inf, w[j]) for j in range(EXPERTS_PER_GROUP)])

    def take(vals, idx):
        out = vals[0]
        for j in range(1, len(vals)):
            out = jnp.where(idx == j, vals[j], out)
        return out

    s1 = take(sc, i1)
    s2 = take(sc, i2)
    den = s1 + s2
    return grp * EXPERTS_PER_GROUP + i1, grp * EXPERTS_PER_GROUP + i2, s1 / den, s2 / den


def _oproj_kernel(o_ref, oc_ref, x_ref, mod_ref, w_ref, g_ref, rw_ref, rb_ref, tri_ref,
                  x1_ref, h_ref, ri_ref, rf_ref, cnt_ref, carry_ref):
    i = pl.program_id(0)

    @pl.when(i == 0)
    def _():
        carry_ref[...] = jnp.zeros_like(carry_ref)

    m = mod_ref[0]
    rw = rw_ref[...].astype(BF16)
    ri_ref[...] = jnp.zeros_like(ri_ref)
    rf_ref[...] = jnp.zeros_like(rf_ref)
    carry = carry_ref[...]
    for r0 in range(0, TM, ROUTE_BLK):
        rows = pl.ds(r0, ROUTE_BLK)
        o = jnp.where(i < LAT_TILES, o_ref[rows, :], oc_ref[rows, :])
        x1 = x_ref[rows, :] + m[2:3] * _dot(o, w_ref[0])
        x1_ref[rows, :] = x1
        h = _rms_modulate(x1, g_ref[...], m[3:4], m[4:5])
        hb = h.astype(BF16)
        h_ref[rows, :] = _pack_bf16_pair(h)

        logits = _dot(hb, rw)
        logits_t = logits.T[:N_EXPERTS]
        e0, e1, g0, g1 = _route_rows(logits_t, rb_ref[...])

        eiota = lax.broadcasted_iota(jnp.int32, (N_EXPERTS, ROUTE_BLK), 0)
        oh0 = eiota == e0
        oh1 = eiota == e1
        oh = jnp.where(oh0 | oh1, 1.0, 0.0)
        before = _dot(oh.astype(BF16), tri_ref[...]) + carry
        rank0 = jnp.sum(jnp.where(oh0, before, 0.0), axis=0, keepdims=True)
        rank1 = jnp.sum(jnp.where(oh1, before, 0.0), axis=0, keepdims=True)
        carry = carry + jnp.sum(oh, axis=1, keepdims=True)

        ri_ref[0:1, rows] = e0
        ri_ref[1:2, rows] = e1
        ri_ref[2:3, rows] = rank0.astype(jnp.int32)
        ri_ref[3:4, rows] = rank1.astype(jnp.int32)
        rf_ref[0:1, rows] = g0
        rf_ref[1:2, rows] = g1
    carry_ref[...] = carry
    cnt_ref[...] = jnp.broadcast_to(carry, cnt_ref.shape)


def _oproj_route(o, o_ctx, x, mod_l, w_o, layer, g, rw_pad, rb_col, tri, n_tok):
    n_tiles = n_tok // TM
    tok = lambda i: (i, 0)
    const2 = lambda i: (0, 0)
    lane_tok = lambda i: (0, i)
    return pl.pallas_call(
        _oproj_kernel,
        grid=(n_tiles,),
        in_specs=[
            pl.BlockSpec((TM, D_MODEL), lambda i: (jnp.minimum(i, LAT_TILES - 1), 0)),
            pl.BlockSpec((TM, D_MODEL), lambda i: (jnp.maximum(i - LAT_TILES, 0), 0)),
            pl.BlockSpec((TM, D_MODEL), tok),
            pl.BlockSpec((1, N_MOD, D_MODEL), lambda i: (_mod_row(i), 0, 0)),
            pl.BlockSpec((1, D_MODEL, D_MODEL), lambda i: (layer, 0, 0)),
            pl.BlockSpec((1, D_MODEL), const2),
            pl.BlockSpec((D_MODEL, HB), const2),
            pl.BlockSpec((N_EXPERTS, 1), const2),
            pl.BlockSpec((ROUTE_BLK, ROUTE_BLK), const2),
        ],
        out_specs=[
            pl.BlockSpec((TM, D_MODEL), tok),
            pl.BlockSpec((TM, D_PACK), tok),
            pl.BlockSpec((8, TM), lane_tok),
            pl.BlockSpec((8, TM), lane_tok),
            pl.BlockSpec((N_EXPERTS, HB), const2),
        ],
        out_shape=[
            jax.ShapeDtypeStruct((n_tok, D_MODEL), F32),
            jax.ShapeDtypeStruct((n_tok, D_PACK), jnp.uint32),
            jax.ShapeDtypeStruct((8, n_tok), jnp.int32),
            jax.ShapeDtypeStruct((8, n_tok), F32),
            jax.ShapeDtypeStruct((N_EXPERTS, HB), F32),
        ],
        scratch_shapes=[pltpu.VMEM((N_EXPERTS, 1), F32)],
        compiler_params=_cparams("arbitrary"),
        name="oproj_norm_route",
    )(o, o if o_ctx is None else o_ctx, x, mod_l, w_o, g.reshape(1, D_MODEL), rw_pad, rb_col, tri)


def _moe_kernel(te_ref, tv_ref, xs_ref, wg_ref, wu_ref, wd_ref, ys_ref, wgb, wub, wdb):
    i = pl.program_id(0)
    valid = tv_ref[i]
    e = te_ref[i]
    prev = te_ref[jnp.maximum(i - 1, 0)]
    half = TM // 2

    @pl.when((i == 0) | (e != prev))
    def _():
        wgb[...] = wg_ref[0, 0].astype(BF16)
        wub[...] = wu_ref[0, 0].astype(BF16)
        wdb[...] = wd_ref[0, 0].astype(BF16)

    def ffn(rows, x):
        x = _unpack_bf16_pair(x).astype(BF16)
        a = _dot(x, wgb[...])
        u = _dot(x, wub[...])
        hm = (a * _sigmoid(a) * u).astype(BF16)
        ys_ref[rows, :] = _pack_bf16_pair(_dot(hm, wdb[...]))

    @pl.when(valid == TM)
    def _():
        for r0 in range(0, TM, half):
            ffn(pl.ds(r0, half), xs_ref[pl.ds(r0, half), :])

    for r0 in range(0, TM, half):
        rows = pl.ds(r0, half)

        @pl.when((valid < TM) & (valid >= r0 + half))
        def _():
            ffn(rows, xs_ref[rows, :])

        @pl.when((valid > r0) & (valid < r0 + half))
        def _():
            row = r0 + lax.broadcasted_iota(jnp.int32, (half, 1), 0)
            x = xs_ref[rows, :]
            ffn(rows, jnp.where(row < valid, x, jnp.zeros_like(x)))

        @pl.when(valid <= r0)
        def _():
            ys_ref[rows, :] = jnp.zeros((half, D_PACK), jnp.uint32)


def _grouped_ffn(tile_expert, tile_valid, xs, w_gate, w_up, w_down, layer):
    n_tiles = xs.shape[0] // TM
    wmap = lambda i, te, tv: (layer, te[i], 0, 0)
    tok = lambda i, te, tv: (i, 0)
    return pl.pallas_call(
        _moe_kernel,
        grid_spec=pltpu.PrefetchScalarGridSpec(
            num_scalar_prefetch=2,
            grid=(n_tiles,),
            in_specs=[
                pl.BlockSpec((TM, D_PACK), tok),
                pl.BlockSpec((1, 1, D_MODEL, D_EXPERT), wmap),
                pl.BlockSpec((1, 1, D_MODEL, D_EXPERT), wmap),
                pl.BlockSpec((1, 1, D_EXPERT, D_MODEL), wmap),
            ],
            out_specs=pl.BlockSpec((TM, D_PACK), tok),
            scratch_shapes=[
                pltpu.VMEM((D_MODEL, D_EXPERT), BF16),
                pltpu.VMEM((D_MODEL, D_EXPERT), BF16),
                pltpu.VMEM((D_EXPERT, D_MODEL), BF16),
            ],
        ),
        out_shape=jax.ShapeDtypeStruct((xs.shape[0], D_PACK), jnp.uint32),
        compiler_params=_cparams("arbitrary"),
        name="grouped_ffn",
    )(tile_expert, tile_valid, xs, w_gate, w_up, w_down)


def _moe_layout(route_i, counts, n_tok):
    n_tiles = 2 * n_tok // TM + N_EXPERTS
    counts = counts.astype(jnp.int32)
    tiles_e = (counts + TM - 1) // TM
    tiles_end = jnp.cumsum(tiles_e)
    tile_start = tiles_end - tiles_e
    eid = jnp.arange(N_EXPERTS, dtype=jnp.int32)
    tok_oh = route_i[0:2][:, :, None] == eid
    pos = jnp.sum(jnp.where(tok_oh, tile_start * TM, 0), axis=-1) + route_i[2:4]
    tile_ids = jnp.arange(n_tiles, dtype=jnp.int32)
    n_used = tiles_end[-1]
    te = jnp.sum(tiles_end[None, :] <= jnp.minimum(tile_ids, n_used - 1)[:, None], axis=1).astype(jnp.int32)
    tile_oh = te[:, None] == eid
    cnt_t = jnp.sum(jnp.where(tile_oh, counts, 0), axis=1)
    start_t = jnp.sum(jnp.where(tile_oh, tile_start, 0), axis=1)
    tv = jnp.clip(cnt_t - (tile_ids - start_t) * TM, 0, TM)
    tv = jnp.where(tile_ids < n_used, tv, 0).astype(jnp.int32)
    return pos.astype(jnp.int32), te, tv, n_tiles


SC_CORES = 2
SC_SUBCORES = 16
SC_WORKERS = SC_CORES * SC_SUBCORES
SC_CHUNK = 32
SC_RING = 4


def _sc_mesh():
    return plsc.VectorSubcoreMesh(core_axis_name="c", subcore_axis_name="s")


def _sc_worker_indices(pos, n_tok):
    n_ch = n_tok // SC_WORKERS // SC_CHUNK
    return pos.reshape(2, SC_WORKERS, n_ch, SC_CHUNK).transpose(1, 2, 0, 3), n_ch


def _sc_dispatch(h, pos, n_slots):
    n_tok = h.shape[0]
    per_w = n_tok // SC_WORKERS
    pos_w, n_ch = _sc_worker_indices(pos, n_tok)
    assert n_ch % 2 == 0 and n_ch * SC_CHUNK * SC_WORKERS == n_tok

    @functools.partial(
        pl.kernel, mesh=_sc_mesh(), out_type=jax.ShapeDtypeStruct((n_slots, h.shape[1]), h.dtype),
        scratch_types=[pltpu.VMEM((n_ch, 2, SC_CHUNK), jnp.int32), pltpu.VMEM((2, SC_CHUNK, h.shape[1]), h.dtype),
                       pltpu.SemaphoreType.DMA((2,)), pltpu.SemaphoreType.DMA((2,))],
        name="sc_dispatch")
    def dispatch(h_hbm, pos_hbm, xs_hbm, idx_v, rows_v, load_sem, scat_sem):
        wid = lax.axis_index("s") * SC_CORES + lax.axis_index("c")
        base = wid * per_w
        pltpu.sync_copy(pos_hbm.at[wid], idx_v)

        def load(c, b):
            return pltpu.make_async_copy(h_hbm.at[pl.ds(base + c * SC_CHUNK, SC_CHUNK)], rows_v.at[b],
                                         load_sem.at[b])

        def scat(c, b, k):
            return pltpu.make_async_copy(rows_v.at[b], xs_hbm.at[idx_v.at[c, k]], scat_sem.at[b])

        load(0, 0).start()

        @pl.loop(0, n_ch, step=2)
        def _(c0):
            for b in range(2):
                c = c0 + b
                load(c, b).wait()
                scat(c, b, 0).start()
                scat(c, b, 1).start()

                @pl.when(c >= 1)
                def _():
                    scat(c - 1, 1 - b, 0).wait()
                    scat(c - 1, 1 - b, 1).wait()

                @pl.when(c + 1 < n_ch)
                def _():
                    load(c + 1, 1 - b).start()

        scat(n_ch - 1, 1, 0).wait()
        scat(n_ch - 1, 1, 1).wait()

    return dispatch(h, pos_w)


def _sc_combine_gather(ys, pos):
    n_tok = pos.shape[1]
    per_w = n_tok // SC_WORKERS
    pos_w, n_ch = _sc_worker_indices(pos, n_tok)
    n_items = 2 * n_ch
    assert n_ch * SC_CHUNK * SC_WORKERS == n_tok and n_items % SC_RING == 0 and SC_RING % 2 == 0

    @functools.partial(
        pl.kernel, mesh=_sc_mesh(), out_type=jax.ShapeDtypeStruct((2, n_tok, ys.shape[1]), ys.dtype),
        scratch_types=[pltpu.VMEM((n_ch, 2, SC_CHUNK), jnp.int32),
                       pltpu.VMEM((SC_RING, SC_CHUNK, ys.shape[1]), ys.dtype),
                       pltpu.SemaphoreType.DMA((SC_RING,)), pltpu.SemaphoreType.DMA((SC_RING,))],
        name="sc_combine_gather")
    def gather(ys_hbm, pos_hbm, yg_hbm, idx_v, rows_v, gath_sem, write_sem):
        wid = lax.axis_index("s") * SC_CORES + lax.axis_index("c")
        base = wid * per_w
        pltpu.sync_copy(pos_hbm.at[wid], idx_v)

        def gath(c, k, b):
            return pltpu.make_async_copy(ys_hbm.at[idx_v.at[c, k]], rows_v.at[b], gath_sem.at[b])

        def write(c, k, b):
            return pltpu.make_async_copy(rows_v.at[b], yg_hbm.at[k, pl.ds(base + c * SC_CHUNK, SC_CHUNK)],
                                         write_sem.at[b])

        for j in range(SC_RING - 1):
            gath(j // 2, j % 2, j).start()

        @pl.loop(0, n_items, step=SC_RING)
        def _(j0):
            for b in range(SC_RING):
                j = j0 + b
                gath(j // 2, b % 2, b).wait()
                write(j // 2, b % 2, b).start()
                pb = (b - 1) % SC_RING

                @pl.when(j >= 1)
                def _():
                    write((j - 1) // 2, (b - 1) % 2, pb).wait()

                @pl.when(j + SC_RING - 1 < n_items)
                def _():
                    gath((j + SC_RING - 1) // 2, (b - 1) % 2, pb).start()

        write(n_ch - 1, 1, (n_items - 1) % SC_RING).wait()

    return gather(ys, pos_w)


def _final_kernel(x1_ref, yg_ref, gt_ref, mod_ref, fg_ref, out_ref):
    x2 = _moe_residual(x1_ref[...], yg_ref[0], yg_ref[1], gt_ref[...], mod_ref[0][5:6])
    ms = jnp.mean(x2 * x2, axis=-1, keepdims=True)
    out_ref[...] = x2 * lax.rsqrt(ms + NORM_EPS) * fg_ref[...]


def _combine_final(x1, yg, gates_t, mod_l, final_g, n_tok):
    tok = lambda i: (i, 0)
    return pl.pallas_call(
        _final_kernel,
        grid=(n_tok // TM,),
        in_specs=[
            pl.BlockSpec((TM, D_MODEL), tok),
            pl.BlockSpec((2, TM, D_PACK), lambda i: (0, i, 0)),
            pl.BlockSpec((TM, 2), tok),
            pl.BlockSpec((1, N_MOD, D_MODEL), lambda i: (_mod_row(i), 0, 0)),
            pl.BlockSpec((1, D_MODEL), lambda i: (0, 0)),
        ],
        out_specs=pl.BlockSpec((TM, D_MODEL), tok),
        out_shape=jax.ShapeDtypeStruct((n_tok, D_MODEL), F32),
        compiler_params=_cparams("parallel"),
        name="moe_combine_final",
    )(x1, yg, gates_t, mod_l, final_g.reshape(1, D_MODEL))


def _diff_lambda_init(layer_idx):
    return 0.8 - 0.6 * math.exp(-0.3 * (layer_idx - 1))


def kernel(x, c, ctx, c_ctx, mod_w, mod_b, norm_mix, norm_ffn, w_qkv, w_o, na_rpb, diff_lambda, diff_subln,
           router_w, router_b, expert_w_gate, expert_w_up, expert_w_down, final_norm):
    mod = _modulation(c, c_ctx, mod_w, mod_b)
    stream = (x.reshape(T_LAT, D_MODEL), ctx.reshape(T_CTX, D_MODEL))
    w_qk_b = w_qkv[:, :, :2 * D_MODEL].astype(BF16)
    w_vt_b = jnp.swapaxes(w_qkv[:, :, 2 * D_MODEL:].astype(BF16), 1, 2)
    w_o_b = w_o.astype(BF16)
    na_tables = _na_pair_table(na_rpb.reshape(-1, N_DR, N_DC))
    cs, sn = _rope_tables()
    rw_pad = jnp.zeros((D_MODEL, HB), F32).at[:, :N_EXPERTS].set(router_w)
    rb_col = router_b.reshape(N_EXPERTS, 1).astype(F32)
    blk = jnp.arange(ROUTE_BLK)
    tri = (blk[:, None] < blk[None, :]).astype(BF16)

    for i in range(DEPTH):
        last = i == DEPTH - 1
        j = i // 2
        is_diff = i % 2 == 1
        xa, q, k, v = _qkv(stream, mod[i], norm_mix[i], w_qk_b, w_vt_b, i, cs, sn, rope=is_diff)
        if not is_diff:
            o, o_ctx = _na_attention(na_tables, j, q, k, v)
        else:
            lam_init = _diff_lambda_init(i + 1)
            sg = diff_subln[j].reshape(HB, 1)
            outs = _diff_attention(diff_lambda[j], sg, q, k, v, lam_init, ctx_out=not last)
            o, o_ctx = outs if not last else (outs[0], None)
        n_tok = T_LAT if last else T_ALL
        x1, h, route_i, route_f, cnt = _oproj_route(o, o_ctx, xa, mod[i], w_o_b, i, norm_ffn[i], rw_pad,
                                                    rb_col, tri, n_tok)
        pos, te, tv, n_tiles = _moe_layout(route_i, cnt[:, 0], n_tok)
        xs = _sc_dispatch(h, pos, n_tiles * TM)
        ys = _grouped_ffn(te, tv, xs, expert_w_gate, expert_w_up, expert_w_down, i)
        yg = _sc_combine_gather(ys, pos)
        stream = (x1, yg, route_f[0:2].T, mod[i])
    out = _combine_final(*stream, final_norm, T_LAT)
    return out.reshape(BATCH, SEQ, D_MODEL)
```

```python
import functools
import math

import jax
import jax.numpy as jnp
from jax import lax
from jax.experimental import pallas as pl
from jax.experimental.pallas import tpu as pltpu
from jax.experimental.pallas import tpu_sc as plsc

F32 = jnp.float32
BF16 = jnp.bfloat16

D_MODEL = 1024
BATCH = 8
SEQ = 2048
DEPTH = 4
CTX_LEN = 256
GRID_W = 64
ROWS = SEQ // GRID_W
NA_HEADS = 16
NA_WIN_ROWS = 8
NA_WIN_COLS = 16
DIFF_HEADS = 8
DIFF_HEAD_DIM = 64
ROPE_THETA = 10000.0
N_EXPERTS = 16
N_GROUPS = 4
EXPERTS_PER_GROUP = 4
D_EXPERT = 1024
N_MOD = 6
NORM_EPS = 1e-6

T_LAT = BATCH * SEQ
T_CTX = BATCH * CTX_LEN
T_ALL = T_LAT + T_CTX
TM = 512
ROUTE_BLK = TM // 2
LAT_TILES = T_LAT // TM
CTX_TILES = T_CTX // TM
TILES_PER_BATCH = SEQ // TM
CTX_ROW = BATCH
MOD_ROWS = 16
HB = 128
N_HB = D_MODEL // HB
NA_RB = 4
NA_BAND_ROWS = 12
NA_Q = NA_RB * GRID_W
NA_BAND = NA_BAND_ROWS * GRID_W
TQ_DIFF = 512
MASK_VALUE = -1e30
LOG2E = math.log2(math.e)
Q_SCALE = 0.125 * LOG2E
VMEM_LIMIT = 56 * 1024 * 1024


def _cparams(*sem):
    return pltpu.CompilerParams(dimension_semantics=sem, vmem_limit_bytes=VMEM_LIMIT)


def _dot(a, b):
    return jnp.dot(a, b, preferred_element_type=F32)


def _dot_nt(a, b):
    return lax.dot_general(a, b, (((1,), (1,)), ((), ())), preferred_element_type=F32)


def _sigmoid(x):
    return 1.0 / (1.0 + jnp.exp(-x))


D_PACK = D_MODEL // 2


def _pack_bf16_pair(x):
    lo = lax.bitcast_convert_type(x[:, :D_PACK].astype(BF16).astype(F32), jnp.uint32)
    hi = lax.bitcast_convert_type(x[:, D_PACK:].astype(BF16).astype(F32), jnp.uint32)
    return (lo >> 16) | hi


def _unpack_bf16_pair(w):
    lo = lax.bitcast_convert_type(w << 16, F32)
    hi = lax.bitcast_convert_type(w & jnp.uint32(0xFFFF0000), F32)
    return jnp.concatenate([lo, hi], axis=1)


def _mod_row(i):
    return jnp.minimum(i // TILES_PER_BATCH, CTX_ROW)


def _mod_kernel(act_ref, w_ref, b_ref, o_ref):
    a = act_ref[...]
    a = a * _sigmoid(a)
    o_ref[0] = _dot(a.astype(BF16), w_ref[0].astype(BF16)) + b_ref[0]


def _modulation(c, c_ctx, mod_w, mod_b):
    tn = 1536
    act = jnp.zeros((MOD_ROWS, D_MODEL), F32).at[:BATCH].set(c).at[CTX_ROW].set(c_ctx)
    out = pl.pallas_call(
        _mod_kernel,
        grid=(DEPTH, N_MOD * D_MODEL // tn),
        in_specs=[
            pl.BlockSpec((MOD_ROWS, D_MODEL), lambda l, j: (0, 0)),
            pl.BlockSpec((1, D_MODEL, tn), lambda l, j: (l, 0, j)),
            pl.BlockSpec((1, 1, tn), lambda l, j: (l, 0, j)),
        ],
        out_specs=pl.BlockSpec((1, MOD_ROWS, tn), lambda l, j: (l, 0, j)),
        out_shape=jax.ShapeDtypeStruct((DEPTH, MOD_ROWS, N_MOD * D_MODEL), F32),
        compiler_params=_cparams("parallel", "parallel"),
        name="adaln_mod",
    )(act, mod_w, mod_b.reshape(DEPTH, 1, N_MOD * D_MODEL))
    return out.reshape(DEPTH, MOD_ROWS, N_MOD, D_MODEL)


def _qkv_weight_kernel(w_ref, wqk_ref, wvt_ref):
    j = pl.program_id(1)

    @pl.when(j < 2)
    def _():
        wqk_ref[0] = w_ref[0].astype(BF16)

    @pl.when(j == 2)
    def _():
        wvt_ref[0] = w_ref[0].T.astype(BF16)


def _qkv_weights(w_qkv):
    return pl.pallas_call(
        _qkv_weight_kernel,
        grid=(DEPTH, 3),
        in_specs=[pl.BlockSpec((1, D_MODEL, D_MODEL), lambda l, j: (l, 0, j))],
        out_specs=[
            pl.BlockSpec((1, D_MODEL, D_MODEL), lambda l, j: (l, 0, jnp.minimum(j, 1))),
            pl.BlockSpec((1, D_MODEL, D_MODEL), lambda l, j: (l, 0, 0)),
        ],
        out_shape=[jax.ShapeDtypeStruct((DEPTH, D_MODEL, 2 * D_MODEL), BF16),
                   jax.ShapeDtypeStruct((DEPTH, D_MODEL, D_MODEL), BF16)],
        compiler_params=_cparams("parallel", "arbitrary"),
        name="qkv_weight_prep",
    )(w_qkv)


def _rms_modulate(x, g, shift, scale):
    ms = jnp.mean(x * x, axis=-1, keepdims=True)
    y = x * lax.rsqrt(ms + NORM_EPS) * g
    return y * (1.0 + scale) + shift


def _rope_block(xb, cs, sn):
    lane = lax.broadcasted_iota(jnp.int32, xb.shape, 1)
    partner = jnp.where((lane & 63) < 32, pltpu.roll(xb, 96, 1), pltpu.roll(xb, 32, 1))
    return xb * cs + partner * sn


def _moe_residual(x1, y0, y1, g, gate_ffn):
    y = g[:, 0:1] * _unpack_bf16_pair(y0) + g[:, 1:2] * _unpack_bf16_pair(y1)
    return x1 + gate_ffn * y


def _qkv_kernel(*refs, rope, first):
    n_stream = 2 if first else 4
    stream_refs, refs = refs[:n_stream], refs[n_stream:]
    mod_ref, g_ref, wqk_ref, wvt_ref, cs_ref, sn_ref, x_ref, q_ref, k_ref, vt_ref = refs
    m = mod_ref[0]
    for r0 in range(0, TM, ROUTE_BLK):
        rows = pl.ds(r0, ROUTE_BLK)
        if first:
            xl_ref, xc_ref = stream_refs
            x = jnp.where(pl.program_id(0) < LAT_TILES, xl_ref[rows, :], xc_ref[rows, :])
        else:
            x1_ref, yg_ref, gt_ref, modp_ref = stream_refs
            x = _moe_residual(x1_ref[rows, :], yg_ref[0, rows, :], yg_ref[1, rows, :], gt_ref[rows, :],
                              modp_ref[0][5:6])
        x_ref[rows, :] = x
        h = _rms_modulate(x, g_ref[...], m[0:1], m[1:2]).astype(BF16)
        for idx, out in enumerate((q_ref, k_ref)):
            acc = _dot(h, wqk_ref[0, :, idx * D_MODEL:(idx + 1) * D_MODEL])
            if rope:
                cs = cs_ref[rows, :]
                sn = sn_ref[rows, :]
                acc = jnp.concatenate(
                    [_rope_block(acc[:, j * HB:(j + 1) * HB], cs, sn) for j in range(N_HB)], axis=1)
            if idx == 0:
                acc = acc * Q_SCALE
            out[rows, :] = acc.astype(BF16)
        vt_ref[:, rows] = _dot_nt(wvt_ref[0], h).astype(BF16)


def _qkv(stream, mod_l, g, w_qk, w_vt, layer, cs, sn, *, rope):
    first = len(stream) == 2
    n_tiles = T_ALL // TM
    tok = lambda i: (i, 0)
    const2 = lambda i: (0, 0)
    mod_spec = pl.BlockSpec((1, N_MOD, D_MODEL), lambda i: (_mod_row(i), 0, 0))
    rope_idx = lambda i: (jnp.where(i < LAT_TILES, i % TILES_PER_BATCH, TILES_PER_BATCH), 0)
    out = jax.ShapeDtypeStruct((T_ALL, D_MODEL), BF16)
    wmap = lambda i: (layer, 0, 0)
    if first:
        stream_specs = [
            pl.BlockSpec((TM, D_MODEL), lambda i: (jnp.minimum(i, LAT_TILES - 1), 0)),
            pl.BlockSpec((TM, D_MODEL), lambda i: (jnp.maximum(i - LAT_TILES, 0), 0)),
        ]
    else:
        stream_specs = [
            pl.BlockSpec((TM, D_MODEL), tok),
            pl.BlockSpec((2, TM, D_PACK), lambda i: (0, i, 0)),
            pl.BlockSpec((TM, 2), tok),
            mod_spec,
        ]
    return pl.pallas_call(
        functools.partial(_qkv_kernel, rope=rope, first=first),
        grid=(n_tiles,),
        in_specs=stream_specs + [
            mod_spec,
            pl.BlockSpec((1, D_MODEL), const2),
            pl.BlockSpec((1, D_MODEL, 2 * D_MODEL), wmap),
            pl.BlockSpec((1, D_MODEL, D_MODEL), wmap),
            pl.BlockSpec((TM, HB), rope_idx),
            pl.BlockSpec((TM, HB), rope_idx),
        ],
        out_specs=[pl.BlockSpec((TM, D_MODEL), tok), pl.BlockSpec((TM, D_MODEL), tok),
                   pl.BlockSpec((TM, D_MODEL), tok), pl.BlockSpec((D_MODEL, TM), lambda i: (0, i))],
        out_shape=[jax.ShapeDtypeStruct((T_ALL, D_MODEL), F32), out, out,
                   jax.ShapeDtypeStruct((D_MODEL, T_ALL), BF16)],
        compiler_params=_cparams("parallel"),
        name="norm_qkv_rope" if rope else "norm_qkv",
    )(*stream, mod_l, g.reshape(1, D_MODEL), w_qk, w_vt, cs, sn)


def _rope_tables():
    t = jnp.arange(SEQ)
    row = (t // GRID_W).astype(F32)
    col = (t % GRID_W).astype(F32)
    n_freq = DIFF_HEAD_DIM // 4
    inv_freq = ROPE_THETA ** (-jnp.arange(n_freq, dtype=F32) / n_freq)
    ang = jnp.concatenate([row[:, None] * inv_freq, col[:, None] * inv_freq], axis=-1)
    cos, sin = jnp.cos(ang), jnp.sin(ang)
    cs = jnp.concatenate([cos, cos, cos, cos], axis=-1)
    sn = jnp.concatenate([-sin, sin, -sin, sin], axis=-1)
    cs = jnp.concatenate([cs, jnp.ones((TM, HB), F32)], axis=0)
    sn = jnp.concatenate([sn, jnp.zeros((TM, HB), F32)], axis=0)
    return cs, sn


Q_BLK = 256
KEY_BLK = 256
SUBLANES = 8
DIFF_Q_BLK = 256
DIFF_Q_PER_TRIP = 8
DIFF_KEY_BLK = 1024


def _colwise(reduce_fn, x):
    return reduce_fn(x.reshape(x.shape[0] // SUBLANES, SUBLANES, x.shape[1]), axis=0)


def _key_blocks(k_ref, vt_ref, key_start=0, n_keys=None, bias=None, blk=KEY_BLK):
    n_keys = k_ref.shape[0] if n_keys is None else n_keys
    blk = min(blk, n_keys)
    blocks = []
    for c, s0 in enumerate(range(0, n_keys, blk)):
        blocks.append((
            lambda s0=s0: k_ref[pl.ds(key_start + s0, blk), :],
            lambda s0=s0: vt_ref[:, pl.ds(key_start + s0, blk)],
            None if bias is None else functools.partial(bias, c),
            blk,
        ))
    return blocks


def _attention_t(operands, s_scr):
    def rows_of(u, c):
        sizes = [blk[3] for blk in operands[u][1]]
        return slice(sum(sizes[:c]), sum(sizes[:c + 1]))

    def score(u, c, mx):
        qm, blocks = operands[u]
        k_tile, _, bias, _ = blocks[c]
        st = _dot_nt(k_tile(), qm)
        if bias is not None:
            st = st + bias()
        s_scr[u % 2, rows_of(u, c), :n_q] = st
        return jnp.maximum(mx, _colwise(jnp.max, st))

    def expo(u, c, m, acc, den):
        vt_tile = operands[u][1][c][1]
        e = jnp.exp2(s_scr[u % 2, rows_of(u, c), :n_q] - m)
        o = _dot(vt_tile(), e.astype(BF16))
        return (o if acc is None else acc + o), den + _colwise(jnp.sum, e)

    n_q = operands[0][0].shape[0]
    neg = jnp.full((SUBLANES, n_q), -jnp.inf, F32)
    zero = jnp.zeros((SUBLANES, n_q), F32)
    n_blocks = [len(blocks) for _, blocks in operands]
    mx = neg
    for c in range(n_blocks[0]):
        mx = score(0, c, mx)
    outs = []
    for u in range(len(operands)):
        m = jnp.max(mx, axis=0, keepdims=True)
        acc, den, mx = None, zero, neg
        n_next = n_blocks[u + 1] if u + 1 < len(operands) else 0
        for c in range(max(n_blocks[u], n_next)):
            if c < n_next:
                mx = score(u + 1, c, mx)
            if c < n_blocks[u]:
                acc, den = expo(u, c, m, acc, den)
        outs.append(acc / jnp.sum(den, axis=0, keepdims=True))
    return outs


def _half_masks(shape):
    lane = lax.broadcasted_iota(jnp.int32, shape, 1)
    return lane < 64, lane >= 64


N_DR = 2 * NA_WIN_ROWS - 1
N_DC = 2 * NA_WIN_COLS - 1
PAIR_MASK_FIRST = N_DR - 1
PAIR_MASK_SECOND = N_DR
PAIR_MASKED = N_DR + 1
N_PAIRS = N_DR + 2
NA_KEY_BLK = 768
DR_LO = NA_WIN_ROWS - 1 - NA_WIN_ROWS // 2
DR_HI = DR_LO + NA_WIN_ROWS - 1


def _na_pair_table(rpb):
    h = rpb.shape[0]
    w = GRID_W
    col = jnp.arange(w)
    dc = jnp.clip(col[:, None] - col[None, :] + NA_WIN_COLS - 1, 0, N_DC - 1)
    onehot = (dc[:, :, None] == jnp.arange(N_DC)).astype(F32)
    t = jnp.einsum("hdc,kqc->hdkq", rpb.astype(F32), onehot, precision=lax.Precision.HIGHEST)
    cstart = jnp.clip(col - NA_WIN_COLS // 2, 0, w - NA_WIN_COLS)
    cvalid = (col[:, None] >= cstart[None, :]) & (col[:, None] < cstart[None, :] + NA_WIN_COLS)
    tt = jnp.where(cvalid, t * LOG2E, MASK_VALUE)
    masked = jnp.full((h, 1, w, w), MASK_VALUE, F32)
    regular = jnp.concatenate([tt[:, 1:], tt[:, :-1]], axis=-1)
    first = jnp.concatenate([masked, tt[:, DR_HI:DR_HI + 1]], axis=-1)
    second = jnp.concatenate([tt[:, DR_LO:DR_LO + 1], masked], axis=-1)
    return jnp.concatenate([regular, first, second, jnp.concatenate([masked, masked], axis=-1)], axis=1)


def _na_pair_entries(q_row0, band_row0, band_rows):
    entries = []
    for j in range(band_rows):
        kr = band_row0 + j
        row = []
        for a in range(0, NA_RB, 2):
            ok, d = [], []
            for r in (q_row0 + a, q_row0 + a + 1):
                r0 = min(max(r - NA_WIN_ROWS // 2, 0), ROWS - NA_WIN_ROWS)
                ok.append(r0 <= kr < r0 + NA_WIN_ROWS)
                d.append(kr - r + NA_WIN_ROWS - 1)
            if ok[0] and ok[1]:
                row.append(d[0] - 1)
            elif ok[1]:
                assert d[1] == DR_HI
                row.append(PAIR_MASK_FIRST)
            elif ok[0]:
                assert d[0] == DR_LO
                row.append(PAIR_MASK_SECOND)
            else:
                row.append(PAIR_MASKED)
        entries.append(row)
    return entries


def _na_kernel(tab_ref, q_ref, k_ref, vt_ref, qc_ref, kc_ref, vct_ref, o_ref, oc_ref, s_scr):
    n_blocks = ROWS // NA_RB
    ctx_blocks = _key_blocks(kc_ref, vct_ref)

    def operands(q_start, band_start, band_rows, entries):
        q = q_ref[pl.ds(q_start, NA_Q), :]
        n_band = band_rows * GRID_W
        blk = NA_KEY_BLK if n_band % NA_KEY_BLK == 0 else n_band
        rows_per_chunk = blk // GRID_W
        ops = []
        for hh, msk in enumerate(_half_masks(q.shape)):
            def bias(c, hh=hh):
                rows = entries[c * rows_per_chunk:(c + 1) * rows_per_chunk]
                return jnp.concatenate(
                    [jnp.concatenate([tab_ref[hh, e] for e in row], axis=1) for row in rows], axis=0)

            band = _key_blocks(k_ref, vt_ref, band_start, n_band, bias, blk)
            ops.append((jnp.where(msk, q, jnp.zeros_like(q)), band + ctx_blocks))
        return ops

    def merge_heads(o_lo, o_hi):
        feat = lax.broadcasted_iota(jnp.int32, o_lo.shape, 0)
        return jnp.where(feat < HB // 2, o_lo, o_hi).T.astype(BF16)

    last_band = ROWS - NA_WIN_ROWS
    interior = _na_pair_entries(NA_RB, 0, NA_BAND_ROWS)
    blocks = [(0, 0, NA_WIN_ROWS, _na_pair_entries(0, 0, NA_WIN_ROWS))]
    blocks += [(rb * NA_Q, (rb - 1) * NA_Q, NA_BAND_ROWS, interior) for rb in range(1, n_blocks - 1)]
    blocks.append(((n_blocks - 1) * NA_Q, last_band * GRID_W, NA_WIN_ROWS,
                   _na_pair_entries(ROWS - NA_RB, last_band, NA_WIN_ROWS)))
    qc = qc_ref[...]
    ctx_ops = [(jnp.where(msk, qc, jnp.zeros_like(qc)), ctx_blocks) for msk in _half_masks(qc.shape)]
    outs = _attention_t([op for blk in blocks for op in operands(*blk)] + ctx_ops, s_scr)
    for i, blk in enumerate(blocks):
        o_ref[pl.ds(blk[0], NA_Q), :] = merge_heads(outs[2 * i], outs[2 * i + 1])
    oc_ref[...] = merge_heads(outs[-2], outs[-1])


def _na_attention(tables, layer_na, q, k, vt):
    assert NA_Q == Q_BLK and CTX_LEN == Q_BLK
    lat = lambda hp, b: (b, hp)
    ctx = lambda hp, b: (T_LAT // CTX_LEN + b, hp)
    return pl.pallas_call(
        _na_kernel,
        grid=(N_HB, BATCH),
        in_specs=[
            pl.BlockSpec((2, N_PAIRS, GRID_W, 2 * GRID_W), lambda hp, b: (layer_na * N_HB + hp, 0, 0, 0)),
            pl.BlockSpec((SEQ, HB), lat),
            pl.BlockSpec((SEQ, HB), lat),
            pl.BlockSpec((HB, SEQ), lambda hp, b: (hp, b)),
            pl.BlockSpec((CTX_LEN, HB), ctx),
            pl.BlockSpec((CTX_LEN, HB), ctx),
            pl.BlockSpec((HB, CTX_LEN), lambda hp, b: (hp, T_LAT // CTX_LEN + b)),
        ],
        out_specs=[pl.BlockSpec((SEQ, HB), lat), pl.BlockSpec((CTX_LEN, HB), lat)],
        out_shape=[jax.ShapeDtypeStruct((T_LAT, D_MODEL), BF16), jax.ShapeDtypeStruct((T_CTX, D_MODEL), BF16)],
        scratch_shapes=[pltpu.VMEM((2, NA_BAND + CTX_LEN, Q_BLK), F32)],
        compiler_params=_cparams("parallel", "parallel"),
        name="na_attention",
    )(tables, q, k, vt, q, k, vt)


def _diff_lambda(lam_ref, lam_init):
    lam = lam_ref[...]
    a = jnp.sum(lam[0:1] * lam[1:2], axis=-1, keepdims=True)
    b = jnp.sum(lam[2:3] * lam[3:4], axis=-1, keepdims=True)
    return jnp.exp(a) - jnp.exp(b) + lam_init


def _diff_kernel(lam_ref, sg_ref, q_ref, k_ref, vt_ref, qc_ref, kc_ref, vct_ref, o_ref, *rest, lam_init):
    *maybe_oc_ref, s_scr = rest
    lam = _diff_lambda(lam_ref, lam_init)
    sg_col = sg_ref[...]

    def rows(qs, pieces):
        blocks = [blk for k_r, vt_r in pieces for blk in _key_blocks(k_r, vt_r, blk=DIFF_KEY_BLK)]
        outs = _attention_t([(jnp.where(msk, q, jnp.zeros_like(q)), blocks)
                             for q in qs for msk in _half_masks(q.shape)], s_scr)
        res = []
        for o1, o2 in zip(outs[0::2], outs[1::2]):
            ot = o1 - lam * o2
            ms = jnp.mean(ot * ot, axis=0, keepdims=True)
            ot = ot * lax.rsqrt(ms + NORM_EPS) * sg_col * (1.0 - lam_init)
            res.append(ot.T.astype(BF16))
        return res

    def body(t, carry):
        starts = [pl.multiple_of((t * DIFF_Q_PER_TRIP + j) * DIFF_Q_BLK, DIFF_Q_BLK)
                  for j in range(DIFF_Q_PER_TRIP)]
        outs = rows([q_ref[pl.ds(r0, DIFF_Q_BLK), :] for r0 in starts], [(k_ref, vt_ref), (kc_ref, vct_ref)])
        for r0, o in zip(starts, outs):
            o_ref[pl.ds(r0, DIFF_Q_BLK), :] = o
        return carry

    lax.fori_loop(0, SEQ // (DIFF_Q_BLK * DIFF_Q_PER_TRIP), body, 0)
    if maybe_oc_ref:
        oc_ref, = maybe_oc_ref
        oc_ref[...], = rows([qc_ref[...]], [(kc_ref, vct_ref)])


def _diff_attention(lam, sg, q, k, vt, lam_init, *, ctx_out):
    lat = lambda b, h: (b, h)
    ctx = lambda b, h: (T_LAT // CTX_LEN + b, h)
    lat_t = lambda b, h: (h, b)
    ctx_t = lambda b, h: (h, T_LAT // CTX_LEN + b)
    out_specs = [pl.BlockSpec((SEQ, HB), lat)]
    out_shape = [jax.ShapeDtypeStruct((T_LAT, D_MODEL), BF16)]
    if ctx_out:
        out_specs.append(pl.BlockSpec((CTX_LEN, HB), lat))
        out_shape.append(jax.ShapeDtypeStruct((T_CTX, D_MODEL), BF16))
    return pl.pallas_call(
        functools.partial(_diff_kernel, lam_init=lam_init),
        grid=(BATCH, DIFF_HEADS),
        in_specs=[
            pl.BlockSpec(lam.shape, lambda b, h: (0, 0)),
            pl.BlockSpec(sg.shape, lambda b, h: (0, 0)),
            pl.BlockSpec((SEQ, HB), lat),
            pl.BlockSpec((SEQ, HB), lat),
            pl.BlockSpec((HB, SEQ), lat_t),
            pl.BlockSpec((CTX_LEN, HB), ctx),
            pl.BlockSpec((CTX_LEN, HB), ctx),
            pl.BlockSpec((HB, CTX_LEN), ctx_t),
        ],
        out_specs=out_specs,
        out_shape=out_shape,
        scratch_shapes=[pltpu.VMEM((2, SEQ + CTX_LEN, DIFF_Q_BLK), F32)],
        compiler_params=_cparams("parallel", "parallel"),
        name="diff_attention",
    )(lam, sg, q, k, vt, q, k, vt)


def _route_rows(logits_t, bias_col):
    s = _sigmoid(logits_t)
    sel = s + bias_col
    sel_r = [sel[e:e + 1] for e in range(N_EXPERTS)]
    s_r = [s[e:e + 1] for e in range(N_EXPERTS)]
    group_scores = []
    for g in range(N_GROUPS):
        v = sel_r[g * EXPERTS_PER_GROUP:(g + 1) * EXPERTS_PER_GROUP]
        pairs = [v[a] + v[b] for a in range(EXPERTS_PER_GROUP) for b in range(a + 1, EXPERTS_PER_GROUP)]
        group_scores.append(functools.reduce(jnp.maximum, pairs))
    best = group_scores[0]
    grp = jnp.zeros(best.shape, jnp.int32)
    for g in range(1, N_GROUPS):
        upd = group_scores[g] > best
        best = jnp.where(upd, group_scores[g], best)
        grp = jnp.where(upd, g, grp)

    def pick(rows, j):
        out = rows[j]
        for g in range(1, N_GROUPS):
            out = jnp.where(grp == g, rows[g * EXPERTS_PER_GROUP + j], out)
        return out

    w = [pick(sel_r, j) for j in range(EXPERTS_PER_GROUP)]
    sc = [pick(s_r, j) for j in range(EXPERTS_PER_GROUP)]

    def argmax_first(vals):
        bv = vals[0]
        bi = jnp.zeros(bv.shape, jnp.int32)
        for j in range(1, len(vals)):
            upd = vals[j] > bv
            bv = jnp.where(upd, vals[j], bv)
            bi = jnp.where(upd, j, bi)
        return bi

    i1 = argmax_first(w)
    i2 = argmax_first([jnp.where(i1 == j, -jnp.inf, w[j]) for j in range(EXPERTS_PER_GROUP)])

    def take(vals, idx):
        out = vals[0]
        for j in range(1, len(vals)):
            out = jnp.where(idx == j, vals[j], out)
        return out

    s1 = take(sc, i1)
    s2 = take(sc, i2)
    den = s1 + s2
    return grp * EXPERTS_PER_GROUP + i1, grp * EXPERTS_PER_GROUP + i2, s1 / den, s2 / den


def _oproj_kernel(o_ref, oc_ref, x_ref, mod_ref, w_ref, g_ref, rw_ref, rb_ref, tri_ref,
                  x1_ref, h_ref, ri_ref, rf_ref, cnt_ref, carry_ref):
    i = pl.program_id(0)

    @pl.when(i == 0)
    def _():
        carry_ref[...] = jnp.zeros_like(carry_ref)

    m = mod_ref[0]
    rw = rw_ref[...].astype(BF16)
    ri_ref[...] = jnp.zeros_like(ri_ref)
    rf_ref[...] = jnp.zeros_like(rf_ref)
    carry = carry_ref[...]
    for r0 in range(0, TM, ROUTE_BLK):
        rows = pl.ds(r0, ROUTE_BLK)
        o = jnp.where(i < LAT_TILES, o_ref[rows, :], oc_ref[rows, :])
        x1 = x_ref[rows, :] + m[2:3] * _dot(o, w_ref[0])
        x1_ref[rows, :] = x1
        h = _rms_modulate(x1, g_ref[...], m[3:4], m[4:5])
        hb = h.astype(BF16)
        h_ref[rows, :] = _pack_bf16_pair(h)

        logits = _dot(hb, rw)
        logits_t = logits.T[:N_EXPERTS]
        e0, e1, g0, g1 = _route_rows(logits_t, rb_ref[...])

        eiota = lax.broadcasted_iota(jnp.int32, (N_EXPERTS, ROUTE_BLK), 0)
        oh0 = eiota == e0
        oh1 = eiota == e1
        oh = jnp.where(oh0 | oh1, 1.0, 0.0)
        before = _dot(oh.astype(BF16), tri_ref[...]) + carry
        rank0 = jnp.sum(jnp.where(oh0, before, 0.0), axis=0, keepdims=True)
        rank1 = jnp.sum(jnp.where(oh1, before, 0.0), axis=0, keepdims=True)
        carry = carry + jnp.sum(oh, axis=1, keepdims=True)

        ri_ref[0:1, rows] = e0
        ri_ref[1:2, rows] = e1
        ri_ref[2:3, rows] = rank0.astype(jnp.int32)
        ri_ref[3:4, rows] = rank1.astype(jnp.int32)
        rf_ref[0:1, rows] = g0
        rf_ref[1:2, rows] = g1
    carry_ref[...] = carry
    cnt_ref[...] = jnp.broadcast_to(carry, cnt_ref.shape)


def _oproj_route(o, o_ctx, x, mod_l, w_o, layer, g, rw_pad, rb_col, tri, n_tok):
    n_tiles = n_tok // TM
    tok = lambda i: (i, 0)
    const2 = lambda i: (0, 0)
    lane_tok = lambda i: (0, i)
    return pl.pallas_call(
        _oproj_kernel,
        grid=(n_tiles,),
        in_specs=[
            pl.BlockSpec((TM, D_MODEL), lambda i: (jnp.minimum(i, LAT_TILES - 1), 0)),
            pl.BlockSpec((TM, D_MODEL), lambda i: (jnp.maximum(i - LAT_TILES, 0), 0)),
            pl.BlockSpec((TM, D_MODEL), tok),
            pl.BlockSpec((1, N_MOD, D_MODEL), lambda i: (_mod_row(i), 0, 0)),
            pl.BlockSpec((1, D_MODEL, D_MODEL), lambda i: (layer, 0, 0)),
            pl.BlockSpec((1, D_MODEL), const2),
            pl.BlockSpec((D_MODEL, HB), const2),
            pl.BlockSpec((N_EXPERTS, 1), const2),
            pl.BlockSpec((ROUTE_BLK, ROUTE_BLK), const2),
        ],
        out_specs=[
            pl.BlockSpec((TM, D_MODEL), tok),
            pl.BlockSpec((TM, D_PACK), tok),
            pl.BlockSpec((8, TM), lane_tok),
            pl.BlockSpec((8, TM), lane_tok),
            pl.BlockSpec((N_EXPERTS, HB), const2),
        ],
        out_shape=[
            jax.ShapeDtypeStruct((n_tok, D_MODEL), F32),
            jax.ShapeDtypeStruct((n_tok, D_PACK), jnp.uint32),
            jax.ShapeDtypeStruct((8, n_tok), jnp.int32),
            jax.ShapeDtypeStruct((8, n_tok), F32),
            jax.ShapeDtypeStruct((N_EXPERTS, HB), F32),
        ],
        scratch_shapes=[pltpu.VMEM((N_EXPERTS, 1), F32)],
        compiler_params=_cparams("arbitrary"),
        name="oproj_norm_route",
    )(o, o if o_ctx is None else o_ctx, x, mod_l, w_o, g.reshape(1, D_MODEL), rw_pad, rb_col, tri)


def _moe_kernel(te_ref, tv_ref, xs_ref, wg_ref, wu_ref, wd_ref, ys_ref, wgb, wub, wdb):
    i = pl.program_id(0)
    valid = tv_ref[i]
    e = te_ref[i]
    prev = te_ref[jnp.maximum(i - 1, 0)]
    half = TM // 2

    @pl.when((i == 0) | (e != prev))
    def _():
        wgb[...] = wg_ref[0, 0].astype(BF16)
        wub[...] = wu_ref[0, 0].astype(BF16)
        wdb[...] = wd_ref[0, 0].astype(BF16)

    def ffn(rows, x):
        x = _unpack_bf16_pair(x).astype(BF16)
        a = _dot(x, wgb[...])
        u = _dot(x, wub[...])
        hm = (a * _sigmoid(a) * u).astype(BF16)
        ys_ref[rows, :] = _pack_bf16_pair(_dot(hm, wdb[...]))

    @pl.when(valid == TM)
    def _():
        for r0 in range(0, TM, half):
            ffn(pl.ds(r0, half), xs_ref[pl.ds(r0, half), :])

    for r0 in range(0, TM, half):
        rows = pl.ds(r0, half)

        @pl.when((valid < TM) & (valid >= r0 + half))
        def _():
            ffn(rows, xs_ref[rows, :])

        @pl.when((valid > r0) & (valid < r0 + half))
        def _():
            row = r0 + lax.broadcasted_iota(jnp.int32, (half, 1), 0)
            x = xs_ref[rows, :]
            ffn(rows, jnp.where(row < valid, x, jnp.zeros_like(x)))

        @pl.when(valid <= r0)
        def _():
            ys_ref[rows, :] = jnp.zeros((half, D_PACK), jnp.uint32)


def _grouped_ffn(tile_expert, tile_valid, xs, w_gate, w_up, w_down, layer):
    n_tiles = xs.shape[0] // TM
    wmap = lambda i, te, tv: (layer, te[i], 0, 0)
    tok = lambda i, te, tv: (i, 0)
    return pl.pallas_call(
        _moe_kernel,
        grid_spec=pltpu.PrefetchScalarGridSpec(
            num_scalar_prefetch=2,
            grid=(n_tiles,),
            in_specs=[
                pl.BlockSpec((TM, D_PACK), tok),
                pl.BlockSpec((1, 1, D_MODEL, D_EXPERT), wmap),
                pl.BlockSpec((1, 1, D_MODEL, D_EXPERT), wmap),
                pl.BlockSpec((1, 1, D_EXPERT, D_MODEL), wmap),
            ],
            out_specs=pl.BlockSpec((TM, D_PACK), tok),
            scratch_shapes=[
                pltpu.VMEM((D_MODEL, D_EXPERT), BF16),
                pltpu.VMEM((D_MODEL, D_EXPERT), BF16),
                pltpu.VMEM((D_EXPERT, D_MODEL), BF16),
            ],
        ),
        out_shape=jax.ShapeDtypeStruct((xs.shape[0], D_PACK), jnp.uint32),
        compiler_params=_cparams("arbitrary"),
        name="grouped_ffn",
    )(tile_expert, tile_valid, xs, w_gate, w_up, w_down)


def _moe_layout(route_i, counts, n_tok):
    n_tiles = 2 * n_tok // TM + N_EXPERTS
    counts = counts.astype(jnp.int32)
    tiles_e = (counts + TM - 1) // TM
    tiles_end = jnp.cumsum(tiles_e)
    tile_start = tiles_end - tiles_e
    eid = jnp.arange(N_EXPERTS, dtype=jnp.int32)
    tok_oh = route_i[0:2][:, :, None] == eid
    pos = jnp.sum(jnp.where(tok_oh, tile_start * TM, 0), axis=-1) + route_i[2:4]
    tile_ids = jnp.arange(n_tiles, dtype=jnp.int32)
    n_used = tiles_end[-1]
    te = jnp.sum(tiles_end[None, :] <= jnp.minimum(tile_ids, n_used - 1)[:, None], axis=1).astype(jnp.int32)
    tile_oh = te[:, None] == eid
    cnt_t = jnp.sum(jnp.where(tile_oh, counts, 0), axis=1)
    start_t = jnp.sum(jnp.where(tile_oh, tile_start, 0), axis=1)
    tv = jnp.clip(cnt_t - (tile_ids - start_t) * TM, 0, TM)
    tv = jnp.where(tile_ids < n_used, tv, 0).astype(jnp.int32)
    return pos.astype(jnp.int32), te, tv, n_tiles


SC_CORES = 2
SC_SUBCORES = 16
SC_WORKERS = SC_CORES * SC_SUBCORES
SC_CHUNK = 32
SC_RING = 4


def _sc_mesh():
    return plsc.VectorSubcoreMesh(core_axis_name="c", subcore_axis_name="s")


def _sc_worker_indices(pos, n_tok):
    n_ch = n_tok // SC_WORKERS // SC_CHUNK
    return pos.reshape(2, SC_WORKERS, n_ch, SC_CHUNK).transpose(1, 2, 0, 3), n_ch


def _sc_dispatch(h, pos, n_slots):
    n_tok = h.shape[0]
    per_w = n_tok // SC_WORKERS
    pos_w, n_ch = _sc_worker_indices(pos, n_tok)
    assert n_ch % 2 == 0 and n_ch * SC_CHUNK * SC_WORKERS == n_tok

    @functools.partial(
        pl.kernel, mesh=_sc_mesh(), out_type=jax.ShapeDtypeStruct((n_slots, h.shape[1]), h.dtype),
        scratch_types=[pltpu.VMEM((n_ch, 2, SC_CHUNK), jnp.int32), pltpu.VMEM((2, SC_CHUNK, h.shape[1]), h.dtype),
                       pltpu.SemaphoreType.DMA((2,)), pltpu.SemaphoreType.DMA((2,))],
        name="sc_dispatch")
    def dispatch(h_hbm, pos_hbm, xs_hbm, idx_v, rows_v, load_sem, scat_sem):
        wid = lax.axis_index("s") * SC_CORES + lax.axis_index("c")
        base = wid * per_w
        pltpu.sync_copy(pos_hbm.at[wid], idx_v)

        def load(c, b):
            return pltpu.make_async_copy(h_hbm.at[pl.ds(base + c * SC_CHUNK, SC_CHUNK)], rows_v.at[b],
                                         load_sem.at[b])

        def scat(c, b, k):
            return pltpu.make_async_copy(rows_v.at[b], xs_hbm.at[idx_v.at[c, k]], scat_sem.at[b])

        load(0, 0).start()

        @pl.loop(0, n_ch, step=2)
        def _(c0):
            for b in range(2):
                c = c0 + b
                load(c, b).wait()
                scat(c, b, 0).start()
                scat(c, b, 1).start()

                @pl.when(c >= 1)
                def _():
                    scat(c - 1, 1 - b, 0).wait()
                    scat(c - 1, 1 - b, 1).wait()

                @pl.when(c + 1 < n_ch)
                def _():
                    load(c + 1, 1 - b).start()

        scat(n_ch - 1, 1, 0).wait()
        scat(n_ch - 1, 1, 1).wait()

    return dispatch(h, pos_w)


def _sc_combine_gather(ys, pos):
    n_tok = pos.shape[1]
    per_w = n_tok // SC_WORKERS
    pos_w, n_ch = _sc_worker_indices(pos, n_tok)
    n_items = 2 * n_ch
    assert n_ch * SC_CHUNK * SC_WORKERS == n_tok and n_items % SC_RING == 0 and SC_RING % 2 == 0

    @functools.partial(
        pl.kernel, mesh=_sc_mesh(), out_type=jax.ShapeDtypeStruct((2, n_tok, ys.shape[1]), ys.dtype),
        scratch_types=[pltpu.VMEM((n_ch, 2, SC_CHUNK), jnp.int32),
                       pltpu.VMEM((SC_RING, SC_CHUNK, ys.shape[1]), ys.dtype),
                       pltpu.SemaphoreType.DMA((SC_RING,)), pltpu.SemaphoreType.DMA((SC_RING,))],
        name="sc_combine_gather")
    def gather(ys_hbm, pos_hbm, yg_hbm, idx_v, rows_v, gath_sem, write_sem):
        wid = lax.axis_index("s") * SC_CORES + lax.axis_index("c")
        base = wid * per_w
        pltpu.sync_copy(pos_hbm.at[wid], idx_v)

        def gath(c, k, b):
            return pltpu.make_async_copy(ys_hbm.at[idx_v.at[c, k]], rows_v.at[b], gath_sem.at[b])

        def write(c, k, b):
            return pltpu.make_async_copy(rows_v.at[b], yg_hbm.at[k, pl.ds(base + c * SC_CHUNK, SC_CHUNK)],
                                         write_sem.at[b])

        for j in range(SC_RING - 1):
            gath(j // 2, j % 2, j).start()

        @pl.loop(0, n_items, step=SC_RING)
        def _(j0):
            for b in range(SC_RING):
                j = j0 + b
                gath(j // 2, b % 2, b).wait()
                write(j // 2, b % 2, b).start()
                pb = (b - 1) % SC_RING

                @pl.when(j >= 1)
                def _():
                    write((j - 1) // 2, (b - 1) % 2, pb).wait()

                @pl.when(j + SC_RING - 1 < n_items)
                def _():
                    gath((j + SC_RING - 1) // 2, (b - 1) % 2, pb).start()

        write(n_ch - 1, 1, (n_items - 1) % SC_RING).wait()

    return gather(ys, pos_w)


def _final_kernel(x1_ref, yg_ref, gt_ref, mod_ref, fg_ref, out_ref):
    x2 = _moe_residual(x1_ref[...], yg_ref[0], yg_ref[1], gt_ref[...], mod_ref[0][5:6])
    ms = jnp.mean(x2 * x2, axis=-1, keepdims=True)
    out_ref[...] = x2 * lax.rsqrt(ms + NORM_EPS) * fg_ref[...]


def _combine_final(x1, yg, gates_t, mod_l, final_g, n_tok):
    tok = lambda i: (i, 0)
    return pl.pallas_call(
        _final_kernel,
        grid=(n_tok // TM,),
        in_specs=[
            pl.BlockSpec((TM, D_MODEL), tok),
            pl.BlockSpec((2, TM, D_PACK), lambda i: (0, i, 0)),
            pl.BlockSpec((TM, 2), tok),
            pl.BlockSpec((1, N_MOD, D_MODEL), lambda i: (_mod_row(i), 0, 0)),
            pl.BlockSpec((1, D_MODEL), lambda i: (0, 0)),
        ],
        out_specs=pl.BlockSpec((TM, D_MODEL), tok),
        out_shape=jax.ShapeDtypeStruct((n_tok, D_MODEL), F32),
        compiler_params=_cparams("parallel"),
        name="moe_combine_final",
    )(x1, yg, gates_t, mod_l, final_g.reshape(1, D_MODEL))


def _diff_lambda_init(layer_idx):
    return 0.8 - 0.6 * math.exp(-0.3 * (layer_idx - 1))


def kernel(x, c, ctx, c_ctx, mod_w, mod_b, norm_mix, norm_ffn, w_qkv, w_o, na_rpb, diff_lambda, diff_subln,
           router_w, router_b, expert_w_gate, expert_w_up, expert_w_down, final_norm):
    mod = _modulation(c, c_ctx, mod_w, mod_b)
    stream = (x.reshape(T_LAT, D_MODEL), ctx.reshape(T_CTX, D_MODEL))
    w_qk_b, w_vt_b = _qkv_weights(w_qkv)
    w_o_b = w_o.astype(BF16)
    na_tables = _na_pair_table(na_rpb.reshape(-1, N_DR, N_DC))
    cs, sn = _rope_tables()
    rw_pad = jnp.zeros((D_MODEL, HB), F32).at[:, :N_EXPERTS].set(router_w)
    rb_col = router_b.reshape(N_EXPERTS, 1).astype(F32)
    blk = jnp.arange(ROUTE_BLK)
    tri = (blk[:, None] < blk[None, :]).astype(BF16)

    for i in range(DEPTH):
        last = i == DEPTH - 1
        j = i // 2
        is_diff = i % 2 == 1
        xa, q, k, v = _qkv(stream, mod[i], norm_mix[i], w_qk_b, w_vt_b, i, cs, sn, rope=is_diff)
        if not is_diff:
            o, o_ctx = _na_attention(na_tables, j, q, k, v)
        else:
            lam_init = _diff_lambda_init(i + 1)
            sg = diff_subln[j].reshape(HB, 1)
            outs = _diff_attention(diff_lambda[j], sg, q, k, v, lam_init, ctx_out=not last)
            o, o_ctx = outs if not last else (outs[0], None)
        n_tok = T_LAT if last else T_ALL
        x1, h, route_i, route_f, cnt = _oproj_route(o, o_ctx, xa, mod[i], w_o_b, i, norm_ffn[i], rw_pad,
                                                    rb_col, tri, n_tok)
        pos, te, tv, n_tiles = _moe_layout(route_i, cnt[:, 0], n_tok)
        xs = _sc_dispatch(h, pos, n_tiles * TM)
        ys = _grouped_ffn(te, tv, xs, expert_w_gate, expert_w_up, expert_w_down, i)
        yg = _sc_combine_gather(ys, pos)
        stream = (x1, yg, route_f[0:2].T, mod[i])
    out = _combine_final(*stream, final_norm, T_LAT)
    return out.reshape(BATCH, SEQ, D_MODEL)
```

```python
import functools
import math

import jax
import jax.numpy as jnp
import numpy as np
from jax import lax
from jax.experimental import pallas as pl
from jax.experimental.pallas import tpu as pltpu
from jax.experimental.pallas import tpu_sc as plsc

F32 = jnp.float32
BF16 = jnp.bfloat16

D_MODEL = 1024
BATCH = 8
SEQ = 2048
DEPTH = 4
CTX_LEN = 256
GRID_W = 64
ROWS = SEQ // GRID_W
NA_HEADS = 16
NA_WIN_ROWS = 8
NA_WIN_COLS = 16
DIFF_HEADS = 8
DIFF_HEAD_DIM = 64
ROPE_THETA = 10000.0
N_EXPERTS = 16
N_GROUPS = 4
EXPERTS_PER_GROUP = 4
D_EXPERT = 1024
N_MOD = 6
NORM_EPS = 1e-6

T_LAT = BATCH * SEQ
T_CTX = BATCH * CTX_LEN
T_ALL = T_LAT + T_CTX
TM = 512
QKV_BLK = TM // 2
ROUTE_BLK = TM // 2
LAT_TILES = T_LAT // TM
CTX_TILES = T_CTX // TM
TILES_PER_BATCH = SEQ // TM
CTX_ROW = BATCH
MOD_ROWS = 16
HB = 128
N_HB = D_MODEL // HB
NA_RB = 4
NA_BAND_ROWS = 12
NA_Q = NA_RB * GRID_W
NA_BAND = NA_BAND_ROWS * GRID_W
TQ_DIFF = 512
MASK_VALUE = -1e30
LOG2E = math.log2(math.e)
Q_SCALE = 0.125 * LOG2E
VMEM_LIMIT = 56 * 1024 * 1024


def _cparams(*sem):
    return pltpu.CompilerParams(dimension_semantics=sem, vmem_limit_bytes=VMEM_LIMIT)


def _dot(a, b):
    return jnp.dot(a, b, preferred_element_type=F32)


def _dot_nt(a, b):
    return lax.dot_general(a, b, (((1,), (1,)), ((), ())), preferred_element_type=F32)


def _sigmoid(x):
    return 1.0 / (1.0 + jnp.exp(-x))


D_PACK = D_MODEL // 2


def _pack_bf16_pair(x):
    lo = lax.bitcast_convert_type(x[:, :D_PACK].astype(BF16).astype(F32), jnp.uint32)
    hi = lax.bitcast_convert_type(x[:, D_PACK:].astype(BF16).astype(F32), jnp.uint32)
    return (lo >> 16) | hi


def _unpack_bf16_pair(w):
    lo = lax.bitcast_convert_type(w << 16, F32)
    hi = lax.bitcast_convert_type(w & jnp.uint32(0xFFFF0000), F32)
    return jnp.concatenate([lo, hi], axis=1)


def _mod_row(i):
    return jnp.minimum(i // TILES_PER_BATCH, CTX_ROW)


def _mod_kernel(act_ref, w_ref, b_ref, o_ref):
    a = act_ref[...]
    a = a * _sigmoid(a)
    o_ref[0] = _dot(a.astype(BF16), w_ref[0].astype(BF16)) + b_ref[0]


def _modulation(c, c_ctx, mod_w, mod_b):
    tn = 1536
    act = jnp.zeros((MOD_ROWS, D_MODEL), F32).at[:BATCH].set(c).at[CTX_ROW].set(c_ctx)
    out = pl.pallas_call(
        _mod_kernel,
        grid=(DEPTH, N_MOD * D_MODEL // tn),
        in_specs=[
            pl.BlockSpec((MOD_ROWS, D_MODEL), lambda l, j: (0, 0)),
            pl.BlockSpec((1, D_MODEL, tn), lambda l, j: (l, 0, j)),
            pl.BlockSpec((1, 1, tn), lambda l, j: (l, 0, j)),
        ],
        out_specs=pl.BlockSpec((1, MOD_ROWS, tn), lambda l, j: (l, 0, j)),
        out_shape=jax.ShapeDtypeStruct((DEPTH, MOD_ROWS, N_MOD * D_MODEL), F32),
        compiler_params=_cparams("parallel", "parallel"),
        name="adaln_mod",
    )(act, mod_w, mod_b.reshape(DEPTH, 1, N_MOD * D_MODEL))
    return out.reshape(DEPTH, MOD_ROWS, N_MOD, D_MODEL)


def _qkv_weight_kernel(w_ref, wqk_ref, wvt_ref):
    j = pl.program_id(1)

    @pl.when(j < 2)
    def _():
        wqk_ref[0] = w_ref[0].astype(BF16)

    @pl.when(j == 2)
    def _():
        wvt_ref[0] = w_ref[0].T.astype(BF16)


def _qkv_weights(w_qkv):
    return pl.pallas_call(
        _qkv_weight_kernel,
        grid=(DEPTH, 3),
        in_specs=[pl.BlockSpec((1, D_MODEL, D_MODEL), lambda l, j: (l, 0, j))],
        out_specs=[
            pl.BlockSpec((1, D_MODEL, D_MODEL), lambda l, j: (l, 0, jnp.minimum(j, 1))),
            pl.BlockSpec((1, D_MODEL, D_MODEL), lambda l, j: (l, 0, 0)),
        ],
        out_shape=[jax.ShapeDtypeStruct((DEPTH, D_MODEL, 2 * D_MODEL), BF16),
                   jax.ShapeDtypeStruct((DEPTH, D_MODEL, D_MODEL), BF16)],
        compiler_params=_cparams("parallel", "arbitrary"),
        name="qkv_weight_prep",
    )(w_qkv)


def _rms_modulate(x, g, shift, scale):
    ms = jnp.mean(x * x, axis=-1, keepdims=True)
    y = x * lax.rsqrt(ms + NORM_EPS) * g
    return y * (1.0 + scale) + shift


def _rope_block(xb, cs, sn):
    lane = lax.broadcasted_iota(jnp.int32, xb.shape, 1)
    partner = jnp.where((lane & 63) < 32, pltpu.roll(xb, 96, 1), pltpu.roll(xb, 32, 1))
    return xb * cs + partner * sn


def _moe_residual(x1, y0, y1, g, gate_ffn):
    y = g[:, 0:1] * _unpack_bf16_pair(y0) + g[:, 1:2] * _unpack_bf16_pair(y1)
    return x1 + gate_ffn * y


def _qkv_kernel(*refs, rope, first):
    n_stream = 2 if first else 4
    stream_refs, refs = refs[:n_stream], refs[n_stream:]
    mod_ref, g_ref, wqk_ref, wvt_ref, cs_ref, sn_ref, x_ref, q_ref, k_ref, vt_ref = refs
    m = mod_ref[0]
    for r0 in range(0, TM, QKV_BLK):
        rows = pl.ds(r0, QKV_BLK)
        if first:
            xl_ref, xc_ref = stream_refs
            x = jnp.where(pl.program_id(0) < LAT_TILES, xl_ref[rows, :], xc_ref[rows, :])
        else:
            x1_ref, yg_ref, gt_ref, modp_ref = stream_refs
            x = _moe_residual(x1_ref[rows, :], yg_ref[0, rows, :], yg_ref[1, rows, :], gt_ref[rows, :],
                              modp_ref[0][5:6])
        x_ref[rows, :] = x
        h = _rms_modulate(x, g_ref[...], m[0:1], m[1:2]).astype(BF16)
        for idx, out in enumerate((q_ref, k_ref)):
            acc = _dot(h, wqk_ref[0, :, idx * D_MODEL:(idx + 1) * D_MODEL])
            if rope:
                cs = cs_ref[rows, :]
                sn = sn_ref[rows, :]
                acc = jnp.concatenate(
                    [_rope_block(acc[:, j * HB:(j + 1) * HB], cs, sn) for j in range(N_HB)], axis=1)
            if idx == 0:
                acc = acc * Q_SCALE
            out[rows, :] = acc.astype(BF16)
        vt_ref[:, rows] = _dot_nt(wvt_ref[0], h).astype(BF16)


def _qkv(stream, mod_l, g, w_qk, w_vt, layer, cs, sn, *, rope):
    first = len(stream) == 2
    n_tiles = T_ALL // TM
    tok = lambda i: (i, 0)
    const2 = lambda i: (0, 0)
    mod_spec = pl.BlockSpec((1, N_MOD, D_MODEL), lambda i: (_mod_row(i), 0, 0))
    rope_idx = lambda i: (jnp.where(i < LAT_TILES, i % TILES_PER_BATCH, TILES_PER_BATCH), 0)
    out = jax.ShapeDtypeStruct((T_ALL, D_MODEL), BF16)
    wmap = lambda i: (layer, 0, 0)
    if first:
        stream_specs = [
            pl.BlockSpec((TM, D_MODEL), lambda i: (jnp.minimum(i, LAT_TILES - 1), 0)),
            pl.BlockSpec((TM, D_MODEL), lambda i: (jnp.maximum(i - LAT_TILES, 0), 0)),
        ]
    else:
        stream_specs = [
            pl.BlockSpec((TM, D_MODEL), tok),
            pl.BlockSpec((2, TM, D_PACK), lambda i: (0, i, 0)),
            pl.BlockSpec((TM, 2), tok),
            mod_spec,
        ]
    return pl.pallas_call(
        functools.partial(_qkv_kernel, rope=rope, first=first),
        grid=(n_tiles,),
        in_specs=stream_specs + [
            mod_spec,
            pl.BlockSpec((1, D_MODEL), const2),
            pl.BlockSpec((1, D_MODEL, 2 * D_MODEL), wmap),
            pl.BlockSpec((1, D_MODEL, D_MODEL), wmap),
            pl.BlockSpec((TM, HB), rope_idx),
            pl.BlockSpec((TM, HB), rope_idx),
        ],
        out_specs=[pl.BlockSpec((TM, D_MODEL), tok), pl.BlockSpec((TM, D_MODEL), tok),
                   pl.BlockSpec((TM, D_MODEL), tok), pl.BlockSpec((D_MODEL, TM), lambda i: (0, i))],
        out_shape=[jax.ShapeDtypeStruct((T_ALL, D_MODEL), F32), out, out,
                   jax.ShapeDtypeStruct((D_MODEL, T_ALL), BF16)],
        compiler_params=_cparams("parallel"),
        name="norm_qkv_rope" if rope else "norm_qkv",
    )(*stream, mod_l, g.reshape(1, D_MODEL), w_qk, w_vt, cs, sn)


def _rope_tables():
    t = np.arange(SEQ)
    row = (t // GRID_W).astype(np.float32)
    col = (t % GRID_W).astype(np.float32)
    n_freq = DIFF_HEAD_DIM // 4
    inv_freq = np.float32(ROPE_THETA) ** (-np.arange(n_freq, dtype=np.float32) / np.float32(n_freq))
    ang = np.concatenate([row[:, None] * inv_freq, col[:, None] * inv_freq], axis=-1)
    cos, sin = np.cos(ang), np.sin(ang)
    cs = np.concatenate([cos, cos, cos, cos], axis=-1)
    sn = np.concatenate([-sin, sin, -sin, sin], axis=-1)
    cs = np.concatenate([cs, np.ones((TM, HB), np.float32)], axis=0)
    sn = np.concatenate([sn, np.zeros((TM, HB), np.float32)], axis=0)
    return jnp.asarray(cs, F32), jnp.asarray(sn, F32)


Q_BLK = 256
KEY_BLK = 256
SUBLANES = 8
DIFF_Q_BLK = 256
DIFF_Q_PER_TRIP = 8
DIFF_KEY_BLK = 1024


def _colwise(reduce_fn, x):
    return reduce_fn(x.reshape(x.shape[0] // SUBLANES, SUBLANES, x.shape[1]), axis=0)


def _key_blocks(k_ref, vt_ref, key_start=0, n_keys=None, bias=None, blk=KEY_BLK):
    n_keys = k_ref.shape[0] if n_keys is None else n_keys
    blk = min(blk, n_keys)
    blocks = []
    for c, s0 in enumerate(range(0, n_keys, blk)):
        blocks.append((
            lambda s0=s0: k_ref[pl.ds(key_start + s0, blk), :],
            lambda s0=s0: vt_ref[:, pl.ds(key_start + s0, blk)],
            None if bias is None else functools.partial(bias, c),
            blk,
        ))
    return blocks


def _attention_t(operands, s_scr):
    def rows_of(u, c):
        sizes = [blk[3] for blk in operands[u][1]]
        return slice(sum(sizes[:c]), sum(sizes[:c + 1]))

    def score(u, c, mx):
        qm, blocks = operands[u]
        k_tile, _, bias, _ = blocks[c]
        st = _dot_nt(k_tile(), qm)
        if bias is not None:
            st = st + bias()
        s_scr[u % 2, rows_of(u, c), :n_q] = st
        return jnp.maximum(mx, _colwise(jnp.max, st))

    def expo(u, c, m, acc, den):
        vt_tile = operands[u][1][c][1]
        e = jnp.exp2(s_scr[u % 2, rows_of(u, c), :n_q] - m)
        o = _dot(vt_tile(), e.astype(BF16))
        return (o if acc is None else acc + o), den + _colwise(jnp.sum, e)

    n_q = operands[0][0].shape[0]
    neg = jnp.full((SUBLANES, n_q), -jnp.inf, F32)
    zero = jnp.zeros((SUBLANES, n_q), F32)
    n_blocks = [len(blocks) for _, blocks in operands]
    mx = neg
    for c in range(n_blocks[0]):
        mx = score(0, c, mx)
    outs = []
    for u in range(len(operands)):
        m = jnp.max(mx, axis=0, keepdims=True)
        acc, den, mx = None, zero, neg
        n_next = n_blocks[u + 1] if u + 1 < len(operands) else 0
        for c in range(max(n_blocks[u], n_next)):
            if c < n_next:
                mx = score(u + 1, c, mx)
            if c < n_blocks[u]:
                acc, den = expo(u, c, m, acc, den)
        outs.append(acc / jnp.sum(den, axis=0, keepdims=True))
    return outs


def _half_masks(shape):
    lane = lax.broadcasted_iota(jnp.int32, shape, 1)
    return lane < 64, lane >= 64


N_DR = 2 * NA_WIN_ROWS - 1
N_DC = 2 * NA_WIN_COLS - 1
PAIR_MASK_FIRST = N_DR - 1
PAIR_MASK_SECOND = N_DR
PAIR_MASKED = N_DR + 1
N_PAIRS = N_DR + 2
NA_KEY_BLK = 768
DR_LO = NA_WIN_ROWS - 1 - NA_WIN_ROWS // 2
DR_HI = DR_LO + NA_WIN_ROWS - 1


def _na_pair_table(rpb):
    h = rpb.shape[0]
    w = GRID_W
    col = np.arange(w)
    dc = np.clip(col[:, None] - col[None, :] + NA_WIN_COLS - 1, 0, N_DC - 1)
    onehot = (dc[:, :, None] == np.arange(N_DC)).astype(np.float32)
    t = jnp.einsum("hdc,kqc->hdkq", rpb.astype(F32), onehot, precision=lax.Precision.HIGHEST)
    cstart = np.clip(col - NA_WIN_COLS // 2, 0, w - NA_WIN_COLS)
    cvalid = (col[:, None] >= cstart[None, :]) & (col[:, None] < cstart[None, :] + NA_WIN_COLS)
    tt = jnp.where(cvalid, t * LOG2E, MASK_VALUE)
    masked = jnp.full((h, 1, w, w), MASK_VALUE, F32)
    regular = jnp.concatenate([tt[:, 1:], tt[:, :-1]], axis=-1)
    first = jnp.concatenate([masked, tt[:, DR_HI:DR_HI + 1]], axis=-1)
    second = jnp.concatenate([tt[:, DR_LO:DR_LO + 1], masked], axis=-1)
    return jnp.concatenate([regular, first, second, jnp.concatenate([masked, masked], axis=-1)], axis=1)


def _na_pair_entries(q_row0, band_row0, band_rows):
    entries = []
    for j in range(band_rows):
        kr = band_row0 + j
        row = []
        for a in range(0, NA_RB, 2):
            ok, d = [], []
            for r in (q_row0 + a, q_row0 + a + 1):
                r0 = min(max(r - NA_WIN_ROWS // 2, 0), ROWS - NA_WIN_ROWS)
                ok.append(r0 <= kr < r0 + NA_WIN_ROWS)
                d.append(kr - r + NA_WIN_ROWS - 1)
            if ok[0] and ok[1]:
                row.append(d[0] - 1)
            elif ok[1]:
                assert d[1] == DR_HI
                row.append(PAIR_MASK_FIRST)
            elif ok[0]:
                assert d[0] == DR_LO
                row.append(PAIR_MASK_SECOND)
            else:
                row.append(PAIR_MASKED)
        entries.append(row)
    return entries


def _na_kernel(tab_ref, q_ref, k_ref, vt_ref, qc_ref, kc_ref, vct_ref, o_ref, oc_ref, s_scr):
    n_blocks = ROWS // NA_RB
    ctx_blocks = _key_blocks(kc_ref, vct_ref)

    def operands(q_start, band_start, band_rows, entries):
        q = q_ref[pl.ds(q_start, NA_Q), :]
        n_band = band_rows * GRID_W
        blk = NA_KEY_BLK if n_band % NA_KEY_BLK == 0 else n_band
        rows_per_chunk = blk // GRID_W
        ops = []
        for hh, msk in enumerate(_half_masks(q.shape)):
            def bias(c, hh=hh):
                rows = entries[c * rows_per_chunk:(c + 1) * rows_per_chunk]
                return jnp.concatenate(
                    [jnp.concatenate([tab_ref[hh, e] for e in row], axis=1) for row in rows], axis=0)

            band = _key_blocks(k_ref, vt_ref, band_start, n_band, bias, blk)
            ops.append((jnp.where(msk, q, jnp.zeros_like(q)), band + ctx_blocks))
        return ops

    def merge_heads(o_lo, o_hi):
        feat = lax.broadcasted_iota(jnp.int32, o_lo.shape, 0)
        return jnp.where(feat < HB // 2, o_lo, o_hi).T.astype(BF16)

    last_band = ROWS - NA_WIN_ROWS
    interior = _na_pair_entries(NA_RB, 0, NA_BAND_ROWS)
    blocks = [(0, 0, NA_WIN_ROWS, _na_pair_entries(0, 0, NA_WIN_ROWS))]
    blocks += [(rb * NA_Q, (rb - 1) * NA_Q, NA_BAND_ROWS, interior) for rb in range(1, n_blocks - 1)]
    blocks.append(((n_blocks - 1) * NA_Q, last_band * GRID_W, NA_WIN_ROWS,
                   _na_pair_entries(ROWS - NA_RB, last_band, NA_WIN_ROWS)))
    qc = qc_ref[...]
    ctx_ops = [(jnp.where(msk, qc, jnp.zeros_like(qc)), ctx_blocks) for msk in _half_masks(qc.shape)]
    outs = _attention_t([op for blk in blocks for op in operands(*blk)] + ctx_ops, s_scr)
    for i, blk in enumerate(blocks):
        o_ref[pl.ds(blk[0], NA_Q), :] = merge_heads(outs[2 * i], outs[2 * i + 1])
    oc_ref[...] = merge_heads(outs[-2], outs[-1])


def _na_attention(tables, layer_na, q, k, vt):
    assert NA_Q == Q_BLK and CTX_LEN == Q_BLK
    lat = lambda hp, b: (b, hp)
    ctx = lambda hp, b: (T_LAT // CTX_LEN + b, hp)
    return pl.pallas_call(
        _na_kernel,
        grid=(N_HB, BATCH),
        in_specs=[
            pl.BlockSpec((2, N_PAIRS, GRID_W, 2 * GRID_W), lambda hp, b: (layer_na * N_HB + hp, 0, 0, 0)),
            pl.BlockSpec((SEQ, HB), lat),
            pl.BlockSpec((SEQ, HB), lat),
            pl.BlockSpec((HB, SEQ), lambda hp, b: (hp, b)),
            pl.BlockSpec((CTX_LEN, HB), ctx),
            pl.BlockSpec((CTX_LEN, HB), ctx),
            pl.BlockSpec((HB, CTX_LEN), lambda hp, b: (hp, T_LAT // CTX_LEN + b)),
        ],
        out_specs=[pl.BlockSpec((SEQ, HB), lat), pl.BlockSpec((CTX_LEN, HB), lat)],
        out_shape=[jax.ShapeDtypeStruct((T_LAT, D_MODEL), BF16), jax.ShapeDtypeStruct((T_CTX, D_MODEL), BF16)],
        scratch_shapes=[pltpu.VMEM((2, NA_BAND + CTX_LEN, Q_BLK), F32)],
        compiler_params=_cparams("parallel", "parallel"),
        name="na_attention",
    )(tables, q, k, vt, q, k, vt)


def _diff_lambda(lam_ref, lam_init):
    lam = lam_ref[...]
    a = jnp.sum(lam[0:1] * lam[1:2], axis=-1, keepdims=True)
    b = jnp.sum(lam[2:3] * lam[3:4], axis=-1, keepdims=True)
    return jnp.exp(a) - jnp.exp(b) + lam_init


def _diff_kernel(lam_ref, sg_ref, q_ref, k_ref, vt_ref, qc_ref, kc_ref, vct_ref, o_ref, *rest, lam_init):
    *maybe_oc_ref, s_scr = rest
    lam = _diff_lambda(lam_ref, lam_init)
    sg_col = sg_ref[...]

    def rows(qs, pieces):
        blocks = [blk for k_r, vt_r in pieces for blk in _key_blocks(k_r, vt_r, blk=DIFF_KEY_BLK)]
        outs = _attention_t([(jnp.where(msk, q, jnp.zeros_like(q)), blocks)
                             for q in qs for msk in _half_masks(q.shape)], s_scr)
        res = []
        for o1, o2 in zip(outs[0::2], outs[1::2]):
            ot = o1 - lam * o2
            ms = jnp.mean(ot * ot, axis=0, keepdims=True)
            ot = ot * lax.rsqrt(ms + NORM_EPS) * sg_col * (1.0 - lam_init)
            res.append(ot.T.astype(BF16))
        return res

    def body(t, carry):
        starts = [pl.multiple_of((t * DIFF_Q_PER_TRIP + j) * DIFF_Q_BLK, DIFF_Q_BLK)
                  for j in range(DIFF_Q_PER_TRIP)]
        outs = rows([q_ref[pl.ds(r0, DIFF_Q_BLK), :] for r0 in starts], [(k_ref, vt_ref), (kc_ref, vct_ref)])
        for r0, o in zip(starts, outs):
            o_ref[pl.ds(r0, DIFF_Q_BLK), :] = o
        return carry

    lax.fori_loop(0, SEQ // (DIFF_Q_BLK * DIFF_Q_PER_TRIP), body, 0)
    if maybe_oc_ref:
        oc_ref, = maybe_oc_ref
        oc_ref[...], = rows([qc_ref[...]], [(kc_ref, vct_ref)])


def _diff_attention(lam, sg, q, k, vt, lam_init, *, ctx_out):
    lat = lambda b, h: (b, h)
    ctx = lambda b, h: (T_LAT // CTX_LEN + b, h)
    lat_t = lambda b, h: (h, b)
    ctx_t = lambda b, h: (h, T_LAT // CTX_LEN + b)
    out_specs = [pl.BlockSpec((SEQ, HB), lat)]
    out_shape = [jax.ShapeDtypeStruct((T_LAT, D_MODEL), BF16)]
    if ctx_out:
        out_specs.append(pl.BlockSpec((CTX_LEN, HB), lat))
        out_shape.append(jax.ShapeDtypeStruct((T_CTX, D_MODEL), BF16))
    return pl.pallas_call(
        functools.partial(_diff_kernel, lam_init=lam_init),
        grid=(BATCH, DIFF_HEADS),
        in_specs=[
            pl.BlockSpec(lam.shape, lambda b, h: (0, 0)),
            pl.BlockSpec(sg.shape, lambda b, h: (0, 0)),
            pl.BlockSpec((SEQ, HB), lat),
            pl.BlockSpec((SEQ, HB), lat),
            pl.BlockSpec((HB, SEQ), lat_t),
            pl.BlockSpec((CTX_LEN, HB), ctx),
            pl.BlockSpec((CTX_LEN, HB), ctx),
            pl.BlockSpec((HB, CTX_LEN), ctx_t),
        ],
        out_specs=out_specs,
        out_shape=out_shape,
        scratch_shapes=[pltpu.VMEM((2, SEQ + CTX_LEN, DIFF_Q_BLK), F32)],
        compiler_params=_cparams("parallel", "parallel"),
        name="diff_attention",
    )(lam, sg, q, k, vt, q, k, vt)


def _route_rows(logits_t, bias_col):
    s = _sigmoid(logits_t)
    sel = s + bias_col
    sel_r = [sel[e:e + 1] for e in range(N_EXPERTS)]
    s_r = [s[e:e + 1] for e in range(N_EXPERTS)]
    group_scores = []
    for g in range(N_GROUPS):
        v = sel_r[g * EXPERTS_PER_GROUP:(g + 1) * EXPERTS_PER_GROUP]
        pairs = [v[a] + v[b] for a in range(EXPERTS_PER_GROUP) for b in range(a + 1, EXPERTS_PER_GROUP)]
        group_scores.append(functools.reduce(jnp.maximum, pairs))
    best = group_scores[0]
    grp = jnp.zeros(best.shape, jnp.int32)
    for g in range(1, N_GROUPS):
        upd = group_scores[g] > best
        best = jnp.where(upd, group_scores[g], best)
        grp = jnp.where(upd, g, grp)

    def pick(rows, j):
        out = rows[j]
        for g in range(1, N_GROUPS):
            out = jnp.where(grp == g, rows[g * EXPERTS_PER_GROUP + j], out)
        return out

    w = [pick(sel_r, j) for j in range(EXPERTS_PER_GROUP)]
    sc = [pick(s_r, j) for j in range(EXPERTS_PER_GROUP)]

    def argmax_first(vals):
        bv = vals[0]
        bi = jnp.zeros(bv.shape, jnp.int32)
        for j in range(1, len(vals)):
            upd = vals[j] > bv
            bv = jnp.where(upd, vals[j], bv)
            bi = jnp.where(upd, j, bi)
        return bi

    i1 = argmax_first(w)
    i2 = argmax_first([jnp.where(i1 == j, -jnp.inf, w[j]) for j in range(EXPERTS_PER_GROUP)])

    def take(vals, idx):
        out = vals[0]
        for j in range(1, len(vals)):
            out = jnp.where(idx == j, vals[j], out)
        return out

    s1 = take(sc, i1)
    s2 = take(sc, i2)
    den = s1 + s2
    return grp * EXPERTS_PER_GROUP + i1, grp * EXPERTS_PER_GROUP + i2, s1 / den, s2 / den


def _oproj_kernel(o_ref, oc_ref, x_ref, mod_ref, w_ref, g_ref, rw_ref, rb_ref, tri_ref,
                  x1_ref, h_ref, ri_ref, rf_ref, cnt_ref, carry_ref):
    i = pl.program_id(0)

    @pl.when(i == 0)
    def _():
        carry_ref[...] = jnp.zeros_like(carry_ref)

    m = mod_ref[0]
    rw = rw_ref[...].astype(BF16)
    ri_ref[...] = jnp.zeros_like(ri_ref)
    rf_ref[...] = jnp.zeros_like(rf_ref)
    carry = carry_ref[...]
    for r0 in range(0, TM, ROUTE_BLK):
        rows = pl.ds(r0, ROUTE_BLK)
        o = jnp.where(i < LAT_TILES, o_ref[rows, :], oc_ref[rows, :])
        x1 = x_ref[rows, :] + m[2:3] * _dot(o, w_ref[0])
        x1_ref[rows, :] = x1
        h = _rms_modulate(x1, g_ref[...], m[3:4], m[4:5])
        hb = h.astype(BF16)
        h_ref[rows, :] = _pack_bf16_pair(h)

        logits = _dot(hb, rw)
        logits_t = logits.T[:N_EXPERTS]
        e0, e1, g0, g1 = _route_rows(logits_t, rb_ref[...])

        eiota = lax.broadcasted_iota(jnp.int32, (N_EXPERTS, ROUTE_BLK), 0)
        oh0 = eiota == e0
        oh1 = eiota == e1
        oh = jnp.where(oh0 | oh1, 1.0, 0.0)
        before = _dot(oh.astype(BF16), tri_ref[...]) + carry
        rank0 = jnp.sum(jnp.where(oh0, before, 0.0), axis=0, keepdims=True)
        rank1 = jnp.sum(jnp.where(oh1, before, 0.0), axis=0, keepdims=True)
        carry = carry + jnp.sum(oh, axis=1, keepdims=True)

        ri_ref[0:1, rows] = e0
        ri_ref[1:2, rows] = e1
        ri_ref[2:3, rows] = rank0.astype(jnp.int32)
        ri_ref[3:4, rows] = rank1.astype(jnp.int32)
        rf_ref[0:1, rows] = g0
        rf_ref[1:2, rows] = g1
    carry_ref[...] = carry
    cnt_ref[...] = jnp.broadcast_to(carry, cnt_ref.shape)


def _oproj_route(o, o_ctx, x, mod_l, w_o, layer, g, rw_pad, rb_col, tri, n_tok):
    n_tiles = n_tok // TM
    tok = lambda i: (i, 0)
    const2 = lambda i: (0, 0)
    lane_tok = lambda i: (0, i)
    return pl.pallas_call(
        _oproj_kernel,
        grid=(n_tiles,),
        in_specs=[
            pl.BlockSpec((TM, D_MODEL), lambda i: (jnp.minimum(i, LAT_TILES - 1), 0)),
            pl.BlockSpec((TM, D_MODEL), lambda i: (jnp.maximum(i - LAT_TILES, 0), 0)),
            pl.BlockSpec((TM, D_MODEL), tok),
            pl.BlockSpec((1, N_MOD, D_MODEL), lambda i: (_mod_row(i), 0, 0)),
            pl.BlockSpec((1, D_MODEL, D_MODEL), lambda i: (layer, 0, 0)),
            pl.BlockSpec((1, D_MODEL), const2),
            pl.BlockSpec((D_MODEL, HB), const2),
            pl.BlockSpec((N_EXPERTS, 1), const2),
            pl.BlockSpec((ROUTE_BLK, ROUTE_BLK), const2),
        ],
        out_specs=[
            pl.BlockSpec((TM, D_MODEL), tok),
            pl.BlockSpec((TM, D_PACK), tok),
            pl.BlockSpec((8, TM), lane_tok),
            pl.BlockSpec((8, TM), lane_tok),
            pl.BlockSpec((N_EXPERTS, HB), const2),
        ],
        out_shape=[
            jax.ShapeDtypeStruct((n_tok, D_MODEL), F32),
            jax.ShapeDtypeStruct((n_tok, D_PACK), jnp.uint32),
            jax.ShapeDtypeStruct((8, n_tok), jnp.int32),
            jax.ShapeDtypeStruct((8, n_tok), F32),
            jax.ShapeDtypeStruct((N_EXPERTS, HB), F32),
        ],
        scratch_shapes=[pltpu.VMEM((N_EXPERTS, 1), F32)],
        compiler_params=_cparams("arbitrary"),
        name="oproj_norm_route",
    )(o, o if o_ctx is None else o_ctx, x, mod_l, w_o, g.reshape(1, D_MODEL), rw_pad, rb_col, tri)


def _moe_kernel(te_ref, tv_ref, xs_ref, wg_ref, wu_ref, wd_ref, ys_ref, wgb, wub, wdb):
    i = pl.program_id(0)
    valid = tv_ref[i]
    e = te_ref[i]
    prev = te_ref[jnp.maximum(i - 1, 0)]
    half = TM // 2

    @pl.when((i == 0) | (e != prev))
    def _():
        wgb[...] = wg_ref[0, 0].astype(BF16)
        wub[...] = wu_ref[0, 0].astype(BF16)
        wdb[...] = wd_ref[0, 0].astype(BF16)

    def ffn(rows, x):
        x = _unpack_bf16_pair(x).astype(BF16)
        a = _dot(x, wgb[...])
        u = _dot(x, wub[...])
        hm = (a * _sigmoid(a) * u).astype(BF16)
        ys_ref[rows, :] = _pack_bf16_pair(_dot(hm, wdb[...]))

    @pl.when(valid == TM)
    def _():
        for r0 in range(0, TM, half):
            ffn(pl.ds(r0, half), xs_ref[pl.ds(r0, half), :])

    for r0 in range(0, TM, half):
        rows = pl.ds(r0, half)

        @pl.when((valid < TM) & (valid >= r0 + half))
        def _():
            ffn(rows, xs_ref[rows, :])

        @pl.when((valid > r0) & (valid < r0 + half))
        def _():
            row = r0 + lax.broadcasted_iota(jnp.int32, (half, 1), 0)
            x = xs_ref[rows, :]
            ffn(rows, jnp.where(row < valid, x, jnp.zeros_like(x)))

        @pl.when(valid <= r0)
        def _():
            ys_ref[rows, :] = jnp.zeros((half, D_PACK), jnp.uint32)


def _grouped_ffn(tile_expert, tile_valid, xs, w_gate, w_up, w_down, layer):
    n_tiles = xs.shape[0] // TM
    wmap = lambda i, te, tv: (layer, te[i], 0, 0)
    tok = lambda i, te, tv: (i, 0)
    return pl.pallas_call(
        _moe_kernel,
        grid_spec=pltpu.PrefetchScalarGridSpec(
            num_scalar_prefetch=2,
            grid=(n_tiles,),
            in_specs=[
                pl.BlockSpec((TM, D_PACK), tok),
                pl.BlockSpec((1, 1, D_MODEL, D_EXPERT), wmap),
                pl.BlockSpec((1, 1, D_MODEL, D_EXPERT), wmap),
                pl.BlockSpec((1, 1, D_EXPERT, D_MODEL), wmap),
            ],
            out_specs=pl.BlockSpec((TM, D_PACK), tok),
            scratch_shapes=[
                pltpu.VMEM((D_MODEL, D_EXPERT), BF16),
                pltpu.VMEM((D_MODEL, D_EXPERT), BF16),
                pltpu.VMEM((D_EXPERT, D_MODEL), BF16),
            ],
        ),
        out_shape=jax.ShapeDtypeStruct((xs.shape[0], D_PACK), jnp.uint32),
        compiler_params=_cparams("arbitrary"),
        name="grouped_ffn",
    )(tile_expert, tile_valid, xs, w_gate, w_up, w_down)


def _moe_layout(route_i, counts, n_tok):
    n_tiles = 2 * n_tok // TM + N_EXPERTS
    counts = counts.astype(jnp.int32)
    tiles_e = (counts + TM - 1) // TM
    tiles_end = jnp.cumsum(tiles_e)
    tile_start = tiles_end - tiles_e
    eid = jnp.arange(N_EXPERTS, dtype=jnp.int32)
    tok_oh = route_i[0:2][:, :, None] == eid
    pos = jnp.sum(jnp.where(tok_oh, tile_start * TM, 0), axis=-1) + route_i[2:4]
    tile_ids = jnp.arange(n_tiles, dtype=jnp.int32)
    n_used = tiles_end[-1]
    te = jnp.sum(tiles_end[None, :] <= jnp.minimum(tile_ids, n_used - 1)[:, None], axis=1).astype(jnp.int32)
    tile_oh = te[:, None] == eid
    cnt_t = jnp.sum(jnp.where(tile_oh, counts, 0), axis=1)
    start_t = jnp.sum(jnp.where(tile_oh, tile_start, 0), axis=1)
    tv = jnp.clip(cnt_t - (tile_ids - start_t) * TM, 0, TM)
    tv = jnp.where(tile_ids < n_used, tv, 0).astype(jnp.int32)
    return pos.astype(jnp.int32), te, tv, n_tiles


SC_CORES = 2
SC_SUBCORES = 16
SC_WORKERS = SC_CORES * SC_SUBCORES
SC_CHUNK = 32
SC_RING = 4


def _sc_mesh():
    return plsc.VectorSubcoreMesh(core_axis_name="c", subcore_axis_name="s")


def _sc_worker_indices(pos, n_tok):
    n_ch = n_tok // SC_WORKERS // SC_CHUNK
    return pos.reshape(2, SC_WORKERS, n_ch, SC_CHUNK).transpose(1, 2, 0, 3), n_ch


def _sc_dispatch(h, pos, n_slots):
    n_tok = h.shape[0]
    per_w = n_tok // SC_WORKERS
    pos_w, n_ch = _sc_worker_indices(pos, n_tok)
    assert n_ch % 2 == 0 and n_ch * SC_CHUNK * SC_WORKERS == n_tok

    @functools.partial(
        pl.kernel, mesh=_sc_mesh(), out_type=jax.ShapeDtypeStruct((n_slots, h.shape[1]), h.dtype),
        scratch_types=[pltpu.VMEM((n_ch, 2, SC_CHUNK), jnp.int32), pltpu.VMEM((2, SC_CHUNK, h.shape[1]), h.dtype),
                       pltpu.SemaphoreType.DMA((2,)), pltpu.SemaphoreType.DMA((2,))],
        name="sc_dispatch")
    def dispatch(h_hbm, pos_hbm, xs_hbm, idx_v, rows_v, load_sem, scat_sem):
        wid = lax.axis_index("s") * SC_CORES + lax.axis_index("c")
        base = wid * per_w
        pltpu.sync_copy(pos_hbm.at[wid], idx_v)

        def load(c, b):
            return pltpu.make_async_copy(h_hbm.at[pl.ds(base + c * SC_CHUNK, SC_CHUNK)], rows_v.at[b],
                                         load_sem.at[b])

        def scat(c, b, k):
            return pltpu.make_async_copy(rows_v.at[b], xs_hbm.at[idx_v.at[c, k]], scat_sem.at[b])

        load(0, 0).start()

        @pl.loop(0, n_ch, step=2)
        def _(c0):
            for b in range(2):
                c = c0 + b
                load(c, b).wait()
                scat(c, b, 0).start()
                scat(c, b, 1).start()

                @pl.when(c >= 1)
                def _():
                    scat(c - 1, 1 - b, 0).wait()
                    scat(c - 1, 1 - b, 1).wait()

                @pl.when(c + 1 < n_ch)
                def _():
                    load(c + 1, 1 - b).start()

        scat(n_ch - 1, 1, 0).wait()
        scat(n_ch - 1, 1, 1).wait()

    return dispatch(h, pos_w)


def _sc_combine_gather(ys, pos):
    n_tok = pos.shape[1]
    per_w = n_tok // SC_WORKERS
    pos_w, n_ch = _sc_worker_indices(pos, n_tok)
    n_items = 2 * n_ch
    assert n_ch * SC_CHUNK * SC_WORKERS == n_tok and n_items % SC_RING == 0 and SC_RING % 2 == 0

    @functools.partial(
        pl.kernel, mesh=_sc_mesh(), out_type=jax.ShapeDtypeStruct((2, n_tok, ys.shape[1]), ys.dtype),
        scratch_types=[pltpu.VMEM((n_ch, 2, SC_CHUNK), jnp.int32),
                       pltpu.VMEM((SC_RING, SC_CHUNK, ys.shape[1]), ys.dtype),
                       pltpu.SemaphoreType.DMA((SC_RING,)), pltpu.SemaphoreType.DMA((SC_RING,))],
        name="sc_combine_gather")
    def gather(ys_hbm, pos_hbm, yg_hbm, idx_v, rows_v, gath_sem, write_sem):
        wid = lax.axis_index("s") * SC_CORES + lax.axis_index("c")
        base = wid * per_w
        pltpu.sync_copy(pos_hbm.at[wid], idx_v)

        def gath(c, k, b):
            return pltpu.make_async_copy(ys_hbm.at[idx_v.at[c, k]], rows_v.at[b], gath_sem.at[b])

        def write(c, k, b):
            return pltpu.make_async_copy(rows_v.at[b], yg_hbm.at[k, pl.ds(base + c * SC_CHUNK, SC_CHUNK)],
                                         write_sem.at[b])

        for j in range(SC_RING - 1):
            gath(j // 2, j % 2, j).start()

        @pl.loop(0, n_items, step=SC_RING)
        def _(j0):
            for b in range(SC_RING):
                j = j0 + b
                gath(j // 2, b % 2, b).wait()
                write(j // 2, b % 2, b).start()
                pb = (b - 1) % SC_RING

                @pl.when(j >= 1)
                def _():
                    write((j - 1) // 2, (b - 1) % 2, pb).wait()

                @pl.when(j + SC_RING - 1 < n_items)
                def _():
                    gath((j + SC_RING - 1) // 2, (b - 1) % 2, pb).start()

        write(n_ch - 1, 1, (n_items - 1) % SC_RING).wait()

    return gather(ys, pos_w)


def _final_kernel(x1_ref, yg_ref, gt_ref, mod_ref, fg_ref, out_ref):
    x2 = _moe_residual(x1_ref[...], yg_ref[0], yg_ref[1], gt_ref[...], mod_ref[0][5:6])
    ms = jnp.mean(x2 * x2, axis=-1, keepdims=True)
    out_ref[...] = x2 * lax.rsqrt(ms + NORM_EPS) * fg_ref[...]


def _combine_final(x1, yg, gates_t, mod_l, final_g, n_tok):
    tok = lambda i: (i, 0)
    return pl.pallas_call(
        _final_kernel,
        grid=(n_tok // TM,),
        in_specs=[
            pl.BlockSpec((TM, D_MODEL), tok),
            pl.BlockSpec((2, TM, D_PACK), lambda i: (0, i, 0)),
            pl.BlockSpec((TM, 2), tok),
            pl.BlockSpec((1, N_MOD, D_MODEL), lambda i: (_mod_row(i), 0, 0)),
            pl.BlockSpec((1, D_MODEL), lambda i: (0, 0)),
        ],
        out_specs=pl.BlockSpec((TM, D_MODEL), tok),
        out_shape=jax.ShapeDtypeStruct((n_tok, D_MODEL), F32),
        compiler_params=_cparams("parallel"),
        name="moe_combine_final",
    )(x1, yg, gates_t, mod_l, final_g.reshape(1, D_MODEL))


def _diff_lambda_init(layer_idx):
    return 0.8 - 0.6 * math.exp(-0.3 * (layer_idx - 1))


def kernel(x, c, ctx, c_ctx, mod_w, mod_b, norm_mix, norm_ffn, w_qkv, w_o, na_rpb, diff_lambda, diff_subln,
           router_w, router_b, expert_w_gate, expert_w_up, expert_w_down, final_norm):
    mod = _modulation(c, c_ctx, mod_w, mod_b)
    stream = (x.reshape(T_LAT, D_MODEL), ctx.reshape(T_CTX, D_MODEL))
    w_qk_b, w_vt_b = _qkv_weights(w_qkv)
    w_o_b = w_o.astype(BF16)
    na_tables = _na_pair_table(na_rpb.reshape(-1, N_DR, N_DC))
    cs, sn = _rope_tables()
    rw_pad = jnp.zeros((D_MODEL, HB), F32).at[:, :N_EXPERTS].set(router_w)
    rb_col = router_b.reshape(N_EXPERTS, 1).astype(F32)
    blk = np.arange(ROUTE_BLK)
    tri = jnp.asarray(blk[:, None] < blk[None, :], BF16)

    for i in range(DEPTH):
        last = i == DEPTH - 1
        j = i // 2
        is_diff = i % 2 == 1
        xa, q, k, v = _qkv(stream, mod[i], norm_mix[i], w_qk_b, w_vt_b, i, cs, sn, rope=is_diff)
        if not is_diff:
            o, o_ctx = _na_attention(na_tables, j, q, k, v)
        else:
            lam_init = _diff_lambda_init(i + 1)
            sg = diff_subln[j].reshape(HB, 1)
            outs = _diff_attention(diff_lambda[j], sg, q, k, v, lam_init, ctx_out=not last)
            o, o_ctx = outs if not last else (outs[0], None)
        n_tok = T_LAT if last else T_ALL
        x1, h, route_i, route_f, cnt = _oproj_route(o, o_ctx, xa, mod[i], w_o_b, i, norm_ffn[i], rw_pad,
                                                    rb_col, tri, n_tok)
        pos, te, tv, n_tiles = _moe_layout(route_i, cnt[:, 0], n_tok)
        xs = _sc_dispatch(h, pos, n_tiles * TM)
        ys = _grouped_ffn(te, tv, xs, expert_w_gate, expert_w_up, expert_w_down, i)
        yg = _sc_combine_gather(ys, pos)
        stream = (x1, yg, route_f[0:2].T, mod[i])
    out = _combine_final(*stream, final_norm, T_LAT)
    return out.reshape(BATCH, SEQ, D_MODEL)
```

```python
import functools
import math

import jax
import jax.numpy as jnp
import numpy as np
from jax import lax
from jax.experimental import pallas as pl
from jax.experimental.pallas import tpu as pltpu
from jax.experimental.pallas import tpu_sc as plsc

F32 = jnp.float32
BF16 = jnp.bfloat16

D_MODEL = 1024
BATCH = 8
SEQ = 2048
DEPTH = 4
CTX_LEN = 256
GRID_W = 64
ROWS = SEQ // GRID_W
NA_HEADS = 16
NA_WIN_ROWS = 8
NA_WIN_COLS = 16
DIFF_HEADS = 8
DIFF_HEAD_DIM = 64
ROPE_THETA = 10000.0
N_EXPERTS = 16
N_GROUPS = 4
EXPERTS_PER_GROUP = 4
D_EXPERT = 1024
N_MOD = 6
NORM_EPS = 1e-6

T_LAT = BATCH * SEQ
T_CTX = BATCH * CTX_LEN
T_ALL = T_LAT + T_CTX
TM = 512
QKV_BLK = TM // 2
ROUTE_BLK = TM // 2
LAT_TILES = T_LAT // TM
CTX_TILES = T_CTX // TM
TILES_PER_BATCH = SEQ // TM
CTX_ROW = BATCH
MOD_ROWS = 16
HB = 128
N_HB = D_MODEL // HB
NA_RB = 4
NA_BAND_ROWS = 12
NA_Q = NA_RB * GRID_W
NA_BAND = NA_BAND_ROWS * GRID_W
TQ_DIFF = 512
MASK_VALUE = -1e30
LOG2E = math.log2(math.e)
Q_SCALE = 0.125 * LOG2E
VMEM_LIMIT = 56 * 1024 * 1024


def _cparams(*sem):
    return pltpu.CompilerParams(dimension_semantics=sem, vmem_limit_bytes=VMEM_LIMIT)


def _dot(a, b):
    return jnp.dot(a, b, preferred_element_type=F32)


def _dot_nt(a, b):
    return lax.dot_general(a, b, (((1,), (1,)), ((), ())), preferred_element_type=F32)


def _sigmoid(x):
    return 1.0 / (1.0 + jnp.exp(-x))


D_PACK = D_MODEL // 2


def _pack_bf16_pair(x):
    lo = lax.bitcast_convert_type(x[:, :D_PACK].astype(BF16).astype(F32), jnp.uint32)
    hi = lax.bitcast_convert_type(x[:, D_PACK:].astype(BF16).astype(F32), jnp.uint32)
    return (lo >> 16) | hi


def _unpack_bf16_pair(w):
    lo = lax.bitcast_convert_type(w << 16, F32)
    hi = lax.bitcast_convert_type(w & jnp.uint32(0xFFFF0000), F32)
    return jnp.concatenate([lo, hi], axis=1)


def _mod_row(i):
    return jnp.minimum(i // TILES_PER_BATCH, CTX_ROW)


def _mod_kernel(act_ref, w_ref, b_ref, o_ref):
    a = act_ref[...]
    a = a * _sigmoid(a)
    o_ref[0] = _dot(a.astype(BF16), w_ref[0].astype(BF16)) + b_ref[0]


def _modulation(c, c_ctx, mod_w, mod_b):
    tn = 1536
    act = jnp.zeros((MOD_ROWS, D_MODEL), F32).at[:BATCH].set(c).at[CTX_ROW].set(c_ctx)
    out = pl.pallas_call(
        _mod_kernel,
        grid=(DEPTH, N_MOD * D_MODEL // tn),
        in_specs=[
            pl.BlockSpec((MOD_ROWS, D_MODEL), lambda l, j: (0, 0)),
            pl.BlockSpec((1, D_MODEL, tn), lambda l, j: (l, 0, j)),
            pl.BlockSpec((1, 1, tn), lambda l, j: (l, 0, j)),
        ],
        out_specs=pl.BlockSpec((1, MOD_ROWS, tn), lambda l, j: (l, 0, j)),
        out_shape=jax.ShapeDtypeStruct((DEPTH, MOD_ROWS, N_MOD * D_MODEL), F32),
        compiler_params=_cparams("parallel", "parallel"),
        name="adaln_mod",
    )(act, mod_w, mod_b.reshape(DEPTH, 1, N_MOD * D_MODEL))
    return out.reshape(DEPTH, MOD_ROWS, N_MOD, D_MODEL)


def _qkv_weight_kernel(w_ref, wqk_ref, wvt_ref):
    j = pl.program_id(1)

    @pl.when(j < 2)
    def _():
        wqk_ref[0] = w_ref[0].astype(BF16)

    @pl.when(j == 2)
    def _():
        wvt_ref[0] = w_ref[0].T.astype(BF16)


def _qkv_weights(w_qkv):
    return pl.pallas_call(
        _qkv_weight_kernel,
        grid=(DEPTH, 3),
        in_specs=[pl.BlockSpec((1, D_MODEL, D_MODEL), lambda l, j: (l, 0, j))],
        out_specs=[
            pl.BlockSpec((1, D_MODEL, D_MODEL), lambda l, j: (l, 0, jnp.minimum(j, 1))),
            pl.BlockSpec((1, D_MODEL, D_MODEL), lambda l, j: (l, 0, 0)),
        ],
        out_shape=[jax.ShapeDtypeStruct((DEPTH, D_MODEL, 2 * D_MODEL), BF16),
                   jax.ShapeDtypeStruct((DEPTH, D_MODEL, D_MODEL), BF16)],
        compiler_params=_cparams("parallel", "arbitrary"),
        name="qkv_weight_prep",
    )(w_qkv)


def _rms_modulate(x, g, shift, scale):
    ms = jnp.mean(x * x, axis=-1, keepdims=True)
    y = x * lax.rsqrt(ms + NORM_EPS) * g
    return y * (1.0 + scale) + shift


def _rope_block(xb, cs, sn):
    lane = lax.broadcasted_iota(jnp.int32, xb.shape, 1)
    partner = jnp.where((lane & 63) < 32, pltpu.roll(xb, 96, 1), pltpu.roll(xb, 32, 1))
    return xb * cs + partner * sn


def _moe_residual(x1, y0, y1, g, gate_ffn):
    y = g[:, 0:1] * _unpack_bf16_pair(y0) + g[:, 1:2] * _unpack_bf16_pair(y1)
    return x1 + gate_ffn * y


def _qkv_kernel(*refs, rope, first):
    n_stream = 2 if first else 4
    stream_refs, refs = refs[:n_stream], refs[n_stream:]
    mod_ref, g_ref, wqk_ref, wvt_ref, cs_ref, sn_ref, x_ref, q_ref, k_ref, vt_ref = refs
    m = mod_ref[0]
    for r0 in range(0, TM, QKV_BLK):
        rows = pl.ds(r0, QKV_BLK)
        if first:
            xl_ref, xc_ref = stream_refs
            x = jnp.where(pl.program_id(0) < LAT_TILES, xl_ref[rows, :], xc_ref[rows, :])
        else:
            x1_ref, yg_ref, gt_ref, modp_ref = stream_refs
            x = _moe_residual(x1_ref[rows, :], yg_ref[0, rows, :], yg_ref[1, rows, :], gt_ref[rows, :],
                              modp_ref[0][5:6])
        x_ref[rows, :] = x
        h = _rms_modulate(x, g_ref[...], m[0:1], m[1:2]).astype(BF16)
        for idx, out in enumerate((q_ref, k_ref)):
            acc = _dot(h, wqk_ref[0, :, idx * D_MODEL:(idx + 1) * D_MODEL])
            if rope:
                cs = cs_ref[rows, :]
                sn = sn_ref[rows, :]
                acc = jnp.concatenate(
                    [_rope_block(acc[:, j * HB:(j + 1) * HB], cs, sn) for j in range(N_HB)], axis=1)
            if idx == 0:
                acc = acc * Q_SCALE
            out[rows, :] = acc.astype(BF16)
        vt_ref[:, rows] = _dot_nt(wvt_ref[0], h).astype(BF16)


def _qkv(stream, mod_l, g, w_qk, w_vt, layer, cs, sn, *, rope):
    first = len(stream) == 2
    n_tiles = T_ALL // TM
    tok = lambda i: (i, 0)
    const2 = lambda i: (0, 0)
    mod_spec = pl.BlockSpec((1, N_MOD, D_MODEL), lambda i: (_mod_row(i), 0, 0))
    rope_idx = lambda i: (jnp.where(i < LAT_TILES, i % TILES_PER_BATCH, TILES_PER_BATCH), 0)
    out = jax.ShapeDtypeStruct((T_ALL, D_MODEL), BF16)
    wmap = lambda i: (layer, 0, 0)
    if first:
        stream_specs = [
            pl.BlockSpec((TM, D_MODEL), lambda i: (jnp.minimum(i, LAT_TILES - 1), 0)),
            pl.BlockSpec((TM, D_MODEL), lambda i: (jnp.maximum(i - LAT_TILES, 0), 0)),
        ]
    else:
        stream_specs = [
            pl.BlockSpec((TM, D_MODEL), tok),
            pl.BlockSpec((2, TM, D_PACK), lambda i: (0, i, 0)),
            pl.BlockSpec((TM, 2), tok),
            mod_spec,
        ]
    return pl.pallas_call(
        functools.partial(_qkv_kernel, rope=rope, first=first),
        grid=(n_tiles,),
        in_specs=stream_specs + [
            mod_spec,
            pl.BlockSpec((1, D_MODEL), const2),
            pl.BlockSpec((1, D_MODEL, 2 * D_MODEL), wmap),
            pl.BlockSpec((1, D_MODEL, D_MODEL), wmap),
            pl.BlockSpec((TM, HB), rope_idx),
            pl.BlockSpec((TM, HB), rope_idx),
        ],
        out_specs=[pl.BlockSpec((TM, D_MODEL), tok), pl.BlockSpec((TM, D_MODEL), tok),
                   pl.BlockSpec((TM, D_MODEL), tok), pl.BlockSpec((D_MODEL, TM), lambda i: (0, i))],
        out_shape=[jax.ShapeDtypeStruct((T_ALL, D_MODEL), F32), out, out,
                   jax.ShapeDtypeStruct((D_MODEL, T_ALL), BF16)],
        compiler_params=_cparams("parallel"),
        name="norm_qkv_rope" if rope else "norm_qkv",
    )(*stream, mod_l, g.reshape(1, D_MODEL), w_qk, w_vt, cs, sn)


def _rope_tables():
    t = np.arange(SEQ)
    row = (t // GRID_W).astype(np.float32)
    col = (t % GRID_W).astype(np.float32)
    n_freq = DIFF_HEAD_DIM // 4
    inv_freq = np.float32(ROPE_THETA) ** (-np.arange(n_freq, dtype=np.float32) / np.float32(n_freq))
    ang = np.concatenate([row[:, None] * inv_freq, col[:, None] * inv_freq], axis=-1)
    cos, sin = np.cos(ang), np.sin(ang)
    cs = np.concatenate([cos, cos, cos, cos], axis=-1)
    sn = np.concatenate([-sin, sin, -sin, sin], axis=-1)
    cs = np.concatenate([cs, np.ones((TM, HB), np.float32)], axis=0)
    sn = np.concatenate([sn, np.zeros((TM, HB), np.float32)], axis=0)
    return jnp.asarray(cs, F32), jnp.asarray(sn, F32)


Q_BLK = 256
KEY_BLK = 256
SUBLANES = 8
DIFF_Q_BLK = 256
DIFF_Q_PER_TRIP = 8
DIFF_KEY_BLK = 1024


def _colwise(reduce_fn, x):
    return reduce_fn(x.reshape(x.shape[0] // SUBLANES, SUBLANES, x.shape[1]), axis=0)


def _key_blocks(k_ref, vt_ref, key_start=0, n_keys=None, bias=None, blk=KEY_BLK):
    n_keys = k_ref.shape[0] if n_keys is None else n_keys
    blk = min(blk, n_keys)
    blocks = []
    for c, s0 in enumerate(range(0, n_keys, blk)):
        blocks.append((
            lambda s0=s0: k_ref[pl.ds(key_start + s0, blk), :],
            lambda s0=s0: vt_ref[:, pl.ds(key_start + s0, blk)],
            None if bias is None else functools.partial(bias, c),
            blk,
        ))
    return blocks


def _attention_t(operands, s_scr):
    def rows_of(u, c):
        sizes = [blk[3] for blk in operands[u][1]]
        return slice(sum(sizes[:c]), sum(sizes[:c + 1]))

    def score(u, c, mx):
        qm, blocks = operands[u]
        k_tile, _, bias, _ = blocks[c]
        st = _dot_nt(k_tile(), qm)
        if bias is not None:
            st = st + bias()
        s_scr[u % 2, rows_of(u, c), :n_q] = st
        return jnp.maximum(mx, _colwise(jnp.max, st))

    def expo(u, c, m, acc, den):
        vt_tile = operands[u][1][c][1]
        e = jnp.exp2(s_scr[u % 2, rows_of(u, c), :n_q] - m)
        o = _dot(vt_tile(), e.astype(BF16))
        return (o if acc is None else acc + o), den + _colwise(jnp.sum, e)

    n_q = operands[0][0].shape[0]
    neg = jnp.full((SUBLANES, n_q), -jnp.inf, F32)
    zero = jnp.zeros((SUBLANES, n_q), F32)
    n_blocks = [len(blocks) for _, blocks in operands]
    mx = neg
    for c in range(n_blocks[0]):
        mx = score(0, c, mx)
    outs = []
    for u in range(len(operands)):
        m = jnp.max(mx, axis=0, keepdims=True)
        acc, den, mx = None, zero, neg
        n_next = n_blocks[u + 1] if u + 1 < len(operands) else 0
        for c in range(max(n_blocks[u], n_next)):
            if c < n_next:
                mx = score(u + 1, c, mx)
            if c < n_blocks[u]:
                acc, den = expo(u, c, m, acc, den)
        outs.append(acc / jnp.sum(den, axis=0, keepdims=True))
    return outs


def _half_masks(shape):
    lane = lax.broadcasted_iota(jnp.int32, shape, 1)
    return lane < 64, lane >= 64


N_DR = 2 * NA_WIN_ROWS - 1
N_DC = 2 * NA_WIN_COLS - 1
PAIR_MASK_FIRST = N_DR - 1
PAIR_MASK_SECOND = N_DR
PAIR_MASKED = N_DR + 1
N_PAIRS = N_DR + 2
NA_KEY_BLK = 768
DR_LO = NA_WIN_ROWS - 1 - NA_WIN_ROWS // 2
DR_HI = DR_LO + NA_WIN_ROWS - 1


def _na_pair_table(rpb):
    h = rpb.shape[0]
    w = GRID_W
    col = np.arange(w)
    dc = np.clip(col[:, None] - col[None, :] + NA_WIN_COLS - 1, 0, N_DC - 1)
    onehot = (dc[:, :, None] == np.arange(N_DC)).astype(np.float32)
    t = jnp.einsum("hdc,kqc->hdkq", rpb.astype(F32), onehot, precision=lax.Precision.HIGHEST)
    cstart = np.clip(col - NA_WIN_COLS // 2, 0, w - NA_WIN_COLS)
    cvalid = (col[:, None] >= cstart[None, :]) & (col[:, None] < cstart[None, :] + NA_WIN_COLS)
    tt = jnp.where(cvalid, t * LOG2E, MASK_VALUE)
    masked = jnp.full((h, 1, w, w), MASK_VALUE, F32)
    regular = jnp.concatenate([tt[:, 1:], tt[:, :-1]], axis=-1)
    first = jnp.concatenate([masked, tt[:, DR_HI:DR_HI + 1]], axis=-1)
    second = jnp.concatenate([tt[:, DR_LO:DR_LO + 1], masked], axis=-1)
    return jnp.concatenate([regular, first, second, jnp.concatenate([masked, masked], axis=-1)], axis=1)


def _na_pair_entries(q_row0, band_row0, band_rows):
    entries = []
    for j in range(band_rows):
        kr = band_row0 + j
        row = []
        for a in range(0, NA_RB, 2):
            ok, d = [], []
            for r in (q_row0 + a, q_row0 + a + 1):
                r0 = min(max(r - NA_WIN_ROWS // 2, 0), ROWS - NA_WIN_ROWS)
                ok.append(r0 <= kr < r0 + NA_WIN_ROWS)
                d.append(kr - r + NA_WIN_ROWS - 1)
            if ok[0] and ok[1]:
                row.append(d[0] - 1)
            elif ok[1]:
                assert d[1] == DR_HI
                row.append(PAIR_MASK_FIRST)
            elif ok[0]:
                assert d[0] == DR_LO
                row.append(PAIR_MASK_SECOND)
            else:
                row.append(PAIR_MASKED)
        entries.append(row)
    return entries


def _na_kernel(tab_ref, q_ref, k_ref, vt_ref, qc_ref, kc_ref, vct_ref, o_ref, oc_ref, s_scr):
    n_blocks = ROWS // NA_RB
    ctx_blocks = _key_blocks(kc_ref, vct_ref)

    def operands(q_start, band_start, band_rows, entries):
        q = q_ref[pl.ds(q_start, NA_Q), :]
        n_band = band_rows * GRID_W
        blk = NA_KEY_BLK if n_band % NA_KEY_BLK == 0 else n_band
        rows_per_chunk = blk // GRID_W
        ops = []
        for hh, msk in enumerate(_half_masks(q.shape)):
            def bias(c, hh=hh):
                rows = entries[c * rows_per_chunk:(c + 1) * rows_per_chunk]
                return jnp.concatenate(
                    [jnp.concatenate([tab_ref[hh, e] for e in row], axis=1) for row in rows], axis=0)

            band = _key_blocks(k_ref, vt_ref, band_start, n_band, bias, blk)
            ops.append((jnp.where(msk, q, jnp.zeros_like(q)), band + ctx_blocks))
        return ops

    def merge_heads(o_lo, o_hi):
        feat = lax.broadcasted_iota(jnp.int32, o_lo.shape, 0)
        return jnp.where(feat < HB // 2, o_lo, o_hi).T.astype(BF16)

    last_band = ROWS - NA_WIN_ROWS
    interior = _na_pair_entries(NA_RB, 0, NA_BAND_ROWS)
    blocks = [(0, 0, NA_WIN_ROWS, _na_pair_entries(0, 0, NA_WIN_ROWS))]
    blocks += [(rb * NA_Q, (rb - 1) * NA_Q, NA_BAND_ROWS, interior) for rb in range(1, n_blocks - 1)]
    blocks.append(((n_blocks - 1) * NA_Q, last_band * GRID_W, NA_WIN_ROWS,
                   _na_pair_entries(ROWS - NA_RB, last_band, NA_WIN_ROWS)))
    qc = qc_ref[...]
    ctx_ops = [(jnp.where(msk, qc, jnp.zeros_like(qc)), ctx_blocks) for msk in _half_masks(qc.shape)]
    outs = _attention_t([op for blk in blocks for op in operands(*blk)] + ctx_ops, s_scr)
    for i, blk in enumerate(blocks):
        o_ref[pl.ds(blk[0], NA_Q), :] = merge_heads(outs[2 * i], outs[2 * i + 1])
    oc_ref[...] = merge_heads(outs[-2], outs[-1])


def _na_attention(tables, layer_na, q, k, vt):
    assert NA_Q == Q_BLK and CTX_LEN == Q_BLK
    lat = lambda hp, b: (b, hp)
    ctx = lambda hp, b: (T_LAT // CTX_LEN + b, hp)
    return pl.pallas_call(
        _na_kernel,
        grid=(N_HB, BATCH),
        in_specs=[
            pl.BlockSpec((2, N_PAIRS, GRID_W, 2 * GRID_W), lambda hp, b: (layer_na * N_HB + hp, 0, 0, 0)),
            pl.BlockSpec((SEQ, HB), lat),
            pl.BlockSpec((SEQ, HB), lat),
            pl.BlockSpec((HB, SEQ), lambda hp, b: (hp, b)),
            pl.BlockSpec((CTX_LEN, HB), ctx),
            pl.BlockSpec((CTX_LEN, HB), ctx),
            pl.BlockSpec((HB, CTX_LEN), lambda hp, b: (hp, T_LAT // CTX_LEN + b)),
        ],
        out_specs=[pl.BlockSpec((SEQ, HB), lat), pl.BlockSpec((CTX_LEN, HB), lat)],
        out_shape=[jax.ShapeDtypeStruct((T_LAT, D_MODEL), BF16), jax.ShapeDtypeStruct((T_CTX, D_MODEL), BF16)],
        scratch_shapes=[pltpu.VMEM((2, NA_BAND + CTX_LEN, Q_BLK), F32)],
        compiler_params=_cparams("parallel", "parallel"),
        name="na_attention",
    )(tables, q, k, vt, q, k, vt)


def _diff_lambda(lam_ref, lam_init):
    lam = lam_ref[...]
    a = jnp.sum(lam[0:1] * lam[1:2], axis=-1, keepdims=True)
    b = jnp.sum(lam[2:3] * lam[3:4], axis=-1, keepdims=True)
    return jnp.exp(a) - jnp.exp(b) + lam_init


def _diff_kernel(lam_ref, sg_ref, q_ref, k_ref, vt_ref, qc_ref, kc_ref, vct_ref, o_ref, *rest, lam_init):
    *maybe_oc_ref, s_scr = rest
    lam = _diff_lambda(lam_ref, lam_init)
    sg_col = sg_ref[...]

    def rows(qs, pieces):
        blocks = [blk for k_r, vt_r in pieces for blk in _key_blocks(k_r, vt_r, blk=DIFF_KEY_BLK)]
        outs = _attention_t([(jnp.where(msk, q, jnp.zeros_like(q)), blocks)
                             for q in qs for msk in _half_masks(q.shape)], s_scr)
        res = []
        for o1, o2 in zip(outs[0::2], outs[1::2]):
            ot = o1 - lam * o2
            ms = jnp.mean(ot * ot, axis=0, keepdims=True)
            ot = ot * lax.rsqrt(ms + NORM_EPS) * sg_col * (1.0 - lam_init)
            res.append(ot.T.astype(BF16))
        return res

    def body(t, carry):
        starts = [pl.multiple_of((t * DIFF_Q_PER_TRIP + j) * DIFF_Q_BLK, DIFF_Q_BLK)
                  for j in range(DIFF_Q_PER_TRIP)]
        outs = rows([q_ref[pl.ds(r0, DIFF_Q_BLK), :] for r0 in starts], [(k_ref, vt_ref), (kc_ref, vct_ref)])
        for r0, o in zip(starts, outs):
            o_ref[pl.ds(r0, DIFF_Q_BLK), :] = o
        return carry

    lax.fori_loop(0, SEQ // (DIFF_Q_BLK * DIFF_Q_PER_TRIP), body, 0)
    if maybe_oc_ref:
        oc_ref, = maybe_oc_ref
        oc_ref[...], = rows([qc_ref[...]], [(kc_ref, vct_ref)])


def _diff_attention(lam, sg, q, k, vt, lam_init, *, ctx_out):
    lat = lambda b, h: (b, h)
    ctx = lambda b, h: (T_LAT // CTX_LEN + b, h)
    lat_t = lambda b, h: (h, b)
    ctx_t = lambda b, h: (h, T_LAT // CTX_LEN + b)
    out_specs = [pl.BlockSpec((SEQ, HB), lat)]
    out_shape = [jax.ShapeDtypeStruct((T_LAT, D_MODEL), BF16)]
    if ctx_out:
        out_specs.append(pl.BlockSpec((CTX_LEN, HB), lat))
        out_shape.append(jax.ShapeDtypeStruct((T_CTX, D_MODEL), BF16))
    return pl.pallas_call(
        functools.partial(_diff_kernel, lam_init=lam_init),
        grid=(BATCH, DIFF_HEADS),
        in_specs=[
            pl.BlockSpec(lam.shape, lambda b, h: (0, 0)),
            pl.BlockSpec(sg.shape, lambda b, h: (0, 0)),
            pl.BlockSpec((SEQ, HB), lat),
            pl.BlockSpec((SEQ, HB), lat),
            pl.BlockSpec((HB, SEQ), lat_t),
            pl.BlockSpec((CTX_LEN, HB), ctx),
            pl.BlockSpec((CTX_LEN, HB), ctx),
            pl.BlockSpec((HB, CTX_LEN), ctx_t),
        ],
        out_specs=out_specs,
        out_shape=out_shape,
        scratch_shapes=[pltpu.VMEM((2, SEQ + CTX_LEN, DIFF_Q_BLK), F32)],
        compiler_params=_cparams("parallel", "parallel"),
        name="diff_attention",
    )(lam, sg, q, k, vt, q, k, vt)


def _route_rows(logits_t, bias_col):
    s = _sigmoid(logits_t)
    sel = s + bias_col
    sel_r = [sel[e:e + 1] for e in range(N_EXPERTS)]
    s_r = [s[e:e + 1] for e in range(N_EXPERTS)]
    group_scores = []
    for g in range(N_GROUPS):
        v = sel_r[g * EXPERTS_PER_GROUP:(g + 1) * EXPERTS_PER_GROUP]
        pairs = [v[a] + v[b] for a in range(EXPERTS_PER_GROUP) for b in range(a + 1, EXPERTS_PER_GROUP)]
        group_scores.append(functools.reduce(jnp.maximum, pairs))
    best = group_scores[0]
    grp = jnp.zeros(best.shape, jnp.int32)
    for g in range(1, N_GROUPS):
        upd = group_scores[g] > best
        best = jnp.where(upd, group_scores[g], best)
        grp = jnp.where(upd, g, grp)

    def pick(rows, j):
        out = rows[j]
        for g in range(1, N_GROUPS):
            out = jnp.where(grp == g, rows[g * EXPERTS_PER_GROUP + j], out)
        return out

    w = [pick(sel_r, j) for j in range(EXPERTS_PER_GROUP)]
    sc = [pick(s_r, j) for j in range(EXPERTS_PER_GROUP)]

    def argmax_first(vals):
        bv = vals[0]
        bi = jnp.zeros(bv.shape, jnp.int32)
        for j in range(1, len(vals)):
            upd = vals[j] > bv
            bv = jnp.where(upd, vals[j], bv)
            bi = jnp.where(upd, j, bi)
        return bi

    i1 = argmax_first(w)
    i2 = argmax_first([jnp.where(i1 == j, -jnp.inf, w[j]) for j in range(EXPERTS_PER_GROUP)])

    def take(vals, idx):
        out = vals[0]
        for j in range(1, len(vals)):
            out = jnp.where(idx == j, vals[j], out)
        return out

    s1 = take(sc, i1)
    s2 = take(sc, i2)
    den = s1 + s2
    return grp * EXPERTS_PER_GROUP + i1, grp * EXPERTS_PER_GROUP + i2, s1 / den, s2 / den


def _oproj_kernel(o_ref, oc_ref, x_ref, mod_ref, w_ref, g_ref, rw_ref, rb_ref, tri_ref,
                  x1_ref, h_ref, ri_ref, rf_ref, cnt_ref, carry_ref):
    i = pl.program_id(0)

    @pl.when(i == 0)
    def _():
        carry_ref[...] = jnp.zeros_like(carry_ref)

    m = mod_ref[0]
    rw = rw_ref[...].astype(BF16)
    ri_ref[...] = jnp.zeros_like(ri_ref)
    rf_ref[...] = jnp.zeros_like(rf_ref)
    carry = carry_ref[...]
    for r0 in range(0, TM, ROUTE_BLK):
        rows = pl.ds(r0, ROUTE_BLK)
        o = jnp.where(i < LAT_TILES, o_ref[rows, :], oc_ref[rows, :])
        x1 = x_ref[rows, :] + m[2:3] * _dot(o, w_ref[0])
        x1_ref[rows, :] = x1
        h = _rms_modulate(x1, g_ref[...], m[3:4], m[4:5])
        hb = h.astype(BF16)
        h_ref[rows, :] = _pack_bf16_pair(h)

        logits = _dot(hb, rw)
        logits_t = logits.T[:N_EXPERTS]
        e0, e1, g0, g1 = _route_rows(logits_t, rb_ref[...])

        eiota = lax.broadcasted_iota(jnp.int32, (N_EXPERTS, ROUTE_BLK), 0)
        oh0 = eiota == e0
        oh1 = eiota == e1
        oh = jnp.where(oh0 | oh1, 1.0, 0.0)
        before = _dot(oh.astype(BF16), tri_ref[...]) + carry
        rank0 = jnp.sum(jnp.where(oh0, before, 0.0), axis=0, keepdims=True)
        rank1 = jnp.sum(jnp.where(oh1, before, 0.0), axis=0, keepdims=True)
        carry = carry + jnp.sum(oh, axis=1, keepdims=True)

        ri_ref[0:1, rows] = e0
        ri_ref[1:2, rows] = e1
        ri_ref[2:3, rows] = rank0.astype(jnp.int32)
        ri_ref[3:4, rows] = rank1.astype(jnp.int32)
        rf_ref[0:1, rows] = g0
        rf_ref[1:2, rows] = g1
    carry_ref[...] = carry
    cnt_ref[...] = jnp.broadcast_to(carry, cnt_ref.shape)


def _oproj_route(o, o_ctx, x, mod_l, w_o, layer, g, rw_pad, rb_col, tri, n_tok):
    n_tiles = n_tok // TM
    tok = lambda i: (i, 0)
    const2 = lambda i: (0, 0)
    lane_tok = lambda i: (0, i)
    return pl.pallas_call(
        _oproj_kernel,
        grid=(n_tiles,),
        in_specs=[
            pl.BlockSpec((TM, D_MODEL), lambda i: (jnp.minimum(i, LAT_TILES - 1), 0)),
            pl.BlockSpec((TM, D_MODEL), lambda i: (jnp.maximum(i - LAT_TILES, 0), 0)),
            pl.BlockSpec((TM, D_MODEL), tok),
            pl.BlockSpec((1, N_MOD, D_MODEL), lambda i: (_mod_row(i), 0, 0)),
            pl.BlockSpec((1, D_MODEL, D_MODEL), lambda i: (layer, 0, 0)),
            pl.BlockSpec((1, D_MODEL), const2),
            pl.BlockSpec((D_MODEL, HB), const2),
            pl.BlockSpec((N_EXPERTS, 1), const2),
            pl.BlockSpec((ROUTE_BLK, ROUTE_BLK), const2),
        ],
        out_specs=[
            pl.BlockSpec((TM, D_MODEL), tok),
            pl.BlockSpec((TM, D_PACK), tok),
            pl.BlockSpec((8, TM), lane_tok),
            pl.BlockSpec((8, TM), lane_tok),
            pl.BlockSpec((N_EXPERTS, HB), const2),
        ],
        out_shape=[
            jax.ShapeDtypeStruct((n_tok, D_MODEL), F32),
            jax.ShapeDtypeStruct((n_tok, D_PACK), jnp.uint32),
            jax.ShapeDtypeStruct((8, n_tok), jnp.int32),
            jax.ShapeDtypeStruct((8, n_tok), F32),
            jax.ShapeDtypeStruct((N_EXPERTS, HB), F32),
        ],
        scratch_shapes=[pltpu.VMEM((N_EXPERTS, 1), F32)],
        compiler_params=_cparams("arbitrary"),
        name="oproj_norm_route",
    )(o, o if o_ctx is None else o_ctx, x, mod_l, w_o, g.reshape(1, D_MODEL), rw_pad, rb_col, tri)


def _moe_kernel(te_ref, tv_ref, xs_ref, wg_ref, wu_ref, wd_ref, ys_ref, wgb, wub, wdb):
    i = pl.program_id(0)
    valid = tv_ref[i]
    e = te_ref[i]
    prev = te_ref[jnp.maximum(i - 1, 0)]
    half = TM // 2

    @pl.when((i == 0) | (e != prev))
    def _():
        wgb[...] = wg_ref[0, 0].astype(BF16)
        wub[...] = wu_ref[0, 0].astype(BF16)
        wdb[...] = wd_ref[0, 0].astype(BF16)

    def ffn(rows, x):
        x = _unpack_bf16_pair(x).astype(BF16)
        a = _dot(x, wgb[...])
        u = _dot(x, wub[...])
        hm = (a * _sigmoid(a) * u).astype(BF16)
        ys_ref[rows, :] = _pack_bf16_pair(_dot(hm, wdb[...]))

    @pl.when(valid == TM)
    def _():
        x = _unpack_bf16_pair(xs_ref[...]).astype(BF16)
        y = None
        for c0 in range(0, D_EXPERT, D_EXPERT // 2):
            cols = slice(c0, c0 + D_EXPERT // 2)
            a = _dot(x, wgb[:, cols])
            u = _dot(x, wub[:, cols])
            hm = (a * _sigmoid(a) * u).astype(BF16)
            part = _dot(hm, wdb[cols, :])
            y = part if y is None else y + part
        ys_ref[...] = _pack_bf16_pair(y)

    for r0 in range(0, TM, half):
        rows = pl.ds(r0, half)

        @pl.when((valid < TM) & (valid >= r0 + half))
        def _():
            ffn(rows, xs_ref[rows, :])

        @pl.when((valid > r0) & (valid < r0 + half))
        def _():
            row = r0 + lax.broadcasted_iota(jnp.int32, (half, 1), 0)
            x = xs_ref[rows, :]
            ffn(rows, jnp.where(row < valid, x, jnp.zeros_like(x)))

        @pl.when(valid <= r0)
        def _():
            ys_ref[rows, :] = jnp.zeros((half, D_PACK), jnp.uint32)


def _grouped_ffn(tile_expert, tile_valid, xs, w_gate, w_up, w_down, layer):
    n_tiles = xs.shape[0] // TM
    wmap = lambda i, te, tv: (layer, te[i], 0, 0)
    tok = lambda i, te, tv: (i, 0)
    return pl.pallas_call(
        _moe_kernel,
        grid_spec=pltpu.PrefetchScalarGridSpec(
            num_scalar_prefetch=2,
            grid=(n_tiles,),
            in_specs=[
                pl.BlockSpec((TM, D_PACK), tok),
                pl.BlockSpec((1, 1, D_MODEL, D_EXPERT), wmap),
                pl.BlockSpec((1, 1, D_MODEL, D_EXPERT), wmap),
                pl.BlockSpec((1, 1, D_EXPERT, D_MODEL), wmap),
            ],
            out_specs=pl.BlockSpec((TM, D_PACK), tok),
            scratch_shapes=[
                pltpu.VMEM((D_MODEL, D_EXPERT), BF16),
                pltpu.VMEM((D_MODEL, D_EXPERT), BF16),
                pltpu.VMEM((D_EXPERT, D_MODEL), BF16),
            ],
        ),
        out_shape=jax.ShapeDtypeStruct((xs.shape[0], D_PACK), jnp.uint32),
        compiler_params=_cparams("arbitrary"),
        name="grouped_ffn",
    )(tile_expert, tile_valid, xs, w_gate, w_up, w_down)


def _moe_layout(route_i, counts, n_tok):
    n_tiles = 2 * n_tok // TM + N_EXPERTS
    counts = counts.astype(jnp.int32)
    tiles_e = (counts + TM - 1) // TM
    tiles_end = jnp.cumsum(tiles_e)
    tile_start = tiles_end - tiles_e
    eid = jnp.arange(N_EXPERTS, dtype=jnp.int32)
    tok_oh = route_i[0:2][:, :, None] == eid
    pos = jnp.sum(jnp.where(tok_oh, tile_start * TM, 0), axis=-1) + route_i[2:4]
    tile_ids = jnp.arange(n_tiles, dtype=jnp.int32)
    n_used = tiles_end[-1]
    te = jnp.sum(tiles_end[None, :] <= jnp.minimum(tile_ids, n_used - 1)[:, None], axis=1).astype(jnp.int32)
    tile_oh = te[:, None] == eid
    cnt_t = jnp.sum(jnp.where(tile_oh, counts, 0), axis=1)
    start_t = jnp.sum(jnp.where(tile_oh, tile_start, 0), axis=1)
    tv = jnp.clip(cnt_t - (tile_ids - start_t) * TM, 0, TM)
    tv = jnp.where(tile_ids < n_used, tv, 0).astype(jnp.int32)
    return pos.astype(jnp.int32), te, tv, n_tiles


SC_CORES = 2
SC_SUBCORES = 16
SC_WORKERS = SC_CORES * SC_SUBCORES
SC_CHUNK = 32
SC_RING = 4


def _sc_mesh():
    return plsc.VectorSubcoreMesh(core_axis_name="c", subcore_axis_name="s")


def _sc_worker_indices(pos, n_tok):
    n_ch = n_tok // SC_WORKERS // SC_CHUNK
    return pos.reshape(2, SC_WORKERS, n_ch, SC_CHUNK).transpose(1, 2, 0, 3), n_ch


def _sc_dispatch(h, pos, n_slots):
    n_tok = h.shape[0]
    per_w = n_tok // SC_WORKERS
    pos_w, n_ch = _sc_worker_indices(pos, n_tok)
    assert n_ch % 2 == 0 and n_ch * SC_CHUNK * SC_WORKERS == n_tok

    @functools.partial(
        pl.kernel, mesh=_sc_mesh(), out_type=jax.ShapeDtypeStruct((n_slots, h.shape[1]), h.dtype),
        scratch_types=[pltpu.VMEM((n_ch, 2, SC_CHUNK), jnp.int32), pltpu.VMEM((2, SC_CHUNK, h.shape[1]), h.dtype),
                       pltpu.SemaphoreType.DMA((2,)), pltpu.SemaphoreType.DMA((2,))],
        name="sc_dispatch")
    def dispatch(h_hbm, pos_hbm, xs_hbm, idx_v, rows_v, load_sem, scat_sem):
        wid = lax.axis_index("s") * SC_CORES + lax.axis_index("c")
        base = wid * per_w
        pltpu.sync_copy(pos_hbm.at[wid], idx_v)

        def load(c, b):
            return pltpu.make_async_copy(h_hbm.at[pl.ds(base + c * SC_CHUNK, SC_CHUNK)], rows_v.at[b],
                                         load_sem.at[b])

        def scat(c, b, k):
            return pltpu.make_async_copy(rows_v.at[b], xs_hbm.at[idx_v.at[c, k]], scat_sem.at[b])

        load(0, 0).start()

        @pl.loop(0, n_ch, step=2)
        def _(c0):
            for b in range(2):
                c = c0 + b
                load(c, b).wait()
                scat(c, b, 0).start()
                scat(c, b, 1).start()

                @pl.when(c >= 1)
                def _():
                    scat(c - 1, 1 - b, 0).wait()
                    scat(c - 1, 1 - b, 1).wait()

                @pl.when(c + 1 < n_ch)
                def _():
                    load(c + 1, 1 - b).start()

        scat(n_ch - 1, 1, 0).wait()
        scat(n_ch - 1, 1, 1).wait()

    return dispatch(h, pos_w)


def _sc_combine_gather(ys, pos):
    n_tok = pos.shape[1]
    per_w = n_tok // SC_WORKERS
    pos_w, n_ch = _sc_worker_indices(pos, n_tok)
    n_items = 2 * n_ch
    assert n_ch * SC_CHUNK * SC_WORKERS == n_tok and n_items % SC_RING == 0 and SC_RING % 2 == 0

    @functools.partial(
        pl.kernel, mesh=_sc_mesh(), out_type=jax.ShapeDtypeStruct((2, n_tok, ys.shape[1]), ys.dtype),
        scratch_types=[pltpu.VMEM((n_ch, 2, SC_CHUNK), jnp.int32),
                       pltpu.VMEM((SC_RING, SC_CHUNK, ys.shape[1]), ys.dtype),
                       pltpu.SemaphoreType.DMA((SC_RING,)), pltpu.SemaphoreType.DMA((SC_RING,))],
        name="sc_combine_gather")
    def gather(ys_hbm, pos_hbm, yg_hbm, idx_v, rows_v, gath_sem, write_sem):
        wid = lax.axis_index("s") * SC_CORES + lax.axis_index("c")
        base = wid * per_w
        pltpu.sync_copy(pos_hbm.at[wid], idx_v)

        def gath(c, k, b):
            return pltpu.make_async_copy(ys_hbm.at[idx_v.at[c, k]], rows_v.at[b], gath_sem.at[b])

        def write(c, k, b):
            return pltpu.make_async_copy(rows_v.at[b], yg_hbm.at[k, pl.ds(base + c * SC_CHUNK, SC_CHUNK)],
                                         write_sem.at[b])

        for j in range(SC_RING - 1):
            gath(j // 2, j % 2, j).start()

        @pl.loop(0, n_items, step=SC_RING)
        def _(j0):
            for b in range(SC_RING):
                j = j0 + b
                gath(j // 2, b % 2, b).wait()
                write(j // 2, b % 2, b).start()
                pb = (b - 1) % SC_RING

                @pl.when(j >= 1)
                def _():
                    write((j - 1) // 2, (b - 1) % 2, pb).wait()

                @pl.when(j + SC_RING - 1 < n_items)
                def _():
                    gath((j + SC_RING - 1) // 2, (b - 1) % 2, pb).start()

        write(n_ch - 1, 1, (n_items - 1) % SC_RING).wait()

    return gather(ys, pos_w)


def _final_kernel(x1_ref, yg_ref, gt_ref, mod_ref, fg_ref, out_ref):
    x2 = _moe_residual(x1_ref[...], yg_ref[0], yg_ref[1], gt_ref[...], mod_ref[0][5:6])
    ms = jnp.mean(x2 * x2, axis=-1, keepdims=True)
    out_ref[...] = x2 * lax.rsqrt(ms + NORM_EPS) * fg_ref[...]


def _combine_final(x1, yg, gates_t, mod_l, final_g, n_tok):
    tok = lambda i: (i, 0)
    return pl.pallas_call(
        _final_kernel,
        grid=(n_tok // TM,),
        in_specs=[
            pl.BlockSpec((TM, D_MODEL), tok),
            pl.BlockSpec((2, TM, D_PACK), lambda i: (0, i, 0)),
            pl.BlockSpec((TM, 2), tok),
            pl.BlockSpec((1, N_MOD, D_MODEL), lambda i: (_mod_row(i), 0, 0)),
            pl.BlockSpec((1, D_MODEL), lambda i: (0, 0)),
        ],
        out_specs=pl.BlockSpec((TM, D_MODEL), tok),
        out_shape=jax.ShapeDtypeStruct((n_tok, D_MODEL), F32),
        compiler_params=_cparams("parallel"),
        name="moe_combine_final",
    )(x1, yg, gates_t, mod_l, final_g.reshape(1, D_MODEL))


def _diff_lambda_init(layer_idx):
    return 0.8 - 0.6 * math.exp(-0.3 * (layer_idx - 1))


def kernel(x, c, ctx, c_ctx, mod_w, mod_b, norm_mix, norm_ffn, w_qkv, w_o, na_rpb, diff_lambda, diff_subln,
           router_w, router_b, expert_w_gate, expert_w_up, expert_w_down, final_norm):
    mod = _modulation(c, c_ctx, mod_w, mod_b)
    stream = (x.reshape(T_LAT, D_MODEL), ctx.reshape(T_CTX, D_MODEL))
    w_qk_b, w_vt_b = _qkv_weights(w_qkv)
    w_o_b = w_o.astype(BF16)
    na_tables = _na_pair_table(na_rpb.reshape(-1, N_DR, N_DC))
    cs, sn = _rope_tables()
    rw_pad = jnp.zeros((D_MODEL, HB), F32).at[:, :N_EXPERTS].set(router_w)
    rb_col = router_b.reshape(N_EXPERTS, 1).astype(F32)
    blk = np.arange(ROUTE_BLK)
    tri = jnp.asarray(blk[:, None] < blk[None, :], BF16)

    for i in range(DEPTH):
        last = i == DEPTH - 1
        j = i // 2
        is_diff = i % 2 == 1
        xa, q, k, v = _qkv(stream, mod[i], norm_mix[i], w_qk_b, w_vt_b, i, cs, sn, rope=is_diff)
        if not is_diff:
            o, o_ctx = _na_attention(na_tables, j, q, k, v)
        else:
            lam_init = _diff_lambda_init(i + 1)
            sg = diff_subln[j].reshape(HB, 1)
            outs = _diff_attention(diff_lambda[j], sg, q, k, v, lam_init, ctx_out=not last)
            o, o_ctx = outs if not last else (outs[0], None)
        n_tok = T_LAT if last else T_ALL
        x1, h, route_i, route_f, cnt = _oproj_route(o, o_ctx, xa, mod[i], w_o_b, i, norm_ffn[i], rw_pad,
                                                    rb_col, tri, n_tok)
        pos, te, tv, n_tiles = _moe_layout(route_i, cnt[:, 0], n_tok)
        xs = _sc_dispatch(h, pos, n_tiles * TM)
        ys = _grouped_ffn(te, tv, xs, expert_w_gate, expert_w_up, expert_w_down, i)
        yg = _sc_combine_gather(ys, pos)
        stream = (x1, yg, route_f[0:2].T, mod[i])
    out = _combine_final(*stream, final_norm, T_LAT)
    return out.reshape(BATCH, SEQ, D_MODEL)
```

```python
import functools
import math

import jax
import jax.numpy as jnp
import numpy as np
from jax import lax
from jax.experimental import pallas as pl
from jax.experimental.pallas import tpu as pltpu
from jax.experimental.pallas import tpu_sc as plsc

F32 = jnp.float32
BF16 = jnp.bfloat16

D_MODEL = 1024
BATCH = 8
SEQ = 2048
DEPTH = 4
CTX_LEN = 256
GRID_W = 64
ROWS = SEQ // GRID_W
NA_WIN_ROWS = 8
NA_WIN_COLS = 16
DIFF_HEADS = 8
DIFF_HEAD_DIM = 64
ROPE_THETA = 10000.0
N_EXPERTS = 16
N_GROUPS = 4
EXPERTS_PER_GROUP = 4
D_EXPERT = 1024
N_MOD = 6
NORM_EPS = 1e-6

T_LAT = BATCH * SEQ
T_CTX = BATCH * CTX_LEN
T_ALL = T_LAT + T_CTX
TM = 512
QKV_BLK = TM // 2
ROUTE_BLK = TM // 2
LAT_TILES = T_LAT // TM
TILES_PER_BATCH = SEQ // TM
CTX_ROW = BATCH
MOD_ROWS = 16
MOD_TN = 1536
HB = 128
N_HB = D_MODEL // HB
NA_RB = 4
NA_BAND_ROWS = 12
NA_Q = NA_RB * GRID_W
NA_BAND = NA_BAND_ROWS * GRID_W
MASK_VALUE = -1e30
LOG2E = math.log2(math.e)
Q_SCALE = 0.125 * LOG2E
VMEM_LIMIT = 56 * 1024 * 1024


def _cparams(*sem):
    return pltpu.CompilerParams(dimension_semantics=sem, vmem_limit_bytes=VMEM_LIMIT)


def _dot(a, b):
    return jnp.dot(a, b, preferred_element_type=F32)


def _dot_nt(a, b):
    return lax.dot_general(a, b, (((1,), (1,)), ((), ())), preferred_element_type=F32)


def _sigmoid(x):
    return 1.0 / (1.0 + jnp.exp(-x))


D_PACK = D_MODEL // 2


def _pack_bf16_pair(x):
    lo = lax.bitcast_convert_type(x[:, :D_PACK].astype(BF16).astype(F32), jnp.uint32)
    hi = lax.bitcast_convert_type(x[:, D_PACK:].astype(BF16).astype(F32), jnp.uint32)
    return (lo >> 16) | hi


def _unpack_bf16_pair(w):
    lo = lax.bitcast_convert_type(w << 16, F32)
    hi = lax.bitcast_convert_type(w & jnp.uint32(0xFFFF0000), F32)
    return jnp.concatenate([lo, hi], axis=1)


def _mod_row(i):
    return jnp.minimum(i // TILES_PER_BATCH, CTX_ROW)


def _mod_kernel(act_ref, w_ref, b_ref, o_ref):
    a = act_ref[...]
    a = a * _sigmoid(a)
    o_ref[0] = _dot(a.astype(BF16), w_ref[0].astype(BF16)) + b_ref[0]


def _modulation(c, c_ctx, mod_w, mod_b):
    act = jnp.zeros((MOD_ROWS, D_MODEL), F32).at[:BATCH].set(c).at[CTX_ROW].set(c_ctx)
    out = pl.pallas_call(
        _mod_kernel,
        grid=(DEPTH, N_MOD * D_MODEL // MOD_TN),
        in_specs=[
            pl.BlockSpec((MOD_ROWS, D_MODEL), lambda l, j: (0, 0)),
            pl.BlockSpec((1, D_MODEL, MOD_TN), lambda l, j: (l, 0, j)),
            pl.BlockSpec((1, 1, MOD_TN), lambda l, j: (l, 0, j)),
        ],
        out_specs=pl.BlockSpec((1, MOD_ROWS, MOD_TN), lambda l, j: (l, 0, j)),
        out_shape=jax.ShapeDtypeStruct((DEPTH, MOD_ROWS, N_MOD * D_MODEL), F32),
        compiler_params=_cparams("parallel", "parallel"),
        name="adaln_mod",
    )(act, mod_w, mod_b.reshape(DEPTH, 1, N_MOD * D_MODEL))
    return out.reshape(DEPTH, MOD_ROWS, N_MOD, D_MODEL)


def _qkv_weight_kernel(w_ref, wqk_ref, wvt_ref):
    j = pl.program_id(1)

    @pl.when(j < 2)
    def _():
        wqk_ref[0] = w_ref[0].astype(BF16)

    @pl.when(j == 2)
    def _():
        wvt_ref[0] = w_ref[0].T.astype(BF16)


def _qkv_weights(w_qkv):
    return pl.pallas_call(
        _qkv_weight_kernel,
        grid=(DEPTH, 3),
        in_specs=[pl.BlockSpec((1, D_MODEL, D_MODEL), lambda l, j: (l, 0, j))],
        out_specs=[
            pl.BlockSpec((1, D_MODEL, D_MODEL), lambda l, j: (l, 0, jnp.minimum(j, 1))),
            pl.BlockSpec((1, D_MODEL, D_MODEL), lambda l, j: (l, 0, 0)),
        ],
        out_shape=[jax.ShapeDtypeStruct((DEPTH, D_MODEL, 2 * D_MODEL), BF16),
                   jax.ShapeDtypeStruct((DEPTH, D_MODEL, D_MODEL), BF16)],
        compiler_params=_cparams("parallel", "arbitrary"),
        name="qkv_weight_prep",
    )(w_qkv)


def _rms_modulate(x, g, shift, scale):
    ms = jnp.mean(x * x, axis=-1, keepdims=True)
    y = x * lax.rsqrt(ms + NORM_EPS) * g
    return y * (1.0 + scale) + shift


def _rope_block(xb, cs, sn):
    lane = lax.broadcasted_iota(jnp.int32, xb.shape, 1)
    partner = jnp.where((lane & 63) < 32, pltpu.roll(xb, 96, 1), pltpu.roll(xb, 32, 1))
    return xb * cs + partner * sn


def _moe_residual(x1, y0, y1, g, gate_ffn):
    y = g[:, 0:1] * _unpack_bf16_pair(y0) + g[:, 1:2] * _unpack_bf16_pair(y1)
    return x1 + gate_ffn * y


def _qkv_kernel(*refs, rope, first):
    n_stream = 2 if first else 4
    stream_refs, refs = refs[:n_stream], refs[n_stream:]
    mod_ref, g_ref, wqk_ref, wvt_ref, cs_ref, sn_ref, x_ref, q_ref, k_ref, vt_ref = refs
    m = mod_ref[0]
    for r0 in range(0, TM, QKV_BLK):
        rows = pl.ds(r0, QKV_BLK)
        if first:
            xl_ref, xc_ref = stream_refs
            x = jnp.where(pl.program_id(0) < LAT_TILES, xl_ref[rows, :], xc_ref[rows, :])
        else:
            x1_ref, yg_ref, gt_ref, modp_ref = stream_refs
            x = _moe_residual(x1_ref[rows, :], yg_ref[0, rows, :], yg_ref[1, rows, :], gt_ref[rows, :],
                              modp_ref[0][5:6])
        x_ref[rows, :] = x
        h = _rms_modulate(x, g_ref[...], m[0:1], m[1:2]).astype(BF16)
        for idx, out in enumerate((q_ref, k_ref)):
            acc = _dot(h, wqk_ref[0, :, idx * D_MODEL:(idx + 1) * D_MODEL])
            if rope:
                cs = cs_ref[rows, :]
                sn = sn_ref[rows, :]
                acc = jnp.concatenate(
                    [_rope_block(acc[:, j * HB:(j + 1) * HB], cs, sn) for j in range(N_HB)], axis=1)
            if idx == 0:
                acc = acc * Q_SCALE
            out[rows, :] = acc.astype(BF16)
        vt_ref[:, rows] = _dot_nt(wvt_ref[0], h).astype(BF16)


def _qkv(stream, mod_l, g, w_qk, w_vt, layer, cs, sn, *, rope):
    first = len(stream) == 2
    n_tiles = T_ALL // TM
    tok = lambda i: (i, 0)
    const2 = lambda i: (0, 0)
    mod_spec = pl.BlockSpec((1, N_MOD, D_MODEL), lambda i: (_mod_row(i), 0, 0))
    rope_idx = lambda i: (jnp.where(i < LAT_TILES, i % TILES_PER_BATCH, TILES_PER_BATCH), 0)
    out = jax.ShapeDtypeStruct((T_ALL, D_MODEL), BF16)
    wmap = lambda i: (layer, 0, 0)
    if first:
        stream_specs = [
            pl.BlockSpec((TM, D_MODEL), lambda i: (jnp.minimum(i, LAT_TILES - 1), 0)),
            pl.BlockSpec((TM, D_MODEL), lambda i: (jnp.maximum(i - LAT_TILES, 0), 0)),
        ]
    else:
        stream_specs = [
            pl.BlockSpec((TM, D_MODEL), tok),
            pl.BlockSpec((2, TM, D_PACK), lambda i: (0, i, 0)),
            pl.BlockSpec((TM, 2), tok),
            mod_spec,
        ]
    return pl.pallas_call(
        functools.partial(_qkv_kernel, rope=rope, first=first),
        grid=(n_tiles,),
        in_specs=stream_specs + [
            mod_spec,
            pl.BlockSpec((1, D_MODEL), const2),
            pl.BlockSpec((1, D_MODEL, 2 * D_MODEL), wmap),
            pl.BlockSpec((1, D_MODEL, D_MODEL), wmap),
            pl.BlockSpec((TM, HB), rope_idx),
            pl.BlockSpec((TM, HB), rope_idx),
        ],
        out_specs=[pl.BlockSpec((TM, D_MODEL), tok), pl.BlockSpec((TM, D_MODEL), tok),
                   pl.BlockSpec((TM, D_MODEL), tok), pl.BlockSpec((D_MODEL, TM), lambda i: (0, i))],
        out_shape=[jax.ShapeDtypeStruct((T_ALL, D_MODEL), F32), out, out,
                   jax.ShapeDtypeStruct((D_MODEL, T_ALL), BF16)],
        compiler_params=_cparams("parallel"),
        name="norm_qkv_rope" if rope else "norm_qkv",
    )(*stream, mod_l, g.reshape(1, D_MODEL), w_qk, w_vt, cs, sn)


def _rope_tables():
    t = np.arange(SEQ)
    row = (t // GRID_W).astype(np.float32)
    col = (t % GRID_W).astype(np.float32)
    n_freq = DIFF_HEAD_DIM // 4
    inv_freq = np.float32(ROPE_THETA) ** (-np.arange(n_freq, dtype=np.float32) / np.float32(n_freq))
    ang = np.concatenate([row[:, None] * inv_freq, col[:, None] * inv_freq], axis=-1)
    cos, sin = np.cos(ang), np.sin(ang)
    cs = np.concatenate([cos, cos, cos, cos], axis=-1)
    sn = np.concatenate([-sin, sin, -sin, sin], axis=-1)
    cs = np.concatenate([cs, np.ones((TM, HB), np.float32)], axis=0)
    sn = np.concatenate([sn, np.zeros((TM, HB), np.float32)], axis=0)
    return jnp.asarray(cs, F32), jnp.asarray(sn, F32)


Q_BLK = 256
KEY_BLK = 256
SUBLANES = 8
DIFF_Q_BLK = 256
DIFF_Q_PER_TRIP = 8
DIFF_KEY_BLK = 1024


def _colwise(reduce_fn, x):
    return reduce_fn(x.reshape(x.shape[0] // SUBLANES, SUBLANES, x.shape[1]), axis=0)


def _key_blocks(k_ref, vt_ref, key_start=0, n_keys=None, bias=None, blk=KEY_BLK):
    n_keys = k_ref.shape[0] if n_keys is None else n_keys
    blk = min(blk, n_keys)
    blocks = []
    for c, s0 in enumerate(range(0, n_keys, blk)):
        blocks.append((
            lambda s0=s0: k_ref[pl.ds(key_start + s0, blk), :],
            lambda s0=s0: vt_ref[:, pl.ds(key_start + s0, blk)],
            None if bias is None else functools.partial(bias, c),
            blk,
        ))
    return blocks


def _attention_t(operands, s_scr):
    def rows_of(u, c):
        sizes = [blk[3] for blk in operands[u][1]]
        return slice(sum(sizes[:c]), sum(sizes[:c + 1]))

    def score(u, c, mx):
        qm, blocks = operands[u]
        k_tile, _, bias, _ = blocks[c]
        st = _dot_nt(k_tile(), qm)
        if bias is not None:
            st = st + bias()
        s_scr[u % 2, rows_of(u, c), :n_q] = st
        return jnp.maximum(mx, _colwise(jnp.max, st))

    def expo(u, c, m, acc, den):
        vt_tile = operands[u][1][c][1]
        e = jnp.exp2(s_scr[u % 2, rows_of(u, c), :n_q] - m)
        o = _dot(vt_tile(), e.astype(BF16))
        return (o if acc is None else acc + o), den + _colwise(jnp.sum, e)

    n_q = operands[0][0].shape[0]
    neg = jnp.full((SUBLANES, n_q), -jnp.inf, F32)
    zero = jnp.zeros((SUBLANES, n_q), F32)
    n_blocks = [len(blocks) for _, blocks in operands]
    mx = neg
    for c in range(n_blocks[0]):
        mx = score(0, c, mx)
    outs = []
    for u in range(len(operands)):
        m = jnp.max(mx, axis=0, keepdims=True)
        acc, den, mx = None, zero, neg
        n_next = n_blocks[u + 1] if u + 1 < len(operands) else 0
        for c in range(max(n_blocks[u], n_next)):
            if c < n_next:
                mx = score(u + 1, c, mx)
            if c < n_blocks[u]:
                acc, den = expo(u, c, m, acc, den)
        outs.append(acc / jnp.sum(den, axis=0, keepdims=True))
    return outs


def _half_masks(shape):
    lane = lax.broadcasted_iota(jnp.int32, shape, 1)
    return lane < 64, lane >= 64


N_DR = 2 * NA_WIN_ROWS - 1
N_DC = 2 * NA_WIN_COLS - 1
PAIR_MASK_FIRST = N_DR - 1
PAIR_MASK_SECOND = N_DR
PAIR_MASKED = N_DR + 1
N_PAIRS = N_DR + 2
NA_KEY_BLK = 768
DR_LO = NA_WIN_ROWS - 1 - NA_WIN_ROWS // 2
DR_HI = DR_LO + NA_WIN_ROWS - 1


def _na_pair_table(rpb):
    h = rpb.shape[0]
    w = GRID_W
    col = np.arange(w)
    dc = np.clip(col[:, None] - col[None, :] + NA_WIN_COLS - 1, 0, N_DC - 1)
    onehot = (dc[:, :, None] == np.arange(N_DC)).astype(np.float32)
    t = jnp.einsum("hdc,kqc->hdkq", rpb.astype(F32), onehot, precision=lax.Precision.HIGHEST)
    cstart = np.clip(col - NA_WIN_COLS // 2, 0, w - NA_WIN_COLS)
    cvalid = (col[:, None] >= cstart[None, :]) & (col[:, None] < cstart[None, :] + NA_WIN_COLS)
    tt = jnp.where(cvalid, t * LOG2E, MASK_VALUE)
    masked = jnp.full((h, 1, w, w), MASK_VALUE, F32)
    regular = jnp.concatenate([tt[:, 1:], tt[:, :-1]], axis=-1)
    first = jnp.concatenate([masked, tt[:, DR_HI:DR_HI + 1]], axis=-1)
    second = jnp.concatenate([tt[:, DR_LO:DR_LO + 1], masked], axis=-1)
    return jnp.concatenate([regular, first, second, jnp.concatenate([masked, masked], axis=-1)], axis=1)


def _na_pair_entries(q_row0, band_row0, band_rows):
    entries = []
    for j in range(band_rows):
        kr = band_row0 + j
        row = []
        for a in range(0, NA_RB, 2):
            ok, d = [], []
            for r in (q_row0 + a, q_row0 + a + 1):
                r0 = min(max(r - NA_WIN_ROWS // 2, 0), ROWS - NA_WIN_ROWS)
                ok.append(r0 <= kr < r0 + NA_WIN_ROWS)
                d.append(kr - r + NA_WIN_ROWS - 1)
            if ok[0] and ok[1]:
                row.append(d[0] - 1)
            elif ok[1]:
                assert d[1] == DR_HI
                row.append(PAIR_MASK_FIRST)
            elif ok[0]:
                assert d[0] == DR_LO
                row.append(PAIR_MASK_SECOND)
            else:
                row.append(PAIR_MASKED)
        entries.append(row)
    return entries


def _na_kernel(tab_ref, q_ref, k_ref, vt_ref, qc_ref, kc_ref, vct_ref, o_ref, oc_ref, s_scr):
    n_blocks = ROWS // NA_RB
    ctx_blocks = _key_blocks(kc_ref, vct_ref)

    def operands(q_start, band_start, band_rows, entries):
        q = q_ref[pl.ds(q_start, NA_Q), :]
        n_band = band_rows * GRID_W
        blk = NA_KEY_BLK if n_band % NA_KEY_BLK == 0 else n_band
        rows_per_chunk = blk // GRID_W
        ops = []
        for hh, msk in enumerate(_half_masks(q.shape)):
            def bias(c, hh=hh):
                rows = entries[c * rows_per_chunk:(c + 1) * rows_per_chunk]
                return jnp.concatenate(
                    [jnp.concatenate([tab_ref[hh, e] for e in row], axis=1) for row in rows], axis=0)

            band = _key_blocks(k_ref, vt_ref, band_start, n_band, bias, blk)
            ops.append((jnp.where(msk, q, jnp.zeros_like(q)), band + ctx_blocks))
        return ops

    def merge_heads(o_lo, o_hi):
        feat = lax.broadcasted_iota(jnp.int32, o_lo.shape, 0)
        return jnp.where(feat < HB // 2, o_lo, o_hi).T.astype(BF16)

    last_band = ROWS - NA_WIN_ROWS
    interior = _na_pair_entries(NA_RB, 0, NA_BAND_ROWS)
    blocks = [(0, 0, NA_WIN_ROWS, _na_pair_entries(0, 0, NA_WIN_ROWS))]
    blocks += [(rb * NA_Q, (rb - 1) * NA_Q, NA_BAND_ROWS, interior) for rb in range(1, n_blocks - 1)]
    blocks.append(((n_blocks - 1) * NA_Q, last_band * GRID_W, NA_WIN_ROWS,
                   _na_pair_entries(ROWS - NA_RB, last_band, NA_WIN_ROWS)))
    qc = qc_ref[...]
    ctx_ops = [(jnp.where(msk, qc, jnp.zeros_like(qc)), ctx_blocks) for msk in _half_masks(qc.shape)]
    outs = _attention_t([op for blk in blocks for op in operands(*blk)] + ctx_ops, s_scr)
    for i, blk in enumerate(blocks):
        o_ref[pl.ds(blk[0], NA_Q), :] = merge_heads(outs[2 * i], outs[2 * i + 1])
    oc_ref[...] = merge_heads(outs[-2], outs[-1])


def _na_attention(tables, layer_na, q, k, vt):
    assert NA_Q == Q_BLK and CTX_LEN == Q_BLK
    lat = lambda hp, b: (b, hp)
    ctx = lambda hp, b: (T_LAT // CTX_LEN + b, hp)
    return pl.pallas_call(
        _na_kernel,
        grid=(N_HB, BATCH),
        in_specs=[
            pl.BlockSpec((2, N_PAIRS, GRID_W, 2 * GRID_W), lambda hp, b: (layer_na * N_HB + hp, 0, 0, 0)),
            pl.BlockSpec((SEQ, HB), lat),
            pl.BlockSpec((SEQ, HB), lat),
            pl.BlockSpec((HB, SEQ), lambda hp, b: (hp, b)),
            pl.BlockSpec((CTX_LEN, HB), ctx),
            pl.BlockSpec((CTX_LEN, HB), ctx),
            pl.BlockSpec((HB, CTX_LEN), lambda hp, b: (hp, T_LAT // CTX_LEN + b)),
        ],
        out_specs=[pl.BlockSpec((SEQ, HB), lat), pl.BlockSpec((CTX_LEN, HB), lat)],
        out_shape=[jax.ShapeDtypeStruct((T_LAT, D_MODEL), BF16), jax.ShapeDtypeStruct((T_CTX, D_MODEL), BF16)],
        scratch_shapes=[pltpu.VMEM((2, NA_BAND + CTX_LEN, Q_BLK), F32)],
        compiler_params=_cparams("parallel", "parallel"),
        name="na_attention",
    )(tables, q, k, vt, q, k, vt)


def _diff_lambda(lam_ref, lam_init):
    lam = lam_ref[...]
    a = jnp.sum(lam[0:1] * lam[1:2], axis=-1, keepdims=True)
    b = jnp.sum(lam[2:3] * lam[3:4], axis=-1, keepdims=True)
    return jnp.exp(a) - jnp.exp(b) + lam_init


def _diff_kernel(lam_ref, sg_ref, q_ref, k_ref, vt_ref, qc_ref, kc_ref, vct_ref, o_ref, *rest, lam_init):
    *maybe_oc_ref, s_scr = rest
    lam = _diff_lambda(lam_ref, lam_init)
    sg_col = sg_ref[...]

    def rows(qs, pieces):
        blocks = [blk for k_r, vt_r in pieces for blk in _key_blocks(k_r, vt_r, blk=DIFF_KEY_BLK)]
        outs = _attention_t([(jnp.where(msk, q, jnp.zeros_like(q)), blocks)
                             for q in qs for msk in _half_masks(q.shape)], s_scr)
        res = []
        for o1, o2 in zip(outs[0::2], outs[1::2]):
            ot = o1 - lam * o2
            ms = jnp.mean(ot * ot, axis=0, keepdims=True)
            ot = ot * lax.rsqrt(ms + NORM_EPS) * sg_col * (1.0 - lam_init)
            res.append(ot.T.astype(BF16))
        return res

    def body(t, carry):
        starts = [pl.multiple_of((t * DIFF_Q_PER_TRIP + j) * DIFF_Q_BLK, DIFF_Q_BLK)
                  for j in range(DIFF_Q_PER_TRIP)]
        outs = rows([q_ref[pl.ds(r0, DIFF_Q_BLK), :] for r0 in starts], [(k_ref, vt_ref), (kc_ref, vct_ref)])
        for r0, o in zip(starts, outs):
            o_ref[pl.ds(r0, DIFF_Q_BLK), :] = o
        return carry

    lax.fori_loop(0, SEQ // (DIFF_Q_BLK * DIFF_Q_PER_TRIP), body, 0)
    if maybe_oc_ref:
        oc_ref, = maybe_oc_ref
        oc_ref[...], = rows([qc_ref[...]], [(kc_ref, vct_ref)])


def _diff_attention(lam, sg, q, k, vt, lam_init, *, ctx_out):
    lat = lambda b, h: (b, h)
    ctx = lambda b, h: (T_LAT // CTX_LEN + b, h)
    lat_t = lambda b, h: (h, b)
    ctx_t = lambda b, h: (h, T_LAT // CTX_LEN + b)
    out_specs = [pl.BlockSpec((SEQ, HB), lat)]
    out_shape = [jax.ShapeDtypeStruct((T_LAT, D_MODEL), BF16)]
    if ctx_out:
        out_specs.append(pl.BlockSpec((CTX_LEN, HB), lat))
        out_shape.append(jax.ShapeDtypeStruct((T_CTX, D_MODEL), BF16))
    return pl.pallas_call(
        functools.partial(_diff_kernel, lam_init=lam_init),
        grid=(BATCH, DIFF_HEADS),
        in_specs=[
            pl.BlockSpec(lam.shape, lambda b, h: (0, 0)),
            pl.BlockSpec(sg.shape, lambda b, h: (0, 0)),
            pl.BlockSpec((SEQ, HB), lat),
            pl.BlockSpec((SEQ, HB), lat),
            pl.BlockSpec((HB, SEQ), lat_t),
            pl.BlockSpec((CTX_LEN, HB), ctx),
            pl.BlockSpec((CTX_LEN, HB), ctx),
            pl.BlockSpec((HB, CTX_LEN), ctx_t),
        ],
        out_specs=out_specs,
        out_shape=out_shape,
        scratch_shapes=[pltpu.VMEM((2, SEQ + CTX_LEN, DIFF_Q_BLK), F32)],
        compiler_params=_cparams("parallel", "parallel"),
        name="diff_attention",
    )(lam, sg, q, k, vt, q, k, vt)


def _route_rows(logits_t, bias_col):
    s = _sigmoid(logits_t)
    sel = s + bias_col
    sel_r = [sel[e:e + 1] for e in range(N_EXPERTS)]
    s_r = [s[e:e + 1] for e in range(N_EXPERTS)]
    group_scores = []
    for g in range(N_GROUPS):
        v = sel_r[g * EXPERTS_PER_GROUP:(g + 1) * EXPERTS_PER_GROUP]
        pairs = [v[a] + v[b] for a in range(EXPERTS_PER_GROUP) for b in range(a + 1, EXPERTS_PER_GROUP)]
        group_scores.append(functools.reduce(jnp.maximum, pairs))
    best = group_scores[0]
    grp = jnp.zeros(best.shape, jnp.int32)
    for g in range(1, N_GROUPS):
        upd = group_scores[g] > best
        best = jnp.where(upd, group_scores[g], best)
        grp = jnp.where(upd, g, grp)

    def pick(rows, j):
        out = rows[j]
        for g in range(1, N_GROUPS):
            out = jnp.where(grp == g, rows[g * EXPERTS_PER_GROUP + j], out)
        return out

    w = [pick(sel_r, j) for j in range(EXPERTS_PER_GROUP)]
    sc = [pick(s_r, j) for j in range(EXPERTS_PER_GROUP)]

    def argmax_first(vals):
        bv = vals[0]
        bi = jnp.zeros(bv.shape, jnp.int32)
        for j in range(1, len(vals)):
            upd = vals[j] > bv
            bv = jnp.where(upd, vals[j], bv)
            bi = jnp.where(upd, j, bi)
        return bi

    i1 = argmax_first(w)
    i2 = argmax_first([jnp.where(i1 == j, -jnp.inf, w[j]) for j in range(EXPERTS_PER_GROUP)])

    def take(vals, idx):
        out = vals[0]
        for j in range(1, len(vals)):
            out = jnp.where(idx == j, vals[j], out)
        return out

    s1 = take(sc, i1)
    s2 = take(sc, i2)
    den = s1 + s2
    return grp * EXPERTS_PER_GROUP + i1, grp * EXPERTS_PER_GROUP + i2, s1 / den, s2 / den


def _oproj_kernel(o_ref, oc_ref, x_ref, mod_ref, w_ref, g_ref, rw_ref, rb_ref, tri_ref,
                  x1_ref, h_ref, ri_ref, rf_ref, cnt_ref, carry_ref):
    i = pl.program_id(0)

    @pl.when(i == 0)
    def _():
        carry_ref[...] = jnp.zeros_like(carry_ref)

    m = mod_ref[0]
    rw = rw_ref[...].astype(BF16)
    ri_ref[...] = jnp.zeros_like(ri_ref)
    rf_ref[...] = jnp.zeros_like(rf_ref)
    carry = carry_ref[...]
    for r0 in range(0, TM, ROUTE_BLK):
        rows = pl.ds(r0, ROUTE_BLK)
        o = jnp.where(i < LAT_TILES, o_ref[rows, :], oc_ref[rows, :])
        x1 = x_ref[rows, :] + m[2:3] * _dot(o, w_ref[0])
        x1_ref[rows, :] = x1
        h = _rms_modulate(x1, g_ref[...], m[3:4], m[4:5])
        hb = h.astype(BF16)
        h_ref[rows, :] = _pack_bf16_pair(h)

        logits = _dot(hb, rw)
        logits_t = logits.T[:N_EXPERTS]
        e0, e1, g0, g1 = _route_rows(logits_t, rb_ref[...])

        eiota = lax.broadcasted_iota(jnp.int32, (N_EXPERTS, ROUTE_BLK), 0)
        oh0 = eiota == e0
        oh1 = eiota == e1
        oh = jnp.where(oh0 | oh1, 1.0, 0.0)
        before = _dot(oh.astype(BF16), tri_ref[...]) + carry
        rank0 = jnp.sum(jnp.where(oh0, before, 0.0), axis=0, keepdims=True)
        rank1 = jnp.sum(jnp.where(oh1, before, 0.0), axis=0, keepdims=True)
        carry = carry + jnp.sum(oh, axis=1, keepdims=True)

        ri_ref[0:1, rows] = e0
        ri_ref[1:2, rows] = e1
        ri_ref[2:3, rows] = rank0.astype(jnp.int32)
        ri_ref[3:4, rows] = rank1.astype(jnp.int32)
        rf_ref[0:1, rows] = g0
        rf_ref[1:2, rows] = g1
    carry_ref[...] = carry
    cnt_ref[...] = jnp.broadcast_to(carry, cnt_ref.shape)


def _oproj_route(o, o_ctx, x, mod_l, w_o, layer, g, rw_pad, rb_col, tri, n_tok):
    n_tiles = n_tok // TM
    tok = lambda i: (i, 0)
    const2 = lambda i: (0, 0)
    lane_tok = lambda i: (0, i)
    return pl.pallas_call(
        _oproj_kernel,
        grid=(n_tiles,),
        in_specs=[
            pl.BlockSpec((TM, D_MODEL), lambda i: (jnp.minimum(i, LAT_TILES - 1), 0)),
            pl.BlockSpec((TM, D_MODEL), lambda i: (jnp.maximum(i - LAT_TILES, 0), 0)),
            pl.BlockSpec((TM, D_MODEL), tok),
            pl.BlockSpec((1, N_MOD, D_MODEL), lambda i: (_mod_row(i), 0, 0)),
            pl.BlockSpec((1, D_MODEL, D_MODEL), lambda i: (layer, 0, 0)),
            pl.BlockSpec((1, D_MODEL), const2),
            pl.BlockSpec((D_MODEL, HB), const2),
            pl.BlockSpec((N_EXPERTS, 1), const2),
            pl.BlockSpec((ROUTE_BLK, ROUTE_BLK), const2),
        ],
        out_specs=[
            pl.BlockSpec((TM, D_MODEL), tok),
            pl.BlockSpec((TM, D_PACK), tok),
            pl.BlockSpec((8, TM), lane_tok),
            pl.BlockSpec((8, TM), lane_tok),
            pl.BlockSpec((N_EXPERTS, HB), const2),
        ],
        out_shape=[
            jax.ShapeDtypeStruct((n_tok, D_MODEL), F32),
            jax.ShapeDtypeStruct((n_tok, D_PACK), jnp.uint32),
            jax.ShapeDtypeStruct((8, n_tok), jnp.int32),
            jax.ShapeDtypeStruct((8, n_tok), F32),
            jax.ShapeDtypeStruct((N_EXPERTS, HB), F32),
        ],
        scratch_shapes=[pltpu.VMEM((N_EXPERTS, 1), F32)],
        compiler_params=_cparams("arbitrary"),
        name="oproj_norm_route",
    )(o, o if o_ctx is None else o_ctx, x, mod_l, w_o, g.reshape(1, D_MODEL), rw_pad, rb_col, tri)


def _moe_kernel(te_ref, tv_ref, xs_ref, wg_ref, wu_ref, wd_ref, ys_ref, wgb, wub, wdb):
    i = pl.program_id(0)
    valid = tv_ref[i]
    e = te_ref[i]
    prev = te_ref[jnp.maximum(i - 1, 0)]
    half = TM // 2

    @pl.when((i == 0) | (e != prev))
    def _():
        wgb[...] = wg_ref[0, 0].astype(BF16)
        wub[...] = wu_ref[0, 0].astype(BF16)
        wdb[...] = wd_ref[0, 0].astype(BF16)

    def ffn(rows, x):
        x = _unpack_bf16_pair(x).astype(BF16)
        a = _dot(x, wgb[...])
        u = _dot(x, wub[...])
        hm = (a * _sigmoid(a) * u).astype(BF16)
        ys_ref[rows, :] = _pack_bf16_pair(_dot(hm, wdb[...]))

    @pl.when(valid == TM)
    def _():
        for r0 in range(0, TM, half):
            ffn(pl.ds(r0, half), xs_ref[pl.ds(r0, half), :])

    for r0 in range(0, TM, half):
        rows = pl.ds(r0, half)

        @pl.when((valid < TM) & (valid >= r0 + half))
        def _():
            ffn(rows, xs_ref[rows, :])

        @pl.when((valid > r0) & (valid < r0 + half))
        def _():
            row = r0 + lax.broadcasted_iota(jnp.int32, (half, 1), 0)
            x = xs_ref[rows, :]
            ffn(rows, jnp.where(row < valid, x, jnp.zeros_like(x)))

        @pl.when(valid <= r0)
        def _():
            ys_ref[rows, :] = jnp.zeros((half, D_PACK), jnp.uint32)


def _grouped_ffn(tile_expert, tile_valid, xs, w_gate, w_up, w_down, layer):
    n_tiles = xs.shape[0] // TM
    wmap = lambda i, te, tv: (layer, te[i], 0, 0)
    tok = lambda i, te, tv: (i, 0)
    return pl.pallas_call(
        _moe_kernel,
        grid_spec=pltpu.PrefetchScalarGridSpec(
            num_scalar_prefetch=2,
            grid=(n_tiles,),
            in_specs=[
                pl.BlockSpec((TM, D_PACK), tok),
                pl.BlockSpec((1, 1, D_MODEL, D_EXPERT), wmap),
                pl.BlockSpec((1, 1, D_MODEL, D_EXPERT), wmap),
                pl.BlockSpec((1, 1, D_EXPERT, D_MODEL), wmap),
            ],
            out_specs=pl.BlockSpec((TM, D_PACK), tok),
            scratch_shapes=[
                pltpu.VMEM((D_MODEL, D_EXPERT), BF16),
                pltpu.VMEM((D_MODEL, D_EXPERT), BF16),
                pltpu.VMEM((D_EXPERT, D_MODEL), BF16),
            ],
        ),
        out_shape=jax.ShapeDtypeStruct((xs.shape[0], D_PACK), jnp.uint32),
        compiler_params=_cparams("arbitrary"),
        name="grouped_ffn",
    )(tile_expert, tile_valid, xs, w_gate, w_up, w_down)


def _moe_layout(route_i, counts, n_tok):
    n_tiles = 2 * n_tok // TM + N_EXPERTS
    counts = counts.astype(jnp.int32)
    tiles_e = (counts + TM - 1) // TM
    tiles_end = jnp.cumsum(tiles_e)
    tile_start = tiles_end - tiles_e
    eid = jnp.arange(N_EXPERTS, dtype=jnp.int32)
    tok_oh = route_i[0:2][:, :, None] == eid
    pos = jnp.sum(jnp.where(tok_oh, tile_start * TM, 0), axis=-1) + route_i[2:4]
    tile_ids = jnp.arange(n_tiles, dtype=jnp.int32)
    n_used = tiles_end[-1]
    te = jnp.sum(tiles_end[None, :] <= jnp.minimum(tile_ids, n_used - 1)[:, None], axis=1).astype(jnp.int32)
    tile_oh = te[:, None] == eid
    cnt_t = jnp.sum(jnp.where(tile_oh, counts, 0), axis=1)
    start_t = jnp.sum(jnp.where(tile_oh, tile_start, 0), axis=1)
    tv = jnp.clip(cnt_t - (tile_ids - start_t) * TM, 0, TM)
    tv = jnp.where(tile_ids < n_used, tv, 0).astype(jnp.int32)
    return pos.astype(jnp.int32), te, tv, n_tiles


SC_CORES = 2
SC_SUBCORES = 16
SC_WORKERS = SC_CORES * SC_SUBCORES
SC_CHUNK = 32
SC_RING = 4


def _sc_mesh():
    return plsc.VectorSubcoreMesh(core_axis_name="c", subcore_axis_name="s")


def _sc_worker_indices(pos, n_tok):
    n_ch = n_tok // SC_WORKERS // SC_CHUNK
    return pos.reshape(2, SC_WORKERS, n_ch, SC_CHUNK).transpose(1, 2, 0, 3), n_ch


def _sc_dispatch(h, pos, n_slots):
    n_tok = h.shape[0]
    per_w = n_tok // SC_WORKERS
    pos_w, n_ch = _sc_worker_indices(pos, n_tok)
    assert n_ch % 2 == 0 and n_ch * SC_CHUNK * SC_WORKERS == n_tok

    @functools.partial(
        pl.kernel, mesh=_sc_mesh(), out_type=jax.ShapeDtypeStruct((n_slots, h.shape[1]), h.dtype),
        scratch_types=[pltpu.VMEM((n_ch, 2, SC_CHUNK), jnp.int32), pltpu.VMEM((2, SC_CHUNK, h.shape[1]), h.dtype),
                       pltpu.SemaphoreType.DMA((2,)), pltpu.SemaphoreType.DMA((2,))],
        name="sc_dispatch")
    def dispatch(h_hbm, pos_hbm, xs_hbm, idx_v, rows_v, load_sem, scat_sem):
        wid = lax.axis_index("s") * SC_CORES + lax.axis_index("c")
        base = wid * per_w
        pltpu.sync_copy(pos_hbm.at[wid], idx_v)

        def load(c, b):
            return pltpu.make_async_copy(h_hbm.at[pl.ds(base + c * SC_CHUNK, SC_CHUNK)], rows_v.at[b],
                                         load_sem.at[b])

        def scat(c, b, k):
            return pltpu.make_async_copy(rows_v.at[b], xs_hbm.at[idx_v.at[c, k]], scat_sem.at[b])

        load(0, 0).start()

        @pl.loop(0, n_ch, step=2)
        def _(c0):
            for b in range(2):
                c = c0 + b
                load(c, b).wait()
                scat(c, b, 0).start()
                scat(c, b, 1).start()

                @pl.when(c >= 1)
                def _():
                    scat(c - 1, 1 - b, 0).wait()
                    scat(c - 1, 1 - b, 1).wait()

                @pl.when(c + 1 < n_ch)
                def _():
                    load(c + 1, 1 - b).start()

        scat(n_ch - 1, 1, 0).wait()
        scat(n_ch - 1, 1, 1).wait()

    return dispatch(h, pos_w)


def _sc_combine_gather(ys, pos):
    n_tok = pos.shape[1]
    per_w = n_tok // SC_WORKERS
    pos_w, n_ch = _sc_worker_indices(pos, n_tok)
    n_items = 2 * n_ch
    assert n_ch * SC_CHUNK * SC_WORKERS == n_tok and n_items % SC_RING == 0 and SC_RING % 2 == 0

    @functools.partial(
        pl.kernel, mesh=_sc_mesh(), out_type=jax.ShapeDtypeStruct((2, n_tok, ys.shape[1]), ys.dtype),
        scratch_types=[pltpu.VMEM((n_ch, 2, SC_CHUNK), jnp.int32),
                       pltpu.VMEM((SC_RING, SC_CHUNK, ys.shape[1]), ys.dtype),
                       pltpu.SemaphoreType.DMA((SC_RING,)), pltpu.SemaphoreType.DMA((SC_RING,))],
        name="sc_combine_gather")
    def gather(ys_hbm, pos_hbm, yg_hbm, idx_v, rows_v, gath_sem, write_sem):
        wid = lax.axis_index("s") * SC_CORES + lax.axis_index("c")
        base = wid * per_w
        pltpu.sync_copy(pos_hbm.at[wid], idx_v)

        def gath(c, k, b):
            return pltpu.make_async_copy(ys_hbm.at[idx_v.at[c, k]], rows_v.at[b], gath_sem.at[b])

        def write(c, k, b):
            return pltpu.make_async_copy(rows_v.at[b], yg_hbm.at[k, pl.ds(base + c * SC_CHUNK, SC_CHUNK)],
                                         write_sem.at[b])

        for j in range(SC_RING - 1):
            gath(j // 2, j % 2, j).start()

        @pl.loop(0, n_items, step=SC_RING)
        def _(j0):
            for b in range(SC_RING):
                j = j0 + b
                gath(j // 2, b % 2, b).wait()
                write(j // 2, b % 2, b).start()
                pb = (b - 1) % SC_RING

                @pl.when(j >= 1)
                def _():
                    write((j - 1) // 2, (b - 1) % 2, pb).wait()

                @pl.when(j + SC_RING - 1 < n_items)
                def _():
                    gath((j + SC_RING - 1) // 2, (b - 1) % 2, pb).start()

        write(n_ch - 1, 1, (n_items - 1) % SC_RING).wait()

    return gather(ys, pos_w)


def _final_kernel(x1_ref, yg_ref, gt_ref, mod_ref, fg_ref, out_ref):
    x2 = _moe_residual(x1_ref[...], yg_ref[0], yg_ref[1], gt_ref[...], mod_ref[0][5:6])
    ms = jnp.mean(x2 * x2, axis=-1, keepdims=True)
    out_ref[...] = x2 * lax.rsqrt(ms + NORM_EPS) * fg_ref[...]


def _combine_final(x1, yg, gates_t, mod_l, final_g, n_tok):
    tok = lambda i: (i, 0)
    return pl.pallas_call(
        _final_kernel,
        grid=(n_tok // TM,),
        in_specs=[
            pl.BlockSpec((TM, D_MODEL), tok),
            pl.BlockSpec((2, TM, D_PACK), lambda i: (0, i, 0)),
            pl.BlockSpec((TM, 2), tok),
            pl.BlockSpec((1, N_MOD, D_MODEL), lambda i: (_mod_row(i), 0, 0)),
            pl.BlockSpec((1, D_MODEL), lambda i: (0, 0)),
        ],
        out_specs=pl.BlockSpec((TM, D_MODEL), tok),
        out_shape=jax.ShapeDtypeStruct((n_tok, D_MODEL), F32),
        compiler_params=_cparams("parallel"),
        name="moe_combine_final",
    )(x1, yg, gates_t, mod_l, final_g.reshape(1, D_MODEL))


def _diff_lambda_init(layer_idx):
    return 0.8 - 0.6 * math.exp(-0.3 * (layer_idx - 1))


def kernel(x, c, ctx, c_ctx, mod_w, mod_b, norm_mix, norm_ffn, w_qkv, w_o, na_rpb, diff_lambda, diff_subln,
           router_w, router_b, expert_w_gate, expert_w_up, expert_w_down, final_norm):
    mod = _modulation(c, c_ctx, mod_w, mod_b)
    stream = (x.reshape(T_LAT, D_MODEL), ctx.reshape(T_CTX, D_MODEL))
    w_qk_b, w_vt_b = _qkv_weights(w_qkv)
    w_o_b = w_o.astype(BF16)
    na_tables = _na_pair_table(na_rpb.reshape(-1, N_DR, N_DC))
    cs, sn = _rope_tables()
    rw_pad = jnp.zeros((D_MODEL, HB), F32).at[:, :N_EXPERTS].set(router_w)
    rb_col = router_b.reshape(N_EXPERTS, 1).astype(F32)
    blk = np.arange(ROUTE_BLK)
    tri = jnp.asarray(blk[:, None] < blk[None, :], BF16)

    for i in range(DEPTH):
        last = i == DEPTH - 1
        j = i // 2
        is_diff = i % 2 == 1
        xa, q, k, v = _qkv(stream, mod[i], norm_mix[i], w_qk_b, w_vt_b, i, cs, sn, rope=is_diff)
        if not is_diff:
            o, o_ctx = _na_attention(na_tables, j, q, k, v)
        else:
            lam_init = _diff_lambda_init(i + 1)
            sg = diff_subln[j].reshape(HB, 1)
            outs = _diff_attention(diff_lambda[j], sg, q, k, v, lam_init, ctx_out=not last)
            o, o_ctx = outs if not last else (outs[0], None)
        n_tok = T_LAT if last else T_ALL
        x1, h, route_i, route_f, cnt = _oproj_route(o, o_ctx, xa, mod[i], w_o_b, i, norm_ffn[i], rw_pad,
                                                    rb_col, tri, n_tok)
        pos, te, tv, n_tiles = _moe_layout(route_i, cnt[:, 0], n_tok)
        xs = _sc_dispatch(h, pos, n_tiles * TM)
        ys = _grouped_ffn(te, tv, xs, expert_w_gate, expert_w_up, expert_w_down, i)
        yg = _sc_combine_gather(ys, pos)
        stream = (x1, yg, route_f[0:2].T, mod[i])
    out = _combine_final(*stream, final_norm, T_LAT)
    return out.reshape(BATCH, SEQ, D_MODEL)
```

```python
import functools
import math

import jax
import jax.numpy as jnp
import numpy as np
from jax import lax
from jax.experimental import pallas as pl
from jax.experimental.pallas import tpu as pltpu
from jax.experimental.pallas import tpu_sc as plsc

F32 = jnp.float32
BF16 = jnp.bfloat16

D_MODEL = 1024
BATCH = 8
SEQ = 2048
DEPTH = 4
CTX_LEN = 256
GRID_W = 64
ROWS = SEQ // GRID_W
NA_WIN_ROWS = 8
NA_WIN_COLS = 16
DIFF_HEADS = 8
DIFF_HEAD_DIM = 64
ROPE_THETA = 10000.0
N_EXPERTS = 16
N_GROUPS = 4
EXPERTS_PER_GROUP = 4
D_EXPERT = 1024
N_MOD = 6
NORM_EPS = 1e-6

T_LAT = BATCH * SEQ
T_CTX = BATCH * CTX_LEN
T_ALL = T_LAT + T_CTX
TM = 512
QKV_BLK = TM // 2
ROUTE_BLK = TM // 2
LAT_TILES = T_LAT // TM
TILES_PER_BATCH = SEQ // TM
CTX_ROW = BATCH
MOD_ROWS = 16
MOD_TN = 1536
HB = 128
N_HB = D_MODEL // HB
NA_RB = 4
NA_BAND_ROWS = 12
NA_Q = NA_RB * GRID_W
NA_BAND = NA_BAND_ROWS * GRID_W
MASK_VALUE = -1e30
LOG2E = math.log2(math.e)
Q_SCALE = 0.125 * LOG2E
VMEM_LIMIT = 56 * 1024 * 1024


def _cparams(*sem):
    return pltpu.CompilerParams(dimension_semantics=sem, vmem_limit_bytes=VMEM_LIMIT)


def _dot(a, b):
    return jnp.dot(a, b, preferred_element_type=F32)


def _dot_nt(a, b):
    return lax.dot_general(a, b, (((1,), (1,)), ((), ())), preferred_element_type=F32)


def _sigmoid(x):
    return 1.0 / (1.0 + jnp.exp(-x))


D_PACK = D_MODEL // 2


def _pack_bf16_pair(x):
    lo = lax.bitcast_convert_type(x[:, :D_PACK].astype(BF16).astype(F32), jnp.uint32)
    hi = lax.bitcast_convert_type(x[:, D_PACK:].astype(BF16).astype(F32), jnp.uint32)
    return (lo >> 16) | hi


def _unpack_bf16_pair(w):
    lo = lax.bitcast_convert_type(w << 16, F32)
    hi = lax.bitcast_convert_type(w & jnp.uint32(0xFFFF0000), F32)
    return jnp.concatenate([lo, hi], axis=1)


def _mod_row(i):
    return jnp.minimum(i // TILES_PER_BATCH, CTX_ROW)


def _mod_kernel(act_ref, w_ref, b_ref, o_ref):
    a = act_ref[...]
    a = a * _sigmoid(a)
    o_ref[0] = _dot(a.astype(BF16), w_ref[0].astype(BF16)) + b_ref[0]


def _modulation(c, c_ctx, mod_w, mod_b):
    act = jnp.zeros((MOD_ROWS, D_MODEL), F32).at[:BATCH].set(c).at[CTX_ROW].set(c_ctx)
    out = pl.pallas_call(
        _mod_kernel,
        grid=(DEPTH, N_MOD * D_MODEL // MOD_TN),
        in_specs=[
            pl.BlockSpec((MOD_ROWS, D_MODEL), lambda l, j: (0, 0)),
            pl.BlockSpec((1, D_MODEL, MOD_TN), lambda l, j: (l, 0, j)),
            pl.BlockSpec((1, 1, MOD_TN), lambda l, j: (l, 0, j)),
        ],
        out_specs=pl.BlockSpec((1, MOD_ROWS, MOD_TN), lambda l, j: (l, 0, j)),
        out_shape=jax.ShapeDtypeStruct((DEPTH, MOD_ROWS, N_MOD * D_MODEL), F32),
        compiler_params=_cparams("parallel", "parallel"),
        name="adaln_mod",
    )(act, mod_w, mod_b.reshape(DEPTH, 1, N_MOD * D_MODEL))
    return out.reshape(DEPTH, MOD_ROWS, N_MOD, D_MODEL)


def _qkv_weight_kernel(w_ref, wqk_ref, wvt_ref):
    j = pl.program_id(1)

    @pl.when(j < 2)
    def _():
        wqk_ref[0] = w_ref[0].astype(BF16)

    @pl.when(j == 2)
    def _():
        wvt_ref[0] = w_ref[0].T.astype(BF16)


def _qkv_weights(w_qkv):
    return pl.pallas_call(
        _qkv_weight_kernel,
        grid=(DEPTH, 3),
        in_specs=[pl.BlockSpec((1, D_MODEL, D_MODEL), lambda l, j: (l, 0, j))],
        out_specs=[
            pl.BlockSpec((1, D_MODEL, D_MODEL), lambda l, j: (l, 0, jnp.minimum(j, 1))),
            pl.BlockSpec((1, D_MODEL, D_MODEL), lambda l, j: (l, 0, 0)),
        ],
        out_shape=[jax.ShapeDtypeStruct((DEPTH, D_MODEL, 2 * D_MODEL), BF16),
                   jax.ShapeDtypeStruct((DEPTH, D_MODEL, D_MODEL), BF16)],
        compiler_params=_cparams("parallel", "arbitrary"),
        name="qkv_weight_prep",
    )(w_qkv)


def _rms_modulate(x, g, shift, scale):
    ms = jnp.mean(x * x, axis=-1, keepdims=True)
    y = x * lax.rsqrt(ms + NORM_EPS) * g
    return y * (1.0 + scale) + shift


def _rope_block(xb, cs, sn):
    lane = lax.broadcasted_iota(jnp.int32, xb.shape, 1)
    partner = jnp.where((lane & 63) < 32, pltpu.roll(xb, 96, 1), pltpu.roll(xb, 32, 1))
    return xb * cs + partner * sn


def _moe_residual(x1, y0, y1, g, gate_ffn):
    y = g[:, 0:1] * _unpack_bf16_pair(y0) + g[:, 1:2] * _unpack_bf16_pair(y1)
    return x1 + gate_ffn * y


def _qkv_kernel(*refs, rope, first):
    n_stream = 2 if first else 4
    stream_refs, refs = refs[:n_stream], refs[n_stream:]
    mod_ref, g_ref, wqk_ref, wvt_ref, cs_ref, sn_ref, x_ref, q_ref, k_ref, vt_ref = refs
    m = mod_ref[0]
    for r0 in range(0, TM, QKV_BLK):
        rows = pl.ds(r0, QKV_BLK)
        if first:
            xl_ref, xc_ref = stream_refs
            x = jnp.where(pl.program_id(0) < LAT_TILES, xl_ref[rows, :], xc_ref[rows, :])
        else:
            x1_ref, yg_ref, gt_ref, modp_ref = stream_refs
            x = _moe_residual(x1_ref[rows, :], yg_ref[0, rows, :], yg_ref[1, rows, :], gt_ref[rows, :],
                              modp_ref[0][5:6])
        x_ref[rows, :] = x
        h = _rms_modulate(x, g_ref[...], m[0:1], m[1:2]).astype(BF16)
        for idx, out in enumerate((q_ref, k_ref)):
            acc = _dot(h, wqk_ref[0, :, idx * D_MODEL:(idx + 1) * D_MODEL])
            if rope:
                cs = cs_ref[rows, :]
                sn = sn_ref[rows, :]
                acc = jnp.concatenate(
                    [_rope_block(acc[:, j * HB:(j + 1) * HB], cs, sn) for j in range(N_HB)], axis=1)
            if idx == 0:
                acc = acc * Q_SCALE
            out[rows, :] = acc.astype(BF16)
        vt_ref[:, rows] = _dot_nt(wvt_ref[0], h).astype(BF16)


def _qkv(stream, mod_l, g, w_qk, w_vt, layer, cs, sn, *, rope):
    first = len(stream) == 2
    n_tiles = T_ALL // TM
    tok = lambda i: (i, 0)
    const2 = lambda i: (0, 0)
    mod_spec = pl.BlockSpec((1, N_MOD, D_MODEL), lambda i: (_mod_row(i), 0, 0))
    rope_idx = lambda i: (jnp.where(i < LAT_TILES, i % TILES_PER_BATCH, TILES_PER_BATCH), 0)
    out = jax.ShapeDtypeStruct((T_ALL, D_MODEL), BF16)
    wmap = lambda i: (layer, 0, 0)
    if first:
        stream_specs = [
            pl.BlockSpec((TM, D_MODEL), lambda i: (jnp.minimum(i, LAT_TILES - 1), 0)),
            pl.BlockSpec((TM, D_MODEL), lambda i: (jnp.maximum(i - LAT_TILES, 0), 0)),
        ]
    else:
        stream_specs = [
            pl.BlockSpec((TM, D_MODEL), tok),
            pl.BlockSpec((2, TM, D_PACK), lambda i: (0, i, 0)),
            pl.BlockSpec((TM, 2), tok),
            mod_spec,
        ]
    return pl.pallas_call(
        functools.partial(_qkv_kernel, rope=rope, first=first),
        grid=(n_tiles,),
        in_specs=stream_specs + [
            mod_spec,
            pl.BlockSpec((1, D_MODEL), const2),
            pl.BlockSpec((1, D_MODEL, 2 * D_MODEL), wmap),
            pl.BlockSpec((1, D_MODEL, D_MODEL), wmap),
            pl.BlockSpec((TM, HB), rope_idx),
            pl.BlockSpec((TM, HB), rope_idx),
        ],
        out_specs=[pl.BlockSpec((TM, D_MODEL), tok), pl.BlockSpec((TM, D_MODEL), tok),
                   pl.BlockSpec((TM, D_MODEL), tok), pl.BlockSpec((D_MODEL, TM), lambda i: (0, i))],
        out_shape=[jax.ShapeDtypeStruct((T_ALL, D_MODEL), F32), out, out,
                   jax.ShapeDtypeStruct((D_MODEL, T_ALL), BF16)],
        compiler_params=_cparams("parallel"),
        name="norm_qkv_rope" if rope else "norm_qkv",
    )(*stream, mod_l, g.reshape(1, D_MODEL), w_qk, w_vt, cs, sn)


def _rope_tables():
    t = np.arange(SEQ)
    row = (t // GRID_W).astype(np.float32)
    col = (t % GRID_W).astype(np.float32)
    n_freq = DIFF_HEAD_DIM // 4
    inv_freq = np.float32(ROPE_THETA) ** (-np.arange(n_freq, dtype=np.float32) / np.float32(n_freq))
    ang = np.concatenate([row[:, None] * inv_freq, col[:, None] * inv_freq], axis=-1)
    cos, sin = np.cos(ang), np.sin(ang)
    cs = np.concatenate([cos, cos, cos, cos], axis=-1)
    sn = np.concatenate([-sin, sin, -sin, sin], axis=-1)
    cs = np.concatenate([cs, np.ones((TM, HB), np.float32)], axis=0)
    sn = np.concatenate([sn, np.zeros((TM, HB), np.float32)], axis=0)
    return jnp.asarray(cs, F32), jnp.asarray(sn, F32)


Q_BLK = 256
KEY_BLK = 256
SUBLANES = 8
DIFF_Q_BLK = 256
DIFF_Q_PER_TRIP = 8
DIFF_KEY_BLK = 1024


def _colwise(reduce_fn, x):
    return reduce_fn(x.reshape(x.shape[0] // SUBLANES, SUBLANES, x.shape[1]), axis=0)


def _key_blocks(k_ref, vt_ref, key_start=0, n_keys=None, bias=None, blk=KEY_BLK):
    n_keys = k_ref.shape[0] if n_keys is None else n_keys
    blk = min(blk, n_keys)
    blocks = []
    for c, s0 in enumerate(range(0, n_keys, blk)):
        blocks.append((
            lambda s0=s0: k_ref[pl.ds(key_start + s0, blk), :],
            lambda s0=s0: vt_ref[:, pl.ds(key_start + s0, blk)],
            None if bias is None else functools.partial(bias, c),
            blk,
        ))
    return blocks


def _attention_t(operands, s_scr):
    def rows_of(u, c):
        sizes = [blk[3] for blk in operands[u][1]]
        return slice(sum(sizes[:c]), sum(sizes[:c + 1]))

    def score(u, c, mx):
        qm, blocks = operands[u]
        k_tile, _, bias, _ = blocks[c]
        st = _dot_nt(k_tile(), qm)
        if bias is not None:
            st = st + bias()
        s_scr[u % 2, rows_of(u, c), :n_q] = st
        return jnp.maximum(mx, _colwise(jnp.max, st))

    def expo(u, c, m, acc, den):
        vt_tile = operands[u][1][c][1]
        e = jnp.exp2(s_scr[u % 2, rows_of(u, c), :n_q] - m)
        o = _dot(vt_tile(), e.astype(BF16))
        return (o if acc is None else acc + o), den + _colwise(jnp.sum, e)

    n_q = operands[0][0].shape[0]
    neg = jnp.full((SUBLANES, n_q), -jnp.inf, F32)
    zero = jnp.zeros((SUBLANES, n_q), F32)
    n_blocks = [len(blocks) for _, blocks in operands]
    mx = neg
    for c in range(n_blocks[0]):
        mx = score(0, c, mx)
    outs = []
    for u in range(len(operands)):
        m = jnp.max(mx, axis=0, keepdims=True)
        acc, den, mx = None, zero, neg
        n_next = n_blocks[u + 1] if u + 1 < len(operands) else 0
        for c in range(max(n_blocks[u], n_next)):
            if c < n_next:
                mx = score(u + 1, c, mx)
            if c < n_blocks[u]:
                acc, den = expo(u, c, m, acc, den)
        outs.append(acc / jnp.sum(den, axis=0, keepdims=True))
    return outs


def _half_masks(shape):
    lane = lax.broadcasted_iota(jnp.int32, shape, 1)
    return lane < 64, lane >= 64


N_DR = 2 * NA_WIN_ROWS - 1
N_DC = 2 * NA_WIN_COLS - 1
PAIR_MASK_FIRST = N_DR - 1
PAIR_MASK_SECOND = N_DR
PAIR_MASKED = N_DR + 1
N_PAIRS = N_DR + 2
NA_KEY_BLK = 768
DR_LO = NA_WIN_ROWS - 1 - NA_WIN_ROWS // 2
DR_HI = DR_LO + NA_WIN_ROWS - 1


def _na_pair_table(rpb):
    h = rpb.shape[0]
    w = GRID_W
    col = np.arange(w)
    dc = np.clip(col[:, None] - col[None, :] + NA_WIN_COLS - 1, 0, N_DC - 1)
    onehot = (dc[:, :, None] == np.arange(N_DC)).astype(np.float32)
    t = jnp.einsum("hdc,kqc->hdkq", rpb.astype(F32), onehot, precision=lax.Precision.HIGHEST)
    cstart = np.clip(col - NA_WIN_COLS // 2, 0, w - NA_WIN_COLS)
    cvalid = (col[:, None] >= cstart[None, :]) & (col[:, None] < cstart[None, :] + NA_WIN_COLS)
    tt = jnp.where(cvalid, t * LOG2E, MASK_VALUE)
    masked = jnp.full((h, 1, w, w), MASK_VALUE, F32)
    regular = jnp.concatenate([tt[:, 1:], tt[:, :-1]], axis=-1)
    first = jnp.concatenate([masked, tt[:, DR_HI:DR_HI + 1]], axis=-1)
    second = jnp.concatenate([tt[:, DR_LO:DR_LO + 1], masked], axis=-1)
    return jnp.concatenate([regular, first, second, jnp.concatenate([masked, masked], axis=-1)], axis=1)


def _na_pair_entries(q_row0, band_row0, band_rows):
    entries = []
    for j in range(band_rows):
        kr = band_row0 + j
        row = []
        for a in range(0, NA_RB, 2):
            ok, d = [], []
            for r in (q_row0 + a, q_row0 + a + 1):
                r0 = min(max(r - NA_WIN_ROWS // 2, 0), ROWS - NA_WIN_ROWS)
                ok.append(r0 <= kr < r0 + NA_WIN_ROWS)
                d.append(kr - r + NA_WIN_ROWS - 1)
            if ok[0] and ok[1]:
                row.append(d[0] - 1)
            elif ok[1]:
                assert d[1] == DR_HI
                row.append(PAIR_MASK_FIRST)
            elif ok[0]:
                assert d[0] == DR_LO
                row.append(PAIR_MASK_SECOND)
            else:
                row.append(PAIR_MASKED)
        entries.append(row)
    return entries


def _na_kernel(tab_ref, q_ref, k_ref, vt_ref, qc_ref, kc_ref, vct_ref, o_ref, oc_ref, s_scr):
    n_blocks = ROWS // NA_RB
    ctx_blocks = _key_blocks(kc_ref, vct_ref)

    def operands(q_start, band_start, band_rows, entries):
        q = q_ref[pl.ds(q_start, NA_Q), :]
        n_band = band_rows * GRID_W
        blk = NA_KEY_BLK if n_band % NA_KEY_BLK == 0 else n_band
        rows_per_chunk = blk // GRID_W
        ops = []
        for hh, msk in enumerate(_half_masks(q.shape)):
            def bias(c, hh=hh):
                rows = entries[c * rows_per_chunk:(c + 1) * rows_per_chunk]
                return jnp.concatenate(
                    [jnp.concatenate([tab_ref[hh, e] for e in row], axis=1) for row in rows], axis=0)

            band = _key_blocks(k_ref, vt_ref, band_start, n_band, bias, blk)
            ops.append((jnp.where(msk, q, jnp.zeros_like(q)), band + ctx_blocks))
        return ops

    def merge_heads(o_lo, o_hi):
        feat = lax.broadcasted_iota(jnp.int32, o_lo.shape, 0)
        return jnp.where(feat < HB // 2, o_lo, o_hi).T.astype(BF16)

    last_band = ROWS - NA_WIN_ROWS
    interior = _na_pair_entries(NA_RB, 0, NA_BAND_ROWS)
    blocks = [(0, 0, NA_WIN_ROWS, _na_pair_entries(0, 0, NA_WIN_ROWS))]
    blocks += [(rb * NA_Q, (rb - 1) * NA_Q, NA_BAND_ROWS, interior) for rb in range(1, n_blocks - 1)]
    blocks.append(((n_blocks - 1) * NA_Q, last_band * GRID_W, NA_WIN_ROWS,
                   _na_pair_entries(ROWS - NA_RB, last_band, NA_WIN_ROWS)))
    qc = qc_ref[...]
    ctx_ops = [(jnp.where(msk, qc, jnp.zeros_like(qc)), ctx_blocks) for msk in _half_masks(qc.shape)]
    outs = _attention_t([op for blk in blocks for op in operands(*blk)] + ctx_ops, s_scr)
    for i, blk in enumerate(blocks):
        o_ref[pl.ds(blk[0], NA_Q), :] = merge_heads(outs[2 * i], outs[2 * i + 1])
    oc_ref[...] = merge_heads(outs[-2], outs[-1])


def _na_attention(tables, layer_na, q, k, vt):
    assert NA_Q == Q_BLK and CTX_LEN == Q_BLK
    lat = lambda hp, b: (b, hp)
    ctx = lambda hp, b: (T_LAT // CTX_LEN + b, hp)
    return pl.pallas_call(
        _na_kernel,
        grid=(N_HB, BATCH),
        in_specs=[
            pl.BlockSpec((2, N_PAIRS, GRID_W, 2 * GRID_W), lambda hp, b: (layer_na * N_HB + hp, 0, 0, 0)),
            pl.BlockSpec((SEQ, HB), lat),
            pl.BlockSpec((SEQ, HB), lat),
            pl.BlockSpec((HB, SEQ), lambda hp, b: (hp, b)),
            pl.BlockSpec((CTX_LEN, HB), ctx),
            pl.BlockSpec((CTX_LEN, HB), ctx),
            pl.BlockSpec((HB, CTX_LEN), lambda hp, b: (hp, T_LAT // CTX_LEN + b)),
        ],
        out_specs=[pl.BlockSpec((SEQ, HB), lat), pl.BlockSpec((CTX_LEN, HB), lat)],
        out_shape=[jax.ShapeDtypeStruct((T_LAT, D_MODEL), BF16), jax.ShapeDtypeStruct((T_CTX, D_MODEL), BF16)],
        scratch_shapes=[pltpu.VMEM((2, NA_BAND + CTX_LEN, Q_BLK), F32)],
        compiler_params=_cparams("parallel", "parallel"),
        name="na_attention",
    )(tables, q, k, vt, q, k, vt)


def _diff_lambda(lam_ref, lam_init):
    lam = lam_ref[...]
    a = jnp.sum(lam[0:1] * lam[1:2], axis=-1, keepdims=True)
    b = jnp.sum(lam[2:3] * lam[3:4], axis=-1, keepdims=True)
    return jnp.exp(a) - jnp.exp(b) + lam_init


def _diff_kernel(lam_ref, sg_ref, q_ref, k_ref, vt_ref, qc_ref, kc_ref, vct_ref, o_ref, *rest, lam_init):
    *maybe_oc_ref, s_scr = rest
    lam = _diff_lambda(lam_ref, lam_init)
    sg_col = sg_ref[...]

    def rows(qs, pieces):
        blocks = [blk for k_r, vt_r in pieces for blk in _key_blocks(k_r, vt_r, blk=DIFF_KEY_BLK)]
        outs = _attention_t([(jnp.where(msk, q, jnp.zeros_like(q)), blocks)
                             for q in qs for msk in _half_masks(q.shape)], s_scr)
        res = []
        for o1, o2 in zip(outs[0::2], outs[1::2]):
            ot = o1 - lam * o2
            ms = jnp.mean(ot * ot, axis=0, keepdims=True)
            ot = ot * lax.rsqrt(ms + NORM_EPS) * sg_col * (1.0 - lam_init)
            res.append(ot.T.astype(BF16))
        return res

    def body(t, carry):
        starts = [pl.multiple_of((t * DIFF_Q_PER_TRIP + j) * DIFF_Q_BLK, DIFF_Q_BLK)
                  for j in range(DIFF_Q_PER_TRIP)]
        outs = rows([q_ref[pl.ds(r0, DIFF_Q_BLK), :] for r0 in starts], [(k_ref, vt_ref), (kc_ref, vct_ref)])
        for r0, o in zip(starts, outs):
            o_ref[pl.ds(r0, DIFF_Q_BLK), :] = o
        return carry

    lax.fori_loop(0, SEQ // (DIFF_Q_BLK * DIFF_Q_PER_TRIP), body, 0)
    if maybe_oc_ref:
        oc_ref, = maybe_oc_ref
        oc_ref[...], = rows([qc_ref[...]], [(kc_ref, vct_ref)])


def _diff_attention(lam, sg, q, k, vt, lam_init, *, ctx_out):
    lat = lambda b, h: (b, h)
    ctx = lambda b, h: (T_LAT // CTX_LEN + b, h)
    lat_t = lambda b, h: (h, b)
    ctx_t = lambda b, h: (h, T_LAT // CTX_LEN + b)
    out_specs = [pl.BlockSpec((SEQ, HB), lat)]
    out_shape = [jax.ShapeDtypeStruct((T_LAT, D_MODEL), BF16)]
    if ctx_out:
        out_specs.append(pl.BlockSpec((CTX_LEN, HB), lat))
        out_shape.append(jax.ShapeDtypeStruct((T_CTX, D_MODEL), BF16))
    return pl.pallas_call(
        functools.partial(_diff_kernel, lam_init=lam_init),
        grid=(BATCH, DIFF_HEADS),
        in_specs=[
            pl.BlockSpec(lam.shape, lambda b, h: (0, 0)),
            pl.BlockSpec(sg.shape, lambda b, h: (0, 0)),
            pl.BlockSpec((SEQ, HB), lat),
            pl.BlockSpec((SEQ, HB), lat),
            pl.BlockSpec((HB, SEQ), lat_t),
            pl.BlockSpec((CTX_LEN, HB), ctx),
            pl.BlockSpec((CTX_LEN, HB), ctx),
            pl.BlockSpec((HB, CTX_LEN), ctx_t),
        ],
        out_specs=out_specs,
        out_shape=out_shape,
        scratch_shapes=[pltpu.VMEM((2, SEQ + CTX_LEN, DIFF_Q_BLK), F32)],
        compiler_params=_cparams("parallel", "parallel"),
        name="diff_attention",
    )(lam, sg, q, k, vt, q, k, vt)


def _route_rows(logits_t, bias_col):
    s = _sigmoid(logits_t)
    sel = s + bias_col
    sel_r = [sel[e:e + 1] for e in range(N_EXPERTS)]
    s_r = [s[e:e + 1] for e in range(N_EXPERTS)]
    group_scores = []
    for g in range(N_GROUPS):
        v = sel_r[g * EXPERTS_PER_GROUP:(g + 1) * EXPERTS_PER_GROUP]
        pairs = [v[a] + v[b] for a in range(EXPERTS_PER_GROUP) for b in range(a + 1, EXPERTS_PER_GROUP)]
        group_scores.append(functools.reduce(jnp.maximum, pairs))
    best = group_scores[0]
    grp = jnp.zeros(best.shape, jnp.int32)
    for g in range(1, N_GROUPS):
        upd = group_scores[g] > best
        best = jnp.where(upd, group_scores[g], best)
        grp = jnp.where(upd, g, grp)

    def pick(rows, j):
        out = rows[j]
        for g in range(1, N_GROUPS):
            out = jnp.where(grp == g, rows[g * EXPERTS_PER_GROUP + j], out)
        return out

    w = [pick(sel_r, j) for j in range(EXPERTS_PER_GROUP)]
    sc = [pick(s_r, j) for j in range(EXPERTS_PER_GROUP)]

    def argmax_first(vals):
        bv = vals[0]
        bi = jnp.zeros(bv.shape, jnp.int32)
        for j in range(1, len(vals)):
            upd = vals[j] > bv
            bv = jnp.where(upd, vals[j], bv)
            bi = jnp.where(upd, j, bi)
        return bi

    i1 = argmax_first(w)
    i2 = argmax_first([jnp.where(i1 == j, -jnp.inf, w[j]) for j in range(EXPERTS_PER_GROUP)])

    def take(vals, idx):
        out = vals[0]
        for j in range(1, len(vals)):
            out = jnp.where(idx == j, vals[j], out)
        return out

    s1 = take(sc, i1)
    s2 = take(sc, i2)
    den = s1 + s2
    return grp * EXPERTS_PER_GROUP + i1, grp * EXPERTS_PER_GROUP + i2, s1 / den, s2 / den


def _oproj_kernel(o_ref, oc_ref, x_ref, mod_ref, w_ref, g_ref, rw_ref, rb_ref, tri_ref,
                  x1_ref, h_ref, ri_ref, rf_ref, cnt_ref, carry_ref):
    i = pl.program_id(0)

    @pl.when(i == 0)
    def _():
        carry_ref[...] = jnp.zeros_like(carry_ref)

    m = mod_ref[0]
    rw = rw_ref[...].astype(BF16)
    ri_ref[...] = jnp.zeros_like(ri_ref)
    rf_ref[...] = jnp.zeros_like(rf_ref)
    carry = carry_ref[...]
    for r0 in range(0, TM, ROUTE_BLK):
        rows = pl.ds(r0, ROUTE_BLK)
        o = jnp.where(i < LAT_TILES, o_ref[rows, :], oc_ref[rows, :])
        x1 = x_ref[rows, :] + m[2:3] * _dot(o, w_ref[0])
        x1_ref[rows, :] = x1
        h = _rms_modulate(x1, g_ref[...], m[3:4], m[4:5])
        hb = h.astype(BF16)
        h_ref[rows, :] = _pack_bf16_pair(h)

        logits = _dot(hb, rw)
        logits_t = logits.T[:N_EXPERTS]
        e0, e1, g0, g1 = _route_rows(logits_t, rb_ref[...])

        eiota = lax.broadcasted_iota(jnp.int32, (N_EXPERTS, ROUTE_BLK), 0)
        oh0 = eiota == e0
        oh1 = eiota == e1
        oh = jnp.where(oh0 | oh1, 1.0, 0.0)
        before = _dot(oh.astype(BF16), tri_ref[...]) + carry
        rank0 = jnp.sum(jnp.where(oh0, before, 0.0), axis=0, keepdims=True)
        rank1 = jnp.sum(jnp.where(oh1, before, 0.0), axis=0, keepdims=True)
        carry = carry + jnp.sum(oh, axis=1, keepdims=True)

        ri_ref[0:1, rows] = e0
        ri_ref[1:2, rows] = e1
        ri_ref[2:3, rows] = rank0.astype(jnp.int32)
        ri_ref[3:4, rows] = rank1.astype(jnp.int32)
        rf_ref[0:1, rows] = g0
        rf_ref[1:2, rows] = g1
    carry_ref[...] = carry
    cnt_ref[...] = jnp.broadcast_to(carry, cnt_ref.shape)


def _oproj_route(o, o_ctx, x, mod_l, w_o, layer, g, rw_pad, rb_col, tri, n_tok):
    n_tiles = n_tok // TM
    tok = lambda i: (i, 0)
    const2 = lambda i: (0, 0)
    lane_tok = lambda i: (0, i)
    return pl.pallas_call(
        _oproj_kernel,
        grid=(n_tiles,),
        in_specs=[
            pl.BlockSpec((TM, D_MODEL), lambda i: (jnp.minimum(i, LAT_TILES - 1), 0)),
            pl.BlockSpec((TM, D_MODEL), lambda i: (jnp.maximum(i - LAT_TILES, 0), 0)),
            pl.BlockSpec((TM, D_MODEL), tok),
            pl.BlockSpec((1, N_MOD, D_MODEL), lambda i: (_mod_row(i), 0, 0)),
            pl.BlockSpec((1, D_MODEL, D_MODEL), lambda i: (layer, 0, 0)),
            pl.BlockSpec((1, D_MODEL), const2),
            pl.BlockSpec((D_MODEL, HB), const2),
            pl.BlockSpec((N_EXPERTS, 1), const2),
            pl.BlockSpec((ROUTE_BLK, ROUTE_BLK), const2),
        ],
        out_specs=[
            pl.BlockSpec((TM, D_MODEL), tok),
            pl.BlockSpec((TM, D_PACK), tok),
            pl.BlockSpec((8, TM), lane_tok),
            pl.BlockSpec((8, TM), lane_tok),
            pl.BlockSpec((N_EXPERTS, HB), const2),
        ],
        out_shape=[
            jax.ShapeDtypeStruct((n_tok, D_MODEL), F32),
            jax.ShapeDtypeStruct((n_tok, D_PACK), jnp.uint32),
            jax.ShapeDtypeStruct((8, n_tok), jnp.int32),
            jax.ShapeDtypeStruct((8, n_tok), F32),
            jax.ShapeDtypeStruct((N_EXPERTS, HB), F32),
        ],
        scratch_shapes=[pltpu.VMEM((N_EXPERTS, 1), F32)],
        compiler_params=_cparams("arbitrary"),
        name="oproj_norm_route",
    )(o, o if o_ctx is None else o_ctx, x, mod_l, w_o, g.reshape(1, D_MODEL), rw_pad, rb_col, tri)


def _moe_kernel(te_ref, tv_ref, xs_ref, wg_ref, wu_ref, wd_ref, ys_ref, wgb, wub, wdb):
    i = pl.program_id(0)
    valid = tv_ref[i]
    e = te_ref[i]
    prev = te_ref[jnp.maximum(i - 1, 0)]
    half = TM // 2

    @pl.when((i == 0) | (e != prev))
    def _():
        wgb[...] = wg_ref[0, 0].astype(BF16)
        wub[...] = wu_ref[0, 0].astype(BF16)
        wdb[...] = wd_ref[0, 0].astype(BF16)

    def ffn(rows, x):
        x = _unpack_bf16_pair(x).astype(BF16)
        a = _dot(x, wgb[...])
        u = _dot(x, wub[...])
        hm = (a * _sigmoid(a) * u).astype(BF16)
        ys_ref[rows, :] = _pack_bf16_pair(_dot(hm, wdb[...]))

    @pl.when(valid == TM)
    def _():
        for r0 in range(0, TM, half):
            ffn(pl.ds(r0, half), xs_ref[pl.ds(r0, half), :])

    for r0 in range(0, TM, half):
        rows = pl.ds(r0, half)

        @pl.when((valid < TM) & (valid >= r0 + half))
        def _():
            ffn(rows, xs_ref[rows, :])

        @pl.when((valid > r0) & (valid < r0 + half))
        def _():
            row = r0 + lax.broadcasted_iota(jnp.int32, (half, 1), 0)
            x = xs_ref[rows, :]
            ffn(rows, jnp.where(row < valid, x, jnp.zeros_like(x)))

        @pl.when(valid <= r0)
        def _():
            ys_ref[rows, :] = jnp.zeros((half, D_PACK), jnp.uint32)


def _grouped_ffn(tile_expert, tile_valid, xs, w_gate, w_up, w_down, layer):
    n_tiles = xs.shape[0] // TM
    wmap = lambda i, te, tv: (layer, te[i], 0, 0)
    tok = lambda i, te, tv: (i, 0)
    return pl.pallas_call(
        _moe_kernel,
        grid_spec=pltpu.PrefetchScalarGridSpec(
            num_scalar_prefetch=2,
            grid=(n_tiles,),
            in_specs=[
                pl.BlockSpec((TM, D_PACK), tok),
                pl.BlockSpec((1, 1, D_MODEL, D_EXPERT), wmap),
                pl.BlockSpec((1, 1, D_MODEL, D_EXPERT), wmap),
                pl.BlockSpec((1, 1, D_EXPERT, D_MODEL), wmap),
            ],
            out_specs=pl.BlockSpec((TM, D_PACK), tok),
            scratch_shapes=[
                pltpu.VMEM((D_MODEL, D_EXPERT), BF16),
                pltpu.VMEM((D_MODEL, D_EXPERT), BF16),
                pltpu.VMEM((D_EXPERT, D_MODEL), BF16),
            ],
        ),
        out_shape=jax.ShapeDtypeStruct((xs.shape[0], D_PACK), jnp.uint32),
        compiler_params=_cparams("arbitrary"),
        name="grouped_ffn",
    )(tile_expert, tile_valid, xs, w_gate, w_up, w_down)


def _moe_layout(route_i, counts, n_tok):
    n_tiles = 2 * n_tok // TM + N_EXPERTS
    counts = counts.astype(jnp.int32)
    tiles_e = (counts + TM - 1) // TM
    tiles_end = jnp.cumsum(tiles_e)
    tile_start = tiles_end - tiles_e
    eid = jnp.arange(N_EXPERTS, dtype=jnp.int32)
    tok_oh = route_i[0:2][:, :, None] == eid
    pos = jnp.sum(jnp.where(tok_oh, tile_start * TM, 0), axis=-1) + route_i[2:4]
    tile_ids = jnp.arange(n_tiles, dtype=jnp.int32)
    n_used = tiles_end[-1]
    te = jnp.sum(tiles_end[None, :] <= jnp.minimum(tile_ids, n_used - 1)[:, None], axis=1).astype(jnp.int32)
    tile_oh = te[:, None] == eid
    cnt_t = jnp.sum(jnp.where(tile_oh, counts, 0), axis=1)
    start_t = jnp.sum(jnp.where(tile_oh, tile_start, 0), axis=1)
    tv = jnp.clip(cnt_t - (tile_ids - start_t) * TM, 0, TM)
    tv = jnp.where(tile_ids < n_used, tv, 0).astype(jnp.int32)
    return pos.astype(jnp.int32), te, tv, n_tiles


SC_CORES = 2
SC_SUBCORES = 16
SC_WORKERS = SC_CORES * SC_SUBCORES
SC_CHUNK = 32
SC_RING = 4


def _sc_mesh():
    return plsc.VectorSubcoreMesh(core_axis_name="c", subcore_axis_name="s")


def _sc_worker_indices(pos, n_tok):
    n_ch = n_tok // SC_WORKERS // SC_CHUNK
    return pos.reshape(2, SC_WORKERS, n_ch, SC_CHUNK).transpose(1, 2, 0, 3), n_ch


def _sc_dispatch(h, pos, n_slots):
    n_tok = h.shape[0]
    per_w = n_tok // SC_WORKERS
    pos_w, n_ch = _sc_worker_indices(pos, n_tok)
    ring = next(r for r in (6, 4, 3, 2) if n_ch % r == 0)
    assert n_ch * SC_CHUNK * SC_WORKERS == n_tok

    @functools.partial(
        pl.kernel, mesh=_sc_mesh(), out_type=jax.ShapeDtypeStruct((n_slots, h.shape[1]), h.dtype),
        scratch_types=[pltpu.VMEM((n_ch, 2, SC_CHUNK), jnp.int32),
                       pltpu.VMEM((ring, SC_CHUNK, h.shape[1]), h.dtype),
                       pltpu.SemaphoreType.DMA((ring,)), pltpu.SemaphoreType.DMA((ring,))],
        name="sc_dispatch")
    def dispatch(h_hbm, pos_hbm, xs_hbm, idx_v, rows_v, load_sem, scat_sem):
        wid = lax.axis_index("s") * SC_CORES + lax.axis_index("c")
        base = wid * per_w
        pltpu.sync_copy(pos_hbm.at[wid], idx_v)

        def load(c, b):
            return pltpu.make_async_copy(h_hbm.at[pl.ds(base + c * SC_CHUNK, SC_CHUNK)], rows_v.at[b],
                                         load_sem.at[b])

        def scat(c, b, k):
            return pltpu.make_async_copy(rows_v.at[b], xs_hbm.at[idx_v.at[c, k]], scat_sem.at[b])

        for c in range(ring - 1):
            load(c, c).start()

        @pl.loop(0, n_ch, step=ring)
        def _(c0):
            for b in range(ring):
                c = c0 + b
                load(c, b).wait()
                scat(c, b, 0).start()
                scat(c, b, 1).start()
                pb = (b - 1) % ring

                @pl.when(c >= 1)
                def _():
                    scat(c - 1, pb, 0).wait()
                    scat(c - 1, pb, 1).wait()

                @pl.when(c + ring - 1 < n_ch)
                def _():
                    load(c + ring - 1, pb).start()

        last = (n_ch - 1) % ring
        scat(n_ch - 1, last, 0).wait()
        scat(n_ch - 1, last, 1).wait()

    return dispatch(h, pos_w)


def _sc_combine_gather(ys, pos):
    n_tok = pos.shape[1]
    per_w = n_tok // SC_WORKERS
    pos_w, n_ch = _sc_worker_indices(pos, n_tok)
    n_items = 2 * n_ch
    assert n_ch * SC_CHUNK * SC_WORKERS == n_tok and n_items % SC_RING == 0 and SC_RING % 2 == 0

    @functools.partial(
        pl.kernel, mesh=_sc_mesh(), out_type=jax.ShapeDtypeStruct((2, n_tok, ys.shape[1]), ys.dtype),
        scratch_types=[pltpu.VMEM((n_ch, 2, SC_CHUNK), jnp.int32),
                       pltpu.VMEM((SC_RING, SC_CHUNK, ys.shape[1]), ys.dtype),
                       pltpu.SemaphoreType.DMA((SC_RING,)), pltpu.SemaphoreType.DMA((SC_RING,))],
        name="sc_combine_gather")
    def gather(ys_hbm, pos_hbm, yg_hbm, idx_v, rows_v, gath_sem, write_sem):
        wid = lax.axis_index("s") * SC_CORES + lax.axis_index("c")
        base = wid * per_w
        pltpu.sync_copy(pos_hbm.at[wid], idx_v)

        def gath(c, k, b):
            return pltpu.make_async_copy(ys_hbm.at[idx_v.at[c, k]], rows_v.at[b], gath_sem.at[b])

        def write(c, k, b):
            return pltpu.make_async_copy(rows_v.at[b], yg_hbm.at[k, pl.ds(base + c * SC_CHUNK, SC_CHUNK)],
                                         write_sem.at[b])

        for j in range(SC_RING - 1):
            gath(j // 2, j % 2, j).start()

        @pl.loop(0, n_items, step=SC_RING)
        def _(j0):
            for b in range(SC_RING):
                j = j0 + b
                gath(j // 2, b % 2, b).wait()
                write(j // 2, b % 2, b).start()
                pb = (b - 1) % SC_RING

                @pl.when(j >= 1)
                def _():
                    write((j - 1) // 2, (b - 1) % 2, pb).wait()

                @pl.when(j + SC_RING - 1 < n_items)
                def _():
                    gath((j + SC_RING - 1) // 2, (b - 1) % 2, pb).start()

        write(n_ch - 1, 1, (n_items - 1) % SC_RING).wait()

    return gather(ys, pos_w)


def _final_kernel(x1_ref, yg_ref, gt_ref, mod_ref, fg_ref, out_ref):
    x2 = _moe_residual(x1_ref[...], yg_ref[0], yg_ref[1], gt_ref[...], mod_ref[0][5:6])
    ms = jnp.mean(x2 * x2, axis=-1, keepdims=True)
    out_ref[...] = x2 * lax.rsqrt(ms + NORM_EPS) * fg_ref[...]


def _combine_final(x1, yg, gates_t, mod_l, final_g, n_tok):
    tok = lambda i: (i, 0)
    return pl.pallas_call(
        _final_kernel,
        grid=(n_tok // TM,),
        in_specs=[
            pl.BlockSpec((TM, D_MODEL), tok),
            pl.BlockSpec((2, TM, D_PACK), lambda i: (0, i, 0)),
            pl.BlockSpec((TM, 2), tok),
            pl.BlockSpec((1, N_MOD, D_MODEL), lambda i: (_mod_row(i), 0, 0)),
            pl.BlockSpec((1, D_MODEL), lambda i: (0, 0)),
        ],
        out_specs=pl.BlockSpec((TM, D_MODEL), tok),
        out_shape=jax.ShapeDtypeStruct((n_tok, D_MODEL), F32),
        compiler_params=_cparams("parallel"),
        name="moe_combine_final",
    )(x1, yg, gates_t, mod_l, final_g.reshape(1, D_MODEL))


def _diff_lambda_init(layer_idx):
    return 0.8 - 0.6 * math.exp(-0.3 * (layer_idx - 1))


def kernel(x, c, ctx, c_ctx, mod_w, mod_b, norm_mix, norm_ffn, w_qkv, w_o, na_rpb, diff_lambda, diff_subln,
           router_w, router_b, expert_w_gate, expert_w_up, expert_w_down, final_norm):
    mod = _modulation(c, c_ctx, mod_w, mod_b)
    stream = (x.reshape(T_LAT, D_MODEL), ctx.reshape(T_CTX, D_MODEL))
    w_qk_b, w_vt_b = _qkv_weights(w_qkv)
    w_o_b = w_o.astype(BF16)
    na_tables = _na_pair_table(na_rpb.reshape(-1, N_DR, N_DC))
    cs, sn = _rope_tables()
    rw_pad = jnp.zeros((D_MODEL, HB), F32).at[:, :N_EXPERTS].set(router_w)
    rb_col = router_b.reshape(N_EXPERTS, 1).astype(F32)
    blk = np.arange(ROUTE_BLK)
    tri = jnp.asarray(blk[:, None] < blk[None, :], BF16)

    for i in range(DEPTH):
        last = i == DEPTH - 1
        j = i // 2
        is_diff = i % 2 == 1
        xa, q, k, v = _qkv(stream, mod[i], norm_mix[i], w_qk_b, w_vt_b, i, cs, sn, rope=is_diff)
        if not is_diff:
            o, o_ctx = _na_attention(na_tables, j, q, k, v)
        else:
            lam_init = _diff_lambda_init(i + 1)
            sg = diff_subln[j].reshape(HB, 1)
            outs = _diff_attention(diff_lambda[j], sg, q, k, v, lam_init, ctx_out=not last)
            o, o_ctx = outs if not last else (outs[0], None)
        n_tok = T_LAT if last else T_ALL
        x1, h, route_i, route_f, cnt = _oproj_route(o, o_ctx, xa, mod[i], w_o_b, i, norm_ffn[i], rw_pad,
                                                    rb_col, tri, n_tok)
        pos, te, tv, n_tiles = _moe_layout(route_i, cnt[:, 0], n_tok)
        xs = _sc_dispatch(h, pos, n_tiles * TM)
        ys = _grouped_ffn(te, tv, xs, expert_w_gate, expert_w_up, expert_w_down, i)
        yg = _sc_combine_gather(ys, pos)
        stream = (x1, yg, route_f[0:2].T, mod[i])
    out = _combine_final(*stream, final_norm, T_LAT)
    return out.reshape(BATCH, SEQ, D_MODEL)
```

```python
import functools
import math

import jax
import jax.numpy as jnp
import numpy as np
from jax import lax
from jax.experimental import pallas as pl
from jax.experimental.pallas import tpu as pltpu
from jax.experimental.pallas import tpu_sc as plsc

F32 = jnp.float32
BF16 = jnp.bfloat16

D_MODEL = 1024
BATCH = 8
SEQ = 2048
DEPTH = 4
CTX_LEN = 256
GRID_W = 64
ROWS = SEQ // GRID_W
NA_WIN_ROWS = 8
NA_WIN_COLS = 16
DIFF_HEADS = 8
DIFF_HEAD_DIM = 64
ROPE_THETA = 10000.0
N_EXPERTS = 16
N_GROUPS = 4
EXPERTS_PER_GROUP = 4
D_EXPERT = 1024
N_MOD = 6
NORM_EPS = 1e-6

T_LAT = BATCH * SEQ
T_CTX = BATCH * CTX_LEN
T_ALL = T_LAT + T_CTX
TM = 512
QKV_BLK = TM // 2
ROUTE_BLK = TM // 2
LAT_TILES = T_LAT // TM
TILES_PER_BATCH = SEQ // TM
CTX_ROW = BATCH
MOD_ROWS = 16
MOD_TN = 1536
HB = 128
N_HB = D_MODEL // HB
NA_RB = 4
NA_BAND_ROWS = 12
NA_Q = NA_RB * GRID_W
NA_BAND = NA_BAND_ROWS * GRID_W
MASK_VALUE = -1e30
LOG2E = math.log2(math.e)
Q_SCALE = 0.125 * LOG2E
VMEM_LIMIT = 56 * 1024 * 1024


def _cparams(*sem):
    return pltpu.CompilerParams(dimension_semantics=sem, vmem_limit_bytes=VMEM_LIMIT)


def _dot(a, b):
    return jnp.dot(a, b, preferred_element_type=F32)


def _dot_nt(a, b):
    return lax.dot_general(a, b, (((1,), (1,)), ((), ())), preferred_element_type=F32)


def _sigmoid(x):
    return 1.0 / (1.0 + jnp.exp(-x))


D_PACK = D_MODEL // 2


def _pack_bf16_pair(x):
    lo = lax.bitcast_convert_type(x[:, :D_PACK].astype(BF16).astype(F32), jnp.uint32)
    hi = lax.bitcast_convert_type(x[:, D_PACK:].astype(BF16).astype(F32), jnp.uint32)
    return (lo >> 16) | hi


def _unpack_bf16_pair(w):
    lo = lax.bitcast_convert_type(w << 16, F32)
    hi = lax.bitcast_convert_type(w & jnp.uint32(0xFFFF0000), F32)
    return jnp.concatenate([lo, hi], axis=1)


def _mod_row(i):
    return jnp.minimum(i // TILES_PER_BATCH, CTX_ROW)


def _mod_kernel(act_ref, w_ref, b_ref, o_ref):
    a = act_ref[...]
    a = a * _sigmoid(a)
    o_ref[0] = _dot(a.astype(BF16), w_ref[0].astype(BF16)) + b_ref[0]


def _modulation(c, c_ctx, mod_w, mod_b):
    act = jnp.zeros((MOD_ROWS, D_MODEL), F32).at[:BATCH].set(c).at[CTX_ROW].set(c_ctx)
    out = pl.pallas_call(
        _mod_kernel,
        grid=(DEPTH, N_MOD * D_MODEL // MOD_TN),
        in_specs=[
            pl.BlockSpec((MOD_ROWS, D_MODEL), lambda l, j: (0, 0)),
            pl.BlockSpec((1, D_MODEL, MOD_TN), lambda l, j: (l, 0, j)),
            pl.BlockSpec((1, 1, MOD_TN), lambda l, j: (l, 0, j)),
        ],
        out_specs=pl.BlockSpec((1, MOD_ROWS, MOD_TN), lambda l, j: (l, 0, j)),
        out_shape=jax.ShapeDtypeStruct((DEPTH, MOD_ROWS, N_MOD * D_MODEL), F32),
        compiler_params=_cparams("parallel", "parallel"),
        name="adaln_mod",
    )(act, mod_w, mod_b.reshape(DEPTH, 1, N_MOD * D_MODEL))
    return out.reshape(DEPTH, MOD_ROWS, N_MOD, D_MODEL)


def _qkv_weight_kernel(w_ref, wqk_ref, wvt_ref):
    j = pl.program_id(1)

    @pl.when(j < 2)
    def _():
        wqk_ref[0] = w_ref[0].astype(BF16)

    @pl.when(j == 2)
    def _():
        wvt_ref[0] = w_ref[0].T.astype(BF16)


def _qkv_weights(w_qkv):
    return pl.pallas_call(
        _qkv_weight_kernel,
        grid=(DEPTH, 3),
        in_specs=[pl.BlockSpec((1, D_MODEL, D_MODEL), lambda l, j: (l, 0, j))],
        out_specs=[
            pl.BlockSpec((1, D_MODEL, D_MODEL), lambda l, j: (l, 0, jnp.minimum(j, 1))),
            pl.BlockSpec((1, D_MODEL, D_MODEL), lambda l, j: (l, 0, 0)),
        ],
        out_shape=[jax.ShapeDtypeStruct((DEPTH, D_MODEL, 2 * D_MODEL), BF16),
                   jax.ShapeDtypeStruct((DEPTH, D_MODEL, D_MODEL), BF16)],
        compiler_params=_cparams("parallel", "arbitrary"),
        name="qkv_weight_prep",
    )(w_qkv)


def _rms_modulate(x, g, shift, scale):
    ms = jnp.mean(x * x, axis=-1, keepdims=True)
    y = x * lax.rsqrt(ms + NORM_EPS) * g
    return y * (1.0 + scale) + shift


def _rope_block(xb, cs, sn):
    lane = lax.broadcasted_iota(jnp.int32, xb.shape, 1)
    partner = jnp.where((lane & 63) < 32, pltpu.roll(xb, 96, 1), pltpu.roll(xb, 32, 1))
    return xb * cs + partner * sn


def _moe_residual(x1, y0, y1, g, gate_ffn):
    y = g[:, 0:1] * _unpack_bf16_pair(y0) + g[:, 1:2] * _unpack_bf16_pair(y1)
    return x1 + gate_ffn * y


def _qkv_kernel(*refs, rope, first):
    n_stream = 2 if first else 4
    stream_refs, refs = refs[:n_stream], refs[n_stream:]
    mod_ref, g_ref, wqk_ref, wvt_ref, cs_ref, sn_ref, x_ref, q_ref, k_ref, vt_ref = refs
    m = mod_ref[0]
    for r0 in range(0, TM, QKV_BLK):
        rows = pl.ds(r0, QKV_BLK)
        if first:
            xl_ref, xc_ref = stream_refs
            x = jnp.where(pl.program_id(0) < LAT_TILES, xl_ref[rows, :], xc_ref[rows, :])
        else:
            x1_ref, yg_ref, gt_ref, modp_ref = stream_refs
            x = _moe_residual(x1_ref[rows, :], yg_ref[0, rows, :], yg_ref[1, rows, :], gt_ref[rows, :],
                              modp_ref[0][5:6])
        x_ref[rows, :] = x
        h = _rms_modulate(x, g_ref[...], m[0:1], m[1:2]).astype(BF16)
        for idx, out in enumerate((q_ref, k_ref)):
            acc = _dot(h, wqk_ref[0, :, idx * D_MODEL:(idx + 1) * D_MODEL])
            if rope:
                cs = cs_ref[rows, :]
                sn = sn_ref[rows, :]
                acc = jnp.concatenate(
                    [_rope_block(acc[:, j * HB:(j + 1) * HB], cs, sn) for j in range(N_HB)], axis=1)
            if idx == 0:
                acc = acc * Q_SCALE
            out[rows, :] = acc.astype(BF16)
        vt_ref[:, rows] = _dot_nt(wvt_ref[0], h).astype(BF16)


def _qkv(stream, mod_l, g, w_qk, w_vt, layer, cs, sn, *, rope):
    first = len(stream) == 2
    n_tiles = T_ALL // TM
    tok = lambda i: (i, 0)
    const2 = lambda i: (0, 0)
    mod_spec = pl.BlockSpec((1, N_MOD, D_MODEL), lambda i: (_mod_row(i), 0, 0))
    rope_idx = lambda i: (jnp.where(i < LAT_TILES, i % TILES_PER_BATCH, TILES_PER_BATCH), 0)
    out = jax.ShapeDtypeStruct((T_ALL, D_MODEL), BF16)
    wmap = lambda i: (layer, 0, 0)
    if first:
        stream_specs = [
            pl.BlockSpec((TM, D_MODEL), lambda i: (jnp.minimum(i, LAT_TILES - 1), 0)),
            pl.BlockSpec((TM, D_MODEL), lambda i: (jnp.maximum(i - LAT_TILES, 0), 0)),
        ]
    else:
        stream_specs = [
            pl.BlockSpec((TM, D_MODEL), tok),
            pl.BlockSpec((2, TM, D_PACK), lambda i: (0, i, 0)),
            pl.BlockSpec((TM, 2), tok),
            mod_spec,
        ]
    return pl.pallas_call(
        functools.partial(_qkv_kernel, rope=rope, first=first),
        grid=(n_tiles,),
        in_specs=stream_specs + [
            mod_spec,
            pl.BlockSpec((1, D_MODEL), const2),
            pl.BlockSpec((1, D_MODEL, 2 * D_MODEL), wmap),
            pl.BlockSpec((1, D_MODEL, D_MODEL), wmap),
            pl.BlockSpec((TM, HB), rope_idx),
            pl.BlockSpec((TM, HB), rope_idx),
        ],
        out_specs=[pl.BlockSpec((TM, D_MODEL), tok), pl.BlockSpec((TM, D_MODEL), tok),
                   pl.BlockSpec((TM, D_MODEL), tok), pl.BlockSpec((D_MODEL, TM), lambda i: (0, i))],
        out_shape=[jax.ShapeDtypeStruct((T_ALL, D_MODEL), F32), out, out,
                   jax.ShapeDtypeStruct((D_MODEL, T_ALL), BF16)],
        compiler_params=_cparams("parallel"),
        name="norm_qkv_rope" if rope else "norm_qkv",
    )(*stream, mod_l, g.reshape(1, D_MODEL), w_qk, w_vt, cs, sn)


def _rope_tables():
    t = np.arange(SEQ)
    row = (t // GRID_W).astype(np.float32)
    col = (t % GRID_W).astype(np.float32)
    n_freq = DIFF_HEAD_DIM // 4
    inv_freq = np.float32(ROPE_THETA) ** (-np.arange(n_freq, dtype=np.float32) / np.float32(n_freq))
    ang = np.concatenate([row[:, None] * inv_freq, col[:, None] * inv_freq], axis=-1)
    cos, sin = np.cos(ang), np.sin(ang)
    cs = np.concatenate([cos, cos, cos, cos], axis=-1)
    sn = np.concatenate([-sin, sin, -sin, sin], axis=-1)
    cs = np.concatenate([cs, np.ones((TM, HB), np.float32)], axis=0)
    sn = np.concatenate([sn, np.zeros((TM, HB), np.float32)], axis=0)
    return jnp.asarray(cs, F32), jnp.asarray(sn, F32)


Q_BLK = 256
KEY_BLK = 256
SUBLANES = 8
DIFF_Q_BLK = 256
DIFF_Q_PER_TRIP = 8
DIFF_KEY_BLK = 1024


def _colwise(reduce_fn, x):
    return reduce_fn(x.reshape(x.shape[0] // SUBLANES, SUBLANES, x.shape[1]), axis=0)


def _key_blocks(k_ref, vt_ref, key_start=0, n_keys=None, bias=None, blk=KEY_BLK):
    n_keys = k_ref.shape[0] if n_keys is None else n_keys
    blk = min(blk, n_keys)
    blocks = []
    for c, s0 in enumerate(range(0, n_keys, blk)):
        blocks.append((
            lambda s0=s0: k_ref[pl.ds(key_start + s0, blk), :],
            lambda s0=s0: vt_ref[:, pl.ds(key_start + s0, blk)],
            None if bias is None else functools.partial(bias, c),
            blk,
        ))
    return blocks


def _attention_t(operands, s_scr):
    def rows_of(u, c):
        sizes = [blk[3] for blk in operands[u][1]]
        return slice(sum(sizes[:c]), sum(sizes[:c + 1]))

    def score(u, c, mx):
        qm, blocks = operands[u]
        k_tile, _, bias, _ = blocks[c]
        st = _dot_nt(k_tile(), qm)
        if bias is not None:
            st = st + bias()
        s_scr[u % 2, rows_of(u, c), :n_q] = st
        return jnp.maximum(mx, _colwise(jnp.max, st))

    def expo(u, c, m, acc, den):
        vt_tile = operands[u][1][c][1]
        e = jnp.exp2(s_scr[u % 2, rows_of(u, c), :n_q] - m)
        o = _dot(vt_tile(), e.astype(BF16))
        return (o if acc is None else acc + o), den + _colwise(jnp.sum, e)

    n_q = operands[0][0].shape[0]
    neg = jnp.full((SUBLANES, n_q), -jnp.inf, F32)
    zero = jnp.zeros((SUBLANES, n_q), F32)
    n_blocks = [len(blocks) for _, blocks in operands]
    mx = neg
    for c in range(n_blocks[0]):
        mx = score(0, c, mx)
    outs = []
    for u in range(len(operands)):
        m = jnp.max(mx, axis=0, keepdims=True)
        acc, den, mx = None, zero, neg
        n_next = n_blocks[u + 1] if u + 1 < len(operands) else 0
        for c in range(max(n_blocks[u], n_next)):
            if c < n_next:
                mx = score(u + 1, c, mx)
            if c < n_blocks[u]:
                acc, den = expo(u, c, m, acc, den)
        outs.append(acc / jnp.sum(den, axis=0, keepdims=True))
    return outs


def _half_masks(shape):
    lane = lax.broadcasted_iota(jnp.int32, shape, 1)
    return lane < 64, lane >= 64


N_DR = 2 * NA_WIN_ROWS - 1
N_DC = 2 * NA_WIN_COLS - 1
PAIR_MASK_FIRST = N_DR - 1
PAIR_MASK_SECOND = N_DR
PAIR_MASKED = N_DR + 1
N_PAIRS = N_DR + 2
NA_KEY_BLK = 768
DR_LO = NA_WIN_ROWS - 1 - NA_WIN_ROWS // 2
DR_HI = DR_LO + NA_WIN_ROWS - 1


def _na_pair_table(rpb):
    h = rpb.shape[0]
    w = GRID_W
    col = np.arange(w)
    dc = np.clip(col[:, None] - col[None, :] + NA_WIN_COLS - 1, 0, N_DC - 1)
    onehot = (dc[:, :, None] == np.arange(N_DC)).astype(np.float32)
    t = jnp.einsum("hdc,kqc->hdkq", rpb.astype(F32), onehot, precision=lax.Precision.HIGHEST)
    cstart = np.clip(col - NA_WIN_COLS // 2, 0, w - NA_WIN_COLS)
    cvalid = (col[:, None] >= cstart[None, :]) & (col[:, None] < cstart[None, :] + NA_WIN_COLS)
    tt = jnp.where(cvalid, t * LOG2E, MASK_VALUE)
    masked = jnp.full((h, 1, w, w), MASK_VALUE, F32)
    regular = jnp.concatenate([tt[:, 1:], tt[:, :-1]], axis=-1)
    first = jnp.concatenate([masked, tt[:, DR_HI:DR_HI + 1]], axis=-1)
    second = jnp.concatenate([tt[:, DR_LO:DR_LO + 1], masked], axis=-1)
    return jnp.concatenate([regular, first, second, jnp.concatenate([masked, masked], axis=-1)], axis=1)


def _na_pair_entries(q_row0, band_row0, band_rows):
    entries = []
    for j in range(band_rows):
        kr = band_row0 + j
        row = []
        for a in range(0, NA_RB, 2):
            ok, d = [], []
            for r in (q_row0 + a, q_row0 + a + 1):
                r0 = min(max(r - NA_WIN_ROWS // 2, 0), ROWS - NA_WIN_ROWS)
                ok.append(r0 <= kr < r0 + NA_WIN_ROWS)
                d.append(kr - r + NA_WIN_ROWS - 1)
            if ok[0] and ok[1]:
                row.append(d[0] - 1)
            elif ok[1]:
                assert d[1] == DR_HI
                row.append(PAIR_MASK_FIRST)
            elif ok[0]:
                assert d[0] == DR_LO
                row.append(PAIR_MASK_SECOND)
            else:
                row.append(PAIR_MASKED)
        entries.append(row)
    return entries


def _na_kernel(tab_ref, q_ref, k_ref, vt_ref, qc_ref, kc_ref, vct_ref, o_ref, oc_ref, s_scr):
    n_blocks = ROWS // NA_RB
    ctx_blocks = _key_blocks(kc_ref, vct_ref)

    def operands(q_start, band_start, band_rows, entries):
        q = q_ref[pl.ds(q_start, NA_Q), :]
        n_band = band_rows * GRID_W
        blk = NA_KEY_BLK if n_band % NA_KEY_BLK == 0 else n_band
        rows_per_chunk = blk // GRID_W
        ops = []
        for hh, msk in enumerate(_half_masks(q.shape)):
            def bias(c, hh=hh):
                rows = entries[c * rows_per_chunk:(c + 1) * rows_per_chunk]
                return jnp.concatenate(
                    [jnp.concatenate([tab_ref[hh, e] for e in row], axis=1) for row in rows], axis=0)

            band = _key_blocks(k_ref, vt_ref, band_start, n_band, bias, blk)
            ops.append((jnp.where(msk, q, jnp.zeros_like(q)), band + ctx_blocks))
        return ops

    def merge_heads(o_lo, o_hi):
        feat = lax.broadcasted_iota(jnp.int32, o_lo.shape, 0)
        return jnp.where(feat < HB // 2, o_lo, o_hi).T.astype(BF16)

    last_band = ROWS - NA_WIN_ROWS
    interior = _na_pair_entries(NA_RB, 0, NA_BAND_ROWS)
    blocks = [(0, 0, NA_WIN_ROWS, _na_pair_entries(0, 0, NA_WIN_ROWS))]
    blocks += [(rb * NA_Q, (rb - 1) * NA_Q, NA_BAND_ROWS, interior) for rb in range(1, n_blocks - 1)]
    blocks.append(((n_blocks - 1) * NA_Q, last_band * GRID_W, NA_WIN_ROWS,
                   _na_pair_entries(ROWS - NA_RB, last_band, NA_WIN_ROWS)))
    qc = qc_ref[...]
    ctx_ops = [(jnp.where(msk, qc, jnp.zeros_like(qc)), ctx_blocks) for msk in _half_masks(qc.shape)]
    outs = _attention_t([op for blk in blocks for op in operands(*blk)] + ctx_ops, s_scr)
    for i, blk in enumerate(blocks):
        o_ref[pl.ds(blk[0], NA_Q), :] = merge_heads(outs[2 * i], outs[2 * i + 1])
    oc_ref[...] = merge_heads(outs[-2], outs[-1])


def _na_attention(tables, layer_na, q, k, vt):
    assert NA_Q == Q_BLK and CTX_LEN == Q_BLK
    lat = lambda hp, b: (b, hp)
    ctx = lambda hp, b: (T_LAT // CTX_LEN + b, hp)
    return pl.pallas_call(
        _na_kernel,
        grid=(N_HB, BATCH),
        in_specs=[
            pl.BlockSpec((2, N_PAIRS, GRID_W, 2 * GRID_W), lambda hp, b: (layer_na * N_HB + hp, 0, 0, 0)),
            pl.BlockSpec((SEQ, HB), lat),
            pl.BlockSpec((SEQ, HB), lat),
            pl.BlockSpec((HB, SEQ), lambda hp, b: (hp, b)),
            pl.BlockSpec((CTX_LEN, HB), ctx),
            pl.BlockSpec((CTX_LEN, HB), ctx),
            pl.BlockSpec((HB, CTX_LEN), lambda hp, b: (hp, T_LAT // CTX_LEN + b)),
        ],
        out_specs=[pl.BlockSpec((SEQ, HB), lat), pl.BlockSpec((CTX_LEN, HB), lat)],
        out_shape=[jax.ShapeDtypeStruct((T_LAT, D_MODEL), BF16), jax.ShapeDtypeStruct((T_CTX, D_MODEL), BF16)],
        scratch_shapes=[pltpu.VMEM((2, NA_BAND + CTX_LEN, Q_BLK), F32)],
        compiler_params=_cparams("parallel", "parallel"),
        name="na_attention",
    )(tables, q, k, vt, q, k, vt)


def _diff_lambda(lam_ref, lam_init):
    lam = lam_ref[...]
    a = jnp.sum(lam[0:1] * lam[1:2], axis=-1, keepdims=True)
    b = jnp.sum(lam[2:3] * lam[3:4], axis=-1, keepdims=True)
    return jnp.exp(a) - jnp.exp(b) + lam_init


def _diff_kernel(lam_ref, sg_ref, q_ref, k_ref, vt_ref, qc_ref, kc_ref, vct_ref, o_ref, *rest, lam_init):
    *maybe_oc_ref, s_scr = rest
    lam = _diff_lambda(lam_ref, lam_init)
    sg_col = sg_ref[...]

    def rows(qs, pieces):
        blocks = [blk for k_r, vt_r in pieces for blk in _key_blocks(k_r, vt_r, blk=DIFF_KEY_BLK)]
        outs = _attention_t([(jnp.where(msk, q, jnp.zeros_like(q)), blocks)
                             for q in qs for msk in _half_masks(q.shape)], s_scr)
        res = []
        for o1, o2 in zip(outs[0::2], outs[1::2]):
            ot = o1 - lam * o2
            ms = jnp.mean(ot * ot, axis=0, keepdims=True)
            ot = ot * lax.rsqrt(ms + NORM_EPS) * sg_col * (1.0 - lam_init)
            res.append(ot.T.astype(BF16))
        return res

    def body(t, carry):
        starts = [pl.multiple_of((t * DIFF_Q_PER_TRIP + j) * DIFF_Q_BLK, DIFF_Q_BLK)
                  for j in range(DIFF_Q_PER_TRIP)]
        outs = rows([q_ref[pl.ds(r0, DIFF_Q_BLK), :] for r0 in starts], [(k_ref, vt_ref), (kc_ref, vct_ref)])
        for r0, o in zip(starts, outs):
            o_ref[pl.ds(r0, DIFF_Q_BLK), :] = o
        return carry

    lax.fori_loop(0, SEQ // (DIFF_Q_BLK * DIFF_Q_PER_TRIP), body, 0)
    if maybe_oc_ref:
        oc_ref, = maybe_oc_ref
        oc_ref[...], = rows([qc_ref[...]], [(kc_ref, vct_ref)])


def _diff_attention(lam, sg, q, k, vt, lam_init, *, ctx_out):
    lat = lambda b, h: (b, h)
    ctx = lambda b, h: (T_LAT // CTX_LEN + b, h)
    lat_t = lambda b, h: (h, b)
    ctx_t = lambda b, h: (h, T_LAT // CTX_LEN + b)
    out_specs = [pl.BlockSpec((SEQ, HB), lat)]
    out_shape = [jax.ShapeDtypeStruct((T_LAT, D_MODEL), BF16)]
    if ctx_out:
        out_specs.append(pl.BlockSpec((CTX_LEN, HB), lat))
        out_shape.append(jax.ShapeDtypeStruct((T_CTX, D_MODEL), BF16))
    return pl.pallas_call(
        functools.partial(_diff_kernel, lam_init=lam_init),
        grid=(BATCH, DIFF_HEADS),
        in_specs=[
            pl.BlockSpec(lam.shape, lambda b, h: (0, 0)),
            pl.BlockSpec(sg.shape, lambda b, h: (0, 0)),
            pl.BlockSpec((SEQ, HB), lat),
            pl.BlockSpec((SEQ, HB), lat),
            pl.BlockSpec((HB, SEQ), lat_t),
            pl.BlockSpec((CTX_LEN, HB), ctx),
            pl.BlockSpec((CTX_LEN, HB), ctx),
            pl.BlockSpec((HB, CTX_LEN), ctx_t),
        ],
        out_specs=out_specs,
        out_shape=out_shape,
        scratch_shapes=[pltpu.VMEM((2, SEQ + CTX_LEN, DIFF_Q_BLK), F32)],
        compiler_params=_cparams("parallel", "parallel"),
        name="diff_attention",
    )(lam, sg, q, k, vt, q, k, vt)


def _route_rows(logits_t, bias_col):
    s = _sigmoid(logits_t)
    sel = s + bias_col
    sel_r = [sel[e:e + 1] for e in range(N_EXPERTS)]
    s_r = [s[e:e + 1] for e in range(N_EXPERTS)]
    group_scores = []
    for g in range(N_GROUPS):
        v = sel_r[g * EXPERTS_PER_GROUP:(g + 1) * EXPERTS_PER_GROUP]
        pairs = [v[a] + v[b] for a in range(EXPERTS_PER_GROUP) for b in range(a + 1, EXPERTS_PER_GROUP)]
        group_scores.append(functools.reduce(jnp.maximum, pairs))
    best = group_scores[0]
    grp = jnp.zeros(best.shape, jnp.int32)
    for g in range(1, N_GROUPS):
        upd = group_scores[g] > best
        best = jnp.where(upd, group_scores[g], best)
        grp = jnp.where(upd, g, grp)

    def pick(rows, j):
        out = rows[j]
        for g in range(1, N_GROUPS):
            out = jnp.where(grp == g, rows[g * EXPERTS_PER_GROUP + j], out)
        return out

    w = [pick(sel_r, j) for j in range(EXPERTS_PER_GROUP)]
    sc = [pick(s_r, j) for j in range(EXPERTS_PER_GROUP)]

    def argmax_first(vals):
        bv = vals[0]
        bi = jnp.zeros(bv.shape, jnp.int32)
        for j in range(1, len(vals)):
            upd = vals[j] > bv
            bv = jnp.where(upd, vals[j], bv)
            bi = jnp.where(upd, j, bi)
        return bi

    i1 = argmax_first(w)
    i2 = argmax_first([jnp.where(i1 == j, -jnp.inf, w[j]) for j in range(EXPERTS_PER_GROUP)])

    def take(vals, idx):
        out = vals[0]
        for j in range(1, len(vals)):
            out = jnp.where(idx == j, vals[j], out)
        return out

    s1 = take(sc, i1)
    s2 = take(sc, i2)
    den = s1 + s2
    return grp * EXPERTS_PER_GROUP + i1, grp * EXPERTS_PER_GROUP + i2, s1 / den, s2 / den


def _oproj_kernel(o_ref, oc_ref, x_ref, mod_ref, w_ref, g_ref, rw_ref, rb_ref, tri_ref,
                  x1_ref, h_ref, ri_ref, rf_ref, cnt_ref, carry_ref):
    i = pl.program_id(0)

    @pl.when(i == 0)
    def _():
        carry_ref[...] = jnp.zeros_like(carry_ref)

    m = mod_ref[0]
    rw = rw_ref[...].astype(BF16)
    ri_ref[...] = jnp.zeros_like(ri_ref)
    rf_ref[...] = jnp.zeros_like(rf_ref)
    carry = carry_ref[...]
    for r0 in range(0, TM, ROUTE_BLK):
        rows = pl.ds(r0, ROUTE_BLK)
        o = jnp.where(i < LAT_TILES, o_ref[rows, :], oc_ref[rows, :])
        x1 = x_ref[rows, :] + m[2:3] * _dot(o, w_ref[0])
        x1_ref[rows, :] = x1
        h = _rms_modulate(x1, g_ref[...], m[3:4], m[4:5])
        hb = h.astype(BF16)
        h_ref[rows, :] = _pack_bf16_pair(h)

        logits = _dot(hb, rw)
        logits_t = logits.T[:N_EXPERTS]
        e0, e1, g0, g1 = _route_rows(logits_t, rb_ref[...])

        eiota = lax.broadcasted_iota(jnp.int32, (N_EXPERTS, ROUTE_BLK), 0)
        oh0 = eiota == e0
        oh1 = eiota == e1
        oh = jnp.where(oh0 | oh1, 1.0, 0.0)
        before = _dot(oh.astype(BF16), tri_ref[...]) + carry
        rank0 = jnp.sum(jnp.where(oh0, before, 0.0), axis=0, keepdims=True)
        rank1 = jnp.sum(jnp.where(oh1, before, 0.0), axis=0, keepdims=True)
        carry = carry + jnp.sum(oh, axis=1, keepdims=True)

        ri_ref[0:1, rows] = e0
        ri_ref[1:2, rows] = e1
        ri_ref[2:3, rows] = rank0.astype(jnp.int32)
        ri_ref[3:4, rows] = rank1.astype(jnp.int32)
        rf_ref[0:1, rows] = g0
        rf_ref[1:2, rows] = g1
    carry_ref[...] = carry
    cnt_ref[...] = jnp.broadcast_to(carry, cnt_ref.shape)


def _oproj_route(o, o_ctx, x, mod_l, w_o, layer, g, rw_pad, rb_col, tri, n_tok):
    n_tiles = n_tok // TM
    tok = lambda i: (i, 0)
    const2 = lambda i: (0, 0)
    lane_tok = lambda i: (0, i)
    return pl.pallas_call(
        _oproj_kernel,
        grid=(n_tiles,),
        in_specs=[
            pl.BlockSpec((TM, D_MODEL), lambda i: (jnp.minimum(i, LAT_TILES - 1), 0)),
            pl.BlockSpec((TM, D_MODEL), lambda i: (jnp.maximum(i - LAT_TILES, 0), 0)),
            pl.BlockSpec((TM, D_MODEL), tok),
            pl.BlockSpec((1, N_MOD, D_MODEL), lambda i: (_mod_row(i), 0, 0)),
            pl.BlockSpec((1, D_MODEL, D_MODEL), lambda i: (layer, 0, 0)),
            pl.BlockSpec((1, D_MODEL), const2),
            pl.BlockSpec((D_MODEL, HB), const2),
            pl.BlockSpec((N_EXPERTS, 1), const2),
            pl.BlockSpec((ROUTE_BLK, ROUTE_BLK), const2),
        ],
        out_specs=[
            pl.BlockSpec((TM, D_MODEL), tok),
            pl.BlockSpec((TM, D_PACK), tok),
            pl.BlockSpec((8, TM), lane_tok),
            pl.BlockSpec((8, TM), lane_tok),
            pl.BlockSpec((N_EXPERTS, HB), const2),
        ],
        out_shape=[
            jax.ShapeDtypeStruct((n_tok, D_MODEL), F32),
            jax.ShapeDtypeStruct((n_tok, D_PACK), jnp.uint32),
            jax.ShapeDtypeStruct((8, n_tok), jnp.int32),
            jax.ShapeDtypeStruct((8, n_tok), F32),
            jax.ShapeDtypeStruct((N_EXPERTS, HB), F32),
        ],
        scratch_shapes=[pltpu.VMEM((N_EXPERTS, 1), F32)],
        compiler_params=_cparams("arbitrary"),
        name="oproj_norm_route",
    )(o, o if o_ctx is None else o_ctx, x, mod_l, w_o, g.reshape(1, D_MODEL), rw_pad, rb_col, tri)


def _moe_kernel(te_ref, tv_ref, xs_ref, wg_ref, wu_ref, wd_ref, ys_ref, wgb, wub, wdb):
    i = pl.program_id(0)
    valid = tv_ref[i]
    e = te_ref[i]
    prev = te_ref[jnp.maximum(i - 1, 0)]
    half = TM // 2

    @pl.when((i == 0) | (e != prev))
    def _():
        wgb[...] = wg_ref[0, 0].astype(BF16)
        wub[...] = wu_ref[0, 0].astype(BF16)
        wdb[...] = wd_ref[0, 0].astype(BF16)

    def ffn(rows, x):
        x = _unpack_bf16_pair(x).astype(BF16)
        a = _dot(x, wgb[...])
        u = _dot(x, wub[...])
        hm = (a * _sigmoid(a) * u).astype(BF16)
        ys_ref[rows, :] = _pack_bf16_pair(_dot(hm, wdb[...]))

    @pl.when(valid == TM)
    def _():
        for r0 in range(0, TM, half):
            ffn(pl.ds(r0, half), xs_ref[pl.ds(r0, half), :])

    for r0 in range(0, TM, half):
        rows = pl.ds(r0, half)

        @pl.when((valid < TM) & (valid >= r0 + half))
        def _():
            ffn(rows, xs_ref[rows, :])

        @pl.when((valid > r0) & (valid < r0 + half))
        def _():
            row = r0 + lax.broadcasted_iota(jnp.int32, (half, 1), 0)
            x = xs_ref[rows, :]
            ffn(rows, jnp.where(row < valid, x, jnp.zeros_like(x)))

        @pl.when(valid <= r0)
        def _():
            ys_ref[rows, :] = jnp.zeros((half, D_PACK), jnp.uint32)


def _grouped_ffn(tile_expert, tile_valid, xs, w_gate, w_up, w_down, layer):
    n_tiles = xs.shape[0] // TM
    wmap = lambda i, te, tv: (layer, te[i], 0, 0)
    tok = lambda i, te, tv: (i, 0)
    return pl.pallas_call(
        _moe_kernel,
        grid_spec=pltpu.PrefetchScalarGridSpec(
            num_scalar_prefetch=2,
            grid=(n_tiles,),
            in_specs=[
                pl.BlockSpec((TM, D_PACK), tok),
                pl.BlockSpec((1, 1, D_MODEL, D_EXPERT), wmap),
                pl.BlockSpec((1, 1, D_MODEL, D_EXPERT), wmap),
                pl.BlockSpec((1, 1, D_EXPERT, D_MODEL), wmap),
            ],
            out_specs=pl.BlockSpec((TM, D_PACK), tok),
            scratch_shapes=[
                pltpu.VMEM((D_MODEL, D_EXPERT), BF16),
                pltpu.VMEM((D_MODEL, D_EXPERT), BF16),
                pltpu.VMEM((D_EXPERT, D_MODEL), BF16),
            ],
        ),
        out_shape=jax.ShapeDtypeStruct((xs.shape[0], D_PACK), jnp.uint32),
        compiler_params=_cparams("arbitrary"),
        name="grouped_ffn",
    )(tile_expert, tile_valid, xs, w_gate, w_up, w_down)


def _moe_layout(route_i, counts, n_tok):
    n_tiles = 2 * n_tok // TM + N_EXPERTS
    counts = counts.astype(jnp.int32)
    tiles_e = (counts + TM - 1) // TM
    tiles_end = jnp.cumsum(tiles_e)
    tile_start = tiles_end - tiles_e
    eid = jnp.arange(N_EXPERTS, dtype=jnp.int32)
    tok_oh = route_i[0:2][:, :, None] == eid
    pos = jnp.sum(jnp.where(tok_oh, tile_start * TM, 0), axis=-1) + route_i[2:4]
    tile_ids = jnp.arange(n_tiles, dtype=jnp.int32)
    n_used = tiles_end[-1]
    te = jnp.sum(tiles_end[None, :] <= jnp.minimum(tile_ids, n_used - 1)[:, None], axis=1).astype(jnp.int32)
    tile_oh = te[:, None] == eid
    cnt_t = jnp.sum(jnp.where(tile_oh, counts, 0), axis=1)
    start_t = jnp.sum(jnp.where(tile_oh, tile_start, 0), axis=1)
    tv = jnp.clip(cnt_t - (tile_ids - start_t) * TM, 0, TM)
    tv = jnp.where(tile_ids < n_used, tv, 0).astype(jnp.int32)
    return pos.astype(jnp.int32), te, tv, n_tiles


SC_CORES = 2
SC_SUBCORES = 16
SC_WORKERS = SC_CORES * SC_SUBCORES
SC_CHUNK = 16
SC_RING_ROWS = 192


def _sc_ring(n_items, even=False):
    return next(r for r in range(SC_RING_ROWS // SC_CHUNK, 1, -1)
                if n_items % r == 0 and not (even and r % 2))


def _sc_mesh():
    return plsc.VectorSubcoreMesh(core_axis_name="c", subcore_axis_name="s")


def _sc_worker_indices(pos, n_tok):
    n_ch = n_tok // SC_WORKERS // SC_CHUNK
    return pos.reshape(2, SC_WORKERS, n_ch, SC_CHUNK).transpose(1, 2, 0, 3), n_ch


def _sc_dispatch(h, pos, n_slots):
    n_tok = h.shape[0]
    per_w = n_tok // SC_WORKERS
    pos_w, n_ch = _sc_worker_indices(pos, n_tok)
    ring = _sc_ring(n_ch)
    assert n_ch * SC_CHUNK * SC_WORKERS == n_tok

    @functools.partial(
        pl.kernel, mesh=_sc_mesh(), out_type=jax.ShapeDtypeStruct((n_slots, h.shape[1]), h.dtype),
        scratch_types=[pltpu.VMEM((n_ch, 2, SC_CHUNK), jnp.int32),
                       pltpu.VMEM((ring, SC_CHUNK, h.shape[1]), h.dtype),
                       pltpu.SemaphoreType.DMA((ring,)), pltpu.SemaphoreType.DMA((ring,))],
        name="sc_dispatch")
    def dispatch(h_hbm, pos_hbm, xs_hbm, idx_v, rows_v, load_sem, scat_sem):
        wid = lax.axis_index("s") * SC_CORES + lax.axis_index("c")
        base = wid * per_w
        pltpu.sync_copy(pos_hbm.at[wid], idx_v)

        def load(c, b):
            return pltpu.make_async_copy(h_hbm.at[pl.ds(base + c * SC_CHUNK, SC_CHUNK)], rows_v.at[b],
                                         load_sem.at[b])

        def scat(c, b, k):
            return pltpu.make_async_copy(rows_v.at[b], xs_hbm.at[idx_v.at[c, k]], scat_sem.at[b])

        for c in range(ring - 1):
            load(c, c).start()

        @pl.loop(0, n_ch, step=ring)
        def _(c0):
            for b in range(ring):
                c = c0 + b
                load(c, b).wait()
                scat(c, b, 0).start()
                scat(c, b, 1).start()
                pb = (b - 1) % ring

                @pl.when(c >= 1)
                def _():
                    scat(c - 1, pb, 0).wait()
                    scat(c - 1, pb, 1).wait()

                @pl.when(c + ring - 1 < n_ch)
                def _():
                    load(c + ring - 1, pb).start()

        last = (n_ch - 1) % ring
        scat(n_ch - 1, last, 0).wait()
        scat(n_ch - 1, last, 1).wait()

    return dispatch(h, pos_w)


def _sc_combine_gather(ys, pos):
    n_tok = pos.shape[1]
    per_w = n_tok // SC_WORKERS
    pos_w, n_ch = _sc_worker_indices(pos, n_tok)
    n_items = 2 * n_ch
    ring = _sc_ring(n_items, even=True)
    assert n_ch * SC_CHUNK * SC_WORKERS == n_tok

    @functools.partial(
        pl.kernel, mesh=_sc_mesh(), out_type=jax.ShapeDtypeStruct((2, n_tok, ys.shape[1]), ys.dtype),
        scratch_types=[pltpu.VMEM((n_ch, 2, SC_CHUNK), jnp.int32),
                       pltpu.VMEM((ring, SC_CHUNK, ys.shape[1]), ys.dtype),
                       pltpu.SemaphoreType.DMA((ring,)), pltpu.SemaphoreType.DMA((ring,))],
        name="sc_combine_gather")
    def gather(ys_hbm, pos_hbm, yg_hbm, idx_v, rows_v, gath_sem, write_sem):
        wid = lax.axis_index("s") * SC_CORES + lax.axis_index("c")
        base = wid * per_w
        pltpu.sync_copy(pos_hbm.at[wid], idx_v)

        def gath(c, k, b):
            return pltpu.make_async_copy(ys_hbm.at[idx_v.at[c, k]], rows_v.at[b], gath_sem.at[b])

        def write(c, k, b):
            return pltpu.make_async_copy(rows_v.at[b], yg_hbm.at[k, pl.ds(base + c * SC_CHUNK, SC_CHUNK)],
                                         write_sem.at[b])

        for j in range(ring - 1):
            gath(j // 2, j % 2, j).start()

        @pl.loop(0, n_items, step=ring)
        def _(j0):
            for b in range(ring):
                j = j0 + b
                gath(j // 2, b % 2, b).wait()
                write(j // 2, b % 2, b).start()
                pb = (b - 1) % ring

                @pl.when(j >= 1)
                def _():
                    write((j - 1) // 2, (b - 1) % 2, pb).wait()

                @pl.when(j + ring - 1 < n_items)
                def _():
                    gath((j + ring - 1) // 2, (b - 1) % 2, pb).start()

        write(n_ch - 1, 1, (n_items - 1) % ring).wait()

    return gather(ys, pos_w)


def _final_kernel(x1_ref, yg_ref, gt_ref, mod_ref, fg_ref, out_ref):
    x2 = _moe_residual(x1_ref[...], yg_ref[0], yg_ref[1], gt_ref[...], mod_ref[0][5:6])
    ms = jnp.mean(x2 * x2, axis=-1, keepdims=True)
    out_ref[...] = x2 * lax.rsqrt(ms + NORM_EPS) * fg_ref[...]


def _combine_final(x1, yg, gates_t, mod_l, final_g, n_tok):
    tok = lambda i: (i, 0)
    return pl.pallas_call(
        _final_kernel,
        grid=(n_tok // TM,),
        in_specs=[
            pl.BlockSpec((TM, D_MODEL), tok),
            pl.BlockSpec((2, TM, D_PACK), lambda i: (0, i, 0)),
            pl.BlockSpec((TM, 2), tok),
            pl.BlockSpec((1, N_MOD, D_MODEL), lambda i: (_mod_row(i), 0, 0)),
            pl.BlockSpec((1, D_MODEL), lambda i: (0, 0)),
        ],
        out_specs=pl.BlockSpec((TM, D_MODEL), tok),
        out_shape=jax.ShapeDtypeStruct((n_tok, D_MODEL), F32),
        compiler_params=_cparams("parallel"),
        name="moe_combine_final",
    )(x1, yg, gates_t, mod_l, final_g.reshape(1, D_MODEL))


def _diff_lambda_init(layer_idx):
    return 0.8 - 0.6 * math.exp(-0.3 * (layer_idx - 1))


def kernel(x, c, ctx, c_ctx, mod_w, mod_b, norm_mix, norm_ffn, w_qkv, w_o, na_rpb, diff_lambda, diff_subln,
           router_w, router_b, expert_w_gate, expert_w_up, expert_w_down, final_norm):
    mod = _modulation(c, c_ctx, mod_w, mod_b)
    stream = (x.reshape(T_LAT, D_MODEL), ctx.reshape(T_CTX, D_MODEL))
    w_qk_b, w_vt_b = _qkv_weights(w_qkv)
    w_o_b = w_o.astype(BF16)
    na_tables = _na_pair_table(na_rpb.reshape(-1, N_DR, N_DC))
    cs, sn = _rope_tables()
    rw_pad = jnp.zeros((D_MODEL, HB), F32).at[:, :N_EXPERTS].set(router_w)
    rb_col = router_b.reshape(N_EXPERTS, 1).astype(F32)
    blk = np.arange(ROUTE_BLK)
    tri = jnp.asarray(blk[:, None] < blk[None, :], BF16)

    for i in range(DEPTH):
        last = i == DEPTH - 1
        j = i // 2
        is_diff = i % 2 == 1
        xa, q, k, v = _qkv(stream, mod[i], norm_mix[i], w_qk_b, w_vt_b, i, cs, sn, rope=is_diff)
        if not is_diff:
            o, o_ctx = _na_attention(na_tables, j, q, k, v)
        else:
            lam_init = _diff_lambda_init(i + 1)
            sg = diff_subln[j].reshape(HB, 1)
            outs = _diff_attention(diff_lambda[j], sg, q, k, v, lam_init, ctx_out=not last)
            o, o_ctx = outs if not last else (outs[0], None)
        n_tok = T_LAT if last else T_ALL
        x1, h, route_i, route_f, cnt = _oproj_route(o, o_ctx, xa, mod[i], w_o_b, i, norm_ffn[i], rw_pad,
                                                    rb_col, tri, n_tok)
        pos, te, tv, n_tiles = _moe_layout(route_i, cnt[:, 0], n_tok)
        xs = _sc_dispatch(h, pos, n_tiles * TM)
        ys = _grouped_ffn(te, tv, xs, expert_w_gate, expert_w_up, expert_w_down, i)
        yg = _sc_combine_gather(ys, pos)
        stream = (x1, yg, route_f[0:2].T, mod[i])
    out = _combine_final(*stream, final_norm, T_LAT)
    return out.reshape(BATCH, SEQ, D_MODEL)
```
